```python
import jax
import jax.numpy as jnp
from jax import lax
import numpy as np

D_MODEL = 1024
BATCH = 2
SEQ = 8192
DEPTH = 1

GRID_W = 64
CTX_LEN = 256
NA_HEAD_DIM = 64
N_NA_HEADS = (D_MODEL // 2) // NA_HEAD_DIM
NA_WIDTH = N_NA_HEADS * NA_HEAD_DIM
NA_WIN_H = 8
NA_WIN_W = 16
NA_QBLOCK_W = 16
NA_KSPAN_W = NA_QBLOCK_W + NA_WIN_W
ROPE_BASE = 10000.0
HG_DK = 128
HG_DV = 128
N_HG_HEADS = (D_MODEL // 2) // HG_DV
HG_KEY_WIDTH = N_HG_HEADS * HG_DK
HG_WIDTH = N_HG_HEADS * HG_DV
HG_CHUNK = 32
MIX_WIDTH = NA_WIDTH + HG_WIDTH
IN_COLS = 3 * NA_WIDTH + 3 * HG_KEY_WIDTH + 2 * HG_WIDTH
N_EXPERTS = 16
EC_CAPACITY_FACTOR = 2
D_EXPERT = ((8 * D_MODEL // 3 + 63) // 64) * 64
RMS_EPS = 1e-6

kernel_name = 'hybrid_na_hgrn2_ec_diffusion_block'


def rms_norm(x, g):
    xf = x.astype(jnp.float32)
    y = xf * lax.rsqrt(jnp.mean(xf * xf, axis=-1, keepdims=True) + RMS_EPS)
    return (y * g.astype(jnp.float32)).astype(x.dtype)


def modulate(h, shift, scale):
    return h * (1.0 + scale) + shift


def to_heads(x, n_heads):
    b, t, w = x.shape
    return x.reshape(b, t, n_heads, w // n_heads).transpose(0, 2, 1, 3)


def from_heads(x):
    b, h, t, d = x.shape
    return x.transpose(0, 2, 1, 3).reshape(b, t, h * d)


def split_cols(p):
    sizes = (NA_WIDTH, NA_WIDTH, NA_WIDTH, HG_KEY_WIDTH, HG_KEY_WIDTH, HG_KEY_WIDTH, HG_WIDTH, HG_WIDTH)
    bounds = []
    acc = 0
    for s in sizes[:-1]:
        acc += s
        bounds.append(acc)
    return jnp.split(p, bounds, axis=-1)


def rope_1d(x, pos):
    d = x.shape[-1]
    inv_freq = ROPE_BASE ** (-jnp.arange(0, d, 2, dtype=jnp.float32) / d)
    ang = pos[:, None] * inv_freq[None, :]
    cos = jnp.cos(ang).astype(x.dtype)
    sin = jnp.sin(ang).astype(x.dtype)
    x1, x2 = x[..., : d // 2], x[..., d // 2:]
    return jnp.concatenate([x1 * cos - x2 * sin, x1 * sin + x2 * cos], axis=-1)


def axial_rope(x, row_pos, col_pos):
    half = x.shape[-1] // 2
    return jnp.concatenate([rope_1d(x[..., :half], row_pos), rope_1d(x[..., half:], col_pos)], axis=-1)


def neighbourhood_attention(q, k, v, k_ctx, v_ctx, rpb):
    bsz, nh, seq, dh = q.shape
    rows = seq // GRID_W
    kh = min(NA_WIN_H, rows)
    ncb = GRID_W // NA_QBLOCK_W
    scale = dh ** -0.5
    t = jnp.arange(seq)
    row_pos = (t // GRID_W).astype(jnp.float32)
    col_pos = (t % GRID_W).astype(jnp.float32)
    q_rot = axial_rope(q, row_pos, col_pos)
    k_rot = axial_rope(k, row_pos, col_pos)
    r = jnp.arange(rows)
    r0 = jnp.clip(r - kh // 2, 0, rows - kh)
    key_rows = r0[:, None] + jnp.arange(kh)[None, :]
    j = jnp.arange(ncb)
    c0 = jnp.clip(j * NA_QBLOCK_W - NA_WIN_W // 2, 0, GRID_W - NA_KSPAN_W)
    key_cols = c0[:, None] + jnp.arange(NA_KSPAN_W)[None, :]
    tok = key_rows[:, None, :, None] * GRID_W + key_cols[None, :, None, :]
    k_blk = k_rot[:, :, tok]
    v_blk = v[:, :, tok]
    q_cols = j[:, None] * NA_QBLOCK_W + jnp.arange(NA_QBLOCK_W)[None, :]
    win_c0 = jnp.clip(q_cols - NA_WIN_W // 2, 0, GRID_W - NA_WIN_W)
    kc = key_cols[:, None, :]
    in_win = (kc >= win_c0[:, :, None]) & (kc < win_c0[:, :, None] + NA_WIN_W)
    dr_idx = key_rows - r[:, None] + NA_WIN_H - 1
    dc_idx = jnp.clip(kc - q_cols[:, :, None] + NA_WIN_W - 1, 0, 2 * NA_WIN_W - 2)
    bias = rpb[:, dr_idx[:, None, None, :, None], dc_idx[None, :, :, None, :]]
    q_blk = q_rot.reshape(bsz, nh, rows, ncb, NA_QBLOCK_W, dh)
    s_win = jnp.einsum('bhrjqd,bhrjakd->bhrjqak', q_blk, k_blk).astype(jnp.float32) * scale
    s_win = jnp.where(in_win[:, :, None, :], s_win + bias.astype(jnp.float32), -jnp.inf)
    q_plain = q.reshape(bsz, nh, rows, ncb, NA_QBLOCK_W, dh)
    s_ctx = jnp.einsum('bhrjqd,bhld->bhrjql', q_plain, k_ctx).astype(jnp.float32) * scale
    n_win = kh * NA_KSPAN_W
    s_all = jnp.concatenate([s_win.reshape(bsz, nh, rows, ncb, NA_QBLOCK_W, n_win), s_ctx], axis=-1)
    p = jax.nn.softmax(s_all, axis=-1).astype(v.dtype)
    p_win = p[..., :n_win].reshape(bsz, nh, rows, ncb, NA_QBLOCK_W, kh, NA_KSPAN_W)
    o = (jnp.einsum('bhrjqak,bhrjakd->bhrjqd', p_win, v_blk)
         + jnp.einsum('bhrjql,bhld->bhrjqd', p[..., n_win:], v_ctx))
    return from_heads(o.reshape(bsz, nh, seq, dh))


def context_attention(q, k, v):
    scale = q.shape[-1] ** -0.5
    s = jnp.einsum('bhld,bhmd->bhlm', q, k).astype(jnp.float32) * scale
    p = jax.nn.softmax(s, axis=-1).astype(v.dtype)
    return jnp.einsum('bhlm,bhmd->bhld', p, v)


def log_forget(z, lb):
    lb_h = lb.reshape(N_HG_HEADS, 1, HG_DK)
    return jnp.log(lb_h + (1.0 - lb_h) * jax.nn.sigmoid(z.astype(jnp.float32)))


def gla_chunked(q, k, v, log_f, s0):
    bsz, nh, seq, dk = q.shape
    dv = v.shape[-1]
    n = seq // HG_CHUNK
    qc = q.astype(jnp.float32).reshape(bsz, nh, n, HG_CHUNK, dk)
    kc = k.astype(jnp.float32).reshape(bsz, nh, n, HG_CHUNK, dk)
    vc = v.astype(jnp.float32).reshape(bsz, nh, n, HG_CHUNK, dv)
    b = jnp.cumsum(log_f.reshape(bsz, nh, n, HG_CHUNK, dk), axis=3)
    b_end = b[:, :, :, -1:, :]
    q_dec = qc * jnp.exp(b)
    k_dec = kc * jnp.exp(-b)
    idx = jnp.arange(HG_CHUNK)
    lower = idx[:, None] >= idx[None, :]
    a = jnp.where(lower, jnp.einsum('bhnck,bhnsk->bhncs', q_dec, k_dec), 0.0)
    o_intra = jnp.einsum('bhncs,bhnsv->bhncv', a, vc)
    kv = jnp.einsum('bhnck,bhncv->bhnkv', kc * jnp.exp(b_end - b), vc)
    decay = jnp.exp(b_end[:, :, :, 0, :])

    def step(s, inp):
        dec, upd = inp
        return dec[..., None] * s + upd, s

    s_fin, s_prev = lax.scan(step, s0.astype(jnp.float32),
                             (jnp.moveaxis(decay, 2, 0), jnp.moveaxis(kv, 2, 0)))
    s_prev = jnp.moveaxis(s_prev, 0, 2)
    o_inter = jnp.einsum('bhnck,bhnkv->bhncv', q_dec, s_prev)
    return (o_intra + o_inter).reshape(bsz, nh, seq, dv), s_fin


def hgrn2_bidir(q, z_fwd, z_bwd, i, lb, s0_fwd, s0_bwd):
    lf_f = log_forget(z_fwd, lb)
    lf_b = log_forget(z_bwd, lb)
    o_f, s_f = gla_chunked(q, -jnp.expm1(lf_f), i, lf_f, s0_fwd)
    flip = lambda t: jnp.flip(t, axis=2)
    o_b, s_b = gla_chunked(flip(q), flip(-jnp.expm1(lf_b)), flip(i), flip(lf_b), s0_bwd)
    return o_f + flip(o_b), s_f, s_b


def hgrn2_readout(o, g, norm_g):
    return from_heads(rms_norm(o.astype(g.dtype), norm_g) * jax.nn.sigmoid(g))


def ec_moe(h, w_router, w_gate, w_up, w_down):
    bsz, n_tok, d = h.shape
    cap = EC_CAPACITY_FACTOR * n_tok // N_EXPERTS
    aff = jax.nn.softmax(jnp.einsum('btd,de->bte', h, w_router).astype(jnp.float32), axis=-1)
    gate, idx = lax.top_k(jnp.swapaxes(aff, 1, 2), cap)
    xs = jax.vmap(lambda hb, ib: hb[ib])(h, idx)
    hid = (jax.nn.silu(jnp.einsum('becd,edf->becf', xs, w_gate))
           * jnp.einsum('becd,edf->becf', xs, w_up))
    y = jnp.einsum('becf,efd->becd', hid, w_down) * gate[..., None].astype(h.dtype)
    return jax.vmap(lambda yb, ib: jnp.zeros((n_tok, d), y.dtype).at[ib.reshape(-1)].add(yb.reshape(-1, d)))(y, idx)


def setup_inputs(seed: int = 0) -> dict:
    key = jax.random.key(seed)
    ks = jax.random.split(key, 19)
    nrm = jax.random.normal
    d = D_MODEL
    f32 = jnp.float32
    return {
        'x': nrm(ks[0], (BATCH, SEQ, d), f32),
        'c': nrm(ks[1], (BATCH, d), f32),
        'ctx': nrm(ks[2], (BATCH, CTX_LEN, d), f32),
        'c_ctx': nrm(ks[3], (d,), f32),
        'w_mod': 0.5 * d ** -0.5 * nrm(ks[4], (DEPTH, d, 6 * d), f32),
        'b_mod': 0.01 * nrm(ks[5], (DEPTH, 6 * d), f32),
        'g_pre1': 1.0 + 0.05 * nrm(ks[6], (DEPTH, d), f32),
        'g_post1': 1.0 + 0.05 * nrm(ks[7], (DEPTH, d), f32),
        'g_pre2': 1.0 + 0.05 * nrm(ks[8], (DEPTH, d), f32),
        'g_post2': 1.0 + 0.05 * nrm(ks[9], (DEPTH, d), f32),
        'w_in': d ** -0.5 * nrm(ks[10], (DEPTH, d, IN_COLS), f32),
        'w_out': MIX_WIDTH ** -0.5 * nrm(ks[11], (DEPTH, MIX_WIDTH, d), f32),
        'na_rpb': 0.1 * nrm(ks[12], (DEPTH, N_NA_HEADS, 2 * NA_WIN_H - 1, 2 * NA_WIN_W - 1), f32),
        'hg_lb_logits': 0.5 * nrm(ks[13], (DEPTH + 1, HG_KEY_WIDTH), f32),
        'hg_norm': 1.0 + 0.05 * nrm(ks[14], (DEPTH, HG_DV), f32),
        'w_router': d ** -0.5 * nrm(ks[15], (DEPTH, d, N_EXPERTS), f32),
        'w_gate': d ** -0.5 * nrm(ks[16], (DEPTH, N_EXPERTS, d, D_EXPERT), f32),
        'w_up': d ** -0.5 * nrm(ks[17], (DEPTH, N_EXPERTS, d, D_EXPERT), f32),
        'w_down': D_EXPERT ** -0.5 * nrm(ks[18], (DEPTH, N_EXPERTS, D_EXPERT, d), f32),
    }


def reference(x, c, ctx, c_ctx, w_mod, b_mod, g_pre1, g_post1, g_pre2, g_post2, w_in, w_out,
              na_rpb, hg_lb_logits, hg_norm, w_router, w_gate, w_up, w_down):
    bsz = x.shape[0]
    silu_c = jax.nn.silu(c)
    silu_cc = jax.nn.silu(c_ctx)
    lb_all = jnp.cumsum(jax.nn.softmax(hg_lb_logits.astype(jnp.float32), axis=0), axis=0)
    for l in range(DEPTH):
        mod = silu_c @ w_mod[l] + b_mod[l]
        mod_c = silu_cc @ w_mod[l] + b_mod[l]
        sh1, sc1, gt1, sh2, sc2, gt2 = jnp.split(mod[:, None, :], 6, axis=-1)
        sh1c, sc1c, gt1c, sh2c, sc2c, gt2c = jnp.split(mod_c, 6)
        lb = lb_all[l]

        h = modulate(rms_norm(x, g_pre1[l]), sh1, sc1)
        hc = modulate(rms_norm(ctx, g_pre1[l]), sh1c, sc1c)
        qn, kn, vn, qh, zf, zb, ih, gh = split_cols(h @ w_in[l])
        qn_c, kn_c, vn_c, qh_c, zf_c, zb_c, ih_c, gh_c = split_cols(hc @ w_in[l])
        kn_ch = to_heads(kn_c, N_NA_HEADS)
        vn_ch = to_heads(vn_c, N_NA_HEADS)

        na_o = neighbourhood_attention(to_heads(qn, N_NA_HEADS), to_heads(kn, N_NA_HEADS),
                                       to_heads(vn, N_NA_HEADS), kn_ch, vn_ch, na_rpb[l])

        s0 = jnp.zeros((bsz, N_HG_HEADS, HG_DK, HG_DV), jnp.float32)
        o_hc, s_cf, s_cb = hgrn2_bidir(to_heads(qh_c, N_HG_HEADS), to_heads(zf_c, N_HG_HEADS),
                                       to_heads(zb_c, N_HG_HEADS), to_heads(ih_c, N_HG_HEADS), lb, s0, s0)
        o_hl, _, _ = hgrn2_bidir(to_heads(qh, N_HG_HEADS), to_heads(zf, N_HG_HEADS),
                                 to_heads(zb, N_HG_HEADS), to_heads(ih, N_HG_HEADS), lb, s_cf, s_cb)
        hg_o = hgrn2_readout(o_hl, to_heads(gh, N_HG_HEADS), hg_norm[l])

        mix = jnp.concatenate([na_o, hg_o], axis=-1) @ w_out[l]
        x = x + gt1 * rms_norm(mix, g_post1[l])

        h2 = modulate(rms_norm(x, g_pre2[l]), sh2, sc2)
        x = x + gt2 * rms_norm(ec_moe(h2, w_router[l], w_gate[l], w_up[l], w_down[l]), g_post2[l])

        if l < DEPTH - 1:
            o_cna = context_attention(to_heads(qn_c, N_NA_HEADS), kn_ch, vn_ch)
            hg_oc = hgrn2_readout(o_hc, to_heads(gh_c, N_HG_HEADS), hg_norm[l])
            mix_c = jnp.concatenate([from_heads(o_cna), hg_oc], axis=-1) @ w_out[l]
            ctx = ctx + gt1c * rms_norm(mix_c, g_post1[l])
            hc2 = modulate(rms_norm(ctx, g_pre2[l]), sh2c, sc2c)
            ctx = ctx + gt2c * rms_norm(ec_moe(hc2, w_router[l], w_gate[l], w_up[l], w_down[l]), g_post2[l])
    return x
```

```python
import functools

import jax
import jax.numpy as jnp
import numpy as np
from jax import lax
from jax.experimental import pallas as pl
from jax.experimental.pallas import tpu as pltpu

F32 = jnp.float32
BF16 = jnp.bfloat16
I32 = jnp.int32
HIGHEST = lax.Precision.HIGHEST

D_MODEL = 1024
GRID_W = 64
N_ROWS = 128
SEQ = 8192
CTX_LEN = 256
NA_HEAD_DIM = 64
N_NA_HEADS = 8
NA_WIDTH = 512
NA_WIN_H = 8
NA_WIN_W = 16
NA_QBLOCK_W = 16
NA_KSPAN_W = 32
ROPE_BASE = 10000.0
HG_DK = 128
N_HG_HEADS = 4
HG_CHUNK = 32
N_EXPERTS = 16
D_EXPERT = 2752
RMS_EPS = 1e-6
COL_GROUP = 512

LANES = 128
VMEM_LIMIT_BYTES = 58 * 1024 * 1024

IN_TM = 512
NA_ROWS_PER_STEP = 8
NA_KEYS = NA_WIN_H * GRID_W
HG_BLOCK = 256
OUT_TM = 512
ROUTE_BLK = 256
N_ROUTE_BLK = SEQ // ROUTE_BLK
SLOT_CHUNK = 128
MOE_TD = 2048
MOE_TF = 768
MOE_NF = -(-D_EXPERT // MOE_TF)
MOE_F_LAST = D_EXPERT - (MOE_NF - 1) * MOE_TF
MOE_TM = 256


def _cparams(sem):
    return pltpu.CompilerParams(dimension_semantics=sem, vmem_limit_bytes=VMEM_LIMIT_BYTES)


def _mask_bf16(mask):
    return jnp.where(mask, 1.0, 0.0).astype(BF16)


def _rms(v, g):
    return v * lax.rsqrt(jnp.mean(v * v, axis=-1, keepdims=True) + RMS_EPS) * g


def _mod_body(c_ref, w_ref, b_ref, o_ref):
    cv = c_ref[...]
    s = cv * jax.nn.sigmoid(cv)
    o_ref[...] = jnp.dot(s, w_ref[...], precision=HIGHEST, preferred_element_type=F32) + b_ref[...]


def _mod_call(c_rows, w_mod, b_mod):
    d = D_MODEL
    n = w_mod.shape[1]
    tn = 1024
    return pl.pallas_call(
        _mod_body,
        out_shape=jax.ShapeDtypeStruct((8, n), F32),
        grid=(n // tn,),
        in_specs=[pl.BlockSpec((8, d), lambda j: (0, 0)),
                  pl.BlockSpec((d, tn), lambda j: (0, j)),
                  pl.BlockSpec((1, tn), lambda j: (0, j))],
        out_specs=pl.BlockSpec((8, tn), lambda j: (0, j)),
        compiler_params=_cparams(("arbitrary",)),
        name="mod",
    )(c_rows, w_mod, b_mod)


def _rope128(pg, cos, sa, sb):
    return pg * cos + pltpu.roll(pg, LANES - 16, axis=1) * sa + pltpu.roll(pg, 16, axis=1) * sb


def _inproj_body(x_ref, sh_ref, sc_ref, g_ref, w_ref, cos_ref, sa_ref, sb_ref,
                 qrot_ref, qpl_ref, krot_ref, v_ref, qh_ref, zf_ref, zb_ref, ih_ref, gh_ref, *, rope):
    xv = x_ref[0]
    h = _rms(xv, g_ref[...]) * (1.0 + sc_ref[0]) + sh_ref[0]
    hb = h.astype(BF16)

    def group(j):
        return jnp.dot(hb, w_ref[:, j * COL_GROUP:(j + 1) * COL_GROUP], preferred_element_type=F32)

    def rotated(p):
        if not rope:
            return p
        cos, sa, sb = cos_ref[...], sa_ref[...], sb_ref[...]
        return jnp.concatenate(
            [_rope128(p[:, k * LANES:(k + 1) * LANES], cos, sa, sb) for k in range(COL_GROUP // LANES)], axis=1)

    scale = NA_HEAD_DIM ** -0.5
    p = group(0)
    qpl_ref[0] = (p * scale).astype(BF16)
    qrot_ref[0] = (rotated(p) * scale).astype(BF16)
    krot_ref[0] = rotated(group(1)).astype(BF16)
    v_ref[0] = group(2).astype(BF16)
    qh_ref[0] = group(3).astype(BF16)
    zf_ref[0] = group(4)
    zb_ref[0] = group(5)
    ih_ref[0] = group(6).astype(BF16)
    gh_ref[0] = group(7).astype(BF16)


def _inproj_call(x3, shift, scale, g_pre, w_in_b, cos, sa, sb, *, rope, tm):
    g, t, d = x3.shape
    n_out = 9
    dts = [BF16, BF16, BF16, BF16, BF16, F32, F32, BF16, BF16]
    tok = pl.BlockSpec((1, tm, d), lambda b, i: (b, i, 0))
    vec = pl.BlockSpec((1, 1, d), lambda b, i: (b, 0, 0))
    tab = pl.BlockSpec((tm, LANES), lambda b, i: (i, 0))
    outs = [pl.BlockSpec((1, tm, COL_GROUP), lambda b, i: (b, i, 0)) for _ in range(n_out)]
    return pl.pallas_call(
        functools.partial(_inproj_body, rope=rope),
        out_shape=[jax.ShapeDtypeStruct((g, t, COL_GROUP), dt) for dt in dts],
        grid=(g, t // tm),
        in_specs=[tok, vec, vec,
                  pl.BlockSpec((1, d), lambda b, i: (0, 0)),
                  pl.BlockSpec(w_in_b.shape, lambda b, i: (0, 0)),
                  tab, tab, tab],
        out_specs=outs,
        compiler_params=_cparams(("arbitrary", "arbitrary")),
        name="inproj_rope" if rope else "inproj_ctx",
    )(x3, shift, scale, g_pre, w_in_b, cos, sa, sb)


def _rope_tables(n_tok):
    half = NA_HEAD_DIM // 2
    t = jnp.arange(n_tok)
    row = (t // GRID_W).astype(F32)
    col = (t % GRID_W).astype(F32)
    lane = jnp.arange(LANES)
    d = lane % NA_HEAD_DIM
    dd = d % half
    fi = dd % (half // 2)
    inv_freq = ROPE_BASE ** (-(2.0 * fi.astype(F32)) / half)
    pos = jnp.where((d < half)[None, :], row[:, None], col[:, None])
    ang = pos * inv_freq[None, :]
    first = (dd < half // 2)[None, :]
    cos = jnp.cos(ang)
    sin = jnp.sin(ang)
    return cos, jnp.where(first, -sin, 0.0), jnp.where(first, 0.0, sin)


def _na_body(qrot_ref, qpl_ref, k_ref, v_ref, kc_ref, vc_ref, bias_ref, o_ref):
    m = pl.program_id(2)
    lane = lax.broadcasted_iota(I32, (1, LANES), 1)
    head_masks = (_mask_bf16(lane < NA_HEAD_DIM), _mask_bf16(lane >= NA_HEAD_DIM))
    kc = kc_ref[0]
    vc = vc_ref[0]
    nt = (((1,), (1,)), ((), ()))

    def row_step(i, carry):
        r = m * NA_ROWS_PER_STEP + i
        r0 = jnp.clip(r - NA_WIN_H // 2, 0, N_ROWS - NA_WIN_H)
        di = r0 - r + (NA_WIN_H - 1)
        k0 = pl.multiple_of(r0 * GRID_W, GRID_W)
        q0 = pl.multiple_of(i * GRID_W, GRID_W)
        kw = k_ref[0, pl.ds(k0, NA_KEYS), :]
        vw = v_ref[0, pl.ds(k0, NA_KEYS), :]
        q = qrot_ref[0, pl.ds(q0, GRID_W), :]
        qp = qpl_ref[0, pl.ds(q0, GRID_W), :]
        out = jnp.zeros((GRID_W, LANES), F32)
        for h in range(2):
            hm = head_masks[h]
            s = lax.dot_general(q * hm, kw, nt, preferred_element_type=F32) + bias_ref[0, h, di]
            sc = lax.dot_general(qp * hm, kc, nt, preferred_element_type=F32)
            mx = jnp.maximum(jnp.max(s, axis=-1, keepdims=True), jnp.max(sc, axis=-1, keepdims=True))
            p = jnp.exp(s - mx)
            pc = jnp.exp(sc - mx)
            den = jnp.sum(p, axis=-1, keepdims=True) + jnp.sum(pc, axis=-1, keepdims=True)
            acc = (jnp.dot(p.astype(BF16), vw * hm, preferred_element_type=F32)
                   + jnp.dot(pc.astype(BF16), vc * hm, preferred_element_type=F32))
            out = out + acc / den
        o_ref[0, pl.ds(q0, GRID_W), :] = out.astype(BF16)
        return carry

    lax.fori_loop(0, NA_ROWS_PER_STEP, row_step, 0)


def _na_bias(rpb):
    a = jnp.arange(NA_WIN_H)
    c = jnp.arange(GRID_W)
    cq = jnp.arange(GRID_W)
    di = jnp.arange(NA_WIN_H)
    win_c0 = jnp.clip(cq - NA_WIN_W // 2, 0, GRID_W - NA_WIN_W)
    in_win = (c[None, :] >= win_c0[:, None]) & (c[None, :] < win_c0[:, None] + NA_WIN_W)
    dc = jnp.clip(c[None, :] - cq[:, None] + NA_WIN_W - 1, 0, 2 * NA_WIN_W - 2)
    dr = di[:, None] + a[None, :]
    b = rpb[:, dr[:, None, :, None], dc[None, :, None, :]]
    b = jnp.where(in_win[None, None, :, None, :], b.astype(F32), -jnp.inf)
    b = b.reshape(N_NA_HEADS // 2, 2, NA_WIN_H, GRID_W, NA_KEYS)
    return b


def _na_call(qrot, qpl, krot, v, kc, vc, bias):
    bsz = qrot.shape[0]
    tq = NA_ROWS_PER_STEP * GRID_W
    qspec = pl.BlockSpec((1, tq, LANES), lambda b, p, m: (b, m, p))
    kspec = pl.BlockSpec((1, SEQ, LANES), lambda b, p, m: (b, 0, p))
    cspec = pl.BlockSpec((1, CTX_LEN, LANES), lambda b, p, m: (b, 0, p))
    return pl.pallas_call(
        _na_body,
        out_shape=jax.ShapeDtypeStruct((bsz, SEQ, NA_WIDTH), BF16),
        grid=(bsz, N_NA_HEADS // 2, N_ROWS // NA_ROWS_PER_STEP),
        in_specs=[qspec, qspec, kspec, kspec, cspec, cspec,
                  pl.BlockSpec((1, 2, NA_WIN_H, GRID_W, NA_KEYS), lambda b, p, m: (p, 0, 0, 0, 0))],
        out_specs=qspec,
        compiler_params=_cparams(("arbitrary", "arbitrary", "arbitrary")),
        name="na",
    )(qrot, qpl, krot, v, kc, vc, bias)


def _split3(v):
    hi = v.astype(BF16)
    r1 = v - hi.astype(F32)
    mid = r1.astype(BF16)
    lo = (r1 - mid.astype(F32)).astype(BF16)
    return hi, mid, lo


def _dot3(mat, parts):
    acc = jnp.dot(mat, parts[0], preferred_element_type=F32)
    acc = acc + jnp.dot(mat, parts[1], preferred_element_type=F32)
    return acc + jnp.dot(mat, parts[2], preferred_element_type=F32)


def _hg_block(z, q, v, lb, st, *, backward, want_out):
    nb = HG_BLOCK
    rr = lax.broadcasted_iota(I32, (nb, nb), 0)
    cc = lax.broadcasted_iota(I32, (nb, nb), 1)
    same = (rr // HG_CHUNK) == (cc // HG_CHUNK)
    tri = same & ((cc >= rr) if backward else (cc <= rr))
    tri_m = _mask_bf16(tri)
    same_m = _mask_bf16(same)

    sig = jax.nn.sigmoid(z)
    lf = jnp.log(lb + (1.0 - lb) * sig)
    kk = (1.0 - lb) * (1.0 - sig)
    parts = _split3(lf)
    bcum = _dot3(tri_m, parts)
    btot = _dot3(same_m, parts)
    ke = (kk * jnp.exp(btot - bcum)).astype(BF16)
    qd = None
    o = None
    if want_out:
        qd = (q.astype(F32) * jnp.exp(bcum)).astype(BF16)
        kd = (kk * jnp.exp(-bcum)).astype(BF16)
        a = lax.dot_general(qd, kd, (((1,), (1,)), ((), ())), preferred_element_type=F32)
        a = jnp.where(tri, a, 0.0).astype(BF16)
        o = jnp.dot(a, v, preferred_element_type=F32)
    dec_all = jnp.exp(btot)
    n_chunks = nb // HG_CHUNK
    order = range(n_chunks - 1, -1, -1) if backward else range(n_chunks)
    o_parts = [None] * n_chunks
    for c in order:
        sl = slice(c * HG_CHUNK, (c + 1) * HG_CHUNK)
        if want_out:
            o_parts[c] = lax.dot_general(qd[sl], st.astype(BF16), (((1,), (1,)), ((), ())),
                                         preferred_element_type=F32)
        kvt = lax.dot_general(v[sl], ke[sl], (((0,), (0,)), ((), ())), preferred_element_type=F32)
        st = st * dec_all[c * HG_CHUNK:c * HG_CHUNK + 1, :] + kvt
    if want_out:
        o = o + jnp.concatenate(o_parts, axis=0)
    return o, st


def _hgrn_body(q_ref, zf_ref, zb_ref, i_ref, g_ref, qc_ref, zfc_ref, zbc_ref, ic_ref, lb_ref, ng_ref,
               o_ref, acc_ref, stf_ref, stb_ref):
    lb = lb_ref[...]
    zero_state = jnp.zeros((HG_DK, HG_DK), F32)
    _, stf = _hg_block(zfc_ref[0], qc_ref[0], ic_ref[0], lb, zero_state, backward=False, want_out=False)
    _, stb = _hg_block(zbc_ref[0], qc_ref[0], ic_ref[0], lb, zero_state, backward=True, want_out=False)
    stf_ref[...] = stf
    stb_ref[...] = stb
    n_blk = SEQ // HG_BLOCK

    def fwd_step(n, carry):
        r0 = pl.multiple_of(n * HG_BLOCK, HG_BLOCK)
        rows = pl.ds(r0, HG_BLOCK)
        o, st = _hg_block(zf_ref[0, rows, :], q_ref[0, rows, :], i_ref[0, rows, :], lb, stf_ref[...],
                          backward=False, want_out=True)
        stf_ref[...] = st
        acc_ref[rows, :] = o
        return carry

    lax.fori_loop(0, n_blk, fwd_step, 0)

    def bwd_step(n, carry):
        r0 = pl.multiple_of((n_blk - 1 - n) * HG_BLOCK, HG_BLOCK)
        rows = pl.ds(r0, HG_BLOCK)
        o, st = _hg_block(zb_ref[0, rows, :], q_ref[0, rows, :], i_ref[0, rows, :], lb, stb_ref[...],
                          backward=True, want_out=True)
        stb_ref[...] = st
        tot = acc_ref[rows, :] + o
        y = _rms(tot, ng_ref[...]) * jax.nn.sigmoid(g_ref[0, rows, :].astype(F32))
        o_ref[0, rows, :] = y.astype(BF16)
        return carry

    lax.fori_loop(0, n_blk, bwd_step, 0)


def _hgrn_call(qh, zf, zb, ih, gh, qc, zfc, zbc, ic, lb, norm_g):
    bsz = qh.shape[0]
    seq = pl.BlockSpec((1, SEQ, HG_DK), lambda b, h: (b, 0, h))
    ctx = pl.BlockSpec((1, CTX_LEN, HG_DK), lambda b, h: (b, 0, h))
    return pl.pallas_call(
        _hgrn_body,
        out_shape=jax.ShapeDtypeStruct((bsz, SEQ, N_HG_HEADS * HG_DK), BF16),
        grid=(bsz, N_HG_HEADS),
        in_specs=[seq, seq, seq, seq, seq, ctx, ctx, ctx, ctx,
                  pl.BlockSpec((1, HG_DK), lambda b, h: (0, h)),
                  pl.BlockSpec((1, HG_DK), lambda b, h: (0, 0))],
        out_specs=seq,
        scratch_shapes=[pltpu.VMEM((SEQ, HG_DK), F32),
                        pltpu.VMEM((HG_DK, HG_DK), F32),
                        pltpu.VMEM((HG_DK, HG_DK), F32)],
        compiler_params=_cparams(("arbitrary", "arbitrary")),
        name="hgrn",
    )(qh, zf, zb, ih, gh, qc, zfc, zbc, ic, lb, norm_g)


def _outproj_body(na_ref, hg_ref, x_ref, w_ref, gt1_ref, sh2_ref, sc2_ref, gpost_ref, gpre_ref, wr_ref,
                  x1_ref, h2_ref, lt_ref):
    mix = (jnp.dot(na_ref[0], w_ref[:NA_WIDTH, :], preferred_element_type=F32)
           + jnp.dot(hg_ref[0], w_ref[NA_WIDTH:, :], preferred_element_type=F32))
    x1 = x_ref[0] + gt1_ref[0] * _rms(mix, gpost_ref[...])
    x1_ref[0] = x1
    h2 = _rms(x1, gpre_ref[...]) * (1.0 + sc2_ref[0]) + sh2_ref[0]
    h2_ref[0] = h2.astype(BF16)
    lt_ref[0] = lax.dot_general(wr_ref[...], h2, (((1,), (1,)), ((), ())), precision=HIGHEST,
                                preferred_element_type=F32)


def _outproj_call(na_o, hg_o, x, w_out_b, gt1, sh2, sc2, g_post1, g_pre2, w_router_t):
    bsz, s, d = x.shape
    tm = OUT_TM
    half = pl.BlockSpec((1, tm, NA_WIDTH), lambda b, i: (b, i, 0))
    tok = pl.BlockSpec((1, tm, d), lambda b, i: (b, i, 0))
    vec = pl.BlockSpec((1, 1, d), lambda b, i: (b, 0, 0))
    par = pl.BlockSpec((1, d), lambda b, i: (0, 0))
    return pl.pallas_call(
        _outproj_body,
        out_shape=[jax.ShapeDtypeStruct((bsz, s, d), F32),
                   jax.ShapeDtypeStruct((bsz, s, d), BF16),
                   jax.ShapeDtypeStruct((bsz, N_EXPERTS, s), F32)],
        grid=(bsz, s // tm),
        in_specs=[half, half, tok, pl.BlockSpec((d, d), lambda b, i: (0, 0)), vec, vec, vec, par, par,
                  pl.BlockSpec((N_EXPERTS, d), lambda b, i: (0, 0))],
        out_specs=[tok, tok, pl.BlockSpec((1, N_EXPERTS, tm), lambda b, i: (b, 0, i))],
        compiler_params=_cparams(("arbitrary", "arbitrary")),
        name="outproj",
    )(na_o, hg_o, x, w_out_b, gt1, sh2, sc2, g_post1, g_pre2, w_router_t)


def _route_body(lt_ref, posm_ref, gate_ref, tab_ref, *, cap):
    l = lt_ref[0]
    mx = jnp.max(l, axis=0, keepdims=True)
    ex = jnp.exp(l - mx)
    aff = ex / jnp.sum(ex, axis=0, keepdims=True)
    capf = jnp.float32(cap)

    def count(mask):
        return jnp.sum(jnp.where(mask, 1.0, 0.0), axis=1, keepdims=True)

    def enough(v):
        return count(aff >= v) >= capf

    def bit_step(it, thr):
        cand = thr | (jnp.int32(1) << (30 - it))
        return jnp.where(enough(pltpu.bitcast(cand, F32)), cand, thr)

    thr = lax.fori_loop(0, 31, bit_step, jnp.zeros((N_EXPERTS, 1), I32))
    lo = pltpu.bitcast(thr, F32)
    hi = pltpu.bitcast(thr + 1, F32)

    def mid_step(it, lohi):
        lo, hi = lohi
        mid = 0.5 * (lo + hi)
        ok = enough(mid)
        return jnp.where(ok, mid, lo), jnp.where(ok, hi, mid)

    lo, hi = lax.fori_loop(0, 30, mid_step, (lo, hi))
    gt = aff >= hi
    eq = (aff >= lo) & jnp.logical_not(gt)
    need = capf - count(gt)

    rr = lax.broadcasted_iota(I32, (ROUTE_BLK, ROUTE_BLK), 0)
    cc = lax.broadcasted_iota(I32, (ROUTE_BLK, ROUTE_BLK), 1)
    upper = _mask_bf16(rr <= cc)
    lane = lax.broadcasted_iota(I32, (N_EXPERTS, LANES), 1)

    off_eq = jnp.zeros((N_EXPERTS, 1), F32)
    off_sel = jnp.zeros((N_EXPERTS, 1), F32)
    tab = jnp.zeros((N_EXPERTS, LANES), F32)
    for j in range(N_ROUTE_BLK):
        sl = slice(j * ROUTE_BLK, (j + 1) * ROUTE_BLK)
        eq_j = eq[:, sl]
        eq_b = _mask_bf16(eq_j)
        incl_eq = jnp.dot(eq_b, upper, preferred_element_type=F32) + off_eq
        rank_eq = incl_eq - eq_b.astype(F32)
        sel_j = gt[:, sl] | (eq_j & (rank_eq < need))
        sel_b = _mask_bf16(sel_j)
        incl_sel = jnp.dot(sel_b, upper, preferred_element_type=F32) + off_sel
        pos = incl_sel - sel_b.astype(F32)
        posm_ref[0, :, sl] = jnp.where(sel_j, pos.astype(I32), -1)
        gate_ref[0, :, sl] = jnp.where(sel_j, aff[:, sl], 0.0)
        tab = jnp.where(lane == j, off_sel, tab)
        off_eq = incl_eq[:, ROUTE_BLK - 1:ROUTE_BLK]
        off_sel = incl_sel[:, ROUTE_BLK - 1:ROUTE_BLK]
    tab = jnp.where(lane == N_ROUTE_BLK, off_sel, tab)
    tab_ref[0] = tab.astype(I32)


def _route_call(lt, cap):
    bsz = lt.shape[0]
    big = pl.BlockSpec((1, N_EXPERTS, SEQ), lambda b: (b, 0, 0))
    return pl.pallas_call(
        functools.partial(_route_body, cap=cap),
        out_shape=[jax.ShapeDtypeStruct((bsz, N_EXPERTS, SEQ), I32),
                   jax.ShapeDtypeStruct((bsz, N_EXPERTS, SEQ), F32),
                   jax.ShapeDtypeStruct((bsz, N_EXPERTS, LANES), I32)],
        grid=(bsz,),
        in_specs=[big],
        out_specs=[big, big, pl.BlockSpec((1, N_EXPERTS, LANES), lambda b: (b, 0, 0))],
        compiler_params=_cparams(("arbitrary",)),
        name="route",
    )(lt)


def _chunk_range(start, end):
    c_lo = start // SLOT_CHUNK
    n = jnp.where(end > start, (end - 1) // SLOT_CHUNK - c_lo + 1, 0)
    return c_lo, n


def _moe_body(tab_ref, posm_ref, h2_ref, wg_ref, wu_ref, wd_ref, y_ref,
              x_scr, acc_scr, wgb_scr, wub_scr, wdb_scr, *, n_disp, cap):
    e = pl.program_id(0)
    s = pl.program_id(1)
    steps_per_batch = SEQ // MOE_TD
    tiles_per_step = MOE_TD // ROUTE_BLK

    @pl.when(s == 0)
    def _():
        x_scr[...] = jnp.zeros_like(x_scr)
        acc_scr[...] = jnp.zeros_like(acc_scr)

    @pl.when(s < n_disp)
    def _():
        b = s // steps_per_batch
        tile0 = (s % steps_per_batch) * tiles_per_step
        tab_base = (b * N_EXPERTS + e) * (N_ROUTE_BLK + 1)
        slot = lax.broadcasted_iota(I32, (SLOT_CHUNK, ROUTE_BLK), 0)
        for sub in range(tiles_per_step):
            start = tab_ref[tab_base + tile0 + sub]
            end = tab_ref[tab_base + tile0 + sub + 1]
            c_lo, n = _chunk_range(start, end)
            pm = posm_ref[0, :, sub * ROUTE_BLK:(sub + 1) * ROUTE_BLK]
            ht = h2_ref[sub * ROUTE_BLK:(sub + 1) * ROUTE_BLK, :]

            def chunk_step(k, carry, pm=pm, ht=ht):
                base = (c_lo + k) * SLOT_CHUNK
                onehot = _mask_bf16(pm == slot + base)
                got = jnp.dot(onehot, ht, preferred_element_type=F32).astype(BF16)
                r0 = pl.multiple_of(b * cap + base, SLOT_CHUNK)
                x_scr[pl.ds(r0, SLOT_CHUNK), :] = x_scr[pl.ds(r0, SLOT_CHUNK), :] + got
                return carry

            lax.fori_loop(0, n, chunk_step, 0)

    def mlp(width):
        wgb_scr[:, :width] = wg_ref[0, :, :width].astype(BF16)
        wub_scr[:, :width] = wu_ref[0, :, :width].astype(BF16)
        wdb_scr[:width, :] = wd_ref[0, :width, :].astype(BF16)

        def m_step(mi, carry):
            r0 = pl.multiple_of(mi * MOE_TM, MOE_TM)
            xm = x_scr[pl.ds(r0, MOE_TM), :]
            g = jnp.dot(xm, wgb_scr[:, :width], preferred_element_type=F32)
            u = jnp.dot(xm, wub_scr[:, :width], preferred_element_type=F32)
            hid = (g * jax.nn.sigmoid(g) * u).astype(BF16)
            acc_scr[pl.ds(r0, MOE_TM), :] = acc_scr[pl.ds(r0, MOE_TM), :] + jnp.dot(
                hid, wdb_scr[:width, :], preferred_element_type=F32)
            return carry

        lax.fori_loop(0, x_scr.shape[0] // MOE_TM, m_step, 0)

    @pl.when((s >= n_disp) & (s < n_disp + MOE_NF - 1))
    def _():
        mlp(MOE_TF)

    @pl.when(s == n_disp + MOE_NF - 1)
    def _():
        mlp(MOE_F_LAST)
        y_ref[0] = acc_scr[...].astype(BF16)


def _moe_call(tab_flat, posm_rows, h2_flat, w_gate, w_up, w_down, *, bsz, cap):
    n_tok = h2_flat.shape[0]
    d = D_MODEL
    n_disp = n_tok // MOE_TD
    steps_per_batch = SEQ // MOE_TD
    n_steps = n_disp + MOE_NF
    slots = bsz * cap

    def fidx(s):
        return jnp.maximum(s - n_disp, 0)

    def didx(s):
        return jnp.minimum(s, n_disp - 1)

    grid_spec = pltpu.PrefetchScalarGridSpec(
        num_scalar_prefetch=1,
        grid=(N_EXPERTS, n_steps),
        in_specs=[
            pl.BlockSpec((1, 1, MOE_TD),
                         lambda e, s, tab: ((didx(s) // steps_per_batch) * N_EXPERTS + e, 0,
                                            didx(s) % steps_per_batch)),
            pl.BlockSpec((MOE_TD, d), lambda e, s, tab: (didx(s), 0)),
            pl.BlockSpec((1, d, MOE_TF), lambda e, s, tab: (e, 0, fidx(s))),
            pl.BlockSpec((1, d, MOE_TF), lambda e, s, tab: (e, 0, fidx(s))),
            pl.BlockSpec((1, MOE_TF, d), lambda e, s, tab: (e, fidx(s), 0)),
        ],
        out_specs=pl.BlockSpec((1, slots, d), lambda e, s, tab: (e, 0, 0)),
        scratch_shapes=[pltpu.VMEM((slots, d), BF16),
                        pltpu.VMEM((slots, d), F32),
                        pltpu.VMEM((d, MOE_TF), BF16),
                        pltpu.VMEM((d, MOE_TF), BF16),
                        pltpu.VMEM((MOE_TF, d), BF16)],
    )
    return pl.pallas_call(
        functools.partial(_moe_body, n_disp=n_disp, cap=cap),
        out_shape=jax.ShapeDtypeStruct((N_EXPERTS, slots, d), BF16),
        grid_spec=grid_spec,
        compiler_params=_cparams(("arbitrary", "arbitrary")),
        name="moe",
    )(tab_flat, posm_rows, h2_flat, w_gate, w_up, w_down)


def _combine_body(tab_ref, y_ref, posm_ref, gate_ref, part_ref, x1_ref, gt2_ref, gpost_ref, o_ref,
                  *, e0, n_e, cap, last):
    b = pl.program_id(0)
    j = pl.program_id(1)
    pm = posm_ref[0]
    gate = gate_ref[0]
    slot = lax.broadcasted_iota(I32, (ROUTE_BLK, SLOT_CHUNK), 1)
    total = part_ref[0]
    for k in range(n_e):
        ee = e0 + k
        tab_base = (b * N_EXPERTS + ee) * (N_ROUTE_BLK + 1)
        start = tab_ref[tab_base + j]
        end = tab_ref[tab_base + j + 1]
        c_lo, n = _chunk_range(start, end)
        pcol = pm[:, ee:ee + 1]

        def chunk_step(kk, acc, k=k, pcol=pcol):
            base = pl.multiple_of((c_lo + kk) * SLOT_CHUNK, SLOT_CHUNK)
            onehot = _mask_bf16(pcol == slot + base)
            return acc + jnp.dot(onehot, y_ref[k, pl.ds(base, SLOT_CHUNK), :], preferred_element_type=F32)

        rows = lax.fori_loop(0, n, chunk_step, jnp.zeros((ROUTE_BLK, D_MODEL), F32))
        total = total + gate[:, ee:ee + 1] * rows
    if last:
        o_ref[0] = x1_ref[0] + gt2_ref[0] * _rms(total, gpost_ref[...])
    else:
        o_ref[0] = total


def _combine_call(tab_flat, y, posm_tm, gate_tm, part, x1, gt2, g_post2, *, e0, n_e, cap, last):
    bsz, s, d = x1.shape
    tok = pl.BlockSpec((1, ROUTE_BLK, d), lambda b, j, tab: (b, j, 0))
    sel = pl.BlockSpec((1, ROUTE_BLK, N_EXPERTS), lambda b, j, tab: (b, j, 0))
    grid_spec = pltpu.PrefetchScalarGridSpec(
        num_scalar_prefetch=1,
        grid=(bsz, s // ROUTE_BLK),
        in_specs=[pl.BlockSpec((n_e, cap, d), lambda b, j, tab: (e0 // n_e, b, 0)),
                  sel, sel, tok, tok,
                  pl.BlockSpec((1, 1, d), lambda b, j, tab: (b, 0, 0)),
                  pl.BlockSpec((1, d), lambda b, j, tab: (0, 0))],
        out_specs=tok,
    )
    return pl.pallas_call(
        functools.partial(_combine_body, e0=e0, n_e=n_e, cap=cap, last=last),
        out_shape=jax.ShapeDtypeStruct((bsz, s, d), F32),
        grid_spec=grid_spec,
        compiler_params=_cparams(("arbitrary", "arbitrary")),
        name="combine_last" if last else "combine_first",
    )(tab_flat, y, posm_tm, gate_tm, part, x1, gt2, g_post2)


def kernel(x, c, ctx, c_ctx, w_mod, b_mod, g_pre1, g_post1, g_pre2, g_post2, w_in, w_out, na_rpb,
           hg_lb_logits, hg_norm, w_router, w_gate, w_up, w_down):
    bsz, seq, d = x.shape
    assert (seq, d) == (SEQ, D_MODEL) and ctx.shape[1] == CTX_LEN and w_mod.shape[0] == 1
    cap = 2 * seq // N_EXPERTS
    assert cap % SLOT_CHUNK == 0

    c_rows = jnp.zeros((8, d), F32).at[:bsz].set(c).at[bsz].set(c_ctx)
    mod = _mod_call(c_rows, w_mod[0], b_mod[0][None, :])
    sh1, sc1, gt1, sh2, sc2, gt2 = [m[:bsz, None, :] for m in jnp.split(mod, 6, axis=1)]
    sh1c, sc1c = mod[bsz:bsz + 1, None, :d], mod[bsz:bsz + 1, None, d:2 * d]

    lb_all = jnp.cumsum(jax.nn.softmax(hg_lb_logits.astype(F32), axis=0), axis=0)
    lb = lb_all[0][None, :]

    w_in_b = w_in[0].astype(BF16)
    w_out_b = w_out[0].astype(BF16)
    cos, sa, sb = _rope_tables(seq)

    qrot, qpl, krot, v, qh, zf, zb, ih, gh = _inproj_call(
        x, sh1, sc1, g_pre1[0][None, :], w_in_b, cos, sa, sb, rope=True, tm=IN_TM)
    ctx_flat = ctx.reshape(1, bsz * CTX_LEN, d)
    ctx_out = _inproj_call(ctx_flat, sh1c, sc1c, g_pre1[0][None, :], w_in_b,
                           cos[:bsz * CTX_LEN], sa[:bsz * CTX_LEN], sb[:bsz * CTX_LEN], rope=False, tm=CTX_LEN)
    _, _, kc, vc, qc, zfc, zbc, ic, _ = [a.reshape(bsz, CTX_LEN, COL_GROUP) for a in ctx_out]

    na_o = _na_call(qrot, qpl, krot, v, kc, vc, _na_bias(na_rpb[0]))
    hg_o = _hgrn_call(qh, zf, zb, ih, gh, qc, zfc, zbc, ic, lb, hg_norm[0][None, :])

    x1, h2, lt = _outproj_call(na_o, hg_o, x, w_out_b, gt1, sh2, sc2, g_post1[0][None, :],
                               g_pre2[0][None, :], w_router[0].T)

    posm, gate, tab = _route_call(lt, cap)
    tab_flat = tab[:, :, :N_ROUTE_BLK + 1].reshape(-1)
    y = _moe_call(tab_flat, posm.reshape(bsz * N_EXPERTS, 1, seq), h2.reshape(bsz * seq, d),
                  w_gate[0], w_up[0], w_down[0], bsz=bsz, cap=cap)

    posm_tm = jnp.swapaxes(posm, 1, 2)
    gate_tm = jnp.swapaxes(gate, 1, 2)
    half = N_EXPERTS // 2
    part = _combine_call(tab_flat, y, posm_tm, gate_tm, jnp.zeros_like(x1), x1, gt2, g_post2[0][None, :],
                         e0=0, n_e=half, cap=cap, last=False)
    return _combine_call(tab_flat, y, posm_tm, gate_tm, part, x1, gt2, g_post2[0][None, :],
                         e0=half, n_e=half, cap=cap, last=True)
```

```python
import functools

import jax
import jax.numpy as jnp
import numpy as np
from jax import lax
from jax.experimental import pallas as pl
from jax.experimental.pallas import tpu as pltpu

F32 = jnp.float32
BF16 = jnp.bfloat16
I32 = jnp.int32
HIGHEST = lax.Precision.HIGHEST

D_MODEL = 1024
GRID_W = 64
N_ROWS = 128
SEQ = 8192
CTX_LEN = 256
NA_HEAD_DIM = 64
N_NA_HEADS = 8
NA_WIDTH = 512
NA_WIN_H = 8
NA_WIN_W = 16
NA_QBLOCK_W = 16
NA_KSPAN_W = 32
ROPE_BASE = 10000.0
HG_DK = 128
N_HG_HEADS = 4
HG_CHUNK = 32
N_EXPERTS = 16
D_EXPERT = 2752
RMS_EPS = 1e-6
COL_GROUP = 512

LANES = 128
VMEM_LIMIT_BYTES = 58 * 1024 * 1024

IN_TM = 512
NA_ROWS_PER_STEP = 8
NA_KEYS = NA_WIN_H * GRID_W
HG_BLOCK = 256
OUT_TM = 512
ROUTE_BLK = 256
N_ROUTE_BLK = SEQ // ROUTE_BLK
SLOT_CHUNK = 128
MOE_TD = 2048
MOE_TF = 768
MOE_NF = -(-D_EXPERT // MOE_TF)
MOE_F_LAST = D_EXPERT - (MOE_NF - 1) * MOE_TF
MOE_TM = 256


def _cparams(sem):
    return pltpu.CompilerParams(dimension_semantics=sem, vmem_limit_bytes=VMEM_LIMIT_BYTES)


def _mask_bf16(mask):
    return jnp.where(mask, 1.0, 0.0).astype(BF16)


def _rms(v, g):
    return v * lax.rsqrt(jnp.mean(v * v, axis=-1, keepdims=True) + RMS_EPS) * g


def _mod_body(c_ref, w_ref, b_ref, o_ref):
    cv = c_ref[...]
    s = cv * jax.nn.sigmoid(cv)
    o_ref[...] = jnp.dot(s, w_ref[...], precision=HIGHEST, preferred_element_type=F32) + b_ref[...]


def _mod_call(c_rows, w_mod, b_mod):
    d = D_MODEL
    n = w_mod.shape[1]
    tn = 1024
    return pl.pallas_call(
        _mod_body,
        out_shape=jax.ShapeDtypeStruct((8, n), F32),
        grid=(n // tn,),
        in_specs=[pl.BlockSpec((8, d), lambda j: (0, 0)),
                  pl.BlockSpec((d, tn), lambda j: (0, j)),
                  pl.BlockSpec((1, tn), lambda j: (0, j))],
        out_specs=pl.BlockSpec((8, tn), lambda j: (0, j)),
        compiler_params=_cparams(("arbitrary",)),
        name="mod",
    )(c_rows, w_mod, b_mod)


def _rope128(pg, cos, sa, sb):
    return pg * cos + pltpu.roll(pg, LANES - 16, axis=1) * sa + pltpu.roll(pg, 16, axis=1) * sb


def _inproj_body(x_ref, sh_ref, sc_ref, g_ref, w_ref, cos_ref, sa_ref, sb_ref,
                 qrot_ref, qpl_ref, krot_ref, v_ref, qh_ref, zf_ref, zb_ref, ih_ref, gh_ref, *, rope):
    xv = x_ref[0]
    h = _rms(xv, g_ref[...]) * (1.0 + sc_ref[0]) + sh_ref[0]
    hb = h.astype(BF16)

    def group(j):
        return jnp.dot(hb, w_ref[:, j * COL_GROUP:(j + 1) * COL_GROUP], preferred_element_type=F32)

    def rotated(p):
        if not rope:
            return p
        cos, sa, sb = cos_ref[...], sa_ref[...], sb_ref[...]
        return jnp.concatenate(
            [_rope128(p[:, k * LANES:(k + 1) * LANES], cos, sa, sb) for k in range(COL_GROUP // LANES)], axis=1)

    scale = NA_HEAD_DIM ** -0.5
    p = group(0)
    qpl_ref[0] = (p * scale).astype(BF16)
    qrot_ref[0] = (rotated(p) * scale).astype(BF16)
    krot_ref[0] = rotated(group(1)).astype(BF16)
    v_ref[0] = group(2).astype(BF16)
    qh_ref[0] = group(3).astype(BF16)
    zf_ref[0] = group(4)
    zb_ref[0] = group(5)
    ih_ref[0] = group(6).astype(BF16)
    gh_ref[0] = group(7).astype(BF16)


def _inproj_call(x3, shift, scale, g_pre, w_in_b, cos, sa, sb, *, rope, tm):
    g, t, d = x3.shape
    n_out = 9
    dts = [BF16, BF16, BF16, BF16, BF16, F32, F32, BF16, BF16]
    tok = pl.BlockSpec((1, tm, d), lambda b, i: (b, i, 0))
    vec = pl.BlockSpec((1, 1, d), lambda b, i: (b, 0, 0))
    tab = pl.BlockSpec((tm, LANES), lambda b, i: (i, 0))
    outs = [pl.BlockSpec((1, tm, COL_GROUP), lambda b, i: (b, i, 0)) for _ in range(n_out)]
    return pl.pallas_call(
        functools.partial(_inproj_body, rope=rope),
        out_shape=[jax.ShapeDtypeStruct((g, t, COL_GROUP), dt) for dt in dts],
        grid=(g, t // tm),
        in_specs=[tok, vec, vec,
                  pl.BlockSpec((1, d), lambda b, i: (0, 0)),
                  pl.BlockSpec(w_in_b.shape, lambda b, i: (0, 0)),
                  tab, tab, tab],
        out_specs=outs,
        compiler_params=_cparams(("arbitrary", "arbitrary")),
        name="inproj_rope" if rope else "inproj_ctx",
    )(x3, shift, scale, g_pre, w_in_b, cos, sa, sb)


def _rope_tables(n_tok):
    half = NA_HEAD_DIM // 2
    t = jnp.arange(n_tok)
    row = (t // GRID_W).astype(F32)
    col = (t % GRID_W).astype(F32)
    lane = jnp.arange(LANES)
    d = lane % NA_HEAD_DIM
    dd = d % half
    fi = dd % (half // 2)
    inv_freq = ROPE_BASE ** (-(2.0 * fi.astype(F32)) / half)
    pos = jnp.where((d < half)[None, :], row[:, None], col[:, None])
    ang = pos * inv_freq[None, :]
    first = (dd < half // 2)[None, :]
    cos = jnp.cos(ang)
    sin = jnp.sin(ang)
    return cos, jnp.where(first, -sin, 0.0), jnp.where(first, 0.0, sin)


def _na_body(qrot_ref, qpl_ref, k_ref, v_ref, kc_ref, vc_ref, bias_ref, o_ref):
    m = pl.program_id(2)
    lane = lax.broadcasted_iota(I32, (1, LANES), 1)
    head_masks = (_mask_bf16(lane < NA_HEAD_DIM), _mask_bf16(lane >= NA_HEAD_DIM))
    kc = kc_ref[0]
    vc = vc_ref[0]
    nt = (((1,), (1,)), ((), ()))

    for i in range(NA_ROWS_PER_STEP):
        r = m * NA_ROWS_PER_STEP + i
        r0 = jnp.clip(r - NA_WIN_H // 2, 0, N_ROWS - NA_WIN_H)
        di = r0 - r + (NA_WIN_H - 1)
        k0 = pl.multiple_of(r0 * GRID_W, GRID_W)
        q0 = i * GRID_W
        kw = k_ref[0, pl.ds(k0, NA_KEYS), :]
        vw = v_ref[0, pl.ds(k0, NA_KEYS), :]
        q = qrot_ref[0, pl.ds(q0, GRID_W), :]
        qp = qpl_ref[0, pl.ds(q0, GRID_W), :]
        out = jnp.zeros((GRID_W, LANES), F32)
        for h in range(2):
            hm = head_masks[h]
            s = lax.dot_general(q * hm, kw, nt, preferred_element_type=F32) + bias_ref[0, h, di]
            sc = lax.dot_general(qp * hm, kc, nt, preferred_element_type=F32)
            mx = jnp.maximum(jnp.max(s, axis=-1, keepdims=True), jnp.max(sc, axis=-1, keepdims=True))
            p = jnp.exp(s - mx)
            pc = jnp.exp(sc - mx)
            den = jnp.sum(p, axis=-1, keepdims=True) + jnp.sum(pc, axis=-1, keepdims=True)
            acc = (jnp.dot(p.astype(BF16), vw * hm, preferred_element_type=F32)
                   + jnp.dot(pc.astype(BF16), vc * hm, preferred_element_type=F32))
            out = out + acc / den
        o_ref[0, pl.ds(q0, GRID_W), :] = out.astype(BF16)


def _na_bias(rpb):
    c = np.arange(GRID_W)
    cq = np.arange(GRID_W)
    win_c0 = np.clip(cq - NA_WIN_W // 2, 0, GRID_W - NA_WIN_W)
    in_win = (c[None, :] >= win_c0[:, None]) & (c[None, :] < win_c0[:, None] + NA_WIN_W)
    dc = np.clip(c[None, :] - cq[:, None] + NA_WIN_W - 1, 0, 2 * NA_WIN_W - 2)
    pick = (dc[None, :, :] == np.arange(2 * NA_WIN_W - 1)[:, None, None]).astype(np.float32)
    t = jnp.einsum('hrj,jqc->hqrc', rpb.astype(F32), jnp.asarray(pick), precision=HIGHEST)
    t = jnp.where(jnp.asarray(in_win)[None, :, None, :], t, -jnp.inf)
    n_dr = 2 * NA_WIN_H - 1
    t = t.reshape(N_NA_HEADS, GRID_W, n_dr * GRID_W)
    b = jnp.stack([t[:, :, di * GRID_W:di * GRID_W + NA_KEYS] for di in range(NA_WIN_H)], axis=1)
    return b.reshape(N_NA_HEADS // 2, 2, NA_WIN_H, GRID_W, NA_KEYS)


def _na_call(qrot, qpl, krot, v, kc, vc, bias):
    bsz = qrot.shape[0]
    tq = NA_ROWS_PER_STEP * GRID_W
    qspec = pl.BlockSpec((1, tq, LANES), lambda b, p, m: (b, m, p))
    kspec = pl.BlockSpec((1, SEQ, LANES), lambda b, p, m: (b, 0, p))
    cspec = pl.BlockSpec((1, CTX_LEN, LANES), lambda b, p, m: (b, 0, p))
    return pl.pallas_call(
        _na_body,
        out_shape=jax.ShapeDtypeStruct((bsz, SEQ, NA_WIDTH), BF16),
        grid=(bsz, N_NA_HEADS // 2, N_ROWS // NA_ROWS_PER_STEP),
        in_specs=[qspec, qspec, kspec, kspec, cspec, cspec,
                  pl.BlockSpec((1, 2, NA_WIN_H, GRID_W, NA_KEYS), lambda b, p, m: (p, 0, 0, 0, 0))],
        out_specs=qspec,
        compiler_params=_cparams(("arbitrary", "arbitrary", "arbitrary")),
        name="na",
    )(qrot, qpl, krot, v, kc, vc, bias)


def _split3(v):
    hi = v.astype(BF16)
    r1 = v - hi.astype(F32)
    mid = r1.astype(BF16)
    lo = (r1 - mid.astype(F32)).astype(BF16)
    return hi, mid, lo


def _dot3(mat, parts):
    acc = jnp.dot(mat, parts[0], preferred_element_type=F32)
    acc = acc + jnp.dot(mat, parts[1], preferred_element_type=F32)
    return acc + jnp.dot(mat, parts[2], preferred_element_type=F32)


def _hg_block(z, q, v, lb, st, *, backward, want_out):
    nb = HG_BLOCK
    rr = lax.broadcasted_iota(I32, (nb, nb), 0)
    cc = lax.broadcasted_iota(I32, (nb, nb), 1)
    same = (rr // HG_CHUNK) == (cc // HG_CHUNK)
    tri = same & ((cc >= rr) if backward else (cc <= rr))
    tri_m = _mask_bf16(tri)
    same_m = _mask_bf16(same)

    sig = jax.nn.sigmoid(z)
    lf = jnp.log(lb + (1.0 - lb) * sig)
    kk = (1.0 - lb) * (1.0 - sig)
    parts = _split3(lf)
    bcum = _dot3(tri_m, parts)
    btot = _dot3(same_m, parts)
    ke = (kk * jnp.exp(btot - bcum)).astype(BF16)
    qd = None
    o = None
    if want_out:
        qd = (q.astype(F32) * jnp.exp(bcum)).astype(BF16)
        kd = (kk * jnp.exp(-bcum)).astype(BF16)
        a = lax.dot_general(qd, kd, (((1,), (1,)), ((), ())), preferred_element_type=F32)
        a = jnp.where(tri, a, 0.0).astype(BF16)
        o = jnp.dot(a, v, preferred_element_type=F32)
    dec_all = jnp.exp(btot)
    n_chunks = nb // HG_CHUNK
    order = range(n_chunks - 1, -1, -1) if backward else range(n_chunks)
    o_parts = [None] * n_chunks
    for c in order:
        sl = slice(c * HG_CHUNK, (c + 1) * HG_CHUNK)
        if want_out:
            o_parts[c] = lax.dot_general(qd[sl], st.astype(BF16), (((1,), (1,)), ((), ())),
                                         preferred_element_type=F32)
        kvt = lax.dot_general(v[sl], ke[sl], (((0,), (0,)), ((), ())), preferred_element_type=F32)
        st = st * dec_all[c * HG_CHUNK:c * HG_CHUNK + 1, :] + kvt
    if want_out:
        o = o + jnp.concatenate(o_parts, axis=0)
    return o, st


def _hgrn_body(q_ref, zf_ref, zb_ref, i_ref, g_ref, qc_ref, zfc_ref, zbc_ref, ic_ref, lb_ref, ng_ref,
               o_ref, accf_ref, accb_ref, stf_ref, stb_ref):
    lb = lb_ref[...]
    zero_state = jnp.zeros((HG_DK, HG_DK), F32)
    _, stf = _hg_block(zfc_ref[0], qc_ref[0], ic_ref[0], lb, zero_state, backward=False, want_out=False)
    _, stb = _hg_block(zbc_ref[0], qc_ref[0], ic_ref[0], lb, zero_state, backward=True, want_out=False)
    stf_ref[...] = stf
    stb_ref[...] = stb
    n_blk = SEQ // HG_BLOCK

    def scan_step(n, carry):
        rows_f = pl.ds(pl.multiple_of(n * HG_BLOCK, HG_BLOCK), HG_BLOCK)
        rows_b = pl.ds(pl.multiple_of((n_blk - 1 - n) * HG_BLOCK, HG_BLOCK), HG_BLOCK)
        o_f, st_f = _hg_block(zf_ref[0, rows_f, :], q_ref[0, rows_f, :], i_ref[0, rows_f, :], lb, stf_ref[...],
                              backward=False, want_out=True)
        o_b, st_b = _hg_block(zb_ref[0, rows_b, :], q_ref[0, rows_b, :], i_ref[0, rows_b, :], lb, stb_ref[...],
                              backward=True, want_out=True)
        stf_ref[...] = st_f
        stb_ref[...] = st_b
        accf_ref[rows_f, :] = o_f
        accb_ref[rows_b, :] = o_b
        return carry

    lax.fori_loop(0, n_blk, scan_step, 0, unroll=2)

    def readout_step(n, carry):
        rows = pl.ds(pl.multiple_of(n * HG_BLOCK, HG_BLOCK), HG_BLOCK)
        tot = accf_ref[rows, :] + accb_ref[rows, :]
        y = _rms(tot, ng_ref[...]) * jax.nn.sigmoid(g_ref[0, rows, :].astype(F32))
        o_ref[0, rows, :] = y.astype(BF16)
        return carry

    lax.fori_loop(0, n_blk, readout_step, 0, unroll=2)


def _hgrn_call(qh, zf, zb, ih, gh, qc, zfc, zbc, ic, lb, norm_g):
    bsz = qh.shape[0]
    seq = pl.BlockSpec((1, SEQ, HG_DK), lambda b, h: (b, 0, h))
    ctx = pl.BlockSpec((1, CTX_LEN, HG_DK), lambda b, h: (b, 0, h))
    return pl.pallas_call(
        _hgrn_body,
        out_shape=jax.ShapeDtypeStruct((bsz, SEQ, N_HG_HEADS * HG_DK), BF16),
        grid=(bsz, N_HG_HEADS),
        in_specs=[seq, seq, seq, seq, seq, ctx, ctx, ctx, ctx,
                  pl.BlockSpec((1, HG_DK), lambda b, h: (0, h)),
                  pl.BlockSpec((1, HG_DK), lambda b, h: (0, 0))],
        out_specs=seq,
        scratch_shapes=[pltpu.VMEM((SEQ, HG_DK), F32),
                        pltpu.VMEM((SEQ, HG_DK), F32),
                        pltpu.VMEM((HG_DK, HG_DK), F32),
                        pltpu.VMEM((HG_DK, HG_DK), F32)],
        compiler_params=_cparams(("arbitrary", "arbitrary")),
        name="hgrn",
    )(qh, zf, zb, ih, gh, qc, zfc, zbc, ic, lb, norm_g)


def _outproj_body(na_ref, hg_ref, x_ref, w_ref, gt1_ref, sh2_ref, sc2_ref, gpost_ref, gpre_ref, wr_ref,
                  x1_ref, h2_ref, lt_ref):
    mix = (jnp.dot(na_ref[0], w_ref[:NA_WIDTH, :], preferred_element_type=F32)
           + jnp.dot(hg_ref[0], w_ref[NA_WIDTH:, :], preferred_element_type=F32))
    x1 = x_ref[0] + gt1_ref[0] * _rms(mix, gpost_ref[...])
    x1_ref[0] = x1
    h2 = _rms(x1, gpre_ref[...]) * (1.0 + sc2_ref[0]) + sh2_ref[0]
    h2_ref[0] = h2.astype(BF16)
    lt_ref[0] = lax.dot_general(wr_ref[...], h2, (((1,), (1,)), ((), ())), precision=HIGHEST,
                                preferred_element_type=F32)


def _outproj_call(na_o, hg_o, x, w_out_b, gt1, sh2, sc2, g_post1, g_pre2, w_router_t):
    bsz, s, d = x.shape
    tm = OUT_TM
    half = pl.BlockSpec((1, tm, NA_WIDTH), lambda b, i: (b, i, 0))
    tok = pl.BlockSpec((1, tm, d), lambda b, i: (b, i, 0))
    vec = pl.BlockSpec((1, 1, d), lambda b, i: (b, 0, 0))
    par = pl.BlockSpec((1, d), lambda b, i: (0, 0))
    return pl.pallas_call(
        _outproj_body,
        out_shape=[jax.ShapeDtypeStruct((bsz, s, d), F32),
                   jax.ShapeDtypeStruct((bsz, s, d), BF16),
                   jax.ShapeDtypeStruct((bsz, N_EXPERTS, s), F32)],
        grid=(bsz, s // tm),
        in_specs=[half, half, tok, pl.BlockSpec((d, d), lambda b, i: (0, 0)), vec, vec, vec, par, par,
                  pl.BlockSpec((N_EXPERTS, d), lambda b, i: (0, 0))],
        out_specs=[tok, tok, pl.BlockSpec((1, N_EXPERTS, tm), lambda b, i: (b, 0, i))],
        compiler_params=_cparams(("arbitrary", "arbitrary")),
        name="outproj",
    )(na_o, hg_o, x, w_out_b, gt1, sh2, sc2, g_post1, g_pre2, w_router_t)


def _route_body(lt_ref, posm_ref, gate_ref, tab_ref, posm_tm_ref, gate_tm_ref, *, cap):
    l = lt_ref[0]
    mx = jnp.max(l, axis=0, keepdims=True)
    ex = jnp.exp(l - mx)
    aff = ex / jnp.sum(ex, axis=0, keepdims=True)
    capf = jnp.float32(cap)

    def count(mask):
        return jnp.sum(jnp.where(mask, 1.0, 0.0), axis=1, keepdims=True)

    def enough(v):
        return count(aff >= v) >= capf

    def bit_step(it, thr):
        cand = thr | (jnp.int32(1) << (30 - it))
        return jnp.where(enough(pltpu.bitcast(cand, F32)), cand, thr)

    thr = lax.fori_loop(0, 31, bit_step, jnp.zeros((N_EXPERTS, 1), I32))
    lo = pltpu.bitcast(thr, F32)
    hi = pltpu.bitcast(thr + 1, F32)

    def mid_step(it, lohi):
        lo, hi = lohi
        mid = 0.5 * (lo + hi)
        ok = enough(mid)
        return jnp.where(ok, mid, lo), jnp.where(ok, hi, mid)

    lo, hi = lax.fori_loop(0, 30, mid_step, (lo, hi))
    gt = aff >= hi
    eq = (aff >= lo) & jnp.logical_not(gt)
    need = capf - count(gt)

    rr = lax.broadcasted_iota(I32, (ROUTE_BLK, ROUTE_BLK), 0)
    cc = lax.broadcasted_iota(I32, (ROUTE_BLK, ROUTE_BLK), 1)
    upper = _mask_bf16(rr <= cc)
    lane = lax.broadcasted_iota(I32, (N_EXPERTS, LANES), 1)

    off_eq = jnp.zeros((N_EXPERTS, 1), F32)
    off_sel = jnp.zeros((N_EXPERTS, 1), F32)
    tab = jnp.zeros((N_EXPERTS, LANES), F32)
    for j in range(N_ROUTE_BLK):
        sl = slice(j * ROUTE_BLK, (j + 1) * ROUTE_BLK)
        eq_j = eq[:, sl]
        eq_b = _mask_bf16(eq_j)
        incl_eq = jnp.dot(eq_b, upper, preferred_element_type=F32) + off_eq
        rank_eq = incl_eq - eq_b.astype(F32)
        sel_j = gt[:, sl] | (eq_j & (rank_eq < need))
        sel_b = _mask_bf16(sel_j)
        incl_sel = jnp.dot(sel_b, upper, preferred_element_type=F32) + off_sel
        pos = incl_sel - sel_b.astype(F32)
        posm_ref[0, :, sl] = jnp.where(sel_j, pos.astype(I32), -1)
        gate_ref[0, :, sl] = jnp.where(sel_j, aff[:, sl], 0.0)
        tab = jnp.where(lane == j, off_sel, tab)
        off_eq = incl_eq[:, ROUTE_BLK - 1:ROUTE_BLK]
        off_sel = incl_sel[:, ROUTE_BLK - 1:ROUTE_BLK]
    tab = jnp.where(lane == N_ROUTE_BLK, off_sel, tab)
    tab_ref[0] = tab.astype(I32)
    posm_tm_ref[0] = posm_ref[0].T
    gate_tm_ref[0] = gate_ref[0].T


def _route_call(lt, cap):
    bsz = lt.shape[0]
    big = pl.BlockSpec((1, N_EXPERTS, SEQ), lambda b: (b, 0, 0))
    big_tm = pl.BlockSpec((1, SEQ, N_EXPERTS), lambda b: (b, 0, 0))
    return pl.pallas_call(
        functools.partial(_route_body, cap=cap),
        out_shape=[jax.ShapeDtypeStruct((bsz, N_EXPERTS, SEQ), I32),
                   jax.ShapeDtypeStruct((bsz, N_EXPERTS, SEQ), F32),
                   jax.ShapeDtypeStruct((bsz, N_EXPERTS, LANES), I32),
                   jax.ShapeDtypeStruct((bsz, SEQ, N_EXPERTS), I32),
                   jax.ShapeDtypeStruct((bsz, SEQ, N_EXPERTS), F32)],
        grid=(bsz,),
        in_specs=[big],
        out_specs=[big, big, pl.BlockSpec((1, N_EXPERTS, LANES), lambda b: (b, 0, 0)), big_tm, big_tm],
        compiler_params=_cparams(("arbitrary",)),
        name="route",
    )(lt)


def _chunk_range(start, end):
    c_lo = start // SLOT_CHUNK
    n = jnp.where(end > start, (end - 1) // SLOT_CHUNK - c_lo + 1, 0)
    return c_lo, n


def _moe_body(tab_ref, posm_ref, h2_ref, wg_ref, wu_ref, wd_ref, y_ref,
              x_scr, acc_scr, wgb_scr, wub_scr, wdb_scr, *, n_disp, cap):
    e = pl.program_id(0)
    s = pl.program_id(1)
    steps_per_batch = SEQ // MOE_TD
    tiles_per_step = MOE_TD // ROUTE_BLK

    @pl.when(s == 0)
    def _():
        x_scr[...] = jnp.zeros_like(x_scr)
        acc_scr[...] = jnp.zeros_like(acc_scr)

    @pl.when(s < n_disp)
    def _():
        b = s // steps_per_batch
        tile0 = (s % steps_per_batch) * tiles_per_step
        tab_base = (b * N_EXPERTS + e) * (N_ROUTE_BLK + 1)
        slot = lax.broadcasted_iota(I32, (SLOT_CHUNK, ROUTE_BLK), 0)
        for sub in range(tiles_per_step):
            start = tab_ref[tab_base + tile0 + sub]
            end = tab_ref[tab_base + tile0 + sub + 1]
            c_lo, n = _chunk_range(start, end)
            pm = posm_ref[0, :, sub * ROUTE_BLK:(sub + 1) * ROUTE_BLK]
            ht = h2_ref[sub * ROUTE_BLK:(sub + 1) * ROUTE_BLK, :]

            def chunk_step(k, carry, pm=pm, ht=ht):
                base = (c_lo + k) * SLOT_CHUNK
                onehot = _mask_bf16(pm == slot + base)
                got = jnp.dot(onehot, ht, preferred_element_type=F32).astype(BF16)
                r0 = pl.multiple_of(b * cap + base, SLOT_CHUNK)
                x_scr[pl.ds(r0, SLOT_CHUNK), :] = x_scr[pl.ds(r0, SLOT_CHUNK), :] + got
                return carry

            lax.fori_loop(0, n, chunk_step, 0)

    def mlp(width):
        wgb_scr[:, :width] = wg_ref[0, :, :width].astype(BF16)
        wub_scr[:, :width] = wu_ref[0, :, :width].astype(BF16)
        wdb_scr[:width, :] = wd_ref[0, :width, :].astype(BF16)

        def m_step(mi, carry):
            r0 = pl.multiple_of(mi * MOE_TM, MOE_TM)
            xm = x_scr[pl.ds(r0, MOE_TM), :]
            g = jnp.dot(xm, wgb_scr[:, :width], preferred_element_type=F32)
            u = jnp.dot(xm, wub_scr[:, :width], preferred_element_type=F32)
            hid = (g * jax.nn.sigmoid(g) * u).astype(BF16)
            acc_scr[pl.ds(r0, MOE_TM), :] = acc_scr[pl.ds(r0, MOE_TM), :] + jnp.dot(
                hid, wdb_scr[:width, :], preferred_element_type=F32)
            return carry

        lax.fori_loop(0, x_scr.shape[0] // MOE_TM, m_step, 0)

    @pl.when((s >= n_disp) & (s < n_disp + MOE_NF - 1))
    def _():
        mlp(MOE_TF)

    @pl.when(s == n_disp + MOE_NF - 1)
    def _():
        mlp(MOE_F_LAST)
        y_ref[0] = acc_scr[...].astype(BF16)


def _moe_call(tab_flat, posm_rows, h2_flat, w_gate, w_up, w_down, *, bsz, cap):
    n_tok = h2_flat.shape[0]
    d = D_MODEL
    n_disp = n_tok // MOE_TD
    steps_per_batch = SEQ // MOE_TD
    n_steps = n_disp + MOE_NF
    slots = bsz * cap

    def fidx(s):
        return jnp.maximum(s - n_disp, 0)

    def didx(s):
        return jnp.minimum(s, n_disp - 1)

    grid_spec = pltpu.PrefetchScalarGridSpec(
        num_scalar_prefetch=1,
        grid=(N_EXPERTS, n_steps),
        in_specs=[
            pl.BlockSpec((1, 1, MOE_TD),
                         lambda e, s, tab: ((didx(s) // steps_per_batch) * N_EXPERTS + e, 0,
                                            didx(s) % steps_per_batch)),
            pl.BlockSpec((MOE_TD, d), lambda e, s, tab: (didx(s), 0)),
            pl.BlockSpec((1, d, MOE_TF), lambda e, s, tab: (e, 0, fidx(s))),
            pl.BlockSpec((1, d, MOE_TF), lambda e, s, tab: (e, 0, fidx(s))),
            pl.BlockSpec((1, MOE_TF, d), lambda e, s, tab: (e, fidx(s), 0)),
        ],
        out_specs=pl.BlockSpec((1, slots, d), lambda e, s, tab: (e, 0, 0)),
        scratch_shapes=[pltpu.VMEM((slots, d), BF16),
                        pltpu.VMEM((slots, d), F32),
                        pltpu.VMEM((d, MOE_TF), BF16),
                        pltpu.VMEM((d, MOE_TF), BF16),
                        pltpu.VMEM((MOE_TF, d), BF16)],
    )
    return pl.pallas_call(
        functools.partial(_moe_body, n_disp=n_disp, cap=cap),
        out_shape=jax.ShapeDtypeStruct((N_EXPERTS, slots, d), BF16),
        grid_spec=grid_spec,
        compiler_params=_cparams(("arbitrary", "arbitrary")),
        name="moe",
    )(tab_flat, posm_rows, h2_flat, w_gate, w_up, w_down)


def _combine_body(tab_ref, y_ref, posm_ref, gate_ref, *rest, e0, n_e, cap, last):
    if last:
        part_ref, x1_ref, gt2_ref, gpost_ref, o_ref = rest
    else:
        (o_ref,) = rest
    b = pl.program_id(0)
    j = pl.program_id(1)
    pm = posm_ref[0]
    gate = gate_ref[0]
    slot = lax.broadcasted_iota(I32, (ROUTE_BLK, SLOT_CHUNK), 1)
    total = part_ref[0] if last else jnp.zeros((ROUTE_BLK, D_MODEL), F32)
    for k in range(n_e):
        ee = e0 + k
        tab_base = (b * N_EXPERTS + ee) * (N_ROUTE_BLK + 1)
        start = tab_ref[tab_base + j]
        end = tab_ref[tab_base + j + 1]
        c_lo, n = _chunk_range(start, end)
        pcol = pm[:, ee:ee + 1]

        def chunk_step(kk, acc, k=k, pcol=pcol):
            base = pl.multiple_of((c_lo + kk) * SLOT_CHUNK, SLOT_CHUNK)
            onehot = _mask_bf16(pcol == slot + base)
            return acc + jnp.dot(onehot, y_ref[k, pl.ds(base, SLOT_CHUNK), :], preferred_element_type=F32)

        rows = lax.fori_loop(0, n, chunk_step, jnp.zeros((ROUTE_BLK, D_MODEL), F32))
        total = total + gate[:, ee:ee + 1] * rows
    if last:
        o_ref[0] = x1_ref[0] + gt2_ref[0] * _rms(total, gpost_ref[...])
    else:
        o_ref[0] = total


def _combine_call(tab_flat, y, posm_tm, gate_tm, tail, *, e0, n_e, cap, last):
    bsz, s, _ = posm_tm.shape
    d = D_MODEL
    tok = pl.BlockSpec((1, ROUTE_BLK, d), lambda b, j, tab: (b, j, 0))
    sel = pl.BlockSpec((1, ROUTE_BLK, N_EXPERTS), lambda b, j, tab: (b, j, 0))
    in_specs = [pl.BlockSpec((n_e, cap, d), lambda b, j, tab: (e0 // n_e, b, 0)), sel, sel]
    if last:
        in_specs += [tok, tok,
                     pl.BlockSpec((1, 1, d), lambda b, j, tab: (b, 0, 0)),
                     pl.BlockSpec((1, d), lambda b, j, tab: (0, 0))]
    grid_spec = pltpu.PrefetchScalarGridSpec(
        num_scalar_prefetch=1,
        grid=(bsz, s // ROUTE_BLK),
        in_specs=in_specs,
        out_specs=tok,
    )
    return pl.pallas_call(
        functools.partial(_combine_body, e0=e0, n_e=n_e, cap=cap, last=last),
        out_shape=jax.ShapeDtypeStruct((bsz, s, d), F32),
        grid_spec=grid_spec,
        compiler_params=_cparams(("arbitrary", "arbitrary")),
        name="combine_last" if last else "combine_first",
    )(tab_flat, y, posm_tm, gate_tm, *tail)


def kernel(x, c, ctx, c_ctx, w_mod, b_mod, g_pre1, g_post1, g_pre2, g_post2, w_in, w_out, na_rpb,
           hg_lb_logits, hg_norm, w_router, w_gate, w_up, w_down):
    bsz, seq, d = x.shape
    assert (seq, d) == (SEQ, D_MODEL) and ctx.shape[1] == CTX_LEN and w_mod.shape[0] == 1
    cap = 2 * seq // N_EXPERTS
    assert cap % SLOT_CHUNK == 0

    c_rows = jnp.zeros((8, d), F32).at[:bsz].set(c).at[bsz].set(c_ctx)
    mod = _mod_call(c_rows, w_mod[0], b_mod[0][None, :])
    sh1, sc1, gt1, sh2, sc2, gt2 = [m[:bsz, None, :] for m in jnp.split(mod, 6, axis=1)]
    sh1c, sc1c = mod[bsz:bsz + 1, None, :d], mod[bsz:bsz + 1, None, d:2 * d]

    lb_all = jnp.cumsum(jax.nn.softmax(hg_lb_logits.astype(F32), axis=0), axis=0)
    lb = lb_all[0][None, :]

    w_in_b = w_in[0].astype(BF16)
    w_out_b = w_out[0].astype(BF16)
    cos, sa, sb = _rope_tables(seq)

    qrot, qpl, krot, v, qh, zf, zb, ih, gh = _inproj_call(
        x, sh1, sc1, g_pre1[0][None, :], w_in_b, cos, sa, sb, rope=True, tm=IN_TM)
    ctx_flat = ctx.reshape(1, bsz * CTX_LEN, d)
    ctx_out = _inproj_call(ctx_flat, sh1c, sc1c, g_pre1[0][None, :], w_in_b,
                           cos[:bsz * CTX_LEN], sa[:bsz * CTX_LEN], sb[:bsz * CTX_LEN], rope=False, tm=CTX_LEN)
    _, _, kc, vc, qc, zfc, zbc, ic, _ = [a.reshape(bsz, CTX_LEN, COL_GROUP) for a in ctx_out]

    na_o = _na_call(qrot, qpl, krot, v, kc, vc, _na_bias(na_rpb[0]))
    hg_o = _hgrn_call(qh, zf, zb, ih, gh, qc, zfc, zbc, ic, lb, hg_norm[0][None, :])

    x1, h2, lt = _outproj_call(na_o, hg_o, x, w_out_b, gt1, sh2, sc2, g_post1[0][None, :],
                               g_pre2[0][None, :], w_router[0].T)

    posm, _, tab, posm_tm, gate_tm = _route_call(lt, cap)
    tab_flat = tab[:, :, :N_ROUTE_BLK + 1].reshape(-1)
    y = _moe_call(tab_flat, posm.reshape(bsz * N_EXPERTS, 1, seq), h2.reshape(bsz * seq, d),
                  w_gate[0], w_up[0], w_down[0], bsz=bsz, cap=cap)

    half = N_EXPERTS // 2
    part = _combine_call(tab_flat, y, posm_tm, gate_tm, (), e0=0, n_e=half, cap=cap, last=False)
    return _combine_call(tab_flat, y, posm_tm, gate_tm, (part, x1, gt2, g_post2[0][None, :]),
                         e0=half, n_e=half, cap=cap, last=True)
```

```python
import functools

import jax
import jax.numpy as jnp
import numpy as np
from jax import lax
from jax.experimental import pallas as pl
from jax.experimental.pallas import tpu as pltpu

F32 = jnp.float32
BF16 = jnp.bfloat16
I32 = jnp.int32
HIGHEST = lax.Precision.HIGHEST

D_MODEL = 1024
GRID_W = 64
N_ROWS = 128
SEQ = 8192
CTX_LEN = 256
NA_HEAD_DIM = 64
N_NA_HEADS = 8
NA_WIDTH = 512
NA_WIN_H = 8
NA_WIN_W = 16
NA_QBLOCK_W = 16
NA_KSPAN_W = 32
ROPE_BASE = 10000.0
HG_DK = 128
N_HG_HEADS = 4
HG_CHUNK = 32
N_EXPERTS = 16
D_EXPERT = 2752
RMS_EPS = 1e-6
COL_GROUP = 512

LANES = 128
VMEM_LIMIT_BYTES = 58 * 1024 * 1024

IN_TM = 512
NA_ROWS_PER_STEP = 8
NA_KEYS = NA_WIN_H * GRID_W
HG_BLOCK = 256
OUT_TM = 512
ROUTE_BLK = 256
N_ROUTE_BLK = SEQ // ROUTE_BLK
SLOT_CHUNK = 128
MOE_TD = 2048
MOE_TF = 768
MOE_NF = -(-D_EXPERT // MOE_TF)
MOE_F_LAST = D_EXPERT - (MOE_NF - 1) * MOE_TF
MOE_TM = 256


def _cparams(sem):
    return pltpu.CompilerParams(dimension_semantics=sem, vmem_limit_bytes=VMEM_LIMIT_BYTES)


def _mask_bf16(mask):
    return jnp.where(mask, 1.0, 0.0).astype(BF16)


def _rms(v, g):
    return v * lax.rsqrt(jnp.mean(v * v, axis=-1, keepdims=True) + RMS_EPS) * g


def _mod_body(c_ref, w_ref, b_ref, o_ref):
    cv = c_ref[...]
    s = cv * jax.nn.sigmoid(cv)
    o_ref[...] = jnp.dot(s, w_ref[...], precision=HIGHEST, preferred_element_type=F32) + b_ref[...]


def _mod_call(c_rows, w_mod, b_mod):
    d = D_MODEL
    n = w_mod.shape[1]
    tn = 1024
    return pl.pallas_call(
        _mod_body,
        out_shape=jax.ShapeDtypeStruct((8, n), F32),
        grid=(n // tn,),
        in_specs=[pl.BlockSpec((8, d), lambda j: (0, 0)),
                  pl.BlockSpec((d, tn), lambda j: (0, j)),
                  pl.BlockSpec((1, tn), lambda j: (0, j))],
        out_specs=pl.BlockSpec((8, tn), lambda j: (0, j)),
        compiler_params=_cparams(("arbitrary",)),
        name="mod",
    )(c_rows, w_mod, b_mod)


def _rope128(pg, cos, sa, sb):
    return pg * cos + pltpu.roll(pg, LANES - 16, axis=1) * sa + pltpu.roll(pg, 16, axis=1) * sb


def _inproj_body(x_ref, sh_ref, sc_ref, g_ref, w_ref, cos_ref, sa_ref, sb_ref,
                 qrot_ref, qpl_ref, krot_ref, v_ref, qh_ref, zf_ref, zb_ref, ih_ref, gh_ref, *, rope):
    xv = x_ref[0]
    h = _rms(xv, g_ref[...]) * (1.0 + sc_ref[0]) + sh_ref[0]
    hb = h.astype(BF16)

    def group(j):
        return jnp.dot(hb, w_ref[:, j * COL_GROUP:(j + 1) * COL_GROUP], preferred_element_type=F32)

    def rotated(p):
        if not rope:
            return p
        cos, sa, sb = cos_ref[...], sa_ref[...], sb_ref[...]
        return jnp.concatenate(
            [_rope128(p[:, k * LANES:(k + 1) * LANES], cos, sa, sb) for k in range(COL_GROUP // LANES)], axis=1)

    scale = NA_HEAD_DIM ** -0.5
    p = group(0)
    qpl_ref[0] = (p * scale).astype(BF16)
    qrot_ref[0] = (rotated(p) * scale).astype(BF16)
    krot_ref[0] = rotated(group(1)).astype(BF16)
    v_ref[0] = group(2).astype(BF16)
    qh_ref[0] = group(3).astype(BF16)
    zf_ref[0] = group(4)
    zb_ref[0] = group(5)
    ih_ref[0] = group(6).astype(BF16)
    gh_ref[0] = group(7).astype(BF16)


def _inproj_call(x3, shift, scale, g_pre, w_in_b, cos, sa, sb, *, rope, tm):
    g, t, d = x3.shape
    n_out = 9
    dts = [BF16, BF16, BF16, BF16, BF16, F32, F32, BF16, BF16]
    tok = pl.BlockSpec((1, tm, d), lambda b, i: (b, i, 0))
    vec = pl.BlockSpec((1, 1, d), lambda b, i: (b, 0, 0))
    tab = pl.BlockSpec((tm, LANES), lambda b, i: (i, 0))
    outs = [pl.BlockSpec((1, tm, COL_GROUP), lambda b, i: (b, i, 0)) for _ in range(n_out)]
    return pl.pallas_call(
        functools.partial(_inproj_body, rope=rope),
        out_shape=[jax.ShapeDtypeStruct((g, t, COL_GROUP), dt) for dt in dts],
        grid=(g, t // tm),
        in_specs=[tok, vec, vec,
                  pl.BlockSpec((1, d), lambda b, i: (0, 0)),
                  pl.BlockSpec(w_in_b.shape, lambda b, i: (0, 0)),
                  tab, tab, tab],
        out_specs=outs,
        compiler_params=_cparams(("arbitrary", "arbitrary")),
        name="inproj_rope" if rope else "inproj_ctx",
    )(x3, shift, scale, g_pre, w_in_b, cos, sa, sb)


def _rope_tables(n_tok):
    half = NA_HEAD_DIM // 2
    t = jnp.arange(n_tok)
    row = (t // GRID_W).astype(F32)
    col = (t % GRID_W).astype(F32)
    lane = jnp.arange(LANES)
    d = lane % NA_HEAD_DIM
    dd = d % half
    fi = dd % (half // 2)
    inv_freq = ROPE_BASE ** (-(2.0 * fi.astype(F32)) / half)
    pos = jnp.where((d < half)[None, :], row[:, None], col[:, None])
    ang = pos * inv_freq[None, :]
    first = (dd < half // 2)[None, :]
    cos = jnp.cos(ang)
    sin = jnp.sin(ang)
    return cos, jnp.where(first, -sin, 0.0), jnp.where(first, 0.0, sin)


def _na_body(qrot_ref, qpl_ref, k_ref, v_ref, kc_ref, vc_ref, bias_ref, o_ref):
    m = pl.program_id(2)
    lane = lax.broadcasted_iota(I32, (1, LANES), 1)
    first_head = lane < NA_HEAD_DIM
    hm0, hm1 = _mask_bf16(first_head), _mask_bf16(lane >= NA_HEAD_DIM)
    nt = (((1,), (1,)), ((), ()))
    tq = NA_ROWS_PER_STEP * GRID_W

    qp = qpl_ref[0]
    sc = lax.dot_general(jnp.concatenate([qp * hm0, qp * hm1], axis=0), kc_ref[0], nt,
                         preferred_element_type=F32)
    mc = jnp.max(sc, axis=-1, keepdims=True)
    pc = jnp.exp(sc - mc)
    lc = jnp.sum(pc, axis=-1, keepdims=True)
    accc = jnp.dot(pc.astype(BF16), vc_ref[0], preferred_element_type=F32)

    def both_heads(a, i):
        return jnp.concatenate([a[i * GRID_W:(i + 1) * GRID_W], a[tq + i * GRID_W:tq + (i + 1) * GRID_W]], axis=0)

    for i in range(NA_ROWS_PER_STEP):
        r = m * NA_ROWS_PER_STEP + i
        r0 = jnp.clip(r - NA_WIN_H // 2, 0, N_ROWS - NA_WIN_H)
        di = r0 - r + (NA_WIN_H - 1)
        k0 = pl.multiple_of(r0 * GRID_W, GRID_W)
        q = qrot_ref[0, i * GRID_W:(i + 1) * GRID_W, :]
        s = lax.dot_general(jnp.concatenate([q * hm0, q * hm1], axis=0), k_ref[0, pl.ds(k0, NA_KEYS), :], nt,
                            preferred_element_type=F32)
        s = s + bias_ref[0, di].reshape(2 * GRID_W, NA_KEYS)
        mw = jnp.max(s, axis=-1, keepdims=True)
        p = jnp.exp(s - mw)
        lw = jnp.sum(p, axis=-1, keepdims=True)
        accw = jnp.dot(p.astype(BF16), v_ref[0, pl.ds(k0, NA_KEYS), :], preferred_element_type=F32)
        mc_i = both_heads(mc, i)
        mm = jnp.maximum(mw, mc_i)
        aw = jnp.exp(mw - mm)
        ac = jnp.exp(mc_i - mm)
        o = (aw * accw + ac * both_heads(accc, i)) / (aw * lw + ac * both_heads(lc, i))
        o_ref[0, i * GRID_W:(i + 1) * GRID_W, :] = jnp.where(first_head, o[:GRID_W], o[GRID_W:]).astype(BF16)


def _na_bias(rpb):
    c = np.arange(GRID_W)
    cq = np.arange(GRID_W)
    win_c0 = np.clip(cq - NA_WIN_W // 2, 0, GRID_W - NA_WIN_W)
    in_win = (c[None, :] >= win_c0[:, None]) & (c[None, :] < win_c0[:, None] + NA_WIN_W)
    dc = np.clip(c[None, :] - cq[:, None] + NA_WIN_W - 1, 0, 2 * NA_WIN_W - 2)
    pick = (dc[None, :, :] == np.arange(2 * NA_WIN_W - 1)[:, None, None]).astype(np.float32)
    t = jnp.einsum('hrj,jqc->hqrc', rpb.astype(F32), jnp.asarray(pick), precision=HIGHEST)
    t = jnp.where(jnp.asarray(in_win)[None, :, None, :], t, -jnp.inf)
    n_dr = 2 * NA_WIN_H - 1
    t = t.reshape(N_NA_HEADS // 2, 2, GRID_W, n_dr * GRID_W)
    return jnp.stack([t[..., di * GRID_W:di * GRID_W + NA_KEYS] for di in range(NA_WIN_H)], axis=1)


def _na_call(qrot, qpl, krot, v, kc, vc, bias):
    bsz = qrot.shape[0]
    tq = NA_ROWS_PER_STEP * GRID_W
    qspec = pl.BlockSpec((1, tq, LANES), lambda b, p, m: (b, m, p))
    kspec = pl.BlockSpec((1, SEQ, LANES), lambda b, p, m: (b, 0, p))
    cspec = pl.BlockSpec((1, CTX_LEN, LANES), lambda b, p, m: (b, 0, p))
    return pl.pallas_call(
        _na_body,
        out_shape=jax.ShapeDtypeStruct((bsz, SEQ, NA_WIDTH), BF16),
        grid=(bsz, N_NA_HEADS // 2, N_ROWS // NA_ROWS_PER_STEP),
        in_specs=[qspec, qspec, kspec, kspec, cspec, cspec,
                  pl.BlockSpec((1, NA_WIN_H, 2, GRID_W, NA_KEYS), lambda b, p, m: (p, 0, 0, 0, 0))],
        out_specs=qspec,
        compiler_params=_cparams(("arbitrary", "arbitrary", "arbitrary")),
        name="na",
    )(qrot, qpl, krot, v, kc, vc, bias)


def _split3(v):
    hi = v.astype(BF16)
    r1 = v - hi.astype(F32)
    mid = r1.astype(BF16)
    lo = (r1 - mid.astype(F32)).astype(BF16)
    return hi, mid, lo


HG_NCHUNK = HG_BLOCK // HG_CHUNK


def _hg_consts(tri_ref, keep_ref, spread_ref):
    nb = HG_BLOCK
    rr = lax.broadcasted_iota(I32, (nb, nb), 0)
    cc = lax.broadcasted_iota(I32, (nb, nb), 1)
    same = (rr // HG_CHUNK) == (cc // HG_CHUNK)
    same_m = _mask_bf16(same)
    tri_ref[0] = jnp.concatenate([_mask_bf16(same & (cc <= rr)), same_m], axis=0)
    tri_ref[1] = jnp.concatenate([_mask_bf16(same & (cc >= rr)), same_m], axis=0)
    keep_ref[0] = jnp.where(same & (cc <= rr), 1.0, 0.0)
    keep_ref[1] = jnp.where(same & (cc >= rr), 1.0, 0.0)
    r8 = lax.broadcasted_iota(I32, (nb, HG_NCHUNK * HG_DK), 0) // HG_CHUNK
    g8 = lax.broadcasted_iota(I32, (nb, HG_NCHUNK * HG_DK), 1) // HG_DK
    spread_ref[...] = _mask_bf16(r8 == g8)


def _hg_block(z, q, v, lb, st, tri_ref, keep_ref, spread_ref, *, backward, want_out):
    nb = HG_BLOCK
    mats = tri_ref[1 if backward else 0]
    spread = spread_ref[...]

    sig = jax.nn.sigmoid(z)
    lf = jnp.log(lb + (1.0 - lb) * sig)
    kk = (1.0 - lb) * (1.0 - sig)
    sums = jnp.dot(mats, jnp.concatenate(_split3(lf), axis=1), preferred_element_type=F32)
    sums = sums[:, :HG_DK] + sums[:, HG_DK:2 * HG_DK] + sums[:, 2 * HG_DK:]
    bcum = sums[:nb]
    btot = sums[nb:]
    ke = (kk * jnp.exp(btot - bcum)).astype(BF16)
    kvt = lax.dot_general(v, jnp.concatenate([ke] * HG_NCHUNK, axis=1) * spread, (((0,), (0,)), ((), ())),
                          preferred_element_type=F32)
    dec_all = jnp.exp(btot)
    order = range(HG_NCHUNK - 1, -1, -1) if backward else range(HG_NCHUNK)
    entering = [None] * HG_NCHUNK
    for c in order:
        entering[c] = st
        st = st * dec_all[c * HG_CHUNK:c * HG_CHUNK + 1, :] + kvt[:, c * HG_DK:(c + 1) * HG_DK]
    if not want_out:
        return None, st
    qd = (q.astype(F32) * jnp.exp(bcum)).astype(BF16)
    kd = (kk * jnp.exp(-bcum)).astype(BF16)
    a = lax.dot_general(qd, kd, (((1,), (1,)), ((), ())), preferred_element_type=F32)
    a = jnp.where(keep_ref[1 if backward else 0] > 0.5, a, 0.0)
    o = jnp.dot(a.astype(BF16), v, preferred_element_type=F32)
    st_cat = jnp.concatenate([s.astype(BF16) for s in entering], axis=1)
    o = o + lax.dot_general(jnp.concatenate([qd] * HG_NCHUNK, axis=1) * spread, st_cat,
                            (((1,), (1,)), ((), ())), preferred_element_type=F32)
    return o, st


def _hgrn_body(q_ref, zf_ref, zb_ref, i_ref, g_ref, qc_ref, zfc_ref, zbc_ref, ic_ref, lb_ref, ng_ref,
               o_ref, accf_ref, accb_ref, stf_ref, stb_ref, tri_ref, keep_ref, spread_ref):
    lb = lb_ref[...]
    _hg_consts(tri_ref, keep_ref, spread_ref)
    block = functools.partial(_hg_block, lb=lb, tri_ref=tri_ref, keep_ref=keep_ref, spread_ref=spread_ref)
    zero_state = jnp.zeros((HG_DK, HG_DK), F32)
    _, stf = block(zfc_ref[0], qc_ref[0], ic_ref[0], st=zero_state, backward=False, want_out=False)
    _, stb = block(zbc_ref[0], qc_ref[0], ic_ref[0], st=zero_state, backward=True, want_out=False)
    stf_ref[...] = stf
    stb_ref[...] = stb
    n_blk = SEQ // HG_BLOCK

    def scan_step(n, carry):
        rows_f = pl.ds(pl.multiple_of(n * HG_BLOCK, HG_BLOCK), HG_BLOCK)
        rows_b = pl.ds(pl.multiple_of((n_blk - 1 - n) * HG_BLOCK, HG_BLOCK), HG_BLOCK)
        o_f, st_f = block(zf_ref[0, rows_f, :], q_ref[0, rows_f, :], i_ref[0, rows_f, :], st=stf_ref[...],
                          backward=False, want_out=True)
        o_b, st_b = block(zb_ref[0, rows_b, :], q_ref[0, rows_b, :], i_ref[0, rows_b, :], st=stb_ref[...],
                          backward=True, want_out=True)
        stf_ref[...] = st_f
        stb_ref[...] = st_b
        accf_ref[rows_f, :] = o_f
        accb_ref[rows_b, :] = o_b
        return carry

    lax.fori_loop(0, n_blk, scan_step, 0, unroll=2)

    def readout_step(n, carry):
        rows = pl.ds(pl.multiple_of(n * HG_BLOCK, HG_BLOCK), HG_BLOCK)
        tot = accf_ref[rows, :] + accb_ref[rows, :]
        y = _rms(tot, ng_ref[...]) * jax.nn.sigmoid(g_ref[0, rows, :].astype(F32))
        o_ref[0, rows, :] = y.astype(BF16)
        return carry

    lax.fori_loop(0, n_blk, readout_step, 0, unroll=2)


def _hgrn_call(qh, zf, zb, ih, gh, qc, zfc, zbc, ic, lb, norm_g):
    bsz = qh.shape[0]
    seq = pl.BlockSpec((1, SEQ, HG_DK), lambda b, h: (b, 0, h))
    ctx = pl.BlockSpec((1, CTX_LEN, HG_DK), lambda b, h: (b, 0, h))
    return pl.pallas_call(
        _hgrn_body,
        out_shape=jax.ShapeDtypeStruct((bsz, SEQ, N_HG_HEADS * HG_DK), BF16),
        grid=(bsz, N_HG_HEADS),
        in_specs=[seq, seq, seq, seq, seq, ctx, ctx, ctx, ctx,
                  pl.BlockSpec((1, HG_DK), lambda b, h: (0, h)),
                  pl.BlockSpec((1, HG_DK), lambda b, h: (0, 0))],
        out_specs=seq,
        scratch_shapes=[pltpu.VMEM((SEQ, HG_DK), F32),
                        pltpu.VMEM((SEQ, HG_DK), F32),
                        pltpu.VMEM((HG_DK, HG_DK), F32),
                        pltpu.VMEM((HG_DK, HG_DK), F32),
                        pltpu.VMEM((2, 2 * HG_BLOCK, HG_BLOCK), BF16),
                        pltpu.VMEM((2, HG_BLOCK, HG_BLOCK), F32),
                        pltpu.VMEM((HG_BLOCK, HG_NCHUNK * HG_DK), BF16)],
        compiler_params=_cparams(("arbitrary", "arbitrary")),
        name="hgrn",
    )(qh, zf, zb, ih, gh, qc, zfc, zbc, ic, lb, norm_g)


def _outproj_body(na_ref, hg_ref, x_ref, w_ref, gt1_ref, sh2_ref, sc2_ref, gpost_ref, gpre_ref, wr_ref,
                  x1_ref, h2_ref, lt_ref):
    mix = (jnp.dot(na_ref[0], w_ref[:NA_WIDTH, :], preferred_element_type=F32)
           + jnp.dot(hg_ref[0], w_ref[NA_WIDTH:, :], preferred_element_type=F32))
    x1 = x_ref[0] + gt1_ref[0] * _rms(mix, gpost_ref[...])
    x1_ref[0] = x1
    h2 = _rms(x1, gpre_ref[...]) * (1.0 + sc2_ref[0]) + sh2_ref[0]
    h2_ref[0] = h2.astype(BF16)
    lt_ref[0] = lax.dot_general(wr_ref[...], h2, (((1,), (1,)), ((), ())), precision=HIGHEST,
                                preferred_element_type=F32)


def _outproj_call(na_o, hg_o, x, w_out_b, gt1, sh2, sc2, g_post1, g_pre2, w_router_t):
    bsz, s, d = x.shape
    tm = OUT_TM
    half = pl.BlockSpec((1, tm, NA_WIDTH), lambda b, i: (b, i, 0))
    tok = pl.BlockSpec((1, tm, d), lambda b, i: (b, i, 0))
    vec = pl.BlockSpec((1, 1, d), lambda b, i: (b, 0, 0))
    par = pl.BlockSpec((1, d), lambda b, i: (0, 0))
    return pl.pallas_call(
        _outproj_body,
        out_shape=[jax.ShapeDtypeStruct((bsz, s, d), F32),
                   jax.ShapeDtypeStruct((bsz, s, d), BF16),
                   jax.ShapeDtypeStruct((bsz, N_EXPERTS, s), F32)],
        grid=(bsz, s // tm),
        in_specs=[half, half, tok, pl.BlockSpec((d, d), lambda b, i: (0, 0)), vec, vec, vec, par, par,
                  pl.BlockSpec((N_EXPERTS, d), lambda b, i: (0, 0))],
        out_specs=[tok, tok, pl.BlockSpec((1, N_EXPERTS, tm), lambda b, i: (b, 0, i))],
        compiler_params=_cparams(("arbitrary", "arbitrary")),
        name="outproj",
    )(na_o, hg_o, x, w_out_b, gt1, sh2, sc2, g_post1, g_pre2, w_router_t)


def _route_body(lt_ref, posm_ref, gate_ref, tab_ref, *, cap):
    l = lt_ref[0]
    mx = jnp.max(l, axis=0, keepdims=True)
    ex = jnp.exp(l - mx)
    aff = ex / jnp.sum(ex, axis=0, keepdims=True)
    capf = jnp.float32(cap)

    def count(mask):
        return jnp.sum(jnp.where(mask, 1.0, 0.0), axis=1, keepdims=True)

    def enough(v):
        return count(aff >= v) >= capf

    def bit_step(it, thr):
        cand = thr | (jnp.int32(1) << (30 - it))
        return jnp.where(enough(pltpu.bitcast(cand, F32)), cand, thr)

    thr = lax.fori_loop(0, 31, bit_step, jnp.zeros((N_EXPERTS, 1), I32))
    lo = pltpu.bitcast(thr, F32)
    hi = pltpu.bitcast(thr + 1, F32)

    def mid_step(it, lohi):
        lo, hi = lohi
        mid = 0.5 * (lo + hi)
        ok = enough(mid)
        return jnp.where(ok, mid, lo), jnp.where(ok, hi, mid)

    lo, hi = lax.fori_loop(0, 30, mid_step, (lo, hi))
    gt = aff >= hi
    eq = (aff >= lo) & jnp.logical_not(gt)
    need = capf - count(gt)

    rr = lax.broadcasted_iota(I32, (ROUTE_BLK, ROUTE_BLK), 0)
    cc = lax.broadcasted_iota(I32, (ROUTE_BLK, ROUTE_BLK), 1)
    upper = _mask_bf16(rr <= cc)
    lane = lax.broadcasted_iota(I32, (N_EXPERTS, LANES), 1)

    off_eq = jnp.zeros((N_EXPERTS, 1), F32)
    off_sel = jnp.zeros((N_EXPERTS, 1), F32)
    tab = jnp.zeros((N_EXPERTS, LANES), F32)
    for j in range(N_ROUTE_BLK):
        sl = slice(j * ROUTE_BLK, (j + 1) * ROUTE_BLK)
        eq_j = eq[:, sl]
        eq_b = _mask_bf16(eq_j)
        incl_eq = jnp.dot(eq_b, upper, preferred_element_type=F32) + off_eq
        rank_eq = incl_eq - eq_b.astype(F32)
        sel_j = gt[:, sl] | (eq_j & (rank_eq < need))
        sel_b = _mask_bf16(sel_j)
        incl_sel = jnp.dot(sel_b, upper, preferred_element_type=F32) + off_sel
        pos = incl_sel - sel_b.astype(F32)
        posm_ref[0, :, sl] = jnp.where(sel_j, pos.astype(I32), -1)
        gate_ref[0, :, sl] = jnp.where(sel_j, aff[:, sl], 0.0)
        tab = jnp.where(lane == j, off_sel, tab)
        off_eq = incl_eq[:, ROUTE_BLK - 1:ROUTE_BLK]
        off_sel = incl_sel[:, ROUTE_BLK - 1:ROUTE_BLK]
    tab = jnp.where(lane == N_ROUTE_BLK, off_sel, tab)
    tab_ref[0] = tab.astype(I32)


def _route_call(lt, cap):
    bsz = lt.shape[0]
    big = pl.BlockSpec((1, N_EXPERTS, SEQ), lambda b: (b, 0, 0))
    return pl.pallas_call(
        functools.partial(_route_body, cap=cap),
        out_shape=[jax.ShapeDtypeStruct((bsz, N_EXPERTS, SEQ), I32),
                   jax.ShapeDtypeStruct((bsz, N_EXPERTS, SEQ), F32),
                   jax.ShapeDtypeStruct((bsz, N_EXPERTS, LANES), I32)],
        grid=(bsz,),
        in_specs=[big],
        out_specs=[big, big, pl.BlockSpec((1, N_EXPERTS, LANES), lambda b: (b, 0, 0))],
        compiler_params=_cparams(("arbitrary",)),
        name="route",
    )(lt)


def _chunk_range(start, end):
    c_lo = start // SLOT_CHUNK
    n = jnp.where(end > start, (end - 1) // SLOT_CHUNK - c_lo + 1, 0)
    return c_lo, n


def _moe_body(tab_ref, posm_ref, gate_ref, h2_ref, wg_ref, wu_ref, wd_ref, y_ref,
              x_scr, acc_scr, gsl_scr, wgb_scr, wub_scr, wdb_scr, *, n_disp, cap):
    e = pl.program_id(0)
    s = pl.program_id(1)
    steps_per_batch = SEQ // MOE_TD
    tiles_per_step = MOE_TD // ROUTE_BLK
    chunks_per_batch = cap // SLOT_CHUNK

    @pl.when(s == 0)
    def _():
        x_scr[...] = jnp.zeros_like(x_scr)
        acc_scr[...] = jnp.zeros_like(acc_scr)
        gsl_scr[...] = jnp.zeros_like(gsl_scr)

    @pl.when(s < n_disp)
    def _():
        b = s // steps_per_batch
        tile0 = (s % steps_per_batch) * tiles_per_step
        tab_base = (b * N_EXPERTS + e) * (N_ROUTE_BLK + 1)
        slot = lax.broadcasted_iota(I32, (SLOT_CHUNK, ROUTE_BLK), 0)
        for sub in range(tiles_per_step):
            start = tab_ref[tab_base + tile0 + sub]
            end = tab_ref[tab_base + tile0 + sub + 1]
            c_lo, n = _chunk_range(start, end)
            pm = posm_ref[0, :, sub * ROUTE_BLK:(sub + 1) * ROUTE_BLK]
            ht = h2_ref[sub * ROUTE_BLK:(sub + 1) * ROUTE_BLK, :]
            g3 = [p.astype(F32) for p in _split3(gate_ref[0, :, sub * ROUTE_BLK:(sub + 1) * ROUTE_BLK])]
            g8 = jnp.concatenate(g3 + [jnp.zeros((8 - len(g3), ROUTE_BLK), F32)], axis=0).astype(BF16)

            def chunk_step(k, carry, pm=pm, ht=ht, g8=g8):
                base = (c_lo + k) * SLOT_CHUNK
                onehot = _mask_bf16(pm == slot + base)
                got = jnp.dot(onehot, ht, preferred_element_type=F32).astype(BF16)
                r0 = pl.multiple_of(b * cap + base, SLOT_CHUNK)
                x_scr[pl.ds(r0, SLOT_CHUNK), :] = x_scr[pl.ds(r0, SLOT_CHUNK), :] + got
                ci = b * chunks_per_batch + c_lo + k
                gsl_scr[ci] = gsl_scr[ci] + lax.dot_general(g8, onehot, (((1,), (1,)), ((), ())),
                                                            preferred_element_type=F32)
                return carry

            lax.fori_loop(0, n, chunk_step, 0)

    def mlp(width):
        wgb_scr[:, :width] = wg_ref[0, :, :width].astype(BF16)
        wub_scr[:, :width] = wu_ref[0, :, :width].astype(BF16)
        wdb_scr[:width, :] = wd_ref[0, :width, :].astype(BF16)

        def m_step(mi, carry):
            r0 = pl.multiple_of(mi * MOE_TM, MOE_TM)
            xm = x_scr[pl.ds(r0, MOE_TM), :]
            g = jnp.dot(xm, wgb_scr[:, :width], preferred_element_type=F32)
            u = jnp.dot(xm, wub_scr[:, :width], preferred_element_type=F32)
            hid = (g * jax.nn.sigmoid(g) * u).astype(BF16)
            acc_scr[pl.ds(r0, MOE_TM), :] = acc_scr[pl.ds(r0, MOE_TM), :] + jnp.dot(
                hid, wdb_scr[:width, :], preferred_element_type=F32)
            return carry

        lax.fori_loop(0, x_scr.shape[0] // MOE_TM, m_step, 0)

    @pl.when((s >= n_disp) & (s < n_disp + MOE_NF - 1))
    def _():
        mlp(MOE_TF)

    @pl.when(s == n_disp + MOE_NF - 1)
    def _():
        mlp(MOE_F_LAST)
        for ci in range(gsl_scr.shape[0]):
            g = jnp.sum(gsl_scr[ci].T, axis=1, keepdims=True)
            rows = slice(ci * SLOT_CHUNK, (ci + 1) * SLOT_CHUNK)
            y_ref[0, rows, :] = (acc_scr[rows, :] * g).astype(BF16)


def _moe_call(tab_flat, posm_rows, gate_rows, h2_flat, w_gate, w_up, w_down, *, bsz, cap):
    n_tok = h2_flat.shape[0]
    d = D_MODEL
    n_disp = n_tok // MOE_TD
    steps_per_batch = SEQ // MOE_TD
    n_steps = n_disp + MOE_NF
    slots = bsz * cap

    def fidx(s):
        return jnp.maximum(s - n_disp, 0)

    def didx(s):
        return jnp.minimum(s, n_disp - 1)

    sel_row = pl.BlockSpec((1, 1, MOE_TD),
                           lambda e, s, tab: ((didx(s) // steps_per_batch) * N_EXPERTS + e, 0,
                                              didx(s) % steps_per_batch))
    grid_spec = pltpu.PrefetchScalarGridSpec(
        num_scalar_prefetch=1,
        grid=(N_EXPERTS, n_steps),
        in_specs=[
            sel_row, sel_row,
            pl.BlockSpec((MOE_TD, d), lambda e, s, tab: (didx(s), 0)),
            pl.BlockSpec((1, d, MOE_TF), lambda e, s, tab: (e, 0, fidx(s))),
            pl.BlockSpec((1, d, MOE_TF), lambda e, s, tab: (e, 0, fidx(s))),
            pl.BlockSpec((1, MOE_TF, d), lambda e, s, tab: (e, fidx(s), 0)),
        ],
        out_specs=pl.BlockSpec((1, slots, d), lambda e, s, tab: (e, 0, 0)),
        scratch_shapes=[pltpu.VMEM((slots, d), BF16),
                        pltpu.VMEM((slots, d), F32),
                        pltpu.VMEM((slots // SLOT_CHUNK, 8, SLOT_CHUNK), F32),
                        pltpu.VMEM((d, MOE_TF), BF16),
                        pltpu.VMEM((d, MOE_TF), BF16),
                        pltpu.VMEM((MOE_TF, d), BF16)],
    )
    return pl.pallas_call(
        functools.partial(_moe_body, n_disp=n_disp, cap=cap),
        out_shape=jax.ShapeDtypeStruct((N_EXPERTS, slots, d), BF16),
        grid_spec=grid_spec,
        compiler_params=_cparams(("arbitrary", "arbitrary")),
        name="moe",
    )(tab_flat, posm_rows, gate_rows, h2_flat, w_gate, w_up, w_down)


def _combine_body(tab_ref, y_ref, posm_ref, *rest, e0, n_e, cap, last):
    if last:
        part_ref, x1_ref, gt2_ref, gpost_ref, o_ref, acc_ref = rest
    else:
        o_ref, acc_ref = rest
    b = pl.program_id(0)
    j = pl.program_id(1)
    pm = posm_ref[0].T
    win = 2 * SLOT_CHUNK
    slot_w = lax.broadcasted_iota(I32, (ROUTE_BLK, win), 1)
    slot_c = lax.broadcasted_iota(I32, (ROUTE_BLK, SLOT_CHUNK), 1)
    total = part_ref[0] if last else jnp.zeros((ROUTE_BLK, D_MODEL), F32)
    bases, extras = [], []
    for k in range(n_e):
        tab_base = (b * N_EXPERTS + e0 + k) * (N_ROUTE_BLK + 1)
        start = tab_ref[tab_base + j]
        end = tab_ref[tab_base + j + 1]
        base = pl.multiple_of(jnp.minimum((start // SLOT_CHUNK) * SLOT_CHUNK, cap - win), SLOT_CHUNK)
        total = total + jnp.dot(_mask_bf16(pm[:, k:k + 1] == slot_w + base), y_ref[k, pl.ds(base, win), :],
                                preferred_element_type=F32)
        bases.append(base)
        extras.append(jnp.maximum(end - (base + win) + SLOT_CHUNK - 1, 0) // SLOT_CHUNK)
    acc_ref[...] = total

    @pl.when(sum(extras) > 0)
    def _():
        for k in range(n_e):
            def extra_step(kk, carry, k=k):
                b2 = pl.multiple_of(bases[k] + win + kk * SLOT_CHUNK, SLOT_CHUNK)
                acc_ref[...] += jnp.dot(_mask_bf16(pm[:, k:k + 1] == slot_c + b2),
                                        y_ref[k, pl.ds(b2, SLOT_CHUNK), :], preferred_element_type=F32)
                return carry

            lax.fori_loop(0, extras[k], extra_step, 0)

    if last:
        o_ref[0] = x1_ref[0] + gt2_ref[0] * _rms(acc_ref[...], gpost_ref[...])
    else:
        o_ref[0] = acc_ref[...]


def _combine_call(tab_flat, y, posm, tail, *, e0, n_e, cap, last):
    bsz, _, s = posm.shape
    d = D_MODEL
    tok = pl.BlockSpec((1, ROUTE_BLK, d), lambda b, j, tab: (b, j, 0))
    sel = pl.BlockSpec((1, n_e, ROUTE_BLK), lambda b, j, tab: (b, e0 // n_e, j))
    in_specs = [pl.BlockSpec((n_e, cap, d), lambda b, j, tab: (e0 // n_e, b, 0)), sel]
    if last:
        in_specs += [tok, tok,
                     pl.BlockSpec((1, 1, d), lambda b, j, tab: (b, 0, 0)),
                     pl.BlockSpec((1, d), lambda b, j, tab: (0, 0))]
    grid_spec = pltpu.PrefetchScalarGridSpec(
        num_scalar_prefetch=1,
        grid=(bsz, s // ROUTE_BLK),
        in_specs=in_specs,
        out_specs=tok,
        scratch_shapes=[pltpu.VMEM((ROUTE_BLK, d), F32)],
    )
    return pl.pallas_call(
        functools.partial(_combine_body, e0=e0, n_e=n_e, cap=cap, last=last),
        out_shape=jax.ShapeDtypeStruct((bsz, s, d), F32),
        grid_spec=grid_spec,
        compiler_params=_cparams(("arbitrary", "arbitrary")),
        name="combine_last" if last else "combine_first",
    )(tab_flat, y, posm, *tail)


def kernel(x, c, ctx, c_ctx, w_mod, b_mod, g_pre1, g_post1, g_pre2, g_post2, w_in, w_out, na_rpb,
           hg_lb_logits, hg_norm, w_router, w_gate, w_up, w_down):
    bsz, seq, d = x.shape
    assert (seq, d) == (SEQ, D_MODEL) and ctx.shape[1] == CTX_LEN and w_mod.shape[0] == 1
    cap = 2 * seq // N_EXPERTS
    assert cap % SLOT_CHUNK == 0

    c_rows = jnp.zeros((8, d), F32).at[:bsz].set(c).at[bsz].set(c_ctx)
    mod = _mod_call(c_rows, w_mod[0], b_mod[0][None, :])
    sh1, sc1, gt1, sh2, sc2, gt2 = [m[:bsz, None, :] for m in jnp.split(mod, 6, axis=1)]
    sh1c, sc1c = mod[bsz:bsz + 1, None, :d], mod[bsz:bsz + 1, None, d:2 * d]

    lb_all = jnp.cumsum(jax.nn.softmax(hg_lb_logits.astype(F32), axis=0), axis=0)
    lb = lb_all[0][None, :]

    w_in_b = w_in[0].astype(BF16)
    w_out_b = w_out[0].astype(BF16)
    cos, sa, sb = _rope_tables(seq)

    qrot, qpl, krot, v, qh, zf, zb, ih, gh = _inproj_call(
        x, sh1, sc1, g_pre1[0][None, :], w_in_b, cos, sa, sb, rope=True, tm=IN_TM)
    ctx_flat = ctx.reshape(1, bsz * CTX_LEN, d)
    ctx_out = _inproj_call(ctx_flat, sh1c, sc1c, g_pre1[0][None, :], w_in_b,
                           cos[:bsz * CTX_LEN], sa[:bsz * CTX_LEN], sb[:bsz * CTX_LEN], rope=False, tm=CTX_LEN)
    _, _, kc, vc, qc, zfc, zbc, ic, _ = [a.reshape(bsz, CTX_LEN, COL_GROUP) for a in ctx_out]

    na_o = _na_call(qrot, qpl, krot, v, kc, vc, _na_bias(na_rpb[0]))
    hg_o = _hgrn_call(qh, zf, zb, ih, gh, qc, zfc, zbc, ic, lb, hg_norm[0][None, :])

    x1, h2, lt = _outproj_call(na_o, hg_o, x, w_out_b, gt1, sh2, sc2, g_post1[0][None, :],
                               g_pre2[0][None, :], w_router[0].T)

    posm, gate, tab = _route_call(lt, cap)
    tab_flat = tab[:, :, :N_ROUTE_BLK + 1].reshape(-1)
    y = _moe_call(tab_flat, posm.reshape(bsz * N_EXPERTS, 1, seq), gate.reshape(bsz * N_EXPERTS, 1, seq),
                  h2.reshape(bsz * seq, d), w_gate[0], w_up[0], w_down[0], bsz=bsz, cap=cap)

    half = N_EXPERTS // 2
    part = _combine_call(tab_flat, y, posm, (), e0=0, n_e=half, cap=cap, last=False)
    return _combine_call(tab_flat, y, posm, (part, x1, gt2, g_post2[0][None, :]),
                         e0=half, n_e=half, cap=cap, last=True)
```

```python
import functools

import jax
import jax.numpy as jnp
import numpy as np
from jax import lax
from jax.experimental import pallas as pl
from jax.experimental.pallas import tpu as pltpu

F32 = jnp.float32
BF16 = jnp.bfloat16
I32 = jnp.int32
HIGHEST = lax.Precision.HIGHEST

D_MODEL = 1024
GRID_W = 64
N_ROWS = 128
SEQ = 8192
CTX_LEN = 256
NA_HEAD_DIM = 64
N_NA_HEADS = 8
NA_WIDTH = 512
NA_WIN_H = 8
NA_WIN_W = 16
NA_QBLOCK_W = 16
NA_KSPAN_W = 32
ROPE_BASE = 10000.0
HG_DK = 128
N_HG_HEADS = 4
HG_CHUNK = 32
N_EXPERTS = 16
D_EXPERT = 2752
RMS_EPS = 1e-6
COL_GROUP = 512

LANES = 128
VMEM_LIMIT_BYTES = 58 * 1024 * 1024

IN_TM = 512
NA_ROWS_PER_STEP = 8
NA_KEYS = NA_WIN_H * GRID_W
HG_BLOCK = 256
OUT_TM = 512
ROUTE_BLK = 256
N_ROUTE_BLK = SEQ // ROUTE_BLK
SLOT_CHUNK = 128
MOE_TD = 2048
MOE_TF = 768
MOE_NF = -(-D_EXPERT // MOE_TF)
MOE_F_LAST = D_EXPERT - (MOE_NF - 1) * MOE_TF
MOE_TM = 256


def _cparams(sem):
    return pltpu.CompilerParams(dimension_semantics=sem, vmem_limit_bytes=VMEM_LIMIT_BYTES)


def _mask_bf16(mask):
    return jnp.where(mask, 1.0, 0.0).astype(BF16)


def _rms(v, g):
    return v * lax.rsqrt(jnp.mean(v * v, axis=-1, keepdims=True) + RMS_EPS) * g


def _mod_body(c_ref, w_ref, b_ref, o_ref):
    cv = c_ref[...]
    s = cv * jax.nn.sigmoid(cv)
    o_ref[...] = jnp.dot(s, w_ref[...], precision=HIGHEST, preferred_element_type=F32) + b_ref[...]


def _mod_call(c_rows, w_mod, b_mod):
    d = D_MODEL
    n = w_mod.shape[1]
    tn = 1024
    return pl.pallas_call(
        _mod_body,
        out_shape=jax.ShapeDtypeStruct((8, n), F32),
        grid=(n // tn,),
        in_specs=[pl.BlockSpec((8, d), lambda j: (0, 0)),
                  pl.BlockSpec((d, tn), lambda j: (0, j)),
                  pl.BlockSpec((1, tn), lambda j: (0, j))],
        out_specs=pl.BlockSpec((8, tn), lambda j: (0, j)),
        compiler_params=_cparams(("arbitrary",)),
        name="mod",
    )(c_rows, w_mod, b_mod)


def _rope128(pg, cos, sa, sb):
    return pg * cos + pltpu.roll(pg, LANES - 16, axis=1) * sa + pltpu.roll(pg, 16, axis=1) * sb


def _inproj_body(x_ref, sh_ref, sc_ref, g_ref, w_ref, cos_ref, sa_ref, sb_ref,
                 qrot_ref, qpl_ref, krot_ref, v_ref, qh_ref, zf_ref, zb_ref, ih_ref, gh_ref, *, rope):
    xv = x_ref[0]
    h = _rms(xv, g_ref[...]) * (1.0 + sc_ref[0]) + sh_ref[0]
    hb = h.astype(BF16)

    def group(j):
        return jnp.dot(hb, w_ref[:, j * COL_GROUP:(j + 1) * COL_GROUP], preferred_element_type=F32)

    def rotated(p):
        if not rope:
            return p
        cos, sa, sb = cos_ref[...], sa_ref[...], sb_ref[...]
        return jnp.concatenate(
            [_rope128(p[:, k * LANES:(k + 1) * LANES], cos, sa, sb) for k in range(COL_GROUP // LANES)], axis=1)

    scale = NA_HEAD_DIM ** -0.5
    p = group(0)
    qpl_ref[0] = (p * scale).astype(BF16)
    qrot_ref[0] = (rotated(p) * scale).astype(BF16)
    krot_ref[0] = rotated(group(1)).astype(BF16)
    v_ref[0] = group(2).astype(BF16)
    qh_ref[0] = group(3).astype(BF16)
    zf_ref[0] = group(4)
    zb_ref[0] = group(5)
    ih_ref[0] = group(6).astype(BF16)
    gh_ref[0] = group(7).astype(BF16)


def _inproj_call(x3, shift, scale, g_pre, w_in_b, cos, sa, sb, *, rope, tm):
    g, t, d = x3.shape
    n_out = 9
    dts = [BF16, BF16, BF16, BF16, BF16, F32, F32, BF16, BF16]
    tok = pl.BlockSpec((1, tm, d), lambda b, i: (b, i, 0))
    vec = pl.BlockSpec((1, 1, d), lambda b, i: (b, 0, 0))
    tab = pl.BlockSpec((tm, LANES), lambda b, i: (i, 0))
    outs = [pl.BlockSpec((1, tm, COL_GROUP), lambda b, i: (b, i, 0)) for _ in range(n_out)]
    return pl.pallas_call(
        functools.partial(_inproj_body, rope=rope),
        out_shape=[jax.ShapeDtypeStruct((g, t, COL_GROUP), dt) for dt in dts],
        grid=(g, t // tm),
        in_specs=[tok, vec, vec,
                  pl.BlockSpec((1, d), lambda b, i: (0, 0)),
                  pl.BlockSpec(w_in_b.shape, lambda b, i: (0, 0)),
                  tab, tab, tab],
        out_specs=outs,
        compiler_params=_cparams(("arbitrary", "arbitrary")),
        name="inproj_rope" if rope else "inproj_ctx",
    )(x3, shift, scale, g_pre, w_in_b, cos, sa, sb)


def _rope_tables(n_tok):
    half = NA_HEAD_DIM // 2
    t = jnp.arange(n_tok)
    row = (t // GRID_W).astype(F32)
    col = (t % GRID_W).astype(F32)
    lane = jnp.arange(LANES)
    d = lane % NA_HEAD_DIM
    dd = d % half
    fi = dd % (half // 2)
    inv_freq = ROPE_BASE ** (-(2.0 * fi.astype(F32)) / half)
    pos = jnp.where((d < half)[None, :], row[:, None], col[:, None])
    ang = pos * inv_freq[None, :]
    first = (dd < half // 2)[None, :]
    cos = jnp.cos(ang)
    sin = jnp.sin(ang)
    return cos, jnp.where(first, -sin, 0.0), jnp.where(first, 0.0, sin)


def _na_body(qrot_ref, qpl_ref, k_ref, v_ref, kc_ref, vc_ref, bias_ref, o_ref):
    m = pl.program_id(2)
    lane = lax.broadcasted_iota(I32, (1, LANES), 1)
    first_head = lane < NA_HEAD_DIM
    hm0, hm1 = _mask_bf16(first_head), _mask_bf16(lane >= NA_HEAD_DIM)
    nt = (((1,), (1,)), ((), ()))
    rows = range(NA_ROWS_PER_STEP)

    def stacked(ref):
        parts = []
        for i in rows:
            q = ref[0, i * GRID_W:(i + 1) * GRID_W, :]
            parts += [q * hm0, q * hm1]
        return jnp.concatenate(parts, axis=0)

    sc = lax.dot_general(stacked(qpl_ref), kc_ref[0], nt, preferred_element_type=F32)
    mc = jnp.max(sc, axis=-1, keepdims=True)
    pc = jnp.exp(sc - mc)
    lc = jnp.sum(pc, axis=-1, keepdims=True)
    accc = jnp.dot(pc.astype(BF16), vc_ref[0], preferred_element_type=F32)

    qs = stacked(qrot_ref)
    k0s, s_parts = [], []
    blk = 2 * GRID_W
    for i in rows:
        r = m * NA_ROWS_PER_STEP + i
        r0 = jnp.clip(r - NA_WIN_H // 2, 0, N_ROWS - NA_WIN_H)
        di = r0 - r + (NA_WIN_H - 1)
        k0s.append(pl.multiple_of(r0 * GRID_W, GRID_W))
        s_parts.append(lax.dot_general(qs[i * blk:(i + 1) * blk], k_ref[0, pl.ds(k0s[i], NA_KEYS), :], nt,
                                       preferred_element_type=F32) + bias_ref[0, di].reshape(blk, NA_KEYS))
    s = jnp.concatenate(s_parts, axis=0)
    mw = jnp.max(s, axis=-1, keepdims=True)
    p = jnp.exp(s - mw)
    lw = jnp.sum(p, axis=-1, keepdims=True)
    pb = p.astype(BF16)
    accw = jnp.concatenate([jnp.dot(pb[i * blk:(i + 1) * blk], v_ref[0, pl.ds(k0s[i], NA_KEYS), :],
                                    preferred_element_type=F32) for i in rows], axis=0)
    mm = jnp.maximum(mw, mc)
    aw = jnp.exp(mw - mm)
    ac = jnp.exp(mc - mm)
    o = (aw * accw + ac * accc) / (aw * lw + ac * lc)
    o_ref[0] = jnp.concatenate([jnp.where(first_head, o[i * blk:i * blk + GRID_W], o[i * blk + GRID_W:(i + 1) * blk])
                                for i in rows], axis=0).astype(BF16)


def _na_bias(rpb):
    c = np.arange(GRID_W)
    cq = np.arange(GRID_W)
    win_c0 = np.clip(cq - NA_WIN_W // 2, 0, GRID_W - NA_WIN_W)
    in_win = (c[None, :] >= win_c0[:, None]) & (c[None, :] < win_c0[:, None] + NA_WIN_W)
    dc = np.clip(c[None, :] - cq[:, None] + NA_WIN_W - 1, 0, 2 * NA_WIN_W - 2)
    pick = (dc[None, :, :] == np.arange(2 * NA_WIN_W - 1)[:, None, None]).astype(np.float32)
    t = jnp.einsum('hrj,jqc->hqrc', rpb.astype(F32), jnp.asarray(pick), precision=HIGHEST)
    t = jnp.where(jnp.asarray(in_win)[None, :, None, :], t, -jnp.inf)
    n_dr = 2 * NA_WIN_H - 1
    t = t.reshape(N_NA_HEADS // 2, 2, GRID_W, n_dr * GRID_W)
    return jnp.stack([t[..., di * GRID_W:di * GRID_W + NA_KEYS] for di in range(NA_WIN_H)], axis=1)


def _na_call(qrot, qpl, krot, v, kc, vc, bias):
    bsz = qrot.shape[0]
    tq = NA_ROWS_PER_STEP * GRID_W
    qspec = pl.BlockSpec((1, tq, LANES), lambda b, p, m: (b, m, p))
    kspec = pl.BlockSpec((1, SEQ, LANES), lambda b, p, m: (b, 0, p))
    cspec = pl.BlockSpec((1, CTX_LEN, LANES), lambda b, p, m: (b, 0, p))
    return pl.pallas_call(
        _na_body,
        out_shape=jax.ShapeDtypeStruct((bsz, SEQ, NA_WIDTH), BF16),
        grid=(bsz, N_NA_HEADS // 2, N_ROWS // NA_ROWS_PER_STEP),
        in_specs=[qspec, qspec, kspec, kspec, cspec, cspec,
                  pl.BlockSpec((1, NA_WIN_H, 2, GRID_W, NA_KEYS), lambda b, p, m: (p, 0, 0, 0, 0))],
        out_specs=qspec,
        compiler_params=_cparams(("arbitrary", "arbitrary", "arbitrary")),
        name="na",
    )(qrot, qpl, krot, v, kc, vc, bias)


def _split3(v):
    hi = v.astype(BF16)
    r1 = v - hi.astype(F32)
    mid = r1.astype(BF16)
    lo = (r1 - mid.astype(F32)).astype(BF16)
    return hi, mid, lo


HG_NCHUNK = HG_BLOCK // HG_CHUNK


def _hg_consts(tri_ref, keep_ref, spread_ref):
    nb = HG_BLOCK
    rr = lax.broadcasted_iota(I32, (nb, nb), 0)
    cc = lax.broadcasted_iota(I32, (nb, nb), 1)
    same = (rr // HG_CHUNK) == (cc // HG_CHUNK)
    same_m = _mask_bf16(same)
    tri_ref[0] = jnp.concatenate([_mask_bf16(same & (cc <= rr)), same_m], axis=0)
    tri_ref[1] = jnp.concatenate([_mask_bf16(same & (cc >= rr)), same_m], axis=0)
    keep_ref[0] = jnp.where(same & (cc <= rr), 1.0, 0.0)
    keep_ref[1] = jnp.where(same & (cc >= rr), 1.0, 0.0)
    r8 = lax.broadcasted_iota(I32, (nb, HG_NCHUNK * HG_DK), 0) // HG_CHUNK
    g8 = lax.broadcasted_iota(I32, (nb, HG_NCHUNK * HG_DK), 1) // HG_DK
    spread_ref[...] = _mask_bf16(r8 == g8)


def _hg_block(z, q, v, lb, st, tri_ref, keep_ref, spread_ref, *, backward, want_out):
    nb = HG_BLOCK
    mats = tri_ref[1 if backward else 0]
    spread = spread_ref[...]

    sig = jax.nn.sigmoid(z)
    lf = jnp.log(lb + (1.0 - lb) * sig)
    kk = (1.0 - lb) * (1.0 - sig)
    sums = jnp.dot(mats, jnp.concatenate(_split3(lf), axis=1), preferred_element_type=F32)
    sums = sums[:, :HG_DK] + sums[:, HG_DK:2 * HG_DK] + sums[:, 2 * HG_DK:]
    bcum = sums[:nb]
    btot = sums[nb:]
    ke = (kk * jnp.exp(btot - bcum)).astype(BF16)
    kvt = lax.dot_general(v, jnp.concatenate([ke] * HG_NCHUNK, axis=1) * spread, (((0,), (0,)), ((), ())),
                          preferred_element_type=F32)
    dec_all = jnp.exp(btot)
    order = range(HG_NCHUNK - 1, -1, -1) if backward else range(HG_NCHUNK)
    entering = [None] * HG_NCHUNK
    for c in order:
        entering[c] = st
        st = st * dec_all[c * HG_CHUNK:c * HG_CHUNK + 1, :] + kvt[:, c * HG_DK:(c + 1) * HG_DK]
    if not want_out:
        return None, st
    qd = (q.astype(F32) * jnp.exp(bcum)).astype(BF16)
    kd = (kk * jnp.exp(-bcum)).astype(BF16)
    a = lax.dot_general(qd, kd, (((1,), (1,)), ((), ())), preferred_element_type=F32)
    a = jnp.where(keep_ref[1 if backward else 0] > 0.5, a, 0.0)
    o = jnp.dot(a.astype(BF16), v, preferred_element_type=F32)
    st_cat = jnp.concatenate([s.astype(BF16) for s in entering], axis=1)
    o = o + lax.dot_general(jnp.concatenate([qd] * HG_NCHUNK, axis=1) * spread, st_cat,
                            (((1,), (1,)), ((), ())), preferred_element_type=F32)
    return o, st


def _hgrn_body(q_ref, zf_ref, zb_ref, i_ref, g_ref, qc_ref, zfc_ref, zbc_ref, ic_ref, lb_ref, ng_ref,
               o_ref, accf_ref, accb_ref, stf_ref, stb_ref, tri_ref, keep_ref, spread_ref):
    lb = lb_ref[...]
    _hg_consts(tri_ref, keep_ref, spread_ref)
    block = functools.partial(_hg_block, lb=lb, tri_ref=tri_ref, keep_ref=keep_ref, spread_ref=spread_ref)
    zero_state = jnp.zeros((HG_DK, HG_DK), F32)
    _, stf = block(zfc_ref[0], qc_ref[0], ic_ref[0], st=zero_state, backward=False, want_out=False)
    _, stb = block(zbc_ref[0], qc_ref[0], ic_ref[0], st=zero_state, backward=True, want_out=False)
    stf_ref[...] = stf
    stb_ref[...] = stb
    n_blk = SEQ // HG_BLOCK

    def scan_step(n, carry):
        rows_f = pl.ds(pl.multiple_of(n * HG_BLOCK, HG_BLOCK), HG_BLOCK)
        rows_b = pl.ds(pl.multiple_of((n_blk - 1 - n) * HG_BLOCK, HG_BLOCK), HG_BLOCK)
        o_f, st_f = block(zf_ref[0, rows_f, :], q_ref[0, rows_f, :], i_ref[0, rows_f, :], st=stf_ref[...],
                          backward=False, want_out=True)
        o_b, st_b = block(zb_ref[0, rows_b, :], q_ref[0, rows_b, :], i_ref[0, rows_b, :], st=stb_ref[...],
                          backward=True, want_out=True)
        stf_ref[...] = st_f
        stb_ref[...] = st_b
        accf_ref[rows_f, :] = o_f
        accb_ref[rows_b, :] = o_b
        return carry

    lax.fori_loop(0, n_blk, scan_step, 0, unroll=2)

    def readout_step(n, carry):
        rows = pl.ds(pl.multiple_of(n * HG_BLOCK, HG_BLOCK), HG_BLOCK)
        tot = accf_ref[rows, :] + accb_ref[rows, :]
        y = _rms(tot, ng_ref[...]) * jax.nn.sigmoid(g_ref[0, rows, :].astype(F32))
        o_ref[0, rows, :] = y.astype(BF16)
        return carry

    lax.fori_loop(0, n_blk, readout_step, 0, unroll=2)


def _hgrn_call(qh, zf, zb, ih, gh, qc, zfc, zbc, ic, lb, norm_g):
    bsz = qh.shape[0]
    seq = pl.BlockSpec((1, SEQ, HG_DK), lambda b, h: (b, 0, h))
    ctx = pl.BlockSpec((1, CTX_LEN, HG_DK), lambda b, h: (b, 0, h))
    return pl.pallas_call(
        _hgrn_body,
        out_shape=jax.ShapeDtypeStruct((bsz, SEQ, N_HG_HEADS * HG_DK), BF16),
        grid=(bsz, N_HG_HEADS),
        in_specs=[seq, seq, seq, seq, seq, ctx, ctx, ctx, ctx,
                  pl.BlockSpec((1, HG_DK), lambda b, h: (0, h)),
                  pl.BlockSpec((1, HG_DK), lambda b, h: (0, 0))],
        out_specs=seq,
        scratch_shapes=[pltpu.VMEM((SEQ, HG_DK), F32),
                        pltpu.VMEM((SEQ, HG_DK), F32),
                        pltpu.VMEM((HG_DK, HG_DK), F32),
                        pltpu.VMEM((HG_DK, HG_DK), F32),
                        pltpu.VMEM((2, 2 * HG_BLOCK, HG_BLOCK), BF16),
                        pltpu.VMEM((2, HG_BLOCK, HG_BLOCK), F32),
                        pltpu.VMEM((HG_BLOCK, HG_NCHUNK * HG_DK), BF16)],
        compiler_params=_cparams(("arbitrary", "arbitrary")),
        name="hgrn",
    )(qh, zf, zb, ih, gh, qc, zfc, zbc, ic, lb, norm_g)


def _outproj_body(na_ref, hg_ref, x_ref, w_ref, gt1_ref, sh2_ref, sc2_ref, gpost_ref, gpre_ref, wr_ref,
                  x1_ref, h2_ref, lt_ref):
    mix = (jnp.dot(na_ref[0], w_ref[:NA_WIDTH, :], preferred_element_type=F32)
           + jnp.dot(hg_ref[0], w_ref[NA_WIDTH:, :], preferred_element_type=F32))
    x1 = x_ref[0] + gt1_ref[0] * _rms(mix, gpost_ref[...])
    x1_ref[0] = x1
    h2 = _rms(x1, gpre_ref[...]) * (1.0 + sc2_ref[0]) + sh2_ref[0]
    h2_ref[0] = h2.astype(BF16)
    lt_ref[0] = lax.dot_general(wr_ref[...], h2, (((1,), (1,)), ((), ())), precision=HIGHEST,
                                preferred_element_type=F32)


def _outproj_call(na_o, hg_o, x, w_out_b, gt1, sh2, sc2, g_post1, g_pre2, w_router_t):
    bsz, s, d = x.shape
    tm = OUT_TM
    half = pl.BlockSpec((1, tm, NA_WIDTH), lambda b, i: (b, i, 0))
    tok = pl.BlockSpec((1, tm, d), lambda b, i: (b, i, 0))
    vec = pl.BlockSpec((1, 1, d), lambda b, i: (b, 0, 0))
    par = pl.BlockSpec((1, d), lambda b, i: (0, 0))
    return pl.pallas_call(
        _outproj_body,
        out_shape=[jax.ShapeDtypeStruct((bsz, s, d), F32),
                   jax.ShapeDtypeStruct((bsz, s, d), BF16),
                   jax.ShapeDtypeStruct((bsz, N_EXPERTS, s), F32)],
        grid=(bsz, s // tm),
        in_specs=[half, half, tok, pl.BlockSpec((d, d), lambda b, i: (0, 0)), vec, vec, vec, par, par,
                  pl.BlockSpec((N_EXPERTS, d), lambda b, i: (0, 0))],
        out_specs=[tok, tok, pl.BlockSpec((1, N_EXPERTS, tm), lambda b, i: (b, 0, i))],
        compiler_params=_cparams(("arbitrary", "arbitrary")),
        name="outproj",
    )(na_o, hg_o, x, w_out_b, gt1, sh2, sc2, g_post1, g_pre2, w_router_t)


def _route_body(lt_ref, posm_ref, gate_ref, tab_ref, *, cap):
    l = lt_ref[0]
    mx = jnp.max(l, axis=0, keepdims=True)
    ex = jnp.exp(l - mx)
    aff = ex / jnp.sum(ex, axis=0, keepdims=True)
    capf = jnp.float32(cap)

    def count(mask):
        return jnp.sum(jnp.where(mask, 1.0, 0.0), axis=1, keepdims=True)

    def enough(v):
        return count(aff >= v) >= capf

    def bit_step(it, thr):
        cand = thr | (jnp.int32(1) << (30 - it))
        return jnp.where(enough(pltpu.bitcast(cand, F32)), cand, thr)

    thr = lax.fori_loop(0, 31, bit_step, jnp.zeros((N_EXPERTS, 1), I32))
    lo = pltpu.bitcast(thr, F32)
    hi = pltpu.bitcast(thr + 1, F32)

    def mid_step(it, lohi):
        lo, hi = lohi
        mid = 0.5 * (lo + hi)
        ok = enough(mid)
        return jnp.where(ok, mid, lo), jnp.where(ok, hi, mid)

    lo, hi = lax.fori_loop(0, 30, mid_step, (lo, hi))
    gt = aff >= hi
    eq = (aff >= lo) & jnp.logical_not(gt)
    need = capf - count(gt)

    rr = lax.broadcasted_iota(I32, (ROUTE_BLK, ROUTE_BLK), 0)
    cc = lax.broadcasted_iota(I32, (ROUTE_BLK, ROUTE_BLK), 1)
    upper = _mask_bf16(rr <= cc)
    lane = lax.broadcasted_iota(I32, (N_EXPERTS, LANES), 1)

    off_eq = jnp.zeros((N_EXPERTS, 1), F32)
    off_sel = jnp.zeros((N_EXPERTS, 1), F32)
    tab = jnp.zeros((N_EXPERTS, LANES), F32)
    for j in range(N_ROUTE_BLK):
        sl = slice(j * ROUTE_BLK, (j + 1) * ROUTE_BLK)
        eq_j = eq[:, sl]
        eq_b = _mask_bf16(eq_j)
        incl_eq = jnp.dot(eq_b, upper, preferred_element_type=F32) + off_eq
        rank_eq = incl_eq - eq_b.astype(F32)
        sel_j = gt[:, sl] | (eq_j & (rank_eq < need))
        sel_b = _mask_bf16(sel_j)
        incl_sel = jnp.dot(sel_b, upper, preferred_element_type=F32) + off_sel
        pos = incl_sel - sel_b.astype(F32)
        posm_ref[0, :, sl] = jnp.where(sel_j, pos.astype(I32), -1)
        gate_ref[0, :, sl] = jnp.where(sel_j, aff[:, sl], 0.0)
        tab = jnp.where(lane == j, off_sel, tab)
        off_eq = incl_eq[:, ROUTE_BLK - 1:ROUTE_BLK]
        off_sel = incl_sel[:, ROUTE_BLK - 1:ROUTE_BLK]
    tab = jnp.where(lane == N_ROUTE_BLK, off_sel, tab)
    tab_ref[0] = tab.astype(I32)


def _route_call(lt, cap):
    bsz = lt.shape[0]
    big = pl.BlockSpec((1, N_EXPERTS, SEQ), lambda b: (b, 0, 0))
    return pl.pallas_call(
        functools.partial(_route_body, cap=cap),
        out_shape=[jax.ShapeDtypeStruct((bsz, N_EXPERTS, SEQ), I32),
                   jax.ShapeDtypeStruct((bsz, N_EXPERTS, SEQ), F32),
                   jax.ShapeDtypeStruct((bsz, N_EXPERTS, LANES), I32)],
        grid=(bsz,),
        in_specs=[big],
        out_specs=[big, big, pl.BlockSpec((1, N_EXPERTS, LANES), lambda b: (b, 0, 0))],
        compiler_params=_cparams(("arbitrary",)),
        name="route",
    )(lt)


def _chunk_range(start, end):
    c_lo = start // SLOT_CHUNK
    n = jnp.where(end > start, (end - 1) // SLOT_CHUNK - c_lo + 1, 0)
    return c_lo, n


def _moe_body(tab_ref, posm_ref, gate_ref, h2_ref, wg_ref, wu_ref, wd_ref, y_ref,
              x_scr, acc_scr, gsl_scr, wgb_scr, wub_scr, wdb_scr, *, n_disp, cap):
    e = pl.program_id(0)
    s = pl.program_id(1)
    steps_per_batch = SEQ // MOE_TD
    tiles_per_step = MOE_TD // ROUTE_BLK
    chunks_per_batch = cap // SLOT_CHUNK

    @pl.when(s == 0)
    def _():
        x_scr[...] = jnp.zeros_like(x_scr)
        acc_scr[...] = jnp.zeros_like(acc_scr)
        gsl_scr[...] = jnp.zeros_like(gsl_scr)

    @pl.when(s < n_disp)
    def _():
        b = s // steps_per_batch
        tile0 = (s % steps_per_batch) * tiles_per_step
        tab_base = (b * N_EXPERTS + e) * (N_ROUTE_BLK + 1)
        slot = lax.broadcasted_iota(I32, (SLOT_CHUNK, ROUTE_BLK), 0)
        for sub in range(tiles_per_step):
            start = tab_ref[tab_base + tile0 + sub]
            end = tab_ref[tab_base + tile0 + sub + 1]
            c_lo, n = _chunk_range(start, end)
            pm = posm_ref[0, :, sub * ROUTE_BLK:(sub + 1) * ROUTE_BLK]
            ht = h2_ref[sub * ROUTE_BLK:(sub + 1) * ROUTE_BLK, :]
            g3 = [p.astype(F32) for p in _split3(gate_ref[0, :, sub * ROUTE_BLK:(sub + 1) * ROUTE_BLK])]
            g8 = jnp.concatenate(g3 + [jnp.zeros((8 - len(g3), ROUTE_BLK), F32)], axis=0).astype(BF16)

            def chunk_step(k, carry, pm=pm, ht=ht, g8=g8):
                base = (c_lo + k) * SLOT_CHUNK
                onehot = _mask_bf16(pm == slot + base)
                got = jnp.dot(onehot, ht, preferred_element_type=F32).astype(BF16)
                r0 = pl.multiple_of(b * cap + base, SLOT_CHUNK)
                x_scr[pl.ds(r0, SLOT_CHUNK), :] = x_scr[pl.ds(r0, SLOT_CHUNK), :] + got
                ci = b * chunks_per_batch + c_lo + k
                gsl_scr[ci] = gsl_scr[ci] + lax.dot_general(g8, onehot, (((1,), (1,)), ((), ())),
                                                            preferred_element_type=F32)
                return carry

            lax.fori_loop(0, n, chunk_step, 0)

    def mlp(width):
        wgb_scr[:width, :] = wg_ref[0, :width, :].astype(BF16)
        wub_scr[:width, :] = wu_ref[0, :width, :].astype(BF16)
        wdb_scr[:width, :] = wd_ref[0, :width, :].astype(BF16)
        nt = (((1,), (1,)), ((), ()))

        def m_step(mi, carry):
            r0 = pl.multiple_of(mi * MOE_TM, MOE_TM)
            xm = x_scr[pl.ds(r0, MOE_TM), :]
            g = lax.dot_general(xm, wgb_scr[:width, :], nt, preferred_element_type=F32)
            u = lax.dot_general(xm, wub_scr[:width, :], nt, preferred_element_type=F32)
            hid = (g * jax.nn.sigmoid(g) * u).astype(BF16)
            acc_scr[pl.ds(r0, MOE_TM), :] = acc_scr[pl.ds(r0, MOE_TM), :] + jnp.dot(
                hid, wdb_scr[:width, :], preferred_element_type=F32)
            return carry

        lax.fori_loop(0, x_scr.shape[0] // MOE_TM, m_step, 0)

    @pl.when((s >= n_disp) & (s < n_disp + MOE_NF - 1))
    def _():
        mlp(MOE_TF)

    @pl.when(s == n_disp + MOE_NF - 1)
    def _():
        mlp(MOE_F_LAST)
        for ci in range(gsl_scr.shape[0]):
            g = jnp.sum(gsl_scr[ci].T, axis=1, keepdims=True)
            rows = slice(ci * SLOT_CHUNK, (ci + 1) * SLOT_CHUNK)
            y_ref[0, rows, :] = (acc_scr[rows, :] * g).astype(BF16)


def _moe_call(tab_flat, posm_rows, gate_rows, h2_flat, w_gate, w_up, w_down, *, bsz, cap):
    n_tok = h2_flat.shape[0]
    d = D_MODEL
    n_disp = n_tok // MOE_TD
    steps_per_batch = SEQ // MOE_TD
    n_steps = n_disp + MOE_NF
    slots = bsz * cap

    def fidx(s):
        return jnp.maximum(s - n_disp, 0)

    def didx(s):
        return jnp.minimum(s, n_disp - 1)

    sel_row = pl.BlockSpec((1, 1, MOE_TD),
                           lambda e, s, tab: ((didx(s) // steps_per_batch) * N_EXPERTS + e, 0,
                                              didx(s) % steps_per_batch))
    grid_spec = pltpu.PrefetchScalarGridSpec(
        num_scalar_prefetch=1,
        grid=(N_EXPERTS, n_steps),
        in_specs=[
            sel_row, sel_row,
            pl.BlockSpec((MOE_TD, d), lambda e, s, tab: (didx(s), 0)),
            pl.BlockSpec((1, MOE_TF, d), lambda e, s, tab: (e, fidx(s), 0)),
            pl.BlockSpec((1, MOE_TF, d), lambda e, s, tab: (e, fidx(s), 0)),
            pl.BlockSpec((1, MOE_TF, d), lambda e, s, tab: (e, fidx(s), 0)),
        ],
        out_specs=pl.BlockSpec((1, slots, d), lambda e, s, tab: (e, 0, 0)),
        scratch_shapes=[pltpu.VMEM((slots, d), BF16),
                        pltpu.VMEM((slots, d), F32),
                        pltpu.VMEM((slots // SLOT_CHUNK, 8, SLOT_CHUNK), F32),
                        pltpu.VMEM((MOE_TF, d), BF16),
                        pltpu.VMEM((MOE_TF, d), BF16),
                        pltpu.VMEM((MOE_TF, d), BF16)],
    )
    return pl.pallas_call(
        functools.partial(_moe_body, n_disp=n_disp, cap=cap),
        out_shape=jax.ShapeDtypeStruct((N_EXPERTS, slots, d), BF16),
        grid_spec=grid_spec,
        compiler_params=_cparams(("arbitrary", "arbitrary")),
        name="moe",
    )(tab_flat, posm_rows, gate_rows, h2_flat, w_gate, w_up, w_down)


def _combine_body(tab_ref, y_ref, posm_ref, *rest, e0, n_e, cap, last):
    if last:
        part_ref, x1_ref, gt2_ref, gpost_ref, o_ref, acc_ref = rest
    else:
        o_ref, acc_ref = rest
    b = pl.program_id(0)
    j = pl.program_id(1)
    pm = posm_ref[0].T
    win = 2 * SLOT_CHUNK
    slot_w = lax.broadcasted_iota(I32, (ROUTE_BLK, win), 1)
    slot_c = lax.broadcasted_iota(I32, (ROUTE_BLK, SLOT_CHUNK), 1)
    total = part_ref[0] if last else jnp.zeros((ROUTE_BLK, D_MODEL), F32)
    bases, extras = [], []
    for k in range(n_e):
        tab_base = (b * N_EXPERTS + e0 + k) * (N_ROUTE_BLK + 1)
        start = tab_ref[tab_base + j]
        end = tab_ref[tab_base + j + 1]
        base = pl.multiple_of(jnp.minimum((start // SLOT_CHUNK) * SLOT_CHUNK, cap - win), SLOT_CHUNK)
        total = total + jnp.dot(_mask_bf16(pm[:, k:k + 1] == slot_w + base), y_ref[k, pl.ds(base, win), :],
                                preferred_element_type=F32)
        bases.append(base)
        extras.append(jnp.maximum(end - (base + win) + SLOT_CHUNK - 1, 0) // SLOT_CHUNK)
    acc_ref[...] = total

    @pl.when(sum(extras) > 0)
    def _():
        for k in range(n_e):
            def extra_step(kk, carry, k=k):
                b2 = pl.multiple_of(bases[k] + win + kk * SLOT_CHUNK, SLOT_CHUNK)
                acc_ref[...] += jnp.dot(_mask_bf16(pm[:, k:k + 1] == slot_c + b2),
                                        y_ref[k, pl.ds(b2, SLOT_CHUNK), :], preferred_element_type=F32)
                return carry

            lax.fori_loop(0, extras[k], extra_step, 0)

    if last:
        o_ref[0] = x1_ref[0] + gt2_ref[0] * _rms(acc_ref[...], gpost_ref[...])
    else:
        o_ref[0] = acc_ref[...]


def _combine_call(tab_flat, y, posm, tail, *, e0, n_e, cap, last):
    bsz, _, s = posm.shape
    d = D_MODEL
    tok = pl.BlockSpec((1, ROUTE_BLK, d), lambda b, j, tab: (b, j, 0))
    sel = pl.BlockSpec((1, n_e, ROUTE_BLK), lambda b, j, tab: (b, e0 // n_e, j))
    in_specs = [pl.BlockSpec((n_e, cap, d), lambda b, j, tab: (e0 // n_e, b, 0)), sel]
    if last:
        in_specs += [tok, tok,
                     pl.BlockSpec((1, 1, d), lambda b, j, tab: (b, 0, 0)),
                     pl.BlockSpec((1, d), lambda b, j, tab: (0, 0))]
    grid_spec = pltpu.PrefetchScalarGridSpec(
        num_scalar_prefetch=1,
        grid=(bsz, s // ROUTE_BLK),
        in_specs=in_specs,
        out_specs=tok,
        scratch_shapes=[pltpu.VMEM((ROUTE_BLK, d), F32)],
    )
    return pl.pallas_call(
        functools.partial(_combine_body, e0=e0, n_e=n_e, cap=cap, last=last),
        out_shape=jax.ShapeDtypeStruct((bsz, s, d), F32),
        grid_spec=grid_spec,
        compiler_params=_cparams(("arbitrary", "arbitrary")),
        name="combine_last" if last else "combine_first",
    )(tab_flat, y, posm, *tail)


def kernel(x, c, ctx, c_ctx, w_mod, b_mod, g_pre1, g_post1, g_pre2, g_post2, w_in, w_out, na_rpb,
           hg_lb_logits, hg_norm, w_router, w_gate, w_up, w_down):
    bsz, seq, d = x.shape
    assert (seq, d) == (SEQ, D_MODEL) and ctx.shape[1] == CTX_LEN and w_mod.shape[0] == 1
    cap = 2 * seq // N_EXPERTS
    assert cap % SLOT_CHUNK == 0

    c_rows = jnp.zeros((8, d), F32).at[:bsz].set(c).at[bsz].set(c_ctx)
    mod = _mod_call(c_rows, w_mod[0], b_mod[0][None, :])
    sh1, sc1, gt1, sh2, sc2, gt2 = [m[:bsz, None, :] for m in jnp.split(mod, 6, axis=1)]
    sh1c, sc1c = mod[bsz:bsz + 1, None, :d], mod[bsz:bsz + 1, None, d:2 * d]

    lb_all = jnp.cumsum(jax.nn.softmax(hg_lb_logits.astype(F32), axis=0), axis=0)
    lb = lb_all[0][None, :]

    w_in_b = w_in[0].astype(BF16)
    w_out_b = w_out[0].astype(BF16)
    cos, sa, sb = _rope_tables(seq)

    qrot, qpl, krot, v, qh, zf, zb, ih, gh = _inproj_call(
        x, sh1, sc1, g_pre1[0][None, :], w_in_b, cos, sa, sb, rope=True, tm=IN_TM)
    ctx_flat = ctx.reshape(1, bsz * CTX_LEN, d)
    ctx_out = _inproj_call(ctx_flat, sh1c, sc1c, g_pre1[0][None, :], w_in_b,
                           cos[:bsz * CTX_LEN], sa[:bsz * CTX_LEN], sb[:bsz * CTX_LEN], rope=False, tm=CTX_LEN)
    _, _, kc, vc, qc, zfc, zbc, ic, _ = [a.reshape(bsz, CTX_LEN, COL_GROUP) for a in ctx_out]

    na_o = _na_call(qrot, qpl, krot, v, kc, vc, _na_bias(na_rpb[0]))
    hg_o = _hgrn_call(qh, zf, zb, ih, gh, qc, zfc, zbc, ic, lb, hg_norm[0][None, :])

    x1, h2, lt = _outproj_call(na_o, hg_o, x, w_out_b, gt1, sh2, sc2, g_post1[0][None, :],
                               g_pre2[0][None, :], w_router[0].T)

    posm, gate, tab = _route_call(lt, cap)
    tab_flat = tab[:, :, :N_ROUTE_BLK + 1].reshape(-1)
    y = _moe_call(tab_flat, posm.reshape(bsz * N_EXPERTS, 1, seq), gate.reshape(bsz * N_EXPERTS, 1, seq),
                  h2.reshape(bsz * seq, d), jnp.swapaxes(w_gate[0], 1, 2), jnp.swapaxes(w_up[0], 1, 2),
                  w_down[0], bsz=bsz, cap=cap)

    half = N_EXPERTS // 2
    part = _combine_call(tab_flat, y, posm, (), e0=0, n_e=half, cap=cap, last=False)
    return _combine_call(tab_flat, y, posm, (part, x1, gt2, g_post2[0][None, :]),
                         e0=half, n_e=half, cap=cap, last=True)
```

```python
import functools

import jax
import jax.numpy as jnp
import numpy as np
from jax import lax
from jax.experimental import pallas as pl
from jax.experimental.pallas import tpu as pltpu

F32 = jnp.float32
BF16 = jnp.bfloat16
I32 = jnp.int32
HIGHEST = lax.Precision.HIGHEST

D_MODEL = 1024
GRID_W = 64
N_ROWS = 128
SEQ = 8192
CTX_LEN = 256
NA_HEAD_DIM = 64
N_NA_HEADS = 8
NA_WIDTH = 512
NA_WIN_H = 8
NA_WIN_W = 16
NA_QBLOCK_W = 16
NA_KSPAN_W = 32
ROPE_BASE = 10000.0
HG_DK = 128
N_HG_HEADS = 4
HG_CHUNK = 32
N_EXPERTS = 16
D_EXPERT = 2752
RMS_EPS = 1e-6
COL_GROUP = 512

LANES = 128
VMEM_LIMIT_BYTES = 58 * 1024 * 1024

IN_TM = 512
NA_ROWS_PER_STEP = 8
NA_KEYS = NA_WIN_H * GRID_W
HG_BLOCK = 256
OUT_TM = 512
ROUTE_BLK = 256
N_ROUTE_BLK = SEQ // ROUTE_BLK
SLOT_CHUNK = 128
MOE_TD = 2048
MOE_TF = 768
MOE_NF = -(-D_EXPERT // MOE_TF)
MOE_F_LAST = D_EXPERT - (MOE_NF - 1) * MOE_TF
MOE_TM = 512


def _cparams(sem):
    return pltpu.CompilerParams(dimension_semantics=sem, vmem_limit_bytes=VMEM_LIMIT_BYTES)


def _mask_bf16(mask):
    return jnp.where(mask, 1.0, 0.0).astype(BF16)


def _rms(v, g):
    return v * lax.rsqrt(jnp.mean(v * v, axis=-1, keepdims=True) + RMS_EPS) * g


def _mod_body(c_ref, w_ref, b_ref, o_ref):
    cv = c_ref[...]
    s = cv * jax.nn.sigmoid(cv)
    o_ref[...] = jnp.dot(s, w_ref[...], precision=HIGHEST, preferred_element_type=F32) + b_ref[...]


def _mod_call(c_rows, w_mod, b_mod):
    d = D_MODEL
    n = w_mod.shape[1]
    tn = 1024
    return pl.pallas_call(
        _mod_body,
        out_shape=jax.ShapeDtypeStruct((8, n), F32),
        grid=(n // tn,),
        in_specs=[pl.BlockSpec((8, d), lambda j: (0, 0)),
                  pl.BlockSpec((d, tn), lambda j: (0, j)),
                  pl.BlockSpec((1, tn), lambda j: (0, j))],
        out_specs=pl.BlockSpec((8, tn), lambda j: (0, j)),
        compiler_params=_cparams(("arbitrary",)),
        name="mod",
    )(c_rows, w_mod, b_mod)


def _rope128(pg, cos, sa, sb):
    return pg * cos + pltpu.roll(pg, LANES - 16, axis=1) * sa + pltpu.roll(pg, 16, axis=1) * sb


def _inproj_body(x_ref, sh_ref, sc_ref, g_ref, w_ref, cos_ref, sa_ref, sb_ref,
                 qrot_ref, qpl_ref, krot_ref, v_ref, qh_ref, zf_ref, zb_ref, ih_ref, gh_ref, *, rope):
    xv = x_ref[0]
    h = _rms(xv, g_ref[...]) * (1.0 + sc_ref[0]) + sh_ref[0]
    hb = h.astype(BF16)

    def group(j):
        return jnp.dot(hb, w_ref[:, j * COL_GROUP:(j + 1) * COL_GROUP], preferred_element_type=F32)

    def rotated(p):
        if not rope:
            return p
        cos, sa, sb = cos_ref[...], sa_ref[...], sb_ref[...]
        return jnp.concatenate(
            [_rope128(p[:, k * LANES:(k + 1) * LANES], cos, sa, sb) for k in range(COL_GROUP // LANES)], axis=1)

    scale = NA_HEAD_DIM ** -0.5
    p = group(0)
    qpl_ref[0] = (p * scale).astype(BF16)
    qrot_ref[0] = (rotated(p) * scale).astype(BF16)
    krot_ref[0] = rotated(group(1)).astype(BF16)
    v_ref[0] = group(2).astype(BF16)
    qh_ref[0] = group(3).astype(BF16)
    zf_ref[0] = group(4)
    zb_ref[0] = group(5)
    ih_ref[0] = group(6).astype(BF16)
    gh_ref[0] = group(7).astype(BF16)


def _inproj_call(x3, shift, scale, g_pre, w_in_b, cos, sa, sb, *, rope, tm):
    g, t, d = x3.shape
    n_out = 9
    dts = [BF16, BF16, BF16, BF16, BF16, F32, F32, BF16, BF16]
    tok = pl.BlockSpec((1, tm, d), lambda b, i: (b, i, 0))
    vec = pl.BlockSpec((1, 1, d), lambda b, i: (b, 0, 0))
    tab = pl.BlockSpec((tm, LANES), lambda b, i: (i, 0))
    outs = [pl.BlockSpec((1, tm, COL_GROUP), lambda b, i: (b, i, 0)) for _ in range(n_out)]
    return pl.pallas_call(
        functools.partial(_inproj_body, rope=rope),
        out_shape=[jax.ShapeDtypeStruct((g, t, COL_GROUP), dt) for dt in dts],
        grid=(g, t // tm),
        in_specs=[tok, vec, vec,
                  pl.BlockSpec((1, d), lambda b, i: (0, 0)),
                  pl.BlockSpec(w_in_b.shape, lambda b, i: (0, 0)),
                  tab, tab, tab],
        out_specs=outs,
        compiler_params=_cparams(("arbitrary", "arbitrary")),
        name="inproj_rope" if rope else "inproj_ctx",
    )(x3, shift, scale, g_pre, w_in_b, cos, sa, sb)


def _rope_tables(n_tok):
    half = NA_HEAD_DIM // 2
    t = jnp.arange(n_tok)
    row = (t // GRID_W).astype(F32)
    col = (t % GRID_W).astype(F32)
    lane = jnp.arange(LANES)
    d = lane % NA_HEAD_DIM
    dd = d % half
    fi = dd % (half // 2)
    inv_freq = ROPE_BASE ** (-(2.0 * fi.astype(F32)) / half)
    pos = jnp.where((d < half)[None, :], row[:, None], col[:, None])
    ang = pos * inv_freq[None, :]
    first = (dd < half // 2)[None, :]
    cos = jnp.cos(ang)
    sin = jnp.sin(ang)
    return cos, jnp.where(first, -sin, 0.0), jnp.where(first, 0.0, sin)


def _na_body(qrot_ref, qpl_ref, k_ref, v_ref, kc_ref, vc_ref, bias_ref, o_ref):
    m = pl.program_id(2)
    lane = lax.broadcasted_iota(I32, (1, LANES), 1)
    first_head = lane < NA_HEAD_DIM
    hm0, hm1 = _mask_bf16(first_head), _mask_bf16(lane >= NA_HEAD_DIM)
    nt = (((1,), (1,)), ((), ()))
    rows = range(NA_ROWS_PER_STEP)

    def stacked(ref):
        parts = []
        for i in rows:
            q = ref[0, i * GRID_W:(i + 1) * GRID_W, :]
            parts += [q * hm0, q * hm1]
        return jnp.concatenate(parts, axis=0)

    sc = lax.dot_general(stacked(qpl_ref), kc_ref[0], nt, preferred_element_type=F32)
    mc = jnp.max(sc, axis=-1, keepdims=True)
    pc = jnp.exp(sc - mc)
    lc = jnp.sum(pc, axis=-1, keepdims=True)
    accc = jnp.dot(pc.astype(BF16), vc_ref[0], preferred_element_type=F32)

    qs = stacked(qrot_ref)
    k0s, s_parts = [], []
    blk = 2 * GRID_W
    for i in rows:
        r = m * NA_ROWS_PER_STEP + i
        r0 = jnp.clip(r - NA_WIN_H // 2, 0, N_ROWS - NA_WIN_H)
        di = r0 - r + (NA_WIN_H - 1)
        k0s.append(pl.multiple_of(r0 * GRID_W, GRID_W))
        s_parts.append(lax.dot_general(qs[i * blk:(i + 1) * blk], k_ref[0, pl.ds(k0s[i], NA_KEYS), :], nt,
                                       preferred_element_type=F32) + bias_ref[0, di].reshape(blk, NA_KEYS))
    s = jnp.concatenate(s_parts, axis=0)
    mw = jnp.max(s, axis=-1, keepdims=True)
    p = jnp.exp(s - mw)
    lw = jnp.sum(p, axis=-1, keepdims=True)
    pb = p.astype(BF16)
    accw = jnp.concatenate([jnp.dot(pb[i * blk:(i + 1) * blk], v_ref[0, pl.ds(k0s[i], NA_KEYS), :],
                                    preferred_element_type=F32) for i in rows], axis=0)
    mm = jnp.maximum(mw, mc)
    aw = jnp.exp(mw - mm)
    ac = jnp.exp(mc - mm)
    o = (aw * accw + ac * accc) / (aw * lw + ac * lc)
    o_ref[0] = jnp.concatenate([jnp.where(first_head, o[i * blk:i * blk + GRID_W], o[i * blk + GRID_W:(i + 1) * blk])
                                for i in rows], axis=0).astype(BF16)


def _na_bias(rpb):
    c = np.arange(GRID_W)
    cq = np.arange(GRID_W)
    win_c0 = np.clip(cq - NA_WIN_W // 2, 0, GRID_W - NA_WIN_W)
    in_win = (c[None, :] >= win_c0[:, None]) & (c[None, :] < win_c0[:, None] + NA_WIN_W)
    dc = np.clip(c[None, :] - cq[:, None] + NA_WIN_W - 1, 0, 2 * NA_WIN_W - 2)
    pick = (dc[None, :, :] == np.arange(2 * NA_WIN_W - 1)[:, None, None]).astype(np.float32)
    t = jnp.einsum('hrj,jqc->hqrc', rpb.astype(F32), jnp.asarray(pick), precision=HIGHEST)
    t = jnp.where(jnp.asarray(in_win)[None, :, None, :], t, -jnp.inf)
    n_dr = 2 * NA_WIN_H - 1
    t = t.reshape(N_NA_HEADS // 2, 2, GRID_W, n_dr * GRID_W)
    return jnp.stack([t[..., di * GRID_W:di * GRID_W + NA_KEYS] for di in range(NA_WIN_H)], axis=1)


def _na_call(qrot, qpl, krot, v, kc, vc, bias):
    bsz = qrot.shape[0]
    tq = NA_ROWS_PER_STEP * GRID_W
    qspec = pl.BlockSpec((1, tq, LANES), lambda b, p, m: (b, m, p))
    kspec = pl.BlockSpec((1, SEQ, LANES), lambda b, p, m: (b, 0, p))
    cspec = pl.BlockSpec((1, CTX_LEN, LANES), lambda b, p, m: (b, 0, p))
    return pl.pallas_call(
        _na_body,
        out_shape=jax.ShapeDtypeStruct((bsz, SEQ, NA_WIDTH), BF16),
        grid=(bsz, N_NA_HEADS // 2, N_ROWS // NA_ROWS_PER_STEP),
        in_specs=[qspec, qspec, kspec, kspec, cspec, cspec,
                  pl.BlockSpec((1, NA_WIN_H, 2, GRID_W, NA_KEYS), lambda b, p, m: (p, 0, 0, 0, 0))],
        out_specs=qspec,
        compiler_params=_cparams(("arbitrary", "arbitrary", "arbitrary")),
        name="na",
    )(qrot, qpl, krot, v, kc, vc, bias)


def _split3(v):
    hi = v.astype(BF16)
    r1 = v - hi.astype(F32)
    mid = r1.astype(BF16)
    lo = (r1 - mid.astype(F32)).astype(BF16)
    return hi, mid, lo


HG_NCHUNK = HG_BLOCK // HG_CHUNK


def _hg_consts(tri_ref, keep_ref, spread_ref):
    nb = HG_BLOCK
    rr = lax.broadcasted_iota(I32, (nb, nb), 0)
    cc = lax.broadcasted_iota(I32, (nb, nb), 1)
    same = (rr // HG_CHUNK) == (cc // HG_CHUNK)
    fwd = jnp.where(same & (cc <= rr), 1.0, 0.0)
    bwd = jnp.where(same & (cc >= rr), 1.0, 0.0)
    tri_ref[0] = fwd.astype(BF16)
    tri_ref[1] = bwd.astype(BF16)
    keep_ref[...] = jnp.concatenate([fwd, bwd], axis=0)
    r8 = lax.broadcasted_iota(I32, (nb, HG_NCHUNK * HG_DK), 0) // HG_CHUNK
    g8 = lax.broadcasted_iota(I32, (nb, HG_NCHUNK * HG_DK), 1) // HG_DK
    spread_ref[...] = _mask_bf16(r8 == g8)


def _hg_pair(zf, zb, qf, qb, vf, vb, stf, stb, lb, tri_ref, keep_ref, spread_ref, *, want_out):
    nb = HG_BLOCK
    spread = spread_ref[...]
    v = (vf, vb)

    sig = jax.nn.sigmoid(jnp.concatenate([zf, zb], axis=0))
    lf = jnp.log(lb + (1.0 - lb) * sig)
    kk = (1.0 - lb) * (1.0 - sig)
    parts = jnp.concatenate(_split3(lf), axis=1)
    sums = jnp.concatenate([jnp.dot(tri_ref[d], parts[d * nb:(d + 1) * nb], preferred_element_type=F32)
                            for d in range(2)], axis=0)
    bcum = sums[:, :HG_DK] + sums[:, HG_DK:2 * HG_DK] + sums[:, 2 * HG_DK:]
    end_rows = [d * nb + c * HG_CHUNK + (HG_CHUNK - 1 if d == 0 else 0) for d in range(2) for c in range(HG_NCHUNK)]
    ends = [bcum[r:r + 1, :] for r in end_rows]
    btot = jnp.concatenate([jnp.broadcast_to(e, (HG_CHUNK, HG_DK)) for e in ends], axis=0)
    dec = jnp.exp(jnp.concatenate(ends, axis=0))
    ke = (kk * jnp.exp(btot - bcum)).astype(BF16)

    new_states, entering = [], []
    for d, st in enumerate((stf, stb)):
        kvt = lax.dot_general(v[d], jnp.concatenate([ke[d * nb:(d + 1) * nb]] * HG_NCHUNK, axis=1) * spread,
                              (((0,), (0,)), ((), ())), preferred_element_type=F32)
        ent = [None] * HG_NCHUNK
        for c in (range(HG_NCHUNK) if d == 0 else range(HG_NCHUNK - 1, -1, -1)):
            ent[c] = st
            i_dec = d * HG_NCHUNK + c
            st = st * dec[i_dec:i_dec + 1, :] + kvt[:, c * HG_DK:(c + 1) * HG_DK]
        new_states.append(st)
        entering.append(ent)
    if not want_out:
        return None, None, new_states[0], new_states[1]

    qd = (jnp.concatenate([qf, qb], axis=0).astype(F32) * jnp.exp(bcum)).astype(BF16)
    kd = (kk * jnp.exp(-bcum)).astype(BF16)
    nt = (((1,), (1,)), ((), ()))
    a = jnp.concatenate([lax.dot_general(qd[d * nb:(d + 1) * nb], kd[d * nb:(d + 1) * nb], nt,
                                         preferred_element_type=F32) for d in range(2)], axis=0)
    a = jnp.where(keep_ref[...] > 0.5, a, 0.0).astype(BF16)
    outs = []
    for d in range(2):
        st_cat = jnp.concatenate([s.astype(BF16) for s in entering[d]], axis=1)
        qd_d = qd[d * nb:(d + 1) * nb]
        outs.append(jnp.dot(a[d * nb:(d + 1) * nb], v[d], preferred_element_type=F32)
                    + lax.dot_general(jnp.concatenate([qd_d] * HG_NCHUNK, axis=1) * spread, st_cat, nt,
                                      preferred_element_type=F32))
    return outs[0], outs[1], new_states[0], new_states[1]


def _hgrn_body(q_ref, zf_ref, zb_ref, i_ref, g_ref, qc_ref, zfc_ref, zbc_ref, ic_ref, lb_ref, ng_ref,
               o_ref, accf_ref, accb_ref, stf_ref, stb_ref, tri_ref, keep_ref, spread_ref):
    lb = lb_ref[...]
    _hg_consts(tri_ref, keep_ref, spread_ref)
    pair = functools.partial(_hg_pair, lb=lb, tri_ref=tri_ref, keep_ref=keep_ref, spread_ref=spread_ref)
    zero_state = jnp.zeros((HG_DK, HG_DK), F32)
    _, _, stf, stb = pair(zfc_ref[0], zbc_ref[0], qc_ref[0], qc_ref[0], ic_ref[0], ic_ref[0],
                          zero_state, zero_state, want_out=False)
    stf_ref[...] = stf
    stb_ref[...] = stb
    n_blk = SEQ // HG_BLOCK

    def scan_step(n, carry):
        rows_f = pl.ds(pl.multiple_of(n * HG_BLOCK, HG_BLOCK), HG_BLOCK)
        rows_b = pl.ds(pl.multiple_of((n_blk - 1 - n) * HG_BLOCK, HG_BLOCK), HG_BLOCK)
        o_f, o_b, st_f, st_b = pair(zf_ref[0, rows_f, :], zb_ref[0, rows_b, :], q_ref[0, rows_f, :],
                                    q_ref[0, rows_b, :], i_ref[0, rows_f, :], i_ref[0, rows_b, :],
                                    stf_ref[...], stb_ref[...], want_out=True)
        stf_ref[...] = st_f
        stb_ref[...] = st_b
        accf_ref[rows_f, :] = o_f
        accb_ref[rows_b, :] = o_b
        return carry

    lax.fori_loop(0, n_blk, scan_step, 0, unroll=2)

    def readout_step(n, carry):
        rows = pl.ds(pl.multiple_of(n * HG_BLOCK, HG_BLOCK), HG_BLOCK)
        tot = accf_ref[rows, :] + accb_ref[rows, :]
        y = _rms(tot, ng_ref[...]) * jax.nn.sigmoid(g_ref[0, rows, :].astype(F32))
        o_ref[0, rows, :] = y.astype(BF16)
        return carry

    lax.fori_loop(0, n_blk, readout_step, 0, unroll=2)


def _hgrn_call(qh, zf, zb, ih, gh, qc, zfc, zbc, ic, lb, norm_g):
    bsz = qh.shape[0]
    seq = pl.BlockSpec((1, SEQ, HG_DK), lambda b, h: (b, 0, h))
    ctx = pl.BlockSpec((1, CTX_LEN, HG_DK), lambda b, h: (b, 0, h))
    return pl.pallas_call(
        _hgrn_body,
        out_shape=jax.ShapeDtypeStruct((bsz, SEQ, N_HG_HEADS * HG_DK), BF16),
        grid=(bsz, N_HG_HEADS),
        in_specs=[seq, seq, seq, seq, seq, ctx, ctx, ctx, ctx,
                  pl.BlockSpec((1, HG_DK), lambda b, h: (0, h)),
                  pl.BlockSpec((1, HG_DK), lambda b, h: (0, 0))],
        out_specs=seq,
        scratch_shapes=[pltpu.VMEM((SEQ, HG_DK), F32),
                        pltpu.VMEM((SEQ, HG_DK), F32),
                        pltpu.VMEM((HG_DK, HG_DK), F32),
                        pltpu.VMEM((HG_DK, HG_DK), F32),
                        pltpu.VMEM((2, HG_BLOCK, HG_BLOCK), BF16),
                        pltpu.VMEM((2 * HG_BLOCK, HG_BLOCK), F32),
                        pltpu.VMEM((HG_BLOCK, HG_NCHUNK * HG_DK), BF16)],
        compiler_params=_cparams(("arbitrary", "arbitrary")),
        name="hgrn",
    )(qh, zf, zb, ih, gh, qc, zfc, zbc, ic, lb, norm_g)


def _outproj_body(na_ref, hg_ref, x_ref, w_ref, gt1_ref, sh2_ref, sc2_ref, gpost_ref, gpre_ref, wr_ref,
                  x1_ref, h2_ref, lt_ref):
    mix = (jnp.dot(na_ref[0], w_ref[:NA_WIDTH, :], preferred_element_type=F32)
           + jnp.dot(hg_ref[0], w_ref[NA_WIDTH:, :], preferred_element_type=F32))
    x1 = x_ref[0] + gt1_ref[0] * _rms(mix, gpost_ref[...])
    x1_ref[0] = x1
    h2 = _rms(x1, gpre_ref[...]) * (1.0 + sc2_ref[0]) + sh2_ref[0]
    h2_ref[0] = h2.astype(BF16)
    lt_ref[0] = lax.dot_general(wr_ref[...], h2, (((1,), (1,)), ((), ())), precision=HIGHEST,
                                preferred_element_type=F32)


def _outproj_call(na_o, hg_o, x, w_out_b, gt1, sh2, sc2, g_post1, g_pre2, w_router_t):
    bsz, s, d = x.shape
    tm = OUT_TM
    half = pl.BlockSpec((1, tm, NA_WIDTH), lambda b, i: (b, i, 0))
    tok = pl.BlockSpec((1, tm, d), lambda b, i: (b, i, 0))
    vec = pl.BlockSpec((1, 1, d), lambda b, i: (b, 0, 0))
    par = pl.BlockSpec((1, d), lambda b, i: (0, 0))
    return pl.pallas_call(
        _outproj_body,
        out_shape=[jax.ShapeDtypeStruct((bsz, s, d), F32),
                   jax.ShapeDtypeStruct((bsz, s, d), BF16),
                   jax.ShapeDtypeStruct((bsz, N_EXPERTS, s), F32)],
        grid=(bsz, s // tm),
        in_specs=[half, half, tok, pl.BlockSpec((d, d), lambda b, i: (0, 0)), vec, vec, vec, par, par,
                  pl.BlockSpec((N_EXPERTS, d), lambda b, i: (0, 0))],
        out_specs=[tok, tok, pl.BlockSpec((1, N_EXPERTS, tm), lambda b, i: (b, 0, i))],
        compiler_params=_cparams(("arbitrary", "arbitrary")),
        name="outproj",
    )(na_o, hg_o, x, w_out_b, gt1, sh2, sc2, g_post1, g_pre2, w_router_t)


def _route_body(lt_ref, posm_ref, gate_ref, tab_ref, *, cap):
    l = lt_ref[0]
    mx = jnp.max(l, axis=0, keepdims=True)
    ex = jnp.exp(l - mx)
    aff = ex / jnp.sum(ex, axis=0, keepdims=True)
    capf = jnp.float32(cap)

    def count(mask):
        return jnp.sum(jnp.where(mask, 1.0, 0.0), axis=1, keepdims=True)

    def enough(v):
        return count(aff >= v) >= capf

    def bit_step(it, thr):
        cand = thr | (jnp.int32(1) << (30 - it))
        return jnp.where(enough(pltpu.bitcast(cand, F32)), cand, thr)

    thr = lax.fori_loop(0, 31, bit_step, jnp.zeros((N_EXPERTS, 1), I32))
    lo = pltpu.bitcast(thr, F32)
    hi = pltpu.bitcast(thr + 1, F32)

    def mid_step(it, lohi):
        lo, hi = lohi
        mid = 0.5 * (lo + hi)
        ok = enough(mid)
        return jnp.where(ok, mid, lo), jnp.where(ok, hi, mid)

    lo, hi = lax.fori_loop(0, 30, mid_step, (lo, hi))
    gt = aff >= hi
    eq = (aff >= lo) & jnp.logical_not(gt)
    need = capf - count(gt)

    rr = lax.broadcasted_iota(I32, (ROUTE_BLK, ROUTE_BLK), 0)
    cc = lax.broadcasted_iota(I32, (ROUTE_BLK, ROUTE_BLK), 1)
    upper = _mask_bf16(rr <= cc)
    lane = lax.broadcasted_iota(I32, (N_EXPERTS, LANES), 1)

    off_eq = jnp.zeros((N_EXPERTS, 1), F32)
    off_sel = jnp.zeros((N_EXPERTS, 1), F32)
    tab = jnp.zeros((N_EXPERTS, LANES), F32)
    for j in range(N_ROUTE_BLK):
        sl = slice(j * ROUTE_BLK, (j + 1) * ROUTE_BLK)
        eq_j = eq[:, sl]
        eq_b = _mask_bf16(eq_j)
        incl_eq = jnp.dot(eq_b, upper, preferred_element_type=F32) + off_eq
        rank_eq = incl_eq - eq_b.astype(F32)
        sel_j = gt[:, sl] | (eq_j & (rank_eq < need))
        sel_b = _mask_bf16(sel_j)
        incl_sel = jnp.dot(sel_b, upper, preferred_element_type=F32) + off_sel
        pos = incl_sel - sel_b.astype(F32)
        posm_ref[0, :, sl] = jnp.where(sel_j, pos.astype(I32), -1)
        gate_ref[0, :, sl] = jnp.where(sel_j, aff[:, sl], 0.0)
        tab = jnp.where(lane == j, off_sel, tab)
        off_eq = incl_eq[:, ROUTE_BLK - 1:ROUTE_BLK]
        off_sel = incl_sel[:, ROUTE_BLK - 1:ROUTE_BLK]
    tab = jnp.where(lane == N_ROUTE_BLK, off_sel, tab)
    tab_ref[0] = tab.astype(I32)


def _route_call(lt, cap):
    bsz = lt.shape[0]
    big = pl.BlockSpec((1, N_EXPERTS, SEQ), lambda b: (b, 0, 0))
    return pl.pallas_call(
        functools.partial(_route_body, cap=cap),
        out_shape=[jax.ShapeDtypeStruct((bsz, N_EXPERTS, SEQ), I32),
                   jax.ShapeDtypeStruct((bsz, N_EXPERTS, SEQ), F32),
                   jax.ShapeDtypeStruct((bsz, N_EXPERTS, LANES), I32)],
        grid=(bsz,),
        in_specs=[big],
        out_specs=[big, big, pl.BlockSpec((1, N_EXPERTS, LANES), lambda b: (b, 0, 0))],
        compiler_params=_cparams(("arbitrary",)),
        name="route",
    )(lt)


def _chunk_range(start, end):
    c_lo = start // SLOT_CHUNK
    n = jnp.where(end > start, (end - 1) // SLOT_CHUNK - c_lo + 1, 0)
    return c_lo, n


def _moe_body(tab_ref, posm_ref, gate_ref, h2_ref, wg_ref, wu_ref, wd_ref, y_ref,
              x_scr, acc_scr, gsl_scr, wgb_scr, wub_scr, wdb_scr, *, n_disp, cap):
    e = pl.program_id(0)
    s = pl.program_id(1)
    steps_per_batch = SEQ // MOE_TD
    tiles_per_step = MOE_TD // ROUTE_BLK
    chunks_per_batch = cap // SLOT_CHUNK

    @pl.when(s == 0)
    def _():
        x_scr[...] = jnp.zeros_like(x_scr)
        acc_scr[...] = jnp.zeros_like(acc_scr)
        gsl_scr[...] = jnp.zeros_like(gsl_scr)

    @pl.when(s < n_disp)
    def _():
        b = s // steps_per_batch
        tile0 = (s % steps_per_batch) * tiles_per_step
        tab_base = (b * N_EXPERTS + e) * (N_ROUTE_BLK + 1)
        slot = lax.broadcasted_iota(I32, (SLOT_CHUNK, ROUTE_BLK), 0)

        def add_chunk(sub, chunk):
            cols = slice(sub * ROUTE_BLK, (sub + 1) * ROUTE_BLK)
            base = chunk * SLOT_CHUNK
            onehot = _mask_bf16(posm_ref[0, :, cols] == slot + base)
            got = jnp.dot(onehot, h2_ref[cols, :], preferred_element_type=F32).astype(BF16)
            r0 = pl.multiple_of(b * cap + base, SLOT_CHUNK)
            x_scr[pl.ds(r0, SLOT_CHUNK), :] = x_scr[pl.ds(r0, SLOT_CHUNK), :] + got
            g3 = [p.astype(F32) for p in _split3(gate_ref[0, :, cols])]
            g8 = jnp.concatenate(g3 + [jnp.zeros((8 - len(g3), ROUTE_BLK), F32)], axis=0).astype(BF16)
            ci = b * chunks_per_batch + chunk
            gsl_scr[ci] = gsl_scr[ci] + lax.dot_general(g8, onehot, (((1,), (1,)), ((), ())),
                                                        preferred_element_type=F32)

        ranges = []
        for sub in range(tiles_per_step):
            ranges.append(_chunk_range(tab_ref[tab_base + tile0 + sub], tab_ref[tab_base + tile0 + sub + 1]))
        for sub, (c_lo, _) in enumerate(ranges):
            add_chunk(sub, jnp.minimum(c_lo, chunks_per_batch - 1))
        for sub, (c_lo, n) in enumerate(ranges):
            def more(k, carry, sub=sub, c_lo=c_lo):
                add_chunk(sub, c_lo + k)
                return carry

            lax.fori_loop(1, n, more, 0)

    def mlp(width):
        wgb_scr[:width, :] = wg_ref[0, :width, :].astype(BF16)
        wub_scr[:width, :] = wu_ref[0, :width, :].astype(BF16)
        wdb_scr[:width, :] = wd_ref[0, :width, :].astype(BF16)
        nt = (((1,), (1,)), ((), ()))

        def m_step(mi, carry):
            r0 = pl.multiple_of(mi * MOE_TM, MOE_TM)
            xm = x_scr[pl.ds(r0, MOE_TM), :]
            g = lax.dot_general(xm, wgb_scr[:width, :], nt, preferred_element_type=F32)
            u = lax.dot_general(xm, wub_scr[:width, :], nt, preferred_element_type=F32)
            hid = (g * jax.nn.sigmoid(g) * u).astype(BF16)
            acc_scr[pl.ds(r0, MOE_TM), :] = acc_scr[pl.ds(r0, MOE_TM), :] + jnp.dot(
                hid, wdb_scr[:width, :], preferred_element_type=F32)
            return carry

        lax.fori_loop(0, x_scr.shape[0] // MOE_TM, m_step, 0)

    @pl.when((s >= n_disp) & (s < n_disp + MOE_NF - 1))
    def _():
        mlp(MOE_TF)

    @pl.when(s == n_disp + MOE_NF - 1)
    def _():
        mlp(MOE_F_LAST)
        for ci in range(gsl_scr.shape[0]):
            g = jnp.sum(gsl_scr[ci].T, axis=1, keepdims=True)
            rows = slice(ci * SLOT_CHUNK, (ci + 1) * SLOT_CHUNK)
            y_ref[0, rows, :] = (acc_scr[rows, :] * g).astype(BF16)


def _moe_call(tab_flat, posm_rows, gate_rows, h2_flat, w_gate, w_up, w_down, *, bsz, cap):
    n_tok = h2_flat.shape[0]
    d = D_MODEL
    n_disp = n_tok // MOE_TD
    steps_per_batch = SEQ // MOE_TD
    n_steps = n_disp + MOE_NF
    slots = bsz * cap

    def fidx(s):
        return jnp.maximum(s - n_disp, 0)

    def didx(s):
        return jnp.minimum(s, n_disp - 1)

    sel_row = pl.BlockSpec((1, 1, MOE_TD),
                           lambda e, s, tab: ((didx(s) // steps_per_batch) * N_EXPERTS + e, 0,
                                              didx(s) % steps_per_batch))
    grid_spec = pltpu.PrefetchScalarGridSpec(
        num_scalar_prefetch=1,
        grid=(N_EXPERTS, n_steps),
        in_specs=[
            sel_row, sel_row,
            pl.BlockSpec((MOE_TD, d), lambda e, s, tab: (didx(s), 0)),
            pl.BlockSpec((1, MOE_TF, d), lambda e, s, tab: (e, fidx(s), 0)),
            pl.BlockSpec((1, MOE_TF, d), lambda e, s, tab: (e, fidx(s), 0)),
            pl.BlockSpec((1, MOE_TF, d), lambda e, s, tab: (e, fidx(s), 0)),
        ],
        out_specs=pl.BlockSpec((1, slots, d), lambda e, s, tab: (e, 0, 0)),
        scratch_shapes=[pltpu.VMEM((slots, d), BF16),
                        pltpu.VMEM((slots, d), F32),
                        pltpu.VMEM((slots // SLOT_CHUNK, 8, SLOT_CHUNK), F32),
                        pltpu.VMEM((MOE_TF, d), BF16),
                        pltpu.VMEM((MOE_TF, d), BF16),
                        pltpu.VMEM((MOE_TF, d), BF16)],
    )
    return pl.pallas_call(
        functools.partial(_moe_body, n_disp=n_disp, cap=cap),
        out_shape=jax.ShapeDtypeStruct((N_EXPERTS, slots, d), BF16),
        grid_spec=grid_spec,
        compiler_params=_cparams(("arbitrary", "arbitrary")),
        name="moe",
    )(tab_flat, posm_rows, gate_rows, h2_flat, w_gate, w_up, w_down)


def _combine_body(tab_ref, y_ref, posm_ref, *rest, e0, n_e, cap, last):
    if last:
        part_ref, x1_ref, gt2_ref, gpost_ref, o_ref, acc_ref = rest
    else:
        o_ref, acc_ref = rest
    b = pl.program_id(0)
    j = pl.program_id(1)
    pm = posm_ref[0].T
    win = 2 * SLOT_CHUNK
    slot_w = lax.broadcasted_iota(I32, (ROUTE_BLK, win), 1)
    slot_c = lax.broadcasted_iota(I32, (ROUTE_BLK, SLOT_CHUNK), 1)
    total = part_ref[0] if last else jnp.zeros((ROUTE_BLK, D_MODEL), F32)
    bases, extras = [], []
    for k in range(n_e):
        tab_base = (b * N_EXPERTS + e0 + k) * (N_ROUTE_BLK + 1)
        start = tab_ref[tab_base + j]
        end = tab_ref[tab_base + j + 1]
        base = pl.multiple_of(jnp.minimum((start // SLOT_CHUNK) * SLOT_CHUNK, cap - win), SLOT_CHUNK)
        total = total + jnp.dot(_mask_bf16(pm[:, k:k + 1] == slot_w + base), y_ref[k, pl.ds(base, win), :],
                                preferred_element_type=F32)
        bases.append(base)
        extras.append(jnp.maximum(end - (base + win) + SLOT_CHUNK - 1, 0) // SLOT_CHUNK)
    acc_ref[...] = total

    @pl.when(sum(extras) > 0)
    def _():
        for k in range(n_e):
            def extra_step(kk, carry, k=k):
                b2 = pl.multiple_of(bases[k] + win + kk * SLOT_CHUNK, SLOT_CHUNK)
                acc_ref[...] += jnp.dot(_mask_bf16(pm[:, k:k + 1] == slot_c + b2),
                                        y_ref[k, pl.ds(b2, SLOT_CHUNK), :], preferred_element_type=F32)
                return carry

            lax.fori_loop(0, extras[k], extra_step, 0)

    if last:
        o_ref[0] = x1_ref[0] + gt2_ref[0] * _rms(acc_ref[...], gpost_ref[...])
    else:
        o_ref[0] = acc_ref[...]


def _combine_call(tab_flat, y, posm, tail, *, e0, n_e, cap, last):
    bsz, _, s = posm.shape
    d = D_MODEL
    tok = pl.BlockSpec((1, ROUTE_BLK, d), lambda b, j, tab: (b, j, 0))
    sel = pl.BlockSpec((1, n_e, ROUTE_BLK), lambda b, j, tab: (b, e0 // n_e, j))
    in_specs = [pl.BlockSpec((n_e, cap, d), lambda b, j, tab: (e0 // n_e, b, 0)), sel]
    if last:
        in_specs += [tok, tok,
                     pl.BlockSpec((1, 1, d), lambda b, j, tab: (b, 0, 0)),
                     pl.BlockSpec((1, d), lambda b, j, tab: (0, 0))]
    grid_spec = pltpu.PrefetchScalarGridSpec(
        num_scalar_prefetch=1,
        grid=(bsz, s // ROUTE_BLK),
        in_specs=in_specs,
        out_specs=tok,
        scratch_shapes=[pltpu.VMEM((ROUTE_BLK, d), F32)],
    )
    return pl.pallas_call(
        functools.partial(_combine_body, e0=e0, n_e=n_e, cap=cap, last=last),
        out_shape=jax.ShapeDtypeStruct((bsz, s, d), F32),
        grid_spec=grid_spec,
        compiler_params=_cparams(("arbitrary", "arbitrary")),
        name="combine_last" if last else "combine_first",
    )(tab_flat, y, posm, *tail)


def kernel(x, c, ctx, c_ctx, w_mod, b_mod, g_pre1, g_post1, g_pre2, g_post2, w_in, w_out, na_rpb,
           hg_lb_logits, hg_norm, w_router, w_gate, w_up, w_down):
    bsz, seq, d = x.shape
    assert (seq, d) == (SEQ, D_MODEL) and ctx.shape[1] == CTX_LEN and w_mod.shape[0] == 1
    cap = 2 * seq // N_EXPERTS
    assert cap % SLOT_CHUNK == 0

    c_rows = jnp.zeros((8, d), F32).at[:bsz].set(c).at[bsz].set(c_ctx)
    mod = _mod_call(c_rows, w_mod[0], b_mod[0][None, :])
    sh1, sc1, gt1, sh2, sc2, gt2 = [m[:bsz, None, :] for m in jnp.split(mod, 6, axis=1)]
    sh1c, sc1c = mod[bsz:bsz + 1, None, :d], mod[bsz:bsz + 1, None, d:2 * d]

    lb_all = jnp.cumsum(jax.nn.softmax(hg_lb_logits.astype(F32), axis=0), axis=0)
    lb = lb_all[0][None, :]

    w_in_b = w_in[0].astype(BF16)
    w_out_b = w_out[0].astype(BF16)
    cos, sa, sb = _rope_tables(seq)

    qrot, qpl, krot, v, qh, zf, zb, ih, gh = _inproj_call(
        x, sh1, sc1, g_pre1[0][None, :], w_in_b, cos, sa, sb, rope=True, tm=IN_TM)
    ctx_flat = ctx.reshape(1, bsz * CTX_LEN, d)
    ctx_out = _inproj_call(ctx_flat, sh1c, sc1c, g_pre1[0][None, :], w_in_b,
                           cos[:bsz * CTX_LEN], sa[:bsz * CTX_LEN], sb[:bsz * CTX_LEN], rope=False, tm=CTX_LEN)
    _, _, kc, vc, qc, zfc, zbc, ic, _ = [a.reshape(bsz, CTX_LEN, COL_GROUP) for a in ctx_out]

    na_o = _na_call(qrot, qpl, krot, v, kc, vc, _na_bias(na_rpb[0]))
    hg_o = _hgrn_call(qh, zf, zb, ih, gh, qc, zfc, zbc, ic, lb, hg_norm[0][None, :])

    x1, h2, lt = _outproj_call(na_o, hg_o, x, w_out_b, gt1, sh2, sc2, g_post1[0][None, :],
                               g_pre2[0][None, :], w_router[0].T)

    posm, gate, tab = _route_call(lt, cap)
    tab_flat = tab[:, :, :N_ROUTE_BLK + 1].reshape(-1)
    y = _moe_call(tab_flat, posm.reshape(bsz * N_EXPERTS, 1, seq), gate.reshape(bsz * N_EXPERTS, 1, seq),
                  h2.reshape(bsz * seq, d), jnp.swapaxes(w_gate[0], 1, 2), jnp.swapaxes(w_up[0], 1, 2),
                  w_down[0], bsz=bsz, cap=cap)

    half = N_EXPERTS // 2
    part = _combine_call(tab_flat, y, posm, (), e0=0, n_e=half, cap=cap, last=False)
    return _combine_call(tab_flat, y, posm, (part, x1, gt2, g_post2[0][None, :]),
                         e0=half, n_e=half, cap=cap, last=True)
```

```python
import functools

import jax
import jax.numpy as jnp
import numpy as np
from jax import lax
from jax.experimental import pallas as pl
from jax.experimental.pallas import tpu as pltpu

F32 = jnp.float32
BF16 = jnp.bfloat16
I32 = jnp.int32
HIGHEST = lax.Precision.HIGHEST

D_MODEL = 1024
GRID_W = 64
N_ROWS = 128
SEQ = 8192
CTX_LEN = 256
NA_HEAD_DIM = 64
N_NA_HEADS = 8
NA_WIDTH = 512
NA_WIN_H = 8
NA_WIN_W = 16
NA_QBLOCK_W = 16
NA_KSPAN_W = 32
ROPE_BASE = 10000.0
HG_DK = 128
N_HG_HEADS = 4
HG_CHUNK = 32
N_EXPERTS = 16
D_EXPERT = 2752
RMS_EPS = 1e-6
COL_GROUP = 512

LANES = 128
VMEM_LIMIT_BYTES = 58 * 1024 * 1024

IN_TM = 512
NA_ROWS_PER_STEP = 8
NA_KEYS = NA_WIN_H * GRID_W
HG_BLOCK = 256
OUT_TM = 512
ROUTE_BLK = 256
N_ROUTE_BLK = SEQ // ROUTE_BLK
SLOT_CHUNK = 128
MOE_TD = 2048
MOE_TF = 768
MOE_NF = -(-D_EXPERT // MOE_TF)
MOE_F_LAST = D_EXPERT - (MOE_NF - 1) * MOE_TF
MOE_TM = 512


def _cparams(sem):
    return pltpu.CompilerParams(dimension_semantics=sem, vmem_limit_bytes=VMEM_LIMIT_BYTES)


def _mask_bf16(mask):
    return jnp.where(mask, 1.0, 0.0).astype(BF16)


def _rms(v, g):
    return v * lax.rsqrt(jnp.mean(v * v, axis=-1, keepdims=True) + RMS_EPS) * g


def _mod_body(c_ref, w_ref, b_ref, o_ref):
    cv = c_ref[...]
    s = cv * jax.nn.sigmoid(cv)
    o_ref[...] = jnp.dot(s, w_ref[...], precision=HIGHEST, preferred_element_type=F32) + b_ref[...]


def _mod_call(c_rows, w_mod, b_mod):
    d = D_MODEL
    n = w_mod.shape[1]
    tn = 1024
    return pl.pallas_call(
        _mod_body,
        out_shape=jax.ShapeDtypeStruct((8, n), F32),
        grid=(n // tn,),
        in_specs=[pl.BlockSpec((8, d), lambda j: (0, 0)),
                  pl.BlockSpec((d, tn), lambda j: (0, j)),
                  pl.BlockSpec((1, tn), lambda j: (0, j))],
        out_specs=pl.BlockSpec((8, tn), lambda j: (0, j)),
        compiler_params=_cparams(("arbitrary",)),
        name="mod",
    )(c_rows, w_mod, b_mod)


def _rope128(pg, cos, sa, sb):
    return pg * cos + pltpu.roll(pg, LANES - 16, axis=1) * sa + pltpu.roll(pg, 16, axis=1) * sb


def _inproj_body(x_ref, sh_ref, sc_ref, g_ref, w_ref, cos_ref, sa_ref, sb_ref,
                 qrot_ref, qpl_ref, krot_ref, v_ref, qh_ref, zf_ref, zb_ref, ih_ref, gh_ref, *, rope):
    xv = x_ref[0]
    h = _rms(xv, g_ref[...]) * (1.0 + sc_ref[0]) + sh_ref[0]
    hb = h.astype(BF16)

    def group(j):
        return jnp.dot(hb, w_ref[:, j * COL_GROUP:(j + 1) * COL_GROUP], preferred_element_type=F32)

    def rotated(p):
        if not rope:
            return p
        cos, sa, sb = cos_ref[...], sa_ref[...], sb_ref[...]
        return jnp.concatenate(
            [_rope128(p[:, k * LANES:(k + 1) * LANES], cos, sa, sb) for k in range(COL_GROUP // LANES)], axis=1)

    scale = NA_HEAD_DIM ** -0.5
    p = group(0)
    qpl_ref[0] = (p * scale).astype(BF16)
    qrot_ref[0] = (rotated(p) * scale).astype(BF16)
    krot_ref[0] = rotated(group(1)).astype(BF16)
    v_ref[0] = group(2).astype(BF16)
    qh_ref[0] = group(3).astype(BF16)
    zf_ref[0] = group(4)
    zb_ref[0] = group(5)
    ih_ref[0] = group(6).astype(BF16)
    gh_ref[0] = group(7).astype(BF16)


def _inproj_call(x3, shift, scale, g_pre, w_in_b, cos, sa, sb, *, rope, tm):
    g, t, d = x3.shape
    n_out = 9
    dts = [BF16, BF16, BF16, BF16, BF16, F32, F32, BF16, BF16]
    tok = pl.BlockSpec((1, tm, d), lambda b, i: (b, i, 0))
    vec = pl.BlockSpec((1, 1, d), lambda b, i: (b, 0, 0))
    tab = pl.BlockSpec((tm, LANES), lambda b, i: (i, 0))
    outs = [pl.BlockSpec((1, tm, COL_GROUP), lambda b, i: (b, i, 0)) for _ in range(n_out)]
    return pl.pallas_call(
        functools.partial(_inproj_body, rope=rope),
        out_shape=[jax.ShapeDtypeStruct((g, t, COL_GROUP), dt) for dt in dts],
        grid=(g, t // tm),
        in_specs=[tok, vec, vec,
                  pl.BlockSpec((1, d), lambda b, i: (0, 0)),
                  pl.BlockSpec(w_in_b.shape, lambda b, i: (0, 0)),
                  tab, tab, tab],
        out_specs=outs,
        compiler_params=_cparams(("arbitrary", "arbitrary")),
        name="inproj_rope" if rope else "inproj_ctx",
    )(x3, shift, scale, g_pre, w_in_b, cos, sa, sb)


def _rope_tables(n_tok):
    half = NA_HEAD_DIM // 2
    t = jnp.arange(n_tok)
    row = (t // GRID_W).astype(F32)
    col = (t % GRID_W).astype(F32)
    lane = jnp.arange(LANES)
    d = lane % NA_HEAD_DIM
    dd = d % half
    fi = dd % (half // 2)
    inv_freq = ROPE_BASE ** (-(2.0 * fi.astype(F32)) / half)
    pos = jnp.where((d < half)[None, :], row[:, None], col[:, None])
    ang = pos * inv_freq[None, :]
    first = (dd < half // 2)[None, :]
    cos = jnp.cos(ang)
    sin = jnp.sin(ang)
    return cos, jnp.where(first, -sin, 0.0), jnp.where(first, 0.0, sin)


def _na_body(qrot_ref, qpl_ref, k_ref, v_ref, kc_ref, vc_ref, bias_ref, o_ref):
    m = pl.program_id(2)
    lane = lax.broadcasted_iota(I32, (1, LANES), 1)
    first_head = lane < NA_HEAD_DIM
    hm0, hm1 = _mask_bf16(first_head), _mask_bf16(lane >= NA_HEAD_DIM)
    nt = (((1,), (1,)), ((), ()))
    rows = range(NA_ROWS_PER_STEP)

    def stacked(ref):
        parts = []
        for i in rows:
            q = ref[0, i * GRID_W:(i + 1) * GRID_W, :]
            parts += [q * hm0, q * hm1]
        return jnp.concatenate(parts, axis=0)

    sc = lax.dot_general(stacked(qpl_ref), kc_ref[0], nt, preferred_element_type=F32)
    mc = jnp.max(sc, axis=-1, keepdims=True)
    pc = jnp.exp(sc - mc)
    lc = jnp.sum(pc, axis=-1, keepdims=True)
    accc = jnp.dot(pc.astype(BF16), vc_ref[0], preferred_element_type=F32)

    qs = stacked(qrot_ref)
    k0s, s_parts = [], []
    blk = 2 * GRID_W
    for i in rows:
        r = m * NA_ROWS_PER_STEP + i
        r0 = jnp.clip(r - NA_WIN_H // 2, 0, N_ROWS - NA_WIN_H)
        di = r0 - r + (NA_WIN_H - 1)
        k0s.append(pl.multiple_of(r0 * GRID_W, GRID_W))
        s_parts.append(lax.dot_general(qs[i * blk:(i + 1) * blk], k_ref[0, pl.ds(k0s[i], NA_KEYS), :], nt,
                                       preferred_element_type=F32) + bias_ref[0, di].reshape(blk, NA_KEYS))
    s = jnp.concatenate(s_parts, axis=0)
    mw = jnp.max(s, axis=-1, keepdims=True)
    p = jnp.exp(s - mw)
    lw = jnp.sum(p, axis=-1, keepdims=True)
    pb = p.astype(BF16)
    accw = jnp.concatenate([jnp.dot(pb[i * blk:(i + 1) * blk], v_ref[0, pl.ds(k0s[i], NA_KEYS), :],
                                    preferred_element_type=F32) for i in rows], axis=0)
    mm = jnp.maximum(mw, mc)
    aw = jnp.exp(mw - mm)
    ac = jnp.exp(mc - mm)
    o = (aw * accw + ac * accc) / (aw * lw + ac * lc)
    o_ref[0] = jnp.concatenate([jnp.where(first_head, o[i * blk:i * blk + GRID_W], o[i * blk + GRID_W:(i + 1) * blk])
                                for i in rows], axis=0).astype(BF16)


def _na_bias(rpb):
    c = np.arange(GRID_W)
    cq = np.arange(GRID_W)
    win_c0 = np.clip(cq - NA_WIN_W // 2, 0, GRID_W - NA_WIN_W)
    in_win = (c[None, :] >= win_c0[:, None]) & (c[None, :] < win_c0[:, None] + NA_WIN_W)
    dc = np.clip(c[None, :] - cq[:, None] + NA_WIN_W - 1, 0, 2 * NA_WIN_W - 2)
    pick = (dc[None, :, :] == np.arange(2 * NA_WIN_W - 1)[:, None, None]).astype(np.float32)
    t = jnp.einsum('hrj,jqc->hqrc', rpb.astype(F32), jnp.asarray(pick), precision=HIGHEST)
    t = jnp.where(jnp.asarray(in_win)[None, :, None, :], t, -jnp.inf)
    n_dr = 2 * NA_WIN_H - 1
    t = t.reshape(N_NA_HEADS // 2, 2, GRID_W, n_dr * GRID_W)
    return jnp.stack([t[..., di * GRID_W:di * GRID_W + NA_KEYS] for di in range(NA_WIN_H)], axis=1)


def _na_call(qrot, qpl, krot, v, kc, vc, bias):
    bsz = qrot.shape[0]
    tq = NA_ROWS_PER_STEP * GRID_W
    qspec = pl.BlockSpec((1, tq, LANES), lambda b, p, m: (b, m, p))
    kspec = pl.BlockSpec((1, SEQ, LANES), lambda b, p, m: (b, 0, p))
    cspec = pl.BlockSpec((1, CTX_LEN, LANES), lambda b, p, m: (b, 0, p))
    return pl.pallas_call(
        _na_body,
        out_shape=jax.ShapeDtypeStruct((bsz, SEQ, NA_WIDTH), BF16),
        grid=(bsz, N_NA_HEADS // 2, N_ROWS // NA_ROWS_PER_STEP),
        in_specs=[qspec, qspec, kspec, kspec, cspec, cspec,
                  pl.BlockSpec((1, NA_WIN_H, 2, GRID_W, NA_KEYS), lambda b, p, m: (p, 0, 0, 0, 0))],
        out_specs=qspec,
        compiler_params=_cparams(("arbitrary", "arbitrary", "arbitrary")),
        name="na",
    )(qrot, qpl, krot, v, kc, vc, bias)


def _split3(v):
    hi = v.astype(BF16)
    r1 = v - hi.astype(F32)
    mid = r1.astype(BF16)
    lo = (r1 - mid.astype(F32)).astype(BF16)
    return hi, mid, lo


HG_NCHUNK = HG_BLOCK // HG_CHUNK


def _hg_consts(tri_ref, keep_ref, spread_ref):
    nb = HG_BLOCK
    rr = lax.broadcasted_iota(I32, (nb, nb), 0)
    cc = lax.broadcasted_iota(I32, (nb, nb), 1)
    same = (rr // HG_CHUNK) == (cc // HG_CHUNK)
    fwd = jnp.where(same & (cc <= rr), 1.0, 0.0)
    bwd = jnp.where(same & (cc >= rr), 1.0, 0.0)
    tri_ref[0] = fwd.astype(BF16)
    tri_ref[1] = bwd.astype(BF16)
    keep_ref[...] = jnp.concatenate([fwd, bwd], axis=0)
    r8 = lax.broadcasted_iota(I32, (nb, HG_NCHUNK * HG_DK), 0) // HG_CHUNK
    g8 = lax.broadcasted_iota(I32, (nb, HG_NCHUNK * HG_DK), 1) // HG_DK
    spread_ref[...] = _mask_bf16(r8 == g8)


def _hg_pair(zf, zb, qf, qb, vf, vb, stf, stb, lb, tri_ref, keep_ref, spread_ref, *, want_out):
    nb = HG_BLOCK
    spread = spread_ref[...]
    v = (vf, vb)

    sig = jax.nn.sigmoid(jnp.concatenate([zf, zb], axis=0))
    lf = jnp.log(lb + (1.0 - lb) * sig)
    kk = (1.0 - lb) * (1.0 - sig)
    parts = jnp.concatenate(_split3(lf), axis=1)
    sums = jnp.concatenate([jnp.dot(tri_ref[d], parts[d * nb:(d + 1) * nb], preferred_element_type=F32)
                            for d in range(2)], axis=0)
    bcum = sums[:, :HG_DK] + sums[:, HG_DK:2 * HG_DK] + sums[:, 2 * HG_DK:]
    end_rows = [d * nb + c * HG_CHUNK + (HG_CHUNK - 1 if d == 0 else 0) for d in range(2) for c in range(HG_NCHUNK)]
    ends = [bcum[r:r + 1, :] for r in end_rows]
    btot = jnp.concatenate([jnp.broadcast_to(e, (HG_CHUNK, HG_DK)) for e in ends], axis=0)
    dec = jnp.exp(jnp.concatenate(ends, axis=0))
    ke = (kk * jnp.exp(btot - bcum)).astype(BF16)

    new_states, entering = [], []
    for d, st in enumerate((stf, stb)):
        kvt = lax.dot_general(v[d], jnp.concatenate([ke[d * nb:(d + 1) * nb]] * HG_NCHUNK, axis=1) * spread,
                              (((0,), (0,)), ((), ())), preferred_element_type=F32)
        ent = [None] * HG_NCHUNK
        for c in (range(HG_NCHUNK) if d == 0 else range(HG_NCHUNK - 1, -1, -1)):
            ent[c] = st
            i_dec = d * HG_NCHUNK + c
            st = st * dec[i_dec:i_dec + 1, :] + kvt[:, c * HG_DK:(c + 1) * HG_DK]
        new_states.append(st)
        entering.append(ent)
    if not want_out:
        return None, None, new_states[0], new_states[1]

    qd = (jnp.concatenate([qf, qb], axis=0).astype(F32) * jnp.exp(bcum)).astype(BF16)
    kd = (kk * jnp.exp(-bcum)).astype(BF16)
    nt = (((1,), (1,)), ((), ()))
    a = jnp.concatenate([lax.dot_general(qd[d * nb:(d + 1) * nb], kd[d * nb:(d + 1) * nb], nt,
                                         preferred_element_type=F32) for d in range(2)], axis=0)
    a = jnp.where(keep_ref[...] > 0.5, a, 0.0).astype(BF16)
    outs = []
    for d in range(2):
        st_cat = jnp.concatenate([s.astype(BF16) for s in entering[d]], axis=1)
        qd_d = qd[d * nb:(d + 1) * nb]
        outs.append(jnp.dot(a[d * nb:(d + 1) * nb], v[d], preferred_element_type=F32)
                    + lax.dot_general(jnp.concatenate([qd_d] * HG_NCHUNK, axis=1) * spread, st_cat, nt,
                                      preferred_element_type=F32))
    return outs[0], outs[1], new_states[0], new_states[1]


def _hgrn_body(q_ref, zf_ref, zb_ref, i_ref, g_ref, qc_ref, zfc_ref, zbc_ref, ic_ref, lb_ref, ng_ref,
               o_ref, accf_ref, accb_ref, stf_ref, stb_ref, tri_ref, keep_ref, spread_ref):
    lb = lb_ref[...]
    _hg_consts(tri_ref, keep_ref, spread_ref)
    pair = functools.partial(_hg_pair, lb=lb, tri_ref=tri_ref, keep_ref=keep_ref, spread_ref=spread_ref)
    zero_state = jnp.zeros((HG_DK, HG_DK), F32)
    _, _, stf, stb = pair(zfc_ref[0], zbc_ref[0], qc_ref[0], qc_ref[0], ic_ref[0], ic_ref[0],
                          zero_state, zero_state, want_out=False)
    stf_ref[...] = stf
    stb_ref[...] = stb
    n_blk = SEQ // HG_BLOCK

    def scan_step(n, carry):
        rows_f = pl.ds(pl.multiple_of(n * HG_BLOCK, HG_BLOCK), HG_BLOCK)
        rows_b = pl.ds(pl.multiple_of((n_blk - 1 - n) * HG_BLOCK, HG_BLOCK), HG_BLOCK)
        o_f, o_b, st_f, st_b = pair(zf_ref[0, rows_f, :], zb_ref[0, rows_b, :], q_ref[0, rows_f, :],
                                    q_ref[0, rows_b, :], i_ref[0, rows_f, :], i_ref[0, rows_b, :],
                                    stf_ref[...], stb_ref[...], want_out=True)
        stf_ref[...] = st_f
        stb_ref[...] = st_b
        accf_ref[rows_f, :] = o_f
        accb_ref[rows_b, :] = o_b
        return carry

    lax.fori_loop(0, n_blk, scan_step, 0, unroll=2)

    def readout_step(n, carry):
        rows = pl.ds(pl.multiple_of(n * HG_BLOCK, HG_BLOCK), HG_BLOCK)
        tot = accf_ref[rows, :] + accb_ref[rows, :]
        y = _rms(tot, ng_ref[...]) * jax.nn.sigmoid(g_ref[0, rows, :].astype(F32))
        o_ref[0, rows, :] = y.astype(BF16)
        return carry

    lax.fori_loop(0, n_blk, readout_step, 0, unroll=2)


def _hgrn_call(qh, zf, zb, ih, gh, qc, zfc, zbc, ic, lb, norm_g):
    bsz = qh.shape[0]
    seq = pl.BlockSpec((1, SEQ, HG_DK), lambda b, h: (b, 0, h))
    ctx = pl.BlockSpec((1, CTX_LEN, HG_DK), lambda b, h: (b, 0, h))
    return pl.pallas_call(
        _hgrn_body,
        out_shape=jax.ShapeDtypeStruct((bsz, SEQ, N_HG_HEADS * HG_DK), BF16),
        grid=(bsz, N_HG_HEADS),
        in_specs=[seq, seq, seq, seq, seq, ctx, ctx, ctx, ctx,
                  pl.BlockSpec((1, HG_DK), lambda b, h: (0, h)),
                  pl.BlockSpec((1, HG_DK), lambda b, h: (0, 0))],
        out_specs=seq,
        scratch_shapes=[pltpu.VMEM((SEQ, HG_DK), F32),
                        pltpu.VMEM((SEQ, HG_DK), F32),
                        pltpu.VMEM((HG_DK, HG_DK), F32),
                        pltpu.VMEM((HG_DK, HG_DK), F32),
                        pltpu.VMEM((2, HG_BLOCK, HG_BLOCK), BF16),
                        pltpu.VMEM((2 * HG_BLOCK, HG_BLOCK), F32),
                        pltpu.VMEM((HG_BLOCK, HG_NCHUNK * HG_DK), BF16)],
        compiler_params=_cparams(("arbitrary", "arbitrary")),
        name="hgrn",
    )(qh, zf, zb, ih, gh, qc, zfc, zbc, ic, lb, norm_g)


def _outproj_body(na_ref, hg_ref, x_ref, w_ref, gt1_ref, sh2_ref, sc2_ref, gpost_ref, gpre_ref, wr_ref,
                  x1_ref, h2_ref, lt_ref):
    mix = (jnp.dot(na_ref[0], w_ref[:NA_WIDTH, :], preferred_element_type=F32)
           + jnp.dot(hg_ref[0], w_ref[NA_WIDTH:, :], preferred_element_type=F32))
    x1 = x_ref[0] + gt1_ref[0] * _rms(mix, gpost_ref[...])
    x1_ref[0] = x1
    h2 = _rms(x1, gpre_ref[...]) * (1.0 + sc2_ref[0]) + sh2_ref[0]
    h2_ref[0] = h2.astype(BF16)
    lt_ref[0] = lax.dot_general(wr_ref[...], h2, (((1,), (1,)), ((), ())), precision=HIGHEST,
                                preferred_element_type=F32)


def _outproj_call(na_o, hg_o, x, w_out_b, gt1, sh2, sc2, g_post1, g_pre2, w_router_t):
    bsz, s, d = x.shape
    tm = OUT_TM
    half = pl.BlockSpec((1, tm, NA_WIDTH), lambda b, i: (b, i, 0))
    tok = pl.BlockSpec((1, tm, d), lambda b, i: (b, i, 0))
    vec = pl.BlockSpec((1, 1, d), lambda b, i: (b, 0, 0))
    par = pl.BlockSpec((1, d), lambda b, i: (0, 0))
    return pl.pallas_call(
        _outproj_body,
        out_shape=[jax.ShapeDtypeStruct((bsz, s, d), F32),
                   jax.ShapeDtypeStruct((bsz, s, d), BF16),
                   jax.ShapeDtypeStruct((bsz, N_EXPERTS, s), F32)],
        grid=(bsz, s // tm),
        in_specs=[half, half, tok, pl.BlockSpec((d, d), lambda b, i: (0, 0)), vec, vec, vec, par, par,
                  pl.BlockSpec((N_EXPERTS, d), lambda b, i: (0, 0))],
        out_specs=[tok, tok, pl.BlockSpec((1, N_EXPERTS, tm), lambda b, i: (b, 0, i))],
        compiler_params=_cparams(("arbitrary", "arbitrary")),
        name="outproj",
    )(na_o, hg_o, x, w_out_b, gt1, sh2, sc2, g_post1, g_pre2, w_router_t)


def _route_body(lt_ref, posm_ref, gate_ref, tab_ref, *, cap):
    l = lt_ref[0]
    mx = jnp.max(l, axis=0, keepdims=True)
    ex = jnp.exp(l - mx)
    aff = ex / jnp.sum(ex, axis=0, keepdims=True)
    capf = jnp.float32(cap)

    def count(mask):
        return jnp.sum(jnp.where(mask, 1.0, 0.0), axis=1, keepdims=True)

    def enough(v):
        return count(aff >= v) >= capf

    def bit_step(it, thr):
        cand = thr | (jnp.int32(1) << (30 - it))
        return jnp.where(enough(pltpu.bitcast(cand, F32)), cand, thr)

    thr = lax.fori_loop(0, 31, bit_step, jnp.zeros((N_EXPERTS, 1), I32))
    lo = pltpu.bitcast(thr, F32)
    hi = pltpu.bitcast(thr + 1, F32)

    def mid_step(it, lohi):
        lo, hi = lohi
        mid = 0.5 * (lo + hi)
        ok = enough(mid)
        return jnp.where(ok, mid, lo), jnp.where(ok, hi, mid)

    lo, hi = lax.fori_loop(0, 30, mid_step, (lo, hi))
    gt = aff >= hi
    eq = (aff >= lo) & jnp.logical_not(gt)
    need = capf - count(gt)

    rr = lax.broadcasted_iota(I32, (ROUTE_BLK, ROUTE_BLK), 0)
    cc = lax.broadcasted_iota(I32, (ROUTE_BLK, ROUTE_BLK), 1)
    upper = _mask_bf16(rr <= cc)
    lane = lax.broadcasted_iota(I32, (N_EXPERTS, LANES), 1)

    off_eq = jnp.zeros((N_EXPERTS, 1), F32)
    off_sel = jnp.zeros((N_EXPERTS, 1), F32)
    tab = jnp.zeros((N_EXPERTS, LANES), F32)
    for j in range(N_ROUTE_BLK):
        sl = slice(j * ROUTE_BLK, (j + 1) * ROUTE_BLK)
        eq_j = eq[:, sl]
        eq_b = _mask_bf16(eq_j)
        incl_eq = jnp.dot(eq_b, upper, preferred_element_type=F32) + off_eq
        rank_eq = incl_eq - eq_b.astype(F32)
        sel_j = gt[:, sl] | (eq_j & (rank_eq < need))
        sel_b = _mask_bf16(sel_j)
        incl_sel = jnp.dot(sel_b, upper, preferred_element_type=F32) + off_sel
        pos = incl_sel - sel_b.astype(F32)
        posm_ref[0, :, sl] = jnp.where(sel_j, pos.astype(I32), -1)
        gate_ref[0, :, sl] = jnp.where(sel_j, aff[:, sl], 0.0)
        tab = jnp.where(lane == j, off_sel, tab)
        off_eq = incl_eq[:, ROUTE_BLK - 1:ROUTE_BLK]
        off_sel = incl_sel[:, ROUTE_BLK - 1:ROUTE_BLK]
    tab = jnp.where(lane == N_ROUTE_BLK, off_sel, tab)
    tab_ref[0] = tab.astype(I32)


def _route_call(lt, cap):
    bsz = lt.shape[0]
    big = pl.BlockSpec((1, N_EXPERTS, SEQ), lambda b: (b, 0, 0))
    return pl.pallas_call(
        functools.partial(_route_body, cap=cap),
        out_shape=[jax.ShapeDtypeStruct((bsz, N_EXPERTS, SEQ), I32),
                   jax.ShapeDtypeStruct((bsz, N_EXPERTS, SEQ), F32),
                   jax.ShapeDtypeStruct((bsz, N_EXPERTS, LANES), I32)],
        grid=(bsz,),
        in_specs=[big],
        out_specs=[big, big, pl.BlockSpec((1, N_EXPERTS, LANES), lambda b: (b, 0, 0))],
        compiler_params=_cparams(("arbitrary",)),
        name="route",
    )(lt)


def _chunk_range(start, end):
    c_lo = start // SLOT_CHUNK
    n = jnp.where(end > start, (end - 1) // SLOT_CHUNK - c_lo + 1, 0)
    return c_lo, n


DISP_NE = 8


def _dispatch_body(tab_ref, posm_ref, gate_ref, h2_ref, x_ref, gsl_ref, *, cap):
    b = pl.program_id(0)
    hf = pl.program_id(1)
    blk = pl.program_id(2)
    tiles_per_step = MOE_TD // ROUTE_BLK
    chunks_per_batch = cap // SLOT_CHUNK

    @pl.when(blk == 0)
    def _():
        x_ref[...] = jnp.zeros_like(x_ref)
        gsl_ref[...] = jnp.zeros_like(gsl_ref)

    slot = lax.broadcasted_iota(I32, (SLOT_CHUNK, ROUTE_BLK), 0)

    def expert_step(k, carry):
        tab_base = (b * N_EXPERTS + hf * DISP_NE + k) * (N_ROUTE_BLK + 1) + blk * tiles_per_step

        def add_chunk(sub, chunk):
            cols = slice(sub * ROUTE_BLK, (sub + 1) * ROUTE_BLK)
            base = pl.multiple_of(chunk * SLOT_CHUNK, SLOT_CHUNK)
            onehot = _mask_bf16(posm_ref[0, pl.ds(k, 1), cols] == slot + base)
            got = jnp.dot(onehot, h2_ref[cols, :], preferred_element_type=F32).astype(BF16)
            x_ref[k, pl.ds(base, SLOT_CHUNK), :] = x_ref[k, pl.ds(base, SLOT_CHUNK), :] + got
            g3 = [p.astype(F32) for p in _split3(gate_ref[0, pl.ds(k, 1), cols])]
            g8 = jnp.concatenate(g3 + [jnp.zeros((8 - len(g3), ROUTE_BLK), F32)], axis=0).astype(BF16)
            gsl_ref[k, chunk] = gsl_ref[k, chunk] + lax.dot_general(g8, onehot, (((1,), (1,)), ((), ())),
                                                                    preferred_element_type=F32)

        ranges = [_chunk_range(tab_ref[tab_base + sub], tab_ref[tab_base + sub + 1]) for sub in range(tiles_per_step)]
        for sub, (c_lo, _) in enumerate(ranges):
            add_chunk(sub, jnp.minimum(c_lo, chunks_per_batch - 1))
        for sub, (c_lo, n) in enumerate(ranges):
            def more(kk, c, sub=sub, c_lo=c_lo):
                add_chunk(sub, c_lo + kk)
                return c

            lax.fori_loop(1, n, more, 0)
        return carry

    lax.fori_loop(0, DISP_NE, expert_step, 0)


def _dispatch_call(tab_flat, posm, gate, h2_flat, *, bsz, cap):
    d = D_MODEL
    n_blk = SEQ // MOE_TD
    chunks = cap // SLOT_CHUNK
    sel = pl.BlockSpec((1, DISP_NE, MOE_TD), lambda b, hf, blk, tab: (b, hf, blk))
    grid_spec = pltpu.PrefetchScalarGridSpec(
        num_scalar_prefetch=1,
        grid=(bsz, N_EXPERTS // DISP_NE, n_blk),
        in_specs=[sel, sel, pl.BlockSpec((MOE_TD, d), lambda b, hf, blk, tab: (b * n_blk + blk, 0))],
        out_specs=[pl.BlockSpec((DISP_NE, cap, d), lambda b, hf, blk, tab: (hf, b, 0)),
                   pl.BlockSpec((DISP_NE, chunks, 8, SLOT_CHUNK), lambda b, hf, blk, tab: (hf, b, 0, 0))],
    )
    return pl.pallas_call(
        functools.partial(_dispatch_body, cap=cap),
        out_shape=[jax.ShapeDtypeStruct((N_EXPERTS, bsz * cap, d), BF16),
                   jax.ShapeDtypeStruct((N_EXPERTS, bsz * chunks, 8, SLOT_CHUNK), F32)],
        grid_spec=grid_spec,
        compiler_params=_cparams(("arbitrary", "arbitrary", "arbitrary")),
        name="dispatch",
    )(tab_flat, posm, gate, h2_flat)


def _moe_body(x_ref, gsl_ref, wg_ref, wu_ref, wd_ref, y_ref, acc_scr):
    s = pl.program_id(1)

    @pl.when(s == 0)
    def _():
        acc_scr[...] = jnp.zeros_like(acc_scr)

    def mlp(width):
        nt = (((1,), (1,)), ((), ()))

        def m_step(mi, carry):
            r0 = pl.multiple_of(mi * MOE_TM, MOE_TM)
            xm = x_ref[0, pl.ds(r0, MOE_TM), :]
            g = lax.dot_general(xm, wg_ref[0, :width, :].astype(BF16), nt, preferred_element_type=F32)
            u = lax.dot_general(xm, wu_ref[0, :width, :].astype(BF16), nt, preferred_element_type=F32)
            hid = (g * jax.nn.sigmoid(g) * u).astype(BF16)
            acc_scr[pl.ds(r0, MOE_TM), :] = acc_scr[pl.ds(r0, MOE_TM), :] + jnp.dot(
                hid, wd_ref[0, :width, :].astype(BF16), preferred_element_type=F32)
            return carry

        lax.fori_loop(0, acc_scr.shape[0] // MOE_TM, m_step, 0)

    @pl.when(s < MOE_NF - 1)
    def _():
        mlp(MOE_TF)

    @pl.when(s == MOE_NF - 1)
    def _():
        mlp(MOE_F_LAST)
        for ci in range(gsl_ref.shape[1]):
            g = jnp.sum(gsl_ref[0, ci].T, axis=1, keepdims=True)
            rows = slice(ci * SLOT_CHUNK, (ci + 1) * SLOT_CHUNK)
            y_ref[0, rows, :] = (acc_scr[rows, :] * g).astype(BF16)


def _moe_call(xs, gsl, w_gate_t, w_up_t, w_down):
    n_e, slots, d = xs.shape
    per_e = pl.BlockSpec((1, slots, d), lambda e, s: (e, 0, 0))
    wspec = pl.BlockSpec((1, MOE_TF, d), lambda e, s: (e, s, 0))
    return pl.pallas_call(
        _moe_body,
        out_shape=jax.ShapeDtypeStruct((n_e, slots, d), BF16),
        grid=(n_e, MOE_NF),
        in_specs=[per_e, pl.BlockSpec((1,) + gsl.shape[1:], lambda e, s: (e, 0, 0, 0)), wspec, wspec, wspec],
        out_specs=per_e,
        scratch_shapes=[pltpu.VMEM((slots, d), F32)],
        compiler_params=_cparams(("arbitrary", "arbitrary")),
        name="moe",
    )(xs, gsl, w_gate_t, w_up_t, w_down)


def _combine_body(tab_ref, y_ref, posm_ref, x1_ref, gt2_ref, gpost_ref, o_ref, acc_ref, *, cap):
    n_e = N_EXPERTS
    b = pl.program_id(0)
    j = pl.program_id(1)
    pm = posm_ref[0].T
    win = 2 * SLOT_CHUNK
    slot_w = lax.broadcasted_iota(I32, (ROUTE_BLK, win), 1)
    slot_c = lax.broadcasted_iota(I32, (ROUTE_BLK, SLOT_CHUNK), 1)
    total = jnp.zeros((ROUTE_BLK, D_MODEL), F32)
    bases, extras = [], []
    for k in range(n_e):
        tab_base = (b * N_EXPERTS + k) * (N_ROUTE_BLK + 1)
        start = tab_ref[tab_base + j]
        end = tab_ref[tab_base + j + 1]
        base = pl.multiple_of(jnp.minimum((start // SLOT_CHUNK) * SLOT_CHUNK, cap - win), SLOT_CHUNK)
        total = total + jnp.dot(_mask_bf16(pm[:, k:k + 1] == slot_w + base), y_ref[k, pl.ds(base, win), :],
                                preferred_element_type=F32)
        bases.append(base)
        extras.append(jnp.maximum(end - (base + win) + SLOT_CHUNK - 1, 0) // SLOT_CHUNK)
    acc_ref[...] = total

    @pl.when(sum(extras) > 0)
    def _():
        for k in range(n_e):
            def extra_step(kk, carry, k=k):
                b2 = pl.multiple_of(bases[k] + win + kk * SLOT_CHUNK, SLOT_CHUNK)
                acc_ref[...] += jnp.dot(_mask_bf16(pm[:, k:k + 1] == slot_c + b2),
                                        y_ref[k, pl.ds(b2, SLOT_CHUNK), :], preferred_element_type=F32)
                return carry

            lax.fori_loop(0, extras[k], extra_step, 0)

    o_ref[0] = x1_ref[0] + gt2_ref[0] * _rms(acc_ref[...], gpost_ref[...])


def _combine_call(tab_flat, y, posm, x1, gt2, g_post2, *, cap):
    bsz, n_e, s = posm.shape
    d = D_MODEL
    tok = pl.BlockSpec((1, ROUTE_BLK, d), lambda b, j, tab: (b, j, 0))
    grid_spec = pltpu.PrefetchScalarGridSpec(
        num_scalar_prefetch=1,
        grid=(bsz, s // ROUTE_BLK),
        in_specs=[
            pl.BlockSpec((n_e, cap, d), lambda b, j, tab: (0, b, 0), pipeline_mode=pl.Buffered(1)),
            pl.BlockSpec((1, n_e, ROUTE_BLK), lambda b, j, tab: (b, 0, j)),
            tok,
            pl.BlockSpec((1, 1, d), lambda b, j, tab: (b, 0, 0)),
            pl.BlockSpec((1, d), lambda b, j, tab: (0, 0))],
        out_specs=tok,
        scratch_shapes=[pltpu.VMEM((ROUTE_BLK, d), F32)],
    )
    return pl.pallas_call(
        functools.partial(_combine_body, cap=cap),
        out_shape=jax.ShapeDtypeStruct((bsz, s, d), F32),
        grid_spec=grid_spec,
        compiler_params=_cparams(("arbitrary", "arbitrary")),
        name="combine",
    )(tab_flat, y, posm, x1, gt2, g_post2)


def kernel(x, c, ctx, c_ctx, w_mod, b_mod, g_pre1, g_post1, g_pre2, g_post2, w_in, w_out, na_rpb,
           hg_lb_logits, hg_norm, w_router, w_gate, w_up, w_down):
    bsz, seq, d = x.shape
    assert (seq, d) == (SEQ, D_MODEL) and ctx.shape[1] == CTX_LEN and w_mod.shape[0] == 1
    cap = 2 * seq // N_EXPERTS
    assert cap % SLOT_CHUNK == 0

    c_rows = jnp.zeros((8, d), F32).at[:bsz].set(c).at[bsz].set(c_ctx)
    mod = _mod_call(c_rows, w_mod[0], b_mod[0][None, :])
    sh1, sc1, gt1, sh2, sc2, gt2 = [m[:bsz, None, :] for m in jnp.split(mod, 6, axis=1)]
    sh1c, sc1c = mod[bsz:bsz + 1, None, :d], mod[bsz:bsz + 1, None, d:2 * d]

    lb_all = jnp.cumsum(jax.nn.softmax(hg_lb_logits.astype(F32), axis=0), axis=0)
    lb = lb_all[0][None, :]

    w_in_b = w_in[0].astype(BF16)
    w_out_b = w_out[0].astype(BF16)
    cos, sa, sb = _rope_tables(seq)

    qrot, qpl, krot, v, qh, zf, zb, ih, gh = _inproj_call(
        x, sh1, sc1, g_pre1[0][None, :], w_in_b, cos, sa, sb, rope=True, tm=IN_TM)
    ctx_flat = ctx.reshape(1, bsz * CTX_LEN, d)
    ctx_out = _inproj_call(ctx_flat, sh1c, sc1c, g_pre1[0][None, :], w_in_b,
                           cos[:bsz * CTX_LEN], sa[:bsz * CTX_LEN], sb[:bsz * CTX_LEN], rope=False, tm=CTX_LEN)
    _, _, kc, vc, qc, zfc, zbc, ic, _ = [a.reshape(bsz, CTX_LEN, COL_GROUP) for a in ctx_out]

    na_o = _na_call(qrot, qpl, krot, v, kc, vc, _na_bias(na_rpb[0]))
    hg_o = _hgrn_call(qh, zf, zb, ih, gh, qc, zfc, zbc, ic, lb, hg_norm[0][None, :])

    x1, h2, lt = _outproj_call(na_o, hg_o, x, w_out_b, gt1, sh2, sc2, g_post1[0][None, :],
                               g_pre2[0][None, :], w_router[0].T)

    posm, gate, tab = _route_call(lt, cap)
    tab_flat = tab[:, :, :N_ROUTE_BLK + 1].reshape(-1)
    xs, gsl = _dispatch_call(tab_flat, posm, gate, h2.reshape(bsz * seq, d), bsz=bsz, cap=cap)
    y = _moe_call(xs, gsl, jnp.swapaxes(w_gate[0], 1, 2), jnp.swapaxes(w_up[0], 1, 2), w_down[0])

    return _combine_call(tab_flat, y, posm, x1, gt2, g_post2[0][None, :], cap=cap)
```

```python
import functools

import jax
import jax.numpy as jnp
import numpy as np
from jax import lax
from jax.experimental import pallas as pl
from jax.experimental.pallas import tpu as pltpu

F32 = jnp.float32
BF16 = jnp.bfloat16
I32 = jnp.int32
HIGHEST = lax.Precision.HIGHEST

D_MODEL = 1024
GRID_W = 64
N_ROWS = 128
SEQ = 8192
CTX_LEN = 256
NA_HEAD_DIM = 64
N_NA_HEADS = 8
NA_WIDTH = 512
NA_WIN_H = 8
NA_WIN_W = 16
NA_QBLOCK_W = 16
NA_KSPAN_W = 32
ROPE_BASE = 10000.0
HG_DK = 128
N_HG_HEADS = 4
HG_CHUNK = 32
N_EXPERTS = 16
D_EXPERT = 2752
RMS_EPS = 1e-6
COL_GROUP = 512

LANES = 128
VMEM_LIMIT_BYTES = 58 * 1024 * 1024

IN_TM = 512
NA_ROWS_PER_STEP = 8
NA_KEYS = NA_WIN_H * GRID_W
HG_BLOCK = 256
OUT_TM = 512
ROUTE_BLK = 256
N_ROUTE_BLK = SEQ // ROUTE_BLK
SLOT_CHUNK = 128
MOE_TD = 2048
MOE_TF = 768
MOE_NF = -(-D_EXPERT // MOE_TF)
MOE_F_LAST = D_EXPERT - (MOE_NF - 1) * MOE_TF
MOE_TM = 1024


def _cparams(sem):
    return pltpu.CompilerParams(dimension_semantics=sem, vmem_limit_bytes=VMEM_LIMIT_BYTES)


def _mask_bf16(mask):
    return jnp.where(mask, 1.0, 0.0).astype(BF16)


def _rms(v, g):
    return v * lax.rsqrt(jnp.mean(v * v, axis=-1, keepdims=True) + RMS_EPS) * g


def _mod_body(c_ref, w_ref, b_ref, o_ref):
    cv = c_ref[...]
    s = cv * jax.nn.sigmoid(cv)
    o_ref[...] = jnp.dot(s, w_ref[...], precision=HIGHEST, preferred_element_type=F32) + b_ref[...]


def _mod_call(c_rows, w_mod, b_mod):
    d = D_MODEL
    n = w_mod.shape[1]
    tn = 1024
    return pl.pallas_call(
        _mod_body,
        out_shape=jax.ShapeDtypeStruct((8, n), F32),
        grid=(n // tn,),
        in_specs=[pl.BlockSpec((8, d), lambda j: (0, 0)),
                  pl.BlockSpec((d, tn), lambda j: (0, j)),
                  pl.BlockSpec((1, tn), lambda j: (0, j))],
        out_specs=pl.BlockSpec((8, tn), lambda j: (0, j)),
        compiler_params=_cparams(("arbitrary",)),
        name="mod",
    )(c_rows, w_mod, b_mod)


def _rope128(pg, cos, sa, sb):
    return pg * cos + pltpu.roll(pg, LANES - 16, axis=1) * sa + pltpu.roll(pg, 16, axis=1) * sb


def _inproj_body(x_ref, sh_ref, sc_ref, g_ref, w_ref, cos_ref, sa_ref, sb_ref,
                 qrot_ref, qpl_ref, krot_ref, v_ref, qh_ref, zf_ref, zb_ref, ih_ref, gh_ref, *, rope):
    xv = x_ref[0]
    h = _rms(xv, g_ref[...]) * (1.0 + sc_ref[0]) + sh_ref[0]
    hb = h.astype(BF16)

    def group(j):
        return jnp.dot(hb, w_ref[:, j * COL_GROUP:(j + 1) * COL_GROUP], preferred_element_type=F32)

    def rotated(p):
        if not rope:
            return p
        cos, sa, sb = cos_ref[...], sa_ref[...], sb_ref[...]
        return jnp.concatenate(
            [_rope128(p[:, k * LANES:(k + 1) * LANES], cos, sa, sb) for k in range(COL_GROUP // LANES)], axis=1)

    scale = NA_HEAD_DIM ** -0.5
    p = group(0)
    qpl_ref[0] = (p * scale).astype(BF16)
    qrot_ref[0] = (rotated(p) * scale).astype(BF16)
    krot_ref[0] = rotated(group(1)).astype(BF16)
    v_ref[0] = group(2).astype(BF16)
    qh_ref[0] = group(3).astype(BF16)
    zf_ref[0] = group(4)
    zb_ref[0] = group(5)
    ih_ref[0] = group(6).astype(BF16)
    gh_ref[0] = group(7).astype(BF16)


def _inproj_call(x3, shift, scale, g_pre, w_in_b, cos, sa, sb, *, rope, tm):
    g, t, d = x3.shape
    n_out = 9
    dts = [BF16, BF16, BF16, BF16, BF16, F32, F32, BF16, BF16]
    tok = pl.BlockSpec((1, tm, d), lambda b, i: (b, i, 0))
    vec = pl.BlockSpec((1, 1, d), lambda b, i: (b, 0, 0))
    tab = pl.BlockSpec((tm, LANES), lambda b, i: (i, 0))
    outs = [pl.BlockSpec((1, tm, COL_GROUP), lambda b, i: (b, i, 0)) for _ in range(n_out)]
    return pl.pallas_call(
        functools.partial(_inproj_body, rope=rope),
        out_shape=[jax.ShapeDtypeStruct((g, t, COL_GROUP), dt) for dt in dts],
        grid=(g, t // tm),
        in_specs=[tok, vec, vec,
                  pl.BlockSpec((1, d), lambda b, i: (0, 0)),
                  pl.BlockSpec(w_in_b.shape, lambda b, i: (0, 0)),
                  tab, tab, tab],
        out_specs=outs,
        compiler_params=_cparams(("arbitrary", "arbitrary")),
        name="inproj_rope" if rope else "inproj_ctx",
    )(x3, shift, scale, g_pre, w_in_b, cos, sa, sb)


def _rope_tables(n_tok):
    half = NA_HEAD_DIM // 2
    t = np.arange(n_tok)
    row = (t // GRID_W).astype(np.float32)
    col = (t % GRID_W).astype(np.float32)
    lane = np.arange(LANES)
    d = lane % NA_HEAD_DIM
    dd = d % half
    fi = dd % (half // 2)
    inv_freq = (ROPE_BASE ** (-(2.0 * fi.astype(np.float32)) / half)).astype(np.float32)
    pos = np.where((d < half)[None, :], row[:, None], col[:, None])
    ang = (pos * inv_freq[None, :]).astype(np.float32)
    first = (dd < half // 2)[None, :]
    cos = np.cos(ang.astype(np.float64)).astype(np.float32)
    sin = np.sin(ang.astype(np.float64)).astype(np.float32)
    zero = np.zeros_like(sin)
    return jnp.asarray(cos), jnp.asarray(np.where(first, -sin, zero)), jnp.asarray(np.where(first, zero, sin))


def _na_body(qrot_ref, qpl_ref, k_ref, v_ref, kc_ref, vc_ref, bias_ref, o_ref):
    m = pl.program_id(2)
    lane = lax.broadcasted_iota(I32, (1, LANES), 1)
    first_head = lane < NA_HEAD_DIM
    hm0, hm1 = _mask_bf16(first_head), _mask_bf16(lane >= NA_HEAD_DIM)
    nt = (((1,), (1,)), ((), ()))
    rows = range(NA_ROWS_PER_STEP)

    def stacked(ref):
        parts = []
        for i in rows:
            q = ref[0, i * GRID_W:(i + 1) * GRID_W, :]
            parts += [q * hm0, q * hm1]
        return jnp.concatenate(parts, axis=0)

    sc = lax.dot_general(stacked(qpl_ref), kc_ref[0], nt, preferred_element_type=F32)

    qs = stacked(qrot_ref)
    k0s, s_parts = [], []
    blk = 2 * GRID_W
    for i in rows:
        r = m * NA_ROWS_PER_STEP + i
        r0 = jnp.clip(r - NA_WIN_H // 2, 0, N_ROWS - NA_WIN_H)
        di = r0 - r + (NA_WIN_H - 1)
        k0s.append(pl.multiple_of(r0 * GRID_W, GRID_W))
        s_parts.append(lax.dot_general(qs[i * blk:(i + 1) * blk], k_ref[0, pl.ds(k0s[i], NA_KEYS), :], nt,
                                       preferred_element_type=F32) + bias_ref[0, di].reshape(blk, NA_KEYS))
    s = jnp.concatenate([jnp.concatenate(s_parts, axis=0), sc], axis=1)
    p = jnp.exp(s - jnp.max(s, axis=-1, keepdims=True))
    den = jnp.sum(p, axis=-1, keepdims=True)
    pb = p.astype(BF16)
    acc = jnp.concatenate([jnp.dot(pb[i * blk:(i + 1) * blk, :NA_KEYS], v_ref[0, pl.ds(k0s[i], NA_KEYS), :],
                                   preferred_element_type=F32) for i in rows], axis=0)
    o = (acc + jnp.dot(pb[:, NA_KEYS:], vc_ref[0], preferred_element_type=F32)) / den
    o_ref[0] = jnp.concatenate([jnp.where(first_head, o[i * blk:i * blk + GRID_W], o[i * blk + GRID_W:(i + 1) * blk])
                                for i in rows], axis=0).astype(BF16)


def _na_bias(rpb):
    c = np.arange(GRID_W)
    cq = np.arange(GRID_W)
    win_c0 = np.clip(cq - NA_WIN_W // 2, 0, GRID_W - NA_WIN_W)
    in_win = (c[None, :] >= win_c0[:, None]) & (c[None, :] < win_c0[:, None] + NA_WIN_W)
    dc = np.clip(c[None, :] - cq[:, None] + NA_WIN_W - 1, 0, 2 * NA_WIN_W - 2)
    pick = (dc[None, :, :] == np.arange(2 * NA_WIN_W - 1)[:, None, None]).astype(np.float32)
    t = jnp.einsum('hrj,jqc->hqrc', rpb.astype(F32), jnp.asarray(pick), precision=HIGHEST)
    t = jnp.where(jnp.asarray(in_win)[None, :, None, :], t, -jnp.inf)
    n_dr = 2 * NA_WIN_H - 1
    t = t.reshape(N_NA_HEADS // 2, 2, GRID_W, n_dr * GRID_W)
    return jnp.stack([t[..., di * GRID_W:di * GRID_W + NA_KEYS] for di in range(NA_WIN_H)], axis=1)


def _na_call(qrot, qpl, krot, v, kc, vc, bias):
    bsz = qrot.shape[0]
    tq = NA_ROWS_PER_STEP * GRID_W
    qspec = pl.BlockSpec((1, tq, LANES), lambda b, p, m: (b, m, p))
    kspec = pl.BlockSpec((1, SEQ, LANES), lambda b, p, m: (b, 0, p))
    cspec = pl.BlockSpec((1, CTX_LEN, LANES), lambda b, p, m: (b, 0, p))
    return pl.pallas_call(
        _na_body,
        out_shape=jax.ShapeDtypeStruct((bsz, SEQ, NA_WIDTH), BF16),
        grid=(bsz, N_NA_HEADS // 2, N_ROWS // NA_ROWS_PER_STEP),
        in_specs=[qspec, qspec, kspec, kspec, cspec, cspec,
                  pl.BlockSpec((1, NA_WIN_H, 2, GRID_W, NA_KEYS), lambda b, p, m: (p, 0, 0, 0, 0))],
        out_specs=qspec,
        compiler_params=_cparams(("arbitrary", "arbitrary", "arbitrary")),
        name="na",
    )(qrot, qpl, krot, v, kc, vc, bias)


def _split3(v):
    hi = v.astype(BF16)
    r1 = v - hi.astype(F32)
    mid = r1.astype(BF16)
    lo = (r1 - mid.astype(F32)).astype(BF16)
    return hi, mid, lo


HG_NCHUNK = HG_BLOCK // HG_CHUNK


def _hg_consts(tri_ref, keep_ref, spread_ref):
    nb = HG_BLOCK
    rr = lax.broadcasted_iota(I32, (nb, nb), 0)
    cc = lax.broadcasted_iota(I32, (nb, nb), 1)
    same = (rr // HG_CHUNK) == (cc // HG_CHUNK)
    fwd = jnp.where(same & (cc <= rr), 1.0, 0.0)
    bwd = jnp.where(same & (cc >= rr), 1.0, 0.0)
    tri_ref[0] = fwd.astype(BF16)
    tri_ref[1] = bwd.astype(BF16)
    keep_ref[...] = jnp.concatenate([fwd, bwd], axis=0)
    r8 = lax.broadcasted_iota(I32, (nb, HG_NCHUNK * HG_DK), 0) // HG_CHUNK
    g8 = lax.broadcasted_iota(I32, (nb, HG_NCHUNK * HG_DK), 1) // HG_DK
    spread_ref[...] = _mask_bf16(r8 == g8)


def _hg_pair(zf, zb, qf, qb, vf, vb, stf, stb, lb, tri_ref, keep_ref, spread_ref, *, want_out):
    nb = HG_BLOCK
    spread = spread_ref[...]
    v = (vf, vb)

    sig = jax.nn.sigmoid(jnp.concatenate([zf, zb], axis=0))
    lf = jnp.log(lb + (1.0 - lb) * sig)
    kk = (1.0 - lb) * (1.0 - sig)
    parts = jnp.concatenate(_split3(lf), axis=1)
    sums = jnp.concatenate([jnp.dot(tri_ref[d], parts[d * nb:(d + 1) * nb], preferred_element_type=F32)
                            for d in range(2)], axis=0)
    bcum = sums[:, :HG_DK] + sums[:, HG_DK:2 * HG_DK] + sums[:, 2 * HG_DK:]
    end_rows = [d * nb + c * HG_CHUNK + (HG_CHUNK - 1 if d == 0 else 0) for d in range(2) for c in range(HG_NCHUNK)]
    ends = [bcum[r:r + 1, :] for r in end_rows]
    btot = jnp.concatenate([jnp.broadcast_to(e, (HG_CHUNK, HG_DK)) for e in ends], axis=0)
    dec = jnp.exp(jnp.concatenate(ends, axis=0))
    ke = (kk * jnp.exp(btot - bcum)).astype(BF16)

    new_states, entering = [], []
    for d, st in enumerate((stf, stb)):
        kvt = lax.dot_general(v[d], jnp.concatenate([ke[d * nb:(d + 1) * nb]] * HG_NCHUNK, axis=1) * spread,
                              (((0,), (0,)), ((), ())), preferred_element_type=F32)
        ent = [None] * HG_NCHUNK
        for c in (range(HG_NCHUNK) if d == 0 else range(HG_NCHUNK - 1, -1, -1)):
            ent[c] = st
            i_dec = d * HG_NCHUNK + c
            st = st * dec[i_dec:i_dec + 1, :] + kvt[:, c * HG_DK:(c + 1) * HG_DK]
        new_states.append(st)
        entering.append(ent)
    if not want_out:
        return None, None, new_states[0], new_states[1]

    qd = (jnp.concatenate([qf, qb], axis=0).astype(F32) * jnp.exp(bcum)).astype(BF16)
    kd = (kk * jnp.exp(-bcum)).astype(BF16)
    nt = (((1,), (1,)), ((), ()))
    a = jnp.concatenate([lax.dot_general(qd[d * nb:(d + 1) * nb], kd[d * nb:(d + 1) * nb], nt,
                                         preferred_element_type=F32) for d in range(2)], axis=0)
    a = jnp.where(keep_ref[...] > 0.5, a, 0.0).astype(BF16)
    outs = []
    for d in range(2):
        qd_d = qd[d * nb:(d + 1) * nb]
        inter = jnp.concatenate(
            [lax.dot_general(qd_d[c * HG_CHUNK:(c + 1) * HG_CHUNK], entering[d][c].astype(BF16), nt,
                             preferred_element_type=F32) for c in range(HG_NCHUNK)], axis=0)
        outs.append(jnp.dot(a[d * nb:(d + 1) * nb], v[d], preferred_element_type=F32) + inter)
    return outs[0], outs[1], new_states[0], new_states[1]


def _hgrn_body(q_ref, zf_ref, zb_ref, i_ref, g_ref, qc_ref, zfc_ref, zbc_ref, ic_ref, lb_ref, ng_ref,
               o_ref, accf_ref, accb_ref, stf_ref, stb_ref, tri_ref, keep_ref, spread_ref):
    lb = lb_ref[...]
    _hg_consts(tri_ref, keep_ref, spread_ref)
    pair = functools.partial(_hg_pair, lb=lb, tri_ref=tri_ref, keep_ref=keep_ref, spread_ref=spread_ref)
    zero_state = jnp.zeros((HG_DK, HG_DK), F32)
    _, _, stf, stb = pair(zfc_ref[0], zbc_ref[0], qc_ref[0], qc_ref[0], ic_ref[0], ic_ref[0],
                          zero_state, zero_state, want_out=False)
    stf_ref[...] = stf
    stb_ref[...] = stb
    n_blk = SEQ // HG_BLOCK

    def scan_step(n, carry):
        rows_f = pl.ds(pl.multiple_of(n * HG_BLOCK, HG_BLOCK), HG_BLOCK)
        rows_b = pl.ds(pl.multiple_of((n_blk - 1 - n) * HG_BLOCK, HG_BLOCK), HG_BLOCK)
        o_f, o_b, st_f, st_b = pair(zf_ref[0, rows_f, :], zb_ref[0, rows_b, :], q_ref[0, rows_f, :],
                                    q_ref[0, rows_b, :], i_ref[0, rows_f, :], i_ref[0, rows_b, :],
                                    stf_ref[...], stb_ref[...], want_out=True)
        stf_ref[...] = st_f
        stb_ref[...] = st_b
        accf_ref[rows_f, :] = o_f
        accb_ref[rows_b, :] = o_b
        return carry

    lax.fori_loop(0, n_blk, scan_step, 0, unroll=2)

    def readout_step(n, carry):
        rows = pl.ds(pl.multiple_of(n * HG_BLOCK, HG_BLOCK), HG_BLOCK)
        tot = accf_ref[rows, :] + accb_ref[rows, :]
        y = _rms(tot, ng_ref[...]) * jax.nn.sigmoid(g_ref[0, rows, :].astype(F32))
        o_ref[0, rows, :] = y.astype(BF16)
        return carry

    lax.fori_loop(0, n_blk, readout_step, 0, unroll=2)


def _hgrn_call(qh, zf, zb, ih, gh, qc, zfc, zbc, ic, lb, norm_g):
    bsz = qh.shape[0]
    seq = pl.BlockSpec((1, SEQ, HG_DK), lambda b, h: (b, 0, h))
    ctx = pl.BlockSpec((1, CTX_LEN, HG_DK), lambda b, h: (b, 0, h))
    return pl.pallas_call(
        _hgrn_body,
        out_shape=jax.ShapeDtypeStruct((bsz, SEQ, N_HG_HEADS * HG_DK), BF16),
        grid=(bsz, N_HG_HEADS),
        in_specs=[seq, seq, seq, seq, seq, ctx, ctx, ctx, ctx,
                  pl.BlockSpec((1, HG_DK), lambda b, h: (0, h)),
                  pl.BlockSpec((1, HG_DK), lambda b, h: (0, 0))],
        out_specs=seq,
        scratch_shapes=[pltpu.VMEM((SEQ, HG_DK), F32),
                        pltpu.VMEM((SEQ, HG_DK), F32),
                        pltpu.VMEM((HG_DK, HG_DK), F32),
                        pltpu.VMEM((HG_DK, HG_DK), F32),
                        pltpu.VMEM((2, HG_BLOCK, HG_BLOCK), BF16),
                        pltpu.VMEM((2 * HG_BLOCK, HG_BLOCK), F32),
                        pltpu.VMEM((HG_BLOCK, HG_NCHUNK * HG_DK), BF16)],
        compiler_params=_cparams(("arbitrary", "arbitrary")),
        name="hgrn",
    )(qh, zf, zb, ih, gh, qc, zfc, zbc, ic, lb, norm_g)


def _outproj_body(na_ref, hg_ref, x_ref, w_ref, gt1_ref, sh2_ref, sc2_ref, gpost_ref, gpre_ref, wr_ref,
                  x1_ref, h2_ref, lt_ref):
    mix = (jnp.dot(na_ref[0], w_ref[:NA_WIDTH, :], preferred_element_type=F32)
           + jnp.dot(hg_ref[0], w_ref[NA_WIDTH:, :], preferred_element_type=F32))
    x1 = x_ref[0] + gt1_ref[0] * _rms(mix, gpost_ref[...])
    x1_ref[0] = x1
    h2 = _rms(x1, gpre_ref[...]) * (1.0 + sc2_ref[0]) + sh2_ref[0]
    h2_ref[0] = h2.astype(BF16)
    lt_ref[0] = lax.dot_general(wr_ref[...], h2, (((1,), (1,)), ((), ())), precision=HIGHEST,
                                preferred_element_type=F32)


def _outproj_call(na_o, hg_o, x, w_out_b, gt1, sh2, sc2, g_post1, g_pre2, w_router_t):
    bsz, s, d = x.shape
    tm = OUT_TM
    half = pl.BlockSpec((1, tm, NA_WIDTH), lambda b, i: (b, i, 0))
    tok = pl.BlockSpec((1, tm, d), lambda b, i: (b, i, 0))
    vec = pl.BlockSpec((1, 1, d), lambda b, i: (b, 0, 0))
    par = pl.BlockSpec((1, d), lambda b, i: (0, 0))
    return pl.pallas_call(
        _outproj_body,
        out_shape=[jax.ShapeDtypeStruct((bsz, s, d), F32),
                   jax.ShapeDtypeStruct((bsz, s, d), BF16),
                   jax.ShapeDtypeStruct((bsz, N_EXPERTS, s), F32)],
        grid=(bsz, s // tm),
        in_specs=[half, half, tok, pl.BlockSpec((d, d), lambda b, i: (0, 0)), vec, vec, vec, par, par,
                  pl.BlockSpec((N_EXPERTS, d), lambda b, i: (0, 0))],
        out_specs=[tok, tok, pl.BlockSpec((1, N_EXPERTS, tm), lambda b, i: (b, 0, i))],
        compiler_params=_cparams(("arbitrary", "arbitrary")),
        name="outproj",
    )(na_o, hg_o, x, w_out_b, gt1, sh2, sc2, g_post1, g_pre2, w_router_t)


def _route_body(lt_ref, posm_ref, gate_ref, tab_ref, *, cap):
    l = lt_ref[0]
    mx = jnp.max(l, axis=0, keepdims=True)
    ex = jnp.exp(l - mx)
    aff = ex / jnp.sum(ex, axis=0, keepdims=True)
    capf = jnp.float32(cap)

    def count(mask):
        return jnp.sum(jnp.where(mask, 1.0, 0.0), axis=1, keepdims=True)

    def enough(v):
        return count(aff >= v) >= capf

    def bit_step(it, thr):
        cand = thr | (jnp.int32(1) << (30 - it))
        return jnp.where(enough(pltpu.bitcast(cand, F32)), cand, thr)

    thr = lax.fori_loop(0, 31, bit_step, jnp.zeros((N_EXPERTS, 1), I32))
    lo = pltpu.bitcast(thr, F32)
    hi = pltpu.bitcast(thr + 1, F32)

    def mid_step(it, lohi):
        lo, hi = lohi
        mid = 0.5 * (lo + hi)
        ok = enough(mid)
        return jnp.where(ok, mid, lo), jnp.where(ok, hi, mid)

    lo, hi = lax.fori_loop(0, 30, mid_step, (lo, hi))
    gt = aff >= hi
    eq = (aff >= lo) & jnp.logical_not(gt)
    need = capf - count(gt)

    rr = lax.broadcasted_iota(I32, (ROUTE_BLK, ROUTE_BLK), 0)
    cc = lax.broadcasted_iota(I32, (ROUTE_BLK, ROUTE_BLK), 1)
    upper = _mask_bf16(rr <= cc)
    lane = lax.broadcasted_iota(I32, (N_EXPERTS, LANES), 1)

    off_eq = jnp.zeros((N_EXPERTS, 1), F32)
    off_sel = jnp.zeros((N_EXPERTS, 1), F32)
    tab = jnp.zeros((N_EXPERTS, LANES), F32)
    for j in range(N_ROUTE_BLK):
        sl = slice(j * ROUTE_BLK, (j + 1) * ROUTE_BLK)
        eq_j = eq[:, sl]
        eq_b = _mask_bf16(eq_j)
        incl_eq = jnp.dot(eq_b, upper, preferred_element_type=F32) + off_eq
        rank_eq = incl_eq - eq_b.astype(F32)
        sel_j = gt[:, sl] | (eq_j & (rank_eq < need))
        sel_b = _mask_bf16(sel_j)
        incl_sel = jnp.dot(sel_b, upper, preferred_element_type=F32) + off_sel
        pos = incl_sel - sel_b.astype(F32)
        posm_ref[0, :, sl] = jnp.where(sel_j, pos.astype(I32), -1)
        gate_ref[0, :, sl] = jnp.where(sel_j, aff[:, sl], 0.0)
        tab = jnp.where(lane == j, off_sel, tab)
        off_eq = incl_eq[:, ROUTE_BLK - 1:ROUTE_BLK]
        off_sel = incl_sel[:, ROUTE_BLK - 1:ROUTE_BLK]
    tab = jnp.where(lane == N_ROUTE_BLK, off_sel, tab)
    tab_ref[0] = tab.astype(I32)


def _route_call(lt, cap):
    bsz = lt.shape[0]
    big = pl.BlockSpec((1, N_EXPERTS, SEQ), lambda b: (b, 0, 0))
    return pl.pallas_call(
        functools.partial(_route_body, cap=cap),
        out_shape=[jax.ShapeDtypeStruct((bsz, N_EXPERTS, SEQ), I32),
                   jax.ShapeDtypeStruct((bsz, N_EXPERTS, SEQ), F32),
                   jax.ShapeDtypeStruct((bsz, N_EXPERTS, LANES), I32)],
        grid=(bsz,),
        in_specs=[big],
        out_specs=[big, big, pl.BlockSpec((1, N_EXPERTS, LANES), lambda b: (b, 0, 0))],
        compiler_params=_cparams(("arbitrary",)),
        name="route",
    )(lt)


def _chunk_range(start, end):
    c_lo = start // SLOT_CHUNK
    n = jnp.where(end > start, (end - 1) // SLOT_CHUNK - c_lo + 1, 0)
    return c_lo, n


DISP_NE = 8


def _dispatch_body(tab_ref, posm_ref, gate_ref, h2_ref, x_ref, gsl_ref, *, cap):
    b = pl.program_id(0)
    hf = pl.program_id(1)
    blk = pl.program_id(2)
    tiles_per_step = MOE_TD // ROUTE_BLK
    chunks_per_batch = cap // SLOT_CHUNK

    @pl.when(blk == 0)
    def _():
        x_ref[...] = jnp.zeros_like(x_ref)
        gsl_ref[...] = jnp.zeros_like(gsl_ref)

    slot = lax.broadcasted_iota(I32, (SLOT_CHUNK, ROUTE_BLK), 0)

    def expert_step(k, carry):
        tab_base = (b * N_EXPERTS + hf * DISP_NE + k) * (N_ROUTE_BLK + 1) + blk * tiles_per_step

        def add_chunk(sub, chunk):
            cols = slice(sub * ROUTE_BLK, (sub + 1) * ROUTE_BLK)
            base = pl.multiple_of(chunk * SLOT_CHUNK, SLOT_CHUNK)
            onehot = _mask_bf16(posm_ref[0, pl.ds(k, 1), cols] == slot + base)
            got = jnp.dot(onehot, h2_ref[cols, :], preferred_element_type=F32).astype(BF16)
            x_ref[k, pl.ds(base, SLOT_CHUNK), :] = x_ref[k, pl.ds(base, SLOT_CHUNK), :] + got
            g3 = [p.astype(F32) for p in _split3(gate_ref[0, pl.ds(k, 1), cols])]
            g8 = jnp.concatenate(g3 + [jnp.zeros((8 - len(g3), ROUTE_BLK), F32)], axis=0).astype(BF16)
            gsl_ref[k, chunk] = gsl_ref[k, chunk] + lax.dot_general(g8, onehot, (((1,), (1,)), ((), ())),
                                                                    preferred_element_type=F32)

        ranges = [_chunk_range(tab_ref[tab_base + sub], tab_ref[tab_base + sub + 1]) for sub in range(tiles_per_step)]
        for sub, (c_lo, _) in enumerate(ranges):
            add_chunk(sub, jnp.minimum(c_lo, chunks_per_batch - 1))
        for sub, (c_lo, n) in enumerate(ranges):
            def more(kk, c, sub=sub, c_lo=c_lo):
                add_chunk(sub, c_lo + kk)
                return c

            lax.fori_loop(1, n, more, 0)
        return carry

    lax.fori_loop(0, DISP_NE, expert_step, 0)


def _dispatch_call(tab_flat, posm, gate, h2_flat, *, bsz, cap):
    d = D_MODEL
    n_blk = SEQ // MOE_TD
    chunks = cap // SLOT_CHUNK
    sel = pl.BlockSpec((1, DISP_NE, MOE_TD), lambda b, hf, blk, tab: (b, hf, blk))
    grid_spec = pltpu.PrefetchScalarGridSpec(
        num_scalar_prefetch=1,
        grid=(bsz, N_EXPERTS // DISP_NE, n_blk),
        in_specs=[sel, sel, pl.BlockSpec((MOE_TD, d), lambda b, hf, blk, tab: (b * n_blk + blk, 0))],
        out_specs=[pl.BlockSpec((DISP_NE, cap, d), lambda b, hf, blk, tab: (hf, b, 0)),
                   pl.BlockSpec((DISP_NE, chunks, 8, SLOT_CHUNK), lambda b, hf, blk, tab: (hf, b, 0, 0))],
    )
    return pl.pallas_call(
        functools.partial(_dispatch_body, cap=cap),
        out_shape=[jax.ShapeDtypeStruct((N_EXPERTS, bsz * cap, d), BF16),
                   jax.ShapeDtypeStruct((N_EXPERTS, bsz * chunks, 8, SLOT_CHUNK), F32)],
        grid_spec=grid_spec,
        compiler_params=_cparams(("arbitrary", "arbitrary", "arbitrary")),
        name="dispatch",
    )(tab_flat, posm, gate, h2_flat)


def _moe_body(x_ref, gsl_ref, wg_ref, wu_ref, wd_ref, y_ref, acc_scr):
    s = pl.program_id(1)

    @pl.when(s == 0)
    def _():
        acc_scr[...] = jnp.zeros_like(acc_scr)

    def mlp(width):
        nt = (((1,), (1,)), ((), ()))

        def m_step(mi, carry):
            r0 = pl.multiple_of(mi * MOE_TM, MOE_TM)
            xm = x_ref[0, pl.ds(r0, MOE_TM), :]
            g = lax.dot_general(xm, wg_ref[0, :width, :].astype(BF16), nt, preferred_element_type=F32)
            u = lax.dot_general(xm, wu_ref[0, :width, :].astype(BF16), nt, preferred_element_type=F32)
            hid = (g * jax.nn.sigmoid(g) * u).astype(BF16)
            acc_scr[pl.ds(r0, MOE_TM), :] = acc_scr[pl.ds(r0, MOE_TM), :] + jnp.dot(
                hid, wd_ref[0, :width, :].astype(BF16), preferred_element_type=F32)
            return carry

        lax.fori_loop(0, acc_scr.shape[0] // MOE_TM, m_step, 0)

    @pl.when(s < MOE_NF - 1)
    def _():
        mlp(MOE_TF)

    @pl.when(s == MOE_NF - 1)
    def _():
        mlp(MOE_F_LAST)
        for ci in range(gsl_ref.shape[1]):
            g = jnp.sum(gsl_ref[0, ci].T, axis=1, keepdims=True)
            rows = slice(ci * SLOT_CHUNK, (ci + 1) * SLOT_CHUNK)
            y_ref[0, rows, :] = (acc_scr[rows, :] * g).astype(BF16)


def _moe_call(xs, gsl, w_gate_t, w_up_t, w_down):
    n_e, slots, d = xs.shape
    per_e = pl.BlockSpec((1, slots, d), lambda e, s: (e, 0, 0))
    wspec = pl.BlockSpec((1, MOE_TF, d), lambda e, s: (e, s, 0))
    return pl.pallas_call(
        _moe_body,
        out_shape=jax.ShapeDtypeStruct((n_e, slots, d), BF16),
        grid=(n_e, MOE_NF),
        in_specs=[per_e, pl.BlockSpec((1,) + gsl.shape[1:], lambda e, s: (e, 0, 0, 0)), wspec, wspec, wspec],
        out_specs=per_e,
        scratch_shapes=[pltpu.VMEM((slots, d), F32)],
        compiler_params=_cparams(("arbitrary", "arbitrary")),
        name="moe",
    )(xs, gsl, w_gate_t, w_up_t, w_down)


def _combine_body(tab_ref, y_ref, posm_ref, x1_ref, gt2_ref, gpost_ref, o_ref, acc_ref, *, cap):
    n_e = N_EXPERTS
    b = pl.program_id(0)
    j = pl.program_id(1)
    pm = posm_ref[0].T
    win = 2 * SLOT_CHUNK
    slot_w = lax.broadcasted_iota(I32, (ROUTE_BLK, win), 1)
    slot_c = lax.broadcasted_iota(I32, (ROUTE_BLK, SLOT_CHUNK), 1)
    total = jnp.zeros((ROUTE_BLK, D_MODEL), F32)
    bases, extras = [], []
    for k in range(n_e):
        tab_base = (b * N_EXPERTS + k) * (N_ROUTE_BLK + 1)
        start = tab_ref[tab_base + j]
        end = tab_ref[tab_base + j + 1]
        base = pl.multiple_of(jnp.minimum((start // SLOT_CHUNK) * SLOT_CHUNK, cap - win), SLOT_CHUNK)
        total = total + jnp.dot(_mask_bf16(pm[:, k:k + 1] == slot_w + base), y_ref[k, pl.ds(base, win), :],
                                preferred_element_type=F32)
        bases.append(base)
        extras.append(jnp.maximum(end - (base + win) + SLOT_CHUNK - 1, 0) // SLOT_CHUNK)
    acc_ref[...] = total

    @pl.when(sum(extras) > 0)
    def _():
        for k in range(n_e):
            def extra_step(kk, carry, k=k):
                b2 = pl.multiple_of(bases[k] + win + kk * SLOT_CHUNK, SLOT_CHUNK)
                acc_ref[...] += jnp.dot(_mask_bf16(pm[:, k:k + 1] == slot_c + b2),
                                        y_ref[k, pl.ds(b2, SLOT_CHUNK), :], preferred_element_type=F32)
                return carry

            lax.fori_loop(0, extras[k], extra_step, 0)

    o_ref[0] = x1_ref[0] + gt2_ref[0] * _rms(acc_ref[...], gpost_ref[...])


def _combine_call(tab_flat, y, posm, x1, gt2, g_post2, *, cap):
    bsz, n_e, s = posm.shape
    d = D_MODEL
    tok = pl.BlockSpec((1, ROUTE_BLK, d), lambda b, j, tab: (b, j, 0))
    grid_spec = pltpu.PrefetchScalarGridSpec(
        num_scalar_prefetch=1,
        grid=(bsz, s // ROUTE_BLK),
        in_specs=[
            pl.BlockSpec((n_e, cap, d), lambda b, j, tab: (0, b, 0), pipeline_mode=pl.Buffered(1)),
            pl.BlockSpec((1, n_e, ROUTE_BLK), lambda b, j, tab: (b, 0, j)),
            tok,
            pl.BlockSpec((1, 1, d), lambda b, j, tab: (b, 0, 0)),
            pl.BlockSpec((1, d), lambda b, j, tab: (0, 0))],
        out_specs=tok,
        scratch_shapes=[pltpu.VMEM((ROUTE_BLK, d), F32)],
    )
    return pl.pallas_call(
        functools.partial(_combine_body, cap=cap),
        out_shape=jax.ShapeDtypeStruct((bsz, s, d), F32),
        grid_spec=grid_spec,
        compiler_params=_cparams(("arbitrary", "arbitrary")),
        name="combine",
    )(tab_flat, y, posm, x1, gt2, g_post2)


def kernel(x, c, ctx, c_ctx, w_mod, b_mod, g_pre1, g_post1, g_pre2, g_post2, w_in, w_out, na_rpb,
           hg_lb_logits, hg_norm, w_router, w_gate, w_up, w_down):
    bsz, seq, d = x.shape
    assert (seq, d) == (SEQ, D_MODEL) and ctx.shape[1] == CTX_LEN and w_mod.shape[0] == 1
    cap = 2 * seq // N_EXPERTS
    assert cap % SLOT_CHUNK == 0

    c_rows = jnp.zeros((8, d), F32).at[:bsz].set(c).at[bsz].set(c_ctx)
    mod = _mod_call(c_rows, w_mod[0], b_mod[0][None, :])
    sh1, sc1, gt1, sh2, sc2, gt2 = [m[:bsz, None, :] for m in jnp.split(mod, 6, axis=1)]
    sh1c, sc1c = mod[bsz:bsz + 1, None, :d], mod[bsz:bsz + 1, None, d:2 * d]

    lb_all = jnp.cumsum(jax.nn.softmax(hg_lb_logits.astype(F32), axis=0), axis=0)
    lb = lb_all[0][None, :]

    w_in_b = w_in[0].astype(BF16)
    w_out_b = w_out[0].astype(BF16)
    cos, sa, sb = _rope_tables(seq)

    qrot, qpl, krot, v, qh, zf, zb, ih, gh = _inproj_call(
        x, sh1, sc1, g_pre1[0][None, :], w_in_b, cos, sa, sb, rope=True, tm=IN_TM)
    ctx_flat = ctx.reshape(1, bsz * CTX_LEN, d)
    ctx_out = _inproj_call(ctx_flat, sh1c, sc1c, g_pre1[0][None, :], w_in_b,
                           cos[:bsz * CTX_LEN], sa[:bsz * CTX_LEN], sb[:bsz * CTX_LEN], rope=False, tm=CTX_LEN)
    _, _, kc, vc, qc, zfc, zbc, ic, _ = [a.reshape(bsz, CTX_LEN, COL_GROUP) for a in ctx_out]

    na_o = _na_call(qrot, qpl, krot, v, kc, vc, _na_bias(na_rpb[0]))
    hg_o = _hgrn_call(qh, zf, zb, ih, gh, qc, zfc, zbc, ic, lb, hg_norm[0][None, :])

    x1, h2, lt = _outproj_call(na_o, hg_o, x, w_out_b, gt1, sh2, sc2, g_post1[0][None, :],
                               g_pre2[0][None, :], w_router[0].T)

    posm, gate, tab = _route_call(lt, cap)
    tab_flat = tab[:, :, :N_ROUTE_BLK + 1].reshape(-1)
    xs, gsl = _dispatch_call(tab_flat, posm, gate, h2.reshape(bsz * seq, d), bsz=bsz, cap=cap)
    y = _moe_call(xs, gsl, jnp.swapaxes(w_gate[0], 1, 2), jnp.swapaxes(w_up[0], 1, 2), w_down[0])

    return _combine_call(tab_flat, y, posm, x1, gt2, g_post2[0][None, :], cap=cap)
```

```python
import functools

import jax
import jax.numpy as jnp
import numpy as np
from jax import lax
from jax.experimental import pallas as pl
from jax.experimental.pallas import tpu as pltpu

F32 = jnp.float32
BF16 = jnp.bfloat16
I32 = jnp.int32
HIGHEST = lax.Precision.HIGHEST

D_MODEL = 1024
GRID_W = 64
N_ROWS = 128
SEQ = 8192
CTX_LEN = 256
NA_HEAD_DIM = 64
N_NA_HEADS = 8
NA_WIDTH = 512
NA_WIN_H = 8
NA_WIN_W = 16
NA_QBLOCK_W = 16
NA_KSPAN_W = 32
ROPE_BASE = 10000.0
HG_DK = 128
N_HG_HEADS = 4
HG_CHUNK = 32
N_EXPERTS = 16
D_EXPERT = 2752
RMS_EPS = 1e-6
COL_GROUP = 512

LANES = 128
VMEM_LIMIT_BYTES = 58 * 1024 * 1024

IN_TM = 512
NA_ROWS_PER_STEP = 16
NA_KEYS = NA_WIN_H * GRID_W
HG_BLOCK = 256
OUT_TM = 512
ROUTE_BLK = 256
N_ROUTE_BLK = SEQ // ROUTE_BLK
SLOT_CHUNK = 128
MOE_TD = 2048
MOE_TF = 768
MOE_NF = -(-D_EXPERT // MOE_TF)
MOE_F_LAST = D_EXPERT - (MOE_NF - 1) * MOE_TF
MOE_TM = 1024


def _cparams(sem):
    return pltpu.CompilerParams(dimension_semantics=sem, vmem_limit_bytes=VMEM_LIMIT_BYTES)


def _mask_bf16(mask):
    return jnp.where(mask, 1.0, 0.0).astype(BF16)


def _rms(v, g):
    return v * lax.rsqrt(jnp.mean(v * v, axis=-1, keepdims=True) + RMS_EPS) * g


def _mod_body(c_ref, w_ref, b_ref, o_ref):
    cv = c_ref[...]
    s = cv * jax.nn.sigmoid(cv)
    o_ref[...] = jnp.dot(s, w_ref[...], precision=HIGHEST, preferred_element_type=F32) + b_ref[...]


def _mod_call(c_rows, w_mod, b_mod):
    d = D_MODEL
    n = w_mod.shape[1]
    tn = 1024
    return pl.pallas_call(
        _mod_body,
        out_shape=jax.ShapeDtypeStruct((8, n), F32),
        grid=(n // tn,),
        in_specs=[pl.BlockSpec((8, d), lambda j: (0, 0)),
                  pl.BlockSpec((d, tn), lambda j: (0, j)),
                  pl.BlockSpec((1, tn), lambda j: (0, j))],
        out_specs=pl.BlockSpec((8, tn), lambda j: (0, j)),
        compiler_params=_cparams(("arbitrary",)),
        name="mod",
    )(c_rows, w_mod, b_mod)


def _rope128(pg, cos, sa, sb):
    return pg * cos + pltpu.roll(pg, LANES - 16, axis=1) * sa + pltpu.roll(pg, 16, axis=1) * sb


def _inproj_body(x_ref, sh_ref, sc_ref, g_ref, w_ref, cos_ref, sa_ref, sb_ref,
                 qrot_ref, qpl_ref, krot_ref, v_ref, qh_ref, zf_ref, zb_ref, ih_ref, gh_ref, *, rope):
    xv = x_ref[0]
    h = _rms(xv, g_ref[...]) * (1.0 + sc_ref[0]) + sh_ref[0]
    hb = h.astype(BF16)

    def group(j):
        return jnp.dot(hb, w_ref[:, j * COL_GROUP:(j + 1) * COL_GROUP], preferred_element_type=F32)

    def rotated(p):
        if not rope:
            return p
        cos, sa, sb = cos_ref[...], sa_ref[...], sb_ref[...]
        return jnp.concatenate(
            [_rope128(p[:, k * LANES:(k + 1) * LANES], cos, sa, sb) for k in range(COL_GROUP // LANES)], axis=1)

    scale = NA_HEAD_DIM ** -0.5
    p = group(0)
    qpl_ref[0] = (p * scale).astype(BF16)
    qrot_ref[0] = (rotated(p) * scale).astype(BF16)
    krot_ref[0] = rotated(group(1)).astype(BF16)
    v_ref[0] = group(2).astype(BF16)
    qh_ref[0] = group(3).astype(BF16)
    zf_ref[0] = group(4)
    zb_ref[0] = group(5)
    ih_ref[0] = group(6).astype(BF16)
    gh_ref[0] = group(7).astype(BF16)


def _inproj_call(x3, shift, scale, g_pre, w_in_b, cos, sa, sb, *, rope, tm):
    g, t, d = x3.shape
    n_out = 9
    dts = [BF16, BF16, BF16, BF16, BF16, F32, F32, BF16, BF16]
    tok = pl.BlockSpec((1, tm, d), lambda b, i: (b, i, 0))
    vec = pl.BlockSpec((1, 1, d), lambda b, i: (b, 0, 0))
    tab = pl.BlockSpec((tm, LANES), lambda b, i: (i, 0))
    outs = [pl.BlockSpec((1, tm, COL_GROUP), lambda b, i: (b, i, 0)) for _ in range(n_out)]
    return pl.pallas_call(
        functools.partial(_inproj_body, rope=rope),
        out_shape=[jax.ShapeDtypeStruct((g, t, COL_GROUP), dt) for dt in dts],
        grid=(g, t // tm),
        in_specs=[tok, vec, vec,
                  pl.BlockSpec((1, d), lambda b, i: (0, 0)),
                  pl.BlockSpec(w_in_b.shape, lambda b, i: (0, 0)),
                  tab, tab, tab],
        out_specs=outs,
        compiler_params=_cparams(("arbitrary", "arbitrary")),
        name="inproj_rope" if rope else "inproj_ctx",
    )(x3, shift, scale, g_pre, w_in_b, cos, sa, sb)


def _rope_tables(n_tok):
    half = NA_HEAD_DIM // 2
    t = np.arange(n_tok)
    row = (t // GRID_W).astype(np.float32)
    col = (t % GRID_W).astype(np.float32)
    lane = np.arange(LANES)
    d = lane % NA_HEAD_DIM
    dd = d % half
    fi = dd % (half // 2)
    inv_freq = (ROPE_BASE ** (-(2.0 * fi.astype(np.float32)) / half)).astype(np.float32)
    pos = np.where((d < half)[None, :], row[:, None], col[:, None])
    ang = (pos * inv_freq[None, :]).astype(np.float32)
    first = (dd < half // 2)[None, :]
    cos = np.cos(ang.astype(np.float64)).astype(np.float32)
    sin = np.sin(ang.astype(np.float64)).astype(np.float32)
    zero = np.zeros_like(sin)
    return jnp.asarray(cos), jnp.asarray(np.where(first, -sin, zero)), jnp.asarray(np.where(first, zero, sin))


def _na_body(qrot_ref, qpl_ref, k_ref, v_ref, kc_ref, vc_ref, bias_ref, o_ref):
    m = pl.program_id(2)
    lane = lax.broadcasted_iota(I32, (1, LANES), 1)
    first_head = lane < NA_HEAD_DIM
    hm0, hm1 = _mask_bf16(first_head), _mask_bf16(lane >= NA_HEAD_DIM)
    nt = (((1,), (1,)), ((), ()))
    rows = range(NA_ROWS_PER_STEP)

    def stacked(ref):
        parts = []
        for i in rows:
            q = ref[0, i * GRID_W:(i + 1) * GRID_W, :]
            parts += [q * hm0, q * hm1]
        return jnp.concatenate(parts, axis=0)

    sc = lax.dot_general(stacked(qpl_ref), kc_ref[0], nt, preferred_element_type=F32)

    qs = stacked(qrot_ref)
    k0s, s_parts = [], []
    blk = 2 * GRID_W
    for i in rows:
        r = m * NA_ROWS_PER_STEP + i
        r0 = jnp.clip(r - NA_WIN_H // 2, 0, N_ROWS - NA_WIN_H)
        di = r0 - r + (NA_WIN_H - 1)
        k0s.append(pl.multiple_of(r0 * GRID_W, GRID_W))
        s_parts.append(lax.dot_general(qs[i * blk:(i + 1) * blk], k_ref[0, pl.ds(k0s[i], NA_KEYS), :], nt,
                                       preferred_element_type=F32) + bias_ref[0, di].reshape(blk, NA_KEYS))
    s = jnp.concatenate([jnp.concatenate(s_parts, axis=0), sc], axis=1)
    p = jnp.exp(s - jnp.max(s, axis=-1, keepdims=True))
    den = jnp.sum(p, axis=-1, keepdims=True)
    pb = p.astype(BF16)
    acc = jnp.concatenate([jnp.dot(pb[i * blk:(i + 1) * blk, :NA_KEYS], v_ref[0, pl.ds(k0s[i], NA_KEYS), :],
                                   preferred_element_type=F32) for i in rows], axis=0)
    o = (acc + jnp.dot(pb[:, NA_KEYS:], vc_ref[0], preferred_element_type=F32)) / den
    o_ref[0] = jnp.concatenate([jnp.where(first_head, o[i * blk:i * blk + GRID_W], o[i * blk + GRID_W:(i + 1) * blk])
                                for i in rows], axis=0).astype(BF16)


def _na_bias(rpb):
    c = np.arange(GRID_W)
    cq = np.arange(GRID_W)
    win_c0 = np.clip(cq - NA_WIN_W // 2, 0, GRID_W - NA_WIN_W)
    in_win = (c[None, :] >= win_c0[:, None]) & (c[None, :] < win_c0[:, None] + NA_WIN_W)
    dc = np.clip(c[None, :] - cq[:, None] + NA_WIN_W - 1, 0, 2 * NA_WIN_W - 2)
    pick = (dc[None, :, :] == np.arange(2 * NA_WIN_W - 1)[:, None, None]).astype(np.float32)
    t = jnp.einsum('hrj,jqc->hqrc', rpb.astype(F32), jnp.asarray(pick), precision=HIGHEST)
    t = jnp.where(jnp.asarray(in_win)[None, :, None, :], t, -jnp.inf)
    n_dr = 2 * NA_WIN_H - 1
    t = t.reshape(N_NA_HEADS // 2, 2, GRID_W, n_dr * GRID_W)
    return jnp.stack([t[..., di * GRID_W:di * GRID_W + NA_KEYS] for di in range(NA_WIN_H)], axis=1)


def _na_call(qrot, qpl, krot, v, kc, vc, bias):
    bsz = qrot.shape[0]
    tq = NA_ROWS_PER_STEP * GRID_W
    qspec = pl.BlockSpec((1, tq, LANES), lambda b, p, m: (b, m, p))
    kspec = pl.BlockSpec((1, SEQ, LANES), lambda b, p, m: (b, 0, p))
    cspec = pl.BlockSpec((1, CTX_LEN, LANES), lambda b, p, m: (b, 0, p))
    return pl.pallas_call(
        _na_body,
        out_shape=jax.ShapeDtypeStruct((bsz, SEQ, NA_WIDTH), BF16),
        grid=(bsz, N_NA_HEADS // 2, N_ROWS // NA_ROWS_PER_STEP),
        in_specs=[qspec, qspec, kspec, kspec, cspec, cspec,
                  pl.BlockSpec((1, NA_WIN_H, 2, GRID_W, NA_KEYS), lambda b, p, m: (p, 0, 0, 0, 0))],
        out_specs=qspec,
        compiler_params=_cparams(("arbitrary", "arbitrary", "arbitrary")),
        name="na",
    )(qrot, qpl, krot, v, kc, vc, bias)


def _split3(v):
    hi = v.astype(BF16)
    r1 = v - hi.astype(F32)
    mid = r1.astype(BF16)
    lo = (r1 - mid.astype(F32)).astype(BF16)
    return hi, mid, lo


HG_NCHUNK = HG_BLOCK // HG_CHUNK


def _hg_consts(tri_ref, keep_ref, spread_ref):
    nb = HG_BLOCK
    rr = lax.broadcasted_iota(I32, (nb, nb), 0)
    cc = lax.broadcasted_iota(I32, (nb, nb), 1)
    same = (rr // HG_CHUNK) == (cc // HG_CHUNK)
    fwd = jnp.where(same & (cc <= rr), 1.0, 0.0)
    bwd = jnp.where(same & (cc >= rr), 1.0, 0.0)
    tri_ref[0] = fwd.astype(BF16)
    tri_ref[1] = bwd.astype(BF16)
    keep_ref[...] = jnp.concatenate([fwd, bwd], axis=0)
    r8 = lax.broadcasted_iota(I32, (nb, HG_NCHUNK * HG_DK), 0) // HG_CHUNK
    g8 = lax.broadcasted_iota(I32, (nb, HG_NCHUNK * HG_DK), 1) // HG_DK
    spread_ref[...] = _mask_bf16(r8 == g8)


def _hg_pair(zf, zb, qf, qb, vf, vb, stf, stb, lb, tri_ref, keep_ref, spread_ref, *, want_out):
    nb = HG_BLOCK
    spread = spread_ref[...]
    v = (vf, vb)

    sig = jax.nn.sigmoid(jnp.concatenate([zf, zb], axis=0))
    lf = jnp.log(lb + (1.0 - lb) * sig)
    kk = (1.0 - lb) * (1.0 - sig)
    parts = jnp.concatenate(_split3(lf), axis=1)
    sums = jnp.concatenate([jnp.dot(tri_ref[d], parts[d * nb:(d + 1) * nb], preferred_element_type=F32)
                            for d in range(2)], axis=0)
    bcum = sums[:, :HG_DK] + sums[:, HG_DK:2 * HG_DK] + sums[:, 2 * HG_DK:]
    end_rows = [d * nb + c * HG_CHUNK + (HG_CHUNK - 1 if d == 0 else 0) for d in range(2) for c in range(HG_NCHUNK)]
    ends = [bcum[r:r + 1, :] for r in end_rows]
    btot = jnp.concatenate([jnp.broadcast_to(e, (HG_CHUNK, HG_DK)) for e in ends], axis=0)
    dec = jnp.exp(jnp.concatenate(ends, axis=0))
    ke = (kk * jnp.exp(btot - bcum)).astype(BF16)

    new_states, entering = [], []
    for d, st in enumerate((stf, stb)):
        kvt = lax.dot_general(v[d], jnp.concatenate([ke[d * nb:(d + 1) * nb]] * HG_NCHUNK, axis=1) * spread,
                              (((0,), (0,)), ((), ())), preferred_element_type=F32)
        ent = [None] * HG_NCHUNK
        for c in (range(HG_NCHUNK) if d == 0 else range(HG_NCHUNK - 1, -1, -1)):
            ent[c] = st
            i_dec = d * HG_NCHUNK + c
            st = st * dec[i_dec:i_dec + 1, :] + kvt[:, c * HG_DK:(c + 1) * HG_DK]
        new_states.append(st)
        entering.append(ent)
    if not want_out:
        return None, None, new_states[0], new_states[1]

    qd = (jnp.concatenate([qf, qb], axis=0).astype(F32) * jnp.exp(bcum)).astype(BF16)
    kd = (kk * jnp.exp(-bcum)).astype(BF16)
    nt = (((1,), (1,)), ((), ()))
    a = jnp.concatenate([lax.dot_general(qd[d * nb:(d + 1) * nb], kd[d * nb:(d + 1) * nb], nt,
                                         preferred_element_type=F32) for d in range(2)], axis=0)
    a = jnp.where(keep_ref[...] > 0.5, a, 0.0).astype(BF16)
    outs = []
    for d in range(2):
        qd_d = qd[d * nb:(d + 1) * nb]
        inter = jnp.concatenate(
            [lax.dot_general(qd_d[c * HG_CHUNK:(c + 1) * HG_CHUNK], entering[d][c].astype(BF16), nt,
                             preferred_element_type=F32) for c in range(HG_NCHUNK)], axis=0)
        outs.append(jnp.dot(a[d * nb:(d + 1) * nb], v[d], preferred_element_type=F32) + inter)
    return outs[0], outs[1], new_states[0], new_states[1]


def _hgrn_body(q_ref, zf_ref, zb_ref, i_ref, g_ref, qc_ref, zfc_ref, zbc_ref, ic_ref, lb_ref, ng_ref,
               o_ref, accf_ref, accb_ref, stf_ref, stb_ref, tri_ref, keep_ref, spread_ref):
    lb = lb_ref[...]
    _hg_consts(tri_ref, keep_ref, spread_ref)
    pair = functools.partial(_hg_pair, lb=lb, tri_ref=tri_ref, keep_ref=keep_ref, spread_ref=spread_ref)
    zero_state = jnp.zeros((HG_DK, HG_DK), F32)
    _, _, stf, stb = pair(zfc_ref[0], zbc_ref[0], qc_ref[0], qc_ref[0], ic_ref[0], ic_ref[0],
                          zero_state, zero_state, want_out=False)
    stf_ref[...] = stf
    stb_ref[...] = stb
    n_blk = SEQ // HG_BLOCK

    def scan_step(n, carry):
        rows_f = pl.ds(pl.multiple_of(n * HG_BLOCK, HG_BLOCK), HG_BLOCK)
        rows_b = pl.ds(pl.multiple_of((n_blk - 1 - n) * HG_BLOCK, HG_BLOCK), HG_BLOCK)
        o_f, o_b, st_f, st_b = pair(zf_ref[0, rows_f, :], zb_ref[0, rows_b, :], q_ref[0, rows_f, :],
                                    q_ref[0, rows_b, :], i_ref[0, rows_f, :], i_ref[0, rows_b, :],
                                    stf_ref[...], stb_ref[...], want_out=True)
        stf_ref[...] = st_f
        stb_ref[...] = st_b
        accf_ref[rows_f, :] = o_f
        accb_ref[rows_b, :] = o_b
        return carry

    lax.fori_loop(0, n_blk, scan_step, 0, unroll=4)

    def readout_step(n, carry):
        rows = pl.ds(pl.multiple_of(n * HG_BLOCK, HG_BLOCK), HG_BLOCK)
        tot = accf_ref[rows, :] + accb_ref[rows, :]
        y = _rms(tot, ng_ref[...]) * jax.nn.sigmoid(g_ref[0, rows, :].astype(F32))
        o_ref[0, rows, :] = y.astype(BF16)
        return carry

    lax.fori_loop(0, n_blk, readout_step, 0, unroll=2)


def _hgrn_call(qh, zf, zb, ih, gh, qc, zfc, zbc, ic, lb, norm_g):
    bsz = qh.shape[0]
    seq = pl.BlockSpec((1, SEQ, HG_DK), lambda b, h: (b, 0, h))
    ctx = pl.BlockSpec((1, CTX_LEN, HG_DK), lambda b, h: (b, 0, h))
    return pl.pallas_call(
        _hgrn_body,
        out_shape=jax.ShapeDtypeStruct((bsz, SEQ, N_HG_HEADS * HG_DK), BF16),
        grid=(bsz, N_HG_HEADS),
        in_specs=[seq, seq, seq, seq, seq, ctx, ctx, ctx, ctx,
                  pl.BlockSpec((1, HG_DK), lambda b, h: (0, h)),
                  pl.BlockSpec((1, HG_DK), lambda b, h: (0, 0))],
        out_specs=seq,
        scratch_shapes=[pltpu.VMEM((SEQ, HG_DK), F32),
                        pltpu.VMEM((SEQ, HG_DK), F32),
                        pltpu.VMEM((HG_DK, HG_DK), F32),
                        pltpu.VMEM((HG_DK, HG_DK), F32),
                        pltpu.VMEM((2, HG_BLOCK, HG_BLOCK), BF16),
                        pltpu.VMEM((2 * HG_BLOCK, HG_BLOCK), F32),
                        pltpu.VMEM((HG_BLOCK, HG_NCHUNK * HG_DK), BF16)],
        compiler_params=_cparams(("arbitrary", "arbitrary")),
        name="hgrn",
    )(qh, zf, zb, ih, gh, qc, zfc, zbc, ic, lb, norm_g)


def _outproj_body(na_ref, hg_ref, x_ref, w_ref, gt1_ref, sh2_ref, sc2_ref, gpost_ref, gpre_ref, wr_ref,
                  x1_ref, h2_ref, lt_ref):
    mix = (jnp.dot(na_ref[0], w_ref[:NA_WIDTH, :], preferred_element_type=F32)
           + jnp.dot(hg_ref[0], w_ref[NA_WIDTH:, :], preferred_element_type=F32))
    x1 = x_ref[0] + gt1_ref[0] * _rms(mix, gpost_ref[...])
    x1_ref[0] = x1
    h2 = _rms(x1, gpre_ref[...]) * (1.0 + sc2_ref[0]) + sh2_ref[0]
    h2_ref[0] = h2.astype(BF16)
    lt_ref[0] = lax.dot_general(wr_ref[...], h2, (((1,), (1,)), ((), ())), precision=HIGHEST,
                                preferred_element_type=F32)


def _outproj_call(na_o, hg_o, x, w_out_b, gt1, sh2, sc2, g_post1, g_pre2, w_router_t):
    bsz, s, d = x.shape
    tm = OUT_TM
    half = pl.BlockSpec((1, tm, NA_WIDTH), lambda b, i: (b, i, 0))
    tok = pl.BlockSpec((1, tm, d), lambda b, i: (b, i, 0))
    vec = pl.BlockSpec((1, 1, d), lambda b, i: (b, 0, 0))
    par = pl.BlockSpec((1, d), lambda b, i: (0, 0))
    return pl.pallas_call(
        _outproj_body,
        out_shape=[jax.ShapeDtypeStruct((bsz, s, d), F32),
                   jax.ShapeDtypeStruct((bsz, s, d), BF16),
                   jax.ShapeDtypeStruct((bsz, N_EXPERTS, s), F32)],
        grid=(bsz, s // tm),
        in_specs=[half, half, tok, pl.BlockSpec((d, d), lambda b, i: (0, 0)), vec, vec, vec, par, par,
                  pl.BlockSpec((N_EXPERTS, d), lambda b, i: (0, 0))],
        out_specs=[tok, tok, pl.BlockSpec((1, N_EXPERTS, tm), lambda b, i: (b, 0, i))],
        compiler_params=_cparams(("arbitrary", "arbitrary")),
        name="outproj",
    )(na_o, hg_o, x, w_out_b, gt1, sh2, sc2, g_post1, g_pre2, w_router_t)


def _route_body(lt_ref, posm_ref, gate_ref, tab_ref, *, cap):
    l = lt_ref[0]
    mx = jnp.max(l, axis=0, keepdims=True)
    ex = jnp.exp(l - mx)
    aff = ex / jnp.sum(ex, axis=0, keepdims=True)
    capf = jnp.float32(cap)

    def count(mask):
        return jnp.sum(jnp.where(mask, 1.0, 0.0), axis=1, keepdims=True)

    def enough(v):
        return count(aff >= v) >= capf

    def bit_step(it, thr):
        cand = thr | (jnp.int32(1) << (30 - it))
        return jnp.where(enough(pltpu.bitcast(cand, F32)), cand, thr)

    thr = lax.fori_loop(0, 31, bit_step, jnp.zeros((N_EXPERTS, 1), I32))
    lo = pltpu.bitcast(thr, F32)
    hi = pltpu.bitcast(thr + 1, F32)

    def mid_step(it, lohi):
        lo, hi = lohi
        mid = 0.5 * (lo + hi)
        ok = enough(mid)
        return jnp.where(ok, mid, lo), jnp.where(ok, hi, mid)

    lo, hi = lax.fori_loop(0, 30, mid_step, (lo, hi))
    gt = aff >= hi
    eq = (aff >= lo) & jnp.logical_not(gt)
    need = capf - count(gt)

    rr = lax.broadcasted_iota(I32, (ROUTE_BLK, ROUTE_BLK), 0)
    cc = lax.broadcasted_iota(I32, (ROUTE_BLK, ROUTE_BLK), 1)
    upper = _mask_bf16(rr <= cc)
    lane = lax.broadcasted_iota(I32, (N_EXPERTS, LANES), 1)

    off_eq = jnp.zeros((N_EXPERTS, 1), F32)
    off_sel = jnp.zeros((N_EXPERTS, 1), F32)
    tab = jnp.zeros((N_EXPERTS, LANES), F32)
    for j in range(N_ROUTE_BLK):
        sl = slice(j * ROUTE_BLK, (j + 1) * ROUTE_BLK)
        eq_j = eq[:, sl]
        eq_b = _mask_bf16(eq_j)
        incl_eq = jnp.dot(eq_b, upper, preferred_element_type=F32) + off_eq
        rank_eq = incl_eq - eq_b.astype(F32)
        sel_j = gt[:, sl] | (eq_j & (rank_eq < need))
        sel_b = _mask_bf16(sel_j)
        incl_sel = jnp.dot(sel_b, upper, preferred_element_type=F32) + off_sel
        pos = incl_sel - sel_b.astype(F32)
        posm_ref[0, :, sl] = jnp.where(sel_j, pos.astype(I32), -1)
        gate_ref[0, :, sl] = jnp.where(sel_j, aff[:, sl], 0.0)
        tab = jnp.where(lane == j, off_sel, tab)
        off_eq = incl_eq[:, ROUTE_BLK - 1:ROUTE_BLK]
        off_sel = incl_sel[:, ROUTE_BLK - 1:ROUTE_BLK]
    tab = jnp.where(lane == N_ROUTE_BLK, off_sel, tab)
    tab_ref[0] = tab.astype(I32)


def _route_call(lt, cap):
    bsz = lt.shape[0]
    big = pl.BlockSpec((1, N_EXPERTS, SEQ), lambda b: (b, 0, 0))
    return pl.pallas_call(
        functools.partial(_route_body, cap=cap),
        out_shape=[jax.ShapeDtypeStruct((bsz, N_EXPERTS, SEQ), I32),
                   jax.ShapeDtypeStruct((bsz, N_EXPERTS, SEQ), F32),
                   jax.ShapeDtypeStruct((bsz, N_EXPERTS, LANES), I32)],
        grid=(bsz,),
        in_specs=[big],
        out_specs=[big, big, pl.BlockSpec((1, N_EXPERTS, LANES), lambda b: (b, 0, 0))],
        compiler_params=_cparams(("arbitrary",)),
        name="route",
    )(lt)


def _chunk_range(start, end):
    c_lo = start // SLOT_CHUNK
    n = jnp.where(end > start, (end - 1) // SLOT_CHUNK - c_lo + 1, 0)
    return c_lo, n


DISP_NE = 8


def _dispatch_body(tab_ref, posm_ref, gate_ref, h2_ref, x_ref, gsl_ref, *, cap):
    b = pl.program_id(0)
    hf = pl.program_id(1)
    blk = pl.program_id(2)
    tiles_per_step = MOE_TD // ROUTE_BLK
    chunks_per_batch = cap // SLOT_CHUNK

    @pl.when(blk == 0)
    def _():
        x_ref[...] = jnp.zeros_like(x_ref)
        gsl_ref[...] = jnp.zeros_like(gsl_ref)

    slot = lax.broadcasted_iota(I32, (SLOT_CHUNK, ROUTE_BLK), 0)

    def expert_step(k, carry):
        tab_base = (b * N_EXPERTS + hf * DISP_NE + k) * (N_ROUTE_BLK + 1) + blk * tiles_per_step

        def add_chunk(sub, chunk):
            cols = slice(sub * ROUTE_BLK, (sub + 1) * ROUTE_BLK)
            base = pl.multiple_of(chunk * SLOT_CHUNK, SLOT_CHUNK)
            onehot = _mask_bf16(posm_ref[0, pl.ds(k, 1), cols] == slot + base)
            got = jnp.dot(onehot, h2_ref[cols, :], preferred_element_type=F32).astype(BF16)
            x_ref[k, pl.ds(base, SLOT_CHUNK), :] = x_ref[k, pl.ds(base, SLOT_CHUNK), :] + got
            g3 = [p.astype(F32) for p in _split3(gate_ref[0, pl.ds(k, 1), cols])]
            g8 = jnp.concatenate(g3 + [jnp.zeros((8 - len(g3), ROUTE_BLK), F32)], axis=0).astype(BF16)
            gsl_ref[k, chunk] = gsl_ref[k, chunk] + lax.dot_general(g8, onehot, (((1,), (1,)), ((), ())),
                                                                    preferred_element_type=F32)

        ranges = [_chunk_range(tab_ref[tab_base + sub], tab_ref[tab_base + sub + 1]) for sub in range(tiles_per_step)]
        for sub, (c_lo, _) in enumerate(ranges):
            add_chunk(sub, jnp.minimum(c_lo, chunks_per_batch - 1))
        for sub, (c_lo, n) in enumerate(ranges):
            def more(kk, c, sub=sub, c_lo=c_lo):
                add_chunk(sub, c_lo + kk)
                return c

            lax.fori_loop(1, n, more, 0)
        return carry

    lax.fori_loop(0, DISP_NE, expert_step, 0)


def _dispatch_call(tab_flat, posm, gate, h2_flat, *, bsz, cap):
    d = D_MODEL
    n_blk = SEQ // MOE_TD
    chunks = cap // SLOT_CHUNK
    sel = pl.BlockSpec((1, DISP_NE, MOE_TD), lambda b, hf, blk, tab: (b, hf, blk))
    grid_spec = pltpu.PrefetchScalarGridSpec(
        num_scalar_prefetch=1,
        grid=(bsz, N_EXPERTS // DISP_NE, n_blk),
        in_specs=[sel, sel, pl.BlockSpec((MOE_TD, d), lambda b, hf, blk, tab: (b * n_blk + blk, 0))],
        out_specs=[pl.BlockSpec((DISP_NE, cap, d), lambda b, hf, blk, tab: (hf, b, 0)),
                   pl.BlockSpec((DISP_NE, chunks, 8, SLOT_CHUNK), lambda b, hf, blk, tab: (hf, b, 0, 0))],
    )
    return pl.pallas_call(
        functools.partial(_dispatch_body, cap=cap),
        out_shape=[jax.ShapeDtypeStruct((N_EXPERTS, bsz * cap, d), BF16),
                   jax.ShapeDtypeStruct((N_EXPERTS, bsz * chunks, 8, SLOT_CHUNK), F32)],
        grid_spec=grid_spec,
        compiler_params=_cparams(("arbitrary", "arbitrary", "arbitrary")),
        name="dispatch",
    )(tab_flat, posm, gate, h2_flat)


def _moe_body(x_ref, gsl_ref, wg_ref, wu_ref, wd_ref, y_ref, acc_scr):
    s = pl.program_id(1)

    chunks_per_tile = MOE_TM // SLOT_CHUNK

    def mlp(width, first, last):
        nt = (((1,), (1,)), ((), ()))

        def m_step(mi, carry):
            r0 = pl.multiple_of(mi * MOE_TM, MOE_TM)
            rows = pl.ds(r0, MOE_TM)
            xm = x_ref[0, rows, :]
            g = lax.dot_general(xm, wg_ref[0, :width, :].astype(BF16), nt, preferred_element_type=F32)
            u = lax.dot_general(xm, wu_ref[0, :width, :].astype(BF16), nt, preferred_element_type=F32)
            hid = (g * jax.nn.sigmoid(g) * u).astype(BF16)
            out = jnp.dot(hid, wd_ref[0, :width, :].astype(BF16), preferred_element_type=F32)
            if not first:
                out = acc_scr[rows, :] + out
            if last:
                gsl = gsl_ref[0, pl.ds(mi * chunks_per_tile, chunks_per_tile)]
                gcol = jnp.concatenate([jnp.sum(gsl[ci].T, axis=1, keepdims=True) for ci in range(chunks_per_tile)],
                                       axis=0)
                y_ref[0, rows, :] = (out * gcol).astype(BF16)
            else:
                acc_scr[rows, :] = out
            return carry

        lax.fori_loop(0, acc_scr.shape[0] // MOE_TM, m_step, 0)

    @pl.when(s == 0)
    def _():
        mlp(MOE_TF, True, False)

    @pl.when((s > 0) & (s < MOE_NF - 1))
    def _():
        mlp(MOE_TF, False, False)

    @pl.when(s == MOE_NF - 1)
    def _():
        mlp(MOE_F_LAST, False, True)


def _moe_call(xs, gsl, w_gate_t, w_up_t, w_down):
    n_e, slots, d = xs.shape
    per_e = pl.BlockSpec((1, slots, d), lambda e, s: (e, 0, 0))
    wspec = pl.BlockSpec((1, MOE_TF, d), lambda e, s: (e, s, 0))
    return pl.pallas_call(
        _moe_body,
        out_shape=jax.ShapeDtypeStruct((n_e, slots, d), BF16),
        grid=(n_e, MOE_NF),
        in_specs=[per_e, pl.BlockSpec((1,) + gsl.shape[1:], lambda e, s: (e, 0, 0, 0)), wspec, wspec, wspec],
        out_specs=per_e,
        scratch_shapes=[pltpu.VMEM((slots, d), F32)],
        compiler_params=_cparams(("arbitrary", "arbitrary")),
        name="moe",
    )(xs, gsl, w_gate_t, w_up_t, w_down)


def _combine_body(tab_ref, y_ref, posm_ref, x1_ref, gt2_ref, gpost_ref, o_ref, acc_ref, *, cap):
    n_e = N_EXPERTS
    b = pl.program_id(0)
    j = pl.program_id(1)
    pm = posm_ref[0].T
    win = 2 * SLOT_CHUNK
    slot_w = lax.broadcasted_iota(I32, (ROUTE_BLK, win), 1)
    slot_c = lax.broadcasted_iota(I32, (ROUTE_BLK, SLOT_CHUNK), 1)
    total = jnp.zeros((ROUTE_BLK, D_MODEL), F32)
    bases, extras = [], []
    for k in range(n_e):
        tab_base = (b * N_EXPERTS + k) * (N_ROUTE_BLK + 1)
        start = tab_ref[tab_base + j]
        end = tab_ref[tab_base + j + 1]
        base = pl.multiple_of(jnp.minimum((start // SLOT_CHUNK) * SLOT_CHUNK, cap - win), SLOT_CHUNK)
        total = total + jnp.dot(_mask_bf16(pm[:, k:k + 1] == slot_w + base), y_ref[k, pl.ds(base, win), :],
                                preferred_element_type=F32)
        bases.append(base)
        extras.append(jnp.maximum(end - (base + win) + SLOT_CHUNK - 1, 0) // SLOT_CHUNK)
    acc_ref[...] = total

    @pl.when(sum(extras) > 0)
    def _():
        for k in range(n_e):
            def extra_step(kk, carry, k=k):
                b2 = pl.multiple_of(bases[k] + win + kk * SLOT_CHUNK, SLOT_CHUNK)
                acc_ref[...] += jnp.dot(_mask_bf16(pm[:, k:k + 1] == slot_c + b2),
                                        y_ref[k, pl.ds(b2, SLOT_CHUNK), :], preferred_element_type=F32)
                return carry

            lax.fori_loop(0, extras[k], extra_step, 0)

    o_ref[0] = x1_ref[0] + gt2_ref[0] * _rms(acc_ref[...], gpost_ref[...])


def _combine_call(tab_flat, y, posm, x1, gt2, g_post2, *, cap):
    bsz, n_e, s = posm.shape
    d = D_MODEL
    tok = pl.BlockSpec((1, ROUTE_BLK, d), lambda b, j, tab: (b, j, 0))
    grid_spec = pltpu.PrefetchScalarGridSpec(
        num_scalar_prefetch=1,
        grid=(bsz, s // ROUTE_BLK),
        in_specs=[
            pl.BlockSpec((n_e, cap, d), lambda b, j, tab: (0, b, 0), pipeline_mode=pl.Buffered(1)),
            pl.BlockSpec((1, n_e, ROUTE_BLK), lambda b, j, tab: (b, 0, j)),
            tok,
            pl.BlockSpec((1, 1, d), lambda b, j, tab: (b, 0, 0)),
            pl.BlockSpec((1, d), lambda b, j, tab: (0, 0))],
        out_specs=tok,
        scratch_shapes=[pltpu.VMEM((ROUTE_BLK, d), F32)],
    )
    return pl.pallas_call(
        functools.partial(_combine_body, cap=cap),
        out_shape=jax.ShapeDtypeStruct((bsz, s, d), F32),
        grid_spec=grid_spec,
        compiler_params=_cparams(("arbitrary", "arbitrary")),
        name="combine",
    )(tab_flat, y, posm, x1, gt2, g_post2)


def kernel(x, c, ctx, c_ctx, w_mod, b_mod, g_pre1, g_post1, g_pre2, g_post2, w_in, w_out, na_rpb,
           hg_lb_logits, hg_norm, w_router, w_gate, w_up, w_down):
    bsz, seq, d = x.shape
    assert (seq, d) == (SEQ, D_MODEL) and ctx.shape[1] == CTX_LEN and w_mod.shape[0] == 1
    cap = 2 * seq // N_EXPERTS
    assert cap % SLOT_CHUNK == 0

    c_rows = jnp.zeros((8, d), F32).at[:bsz].set(c).at[bsz].set(c_ctx)
    mod = _mod_call(c_rows, w_mod[0], b_mod[0][None, :])
    sh1, sc1, gt1, sh2, sc2, gt2 = [m[:bsz, None, :] for m in jnp.split(mod, 6, axis=1)]
    sh1c, sc1c = mod[bsz:bsz + 1, None, :d], mod[bsz:bsz + 1, None, d:2 * d]

    lb_all = jnp.cumsum(jax.nn.softmax(hg_lb_logits.astype(F32), axis=0), axis=0)
    lb = lb_all[0][None, :]

    w_in_b = w_in[0].astype(BF16)
    w_out_b = w_out[0].astype(BF16)
    cos, sa, sb = _rope_tables(seq)

    qrot, qpl, krot, v, qh, zf, zb, ih, gh = _inproj_call(
        x, sh1, sc1, g_pre1[0][None, :], w_in_b, cos, sa, sb, rope=True, tm=IN_TM)
    ctx_flat = ctx.reshape(1, bsz * CTX_LEN, d)
    ctx_out = _inproj_call(ctx_flat, sh1c, sc1c, g_pre1[0][None, :], w_in_b,
                           cos[:bsz * CTX_LEN], sa[:bsz * CTX_LEN], sb[:bsz * CTX_LEN], rope=False, tm=CTX_LEN)
    _, _, kc, vc, qc, zfc, zbc, ic, _ = [a.reshape(bsz, CTX_LEN, COL_GROUP) for a in ctx_out]

    na_o = _na_call(qrot, qpl, krot, v, kc, vc, _na_bias(na_rpb[0]))
    hg_o = _hgrn_call(qh, zf, zb, ih, gh, qc, zfc, zbc, ic, lb, hg_norm[0][None, :])

    x1, h2, lt = _outproj_call(na_o, hg_o, x, w_out_b, gt1, sh2, sc2, g_post1[0][None, :],
                               g_pre2[0][None, :], w_router[0].T)

    posm, gate, tab = _route_call(lt, cap)
    tab_flat = tab[:, :, :N_ROUTE_BLK + 1].reshape(-1)
    xs, gsl = _dispatch_call(tab_flat, posm, gate, h2.reshape(bsz * seq, d), bsz=bsz, cap=cap)
    y = _moe_call(xs, gsl, jnp.swapaxes(w_gate[0], 1, 2), jnp.swapaxes(w_up[0], 1, 2), w_down[0])

    return _combine_call(tab_flat, y, posm, x1, gt2, g_post2[0][None, :], cap=cap)
```

```python
import functools

import jax
import jax.numpy as jnp
import numpy as np
from jax import lax
from jax.experimental import pallas as pl
from jax.experimental.pallas import tpu as pltpu

F32 = jnp.float32
BF16 = jnp.bfloat16
I32 = jnp.int32
HIGHEST = lax.Precision.HIGHEST

D_MODEL = 1024
GRID_W = 64
N_ROWS = 128
SEQ = 8192
CTX_LEN = 256
NA_HEAD_DIM = 64
N_NA_HEADS = 8
NA_WIDTH = 512
NA_WIN_H = 8
NA_WIN_W = 16
NA_QBLOCK_W = 16
NA_KSPAN_W = 32
ROPE_BASE = 10000.0
HG_DK = 128
N_HG_HEADS = 4
HG_CHUNK = 32
N_EXPERTS = 16
D_EXPERT = 2752
RMS_EPS = 1e-6
COL_GROUP = 512

LANES = 128
VMEM_LIMIT_BYTES = 58 * 1024 * 1024

IN_TM = 1024
NA_ROWS_PER_STEP = 32
NA_KEYS = NA_WIN_H * GRID_W
HG_BLOCK = 256
OUT_TM = 512
ROUTE_BLK = 256
N_ROUTE_BLK = SEQ // ROUTE_BLK
SLOT_CHUNK = 128
MOE_TD = 2048
MOE_TF = 768
MOE_NF = -(-D_EXPERT // MOE_TF)
MOE_F_LAST = D_EXPERT - (MOE_NF - 1) * MOE_TF
MOE_TM = 1024


def _cparams(sem):
    return pltpu.CompilerParams(dimension_semantics=sem, vmem_limit_bytes=VMEM_LIMIT_BYTES)


def _mask_bf16(mask):
    return jnp.where(mask, 1.0, 0.0).astype(BF16)


def _rms(v, g):
    return v * lax.rsqrt(jnp.mean(v * v, axis=-1, keepdims=True) + RMS_EPS) * g


def _mod_body(c_ref, w_ref, b_ref, o_ref):
    cv = c_ref[...]
    s = cv * jax.nn.sigmoid(cv)
    o_ref[...] = jnp.dot(s, w_ref[...], precision=HIGHEST, preferred_element_type=F32) + b_ref[...]


def _mod_call(c_rows, w_mod, b_mod):
    d = D_MODEL
    n = w_mod.shape[1]
    tn = 1024
    return pl.pallas_call(
        _mod_body,
        out_shape=jax.ShapeDtypeStruct((8, n), F32),
        grid=(n // tn,),
        in_specs=[pl.BlockSpec((8, d), lambda j: (0, 0)),
                  pl.BlockSpec((d, tn), lambda j: (0, j)),
                  pl.BlockSpec((1, tn), lambda j: (0, j))],
        out_specs=pl.BlockSpec((8, tn), lambda j: (0, j)),
        compiler_params=_cparams(("arbitrary",)),
        name="mod",
    )(c_rows, w_mod, b_mod)


def _rope128(pg, cos, sa, sb):
    return pg * cos + pltpu.roll(pg, LANES - 16, axis=1) * sa + pltpu.roll(pg, 16, axis=1) * sb


def _inproj_body(x_ref, sh_ref, sc_ref, g_ref, w_ref, cos_ref, sa_ref, sb_ref,
                 qrot_ref, qpl_ref, krot_ref, v_ref, qh_ref, zf_ref, zb_ref, ih_ref, gh_ref, *, rope):
    xv = x_ref[0]
    h = _rms(xv, g_ref[...]) * (1.0 + sc_ref[0]) + sh_ref[0]
    hb = h.astype(BF16)

    def group(j):
        return jnp.dot(hb, w_ref[:, j * COL_GROUP:(j + 1) * COL_GROUP], preferred_element_type=F32)

    def rotated(p):
        if not rope:
            return p
        cos, sa, sb = cos_ref[...], sa_ref[...], sb_ref[...]
        return jnp.concatenate(
            [_rope128(p[:, k * LANES:(k + 1) * LANES], cos, sa, sb) for k in range(COL_GROUP // LANES)], axis=1)

    scale = NA_HEAD_DIM ** -0.5
    p = group(0)
    qpl_ref[0] = (p * scale).astype(BF16)
    qrot_ref[0] = (rotated(p) * scale).astype(BF16)
    krot_ref[0] = rotated(group(1)).astype(BF16)
    v_ref[0] = group(2).astype(BF16)
    qh_ref[0] = group(3).astype(BF16)
    zf_ref[0] = group(4)
    zb_ref[0] = group(5)
    ih_ref[0] = group(6).astype(BF16)
    gh_ref[0] = group(7).astype(BF16)


def _inproj_call(x3, shift, scale, g_pre, w_in_b, cos, sa, sb, *, rope, tm):
    g, t, d = x3.shape
    n_out = 9
    dts = [BF16, BF16, BF16, BF16, BF16, F32, F32, BF16, BF16]
    tok = pl.BlockSpec((1, tm, d), lambda b, i: (b, i, 0))
    vec = pl.BlockSpec((1, 1, d), lambda b, i: (b, 0, 0))
    tab = pl.BlockSpec((tm, LANES), lambda b, i: (i, 0))
    outs = [pl.BlockSpec((1, tm, COL_GROUP), lambda b, i: (b, i, 0)) for _ in range(n_out)]
    return pl.pallas_call(
        functools.partial(_inproj_body, rope=rope),
        out_shape=[jax.ShapeDtypeStruct((g, t, COL_GROUP), dt) for dt in dts],
        grid=(g, t // tm),
        in_specs=[tok, vec, vec,
                  pl.BlockSpec((1, d), lambda b, i: (0, 0)),
                  pl.BlockSpec(w_in_b.shape, lambda b, i: (0, 0)),
                  tab, tab, tab],
        out_specs=outs,
        compiler_params=_cparams(("arbitrary", "arbitrary")),
        name="inproj_rope" if rope else "inproj_ctx",
    )(x3, shift, scale, g_pre, w_in_b, cos, sa, sb)


def _rope_tables(n_tok):
    half = NA_HEAD_DIM // 2
    t = np.arange(n_tok)
    row = (t // GRID_W).astype(np.float32)
    col = (t % GRID_W).astype(np.float32)
    lane = np.arange(LANES)
    d = lane % NA_HEAD_DIM
    dd = d % half
    fi = dd % (half // 2)
    inv_freq = (ROPE_BASE ** (-(2.0 * fi.astype(np.float32)) / half)).astype(np.float32)
    pos = np.where((d < half)[None, :], row[:, None], col[:, None])
    ang = (pos * inv_freq[None, :]).astype(np.float32)
    first = (dd < half // 2)[None, :]
    cos = np.cos(ang.astype(np.float64)).astype(np.float32)
    sin = np.sin(ang.astype(np.float64)).astype(np.float32)
    zero = np.zeros_like(sin)
    return jnp.asarray(cos), jnp.asarray(np.where(first, -sin, zero)), jnp.asarray(np.where(first, zero, sin))


def _na_body(qrot_ref, qpl_ref, k_ref, v_ref, kc_ref, vc_ref, bias_ref, o_ref):
    m = pl.program_id(2)
    lane = lax.broadcasted_iota(I32, (1, LANES), 1)
    first_head = lane < NA_HEAD_DIM
    hm0, hm1 = _mask_bf16(first_head), _mask_bf16(lane >= NA_HEAD_DIM)
    nt = (((1,), (1,)), ((), ()))
    rows = range(NA_ROWS_PER_STEP)

    def stacked(ref):
        parts = []
        for i in rows:
            q = ref[0, i * GRID_W:(i + 1) * GRID_W, :]
            parts += [q * hm0, q * hm1]
        return jnp.concatenate(parts, axis=0)

    sc = lax.dot_general(stacked(qpl_ref), kc_ref[0], nt, preferred_element_type=F32)

    qs = stacked(qrot_ref)
    k0s, s_parts = [], []
    blk = 2 * GRID_W
    for i in rows:
        r = m * NA_ROWS_PER_STEP + i
        r0 = jnp.clip(r - NA_WIN_H // 2, 0, N_ROWS - NA_WIN_H)
        di = r0 - r + (NA_WIN_H - 1)
        k0s.append(pl.multiple_of(r0 * GRID_W, GRID_W))
        s_parts.append(lax.dot_general(qs[i * blk:(i + 1) * blk], k_ref[0, pl.ds(k0s[i], NA_KEYS), :], nt,
                                       preferred_element_type=F32) + bias_ref[0, di].reshape(blk, NA_KEYS))
    s = jnp.concatenate([jnp.concatenate(s_parts, axis=0), sc], axis=1)
    p = jnp.exp(s - jnp.max(s, axis=-1, keepdims=True))
    den = jnp.sum(p, axis=-1, keepdims=True)
    pb = p.astype(BF16)
    acc = jnp.concatenate([jnp.dot(pb[i * blk:(i + 1) * blk, :NA_KEYS], v_ref[0, pl.ds(k0s[i], NA_KEYS), :],
                                   preferred_element_type=F32) for i in rows], axis=0)
    o = (acc + jnp.dot(pb[:, NA_KEYS:], vc_ref[0], preferred_element_type=F32)) / den
    o_ref[0] = jnp.concatenate([jnp.where(first_head, o[i * blk:i * blk + GRID_W], o[i * blk + GRID_W:(i + 1) * blk])
                                for i in rows], axis=0).astype(BF16)


def _na_bias(rpb):
    c = np.arange(GRID_W)
    cq = np.arange(GRID_W)
    win_c0 = np.clip(cq - NA_WIN_W // 2, 0, GRID_W - NA_WIN_W)
    in_win = (c[None, :] >= win_c0[:, None]) & (c[None, :] < win_c0[:, None] + NA_WIN_W)
    dc = np.clip(c[None, :] - cq[:, None] + NA_WIN_W - 1, 0, 2 * NA_WIN_W - 2)
    pick = (dc[None, :, :] == np.arange(2 * NA_WIN_W - 1)[:, None, None]).astype(np.float32)
    t = jnp.einsum('hrj,jqc->hqrc', rpb.astype(F32), jnp.asarray(pick), precision=HIGHEST)
    t = jnp.where(jnp.asarray(in_win)[None, :, None, :], t, -jnp.inf)
    n_dr = 2 * NA_WIN_H - 1
    t = t.reshape(N_NA_HEADS // 2, 2, GRID_W, n_dr * GRID_W)
    return jnp.stack([t[..., di * GRID_W:di * GRID_W + NA_KEYS] for di in range(NA_WIN_H)], axis=1)


def _na_call(qrot, qpl, krot, v, kc, vc, bias):
    bsz = qrot.shape[0]
    tq = NA_ROWS_PER_STEP * GRID_W
    qspec = pl.BlockSpec((1, tq, LANES), lambda b, p, m: (b, m, p))
    kspec = pl.BlockSpec((1, SEQ, LANES), lambda b, p, m: (b, 0, p))
    cspec = pl.BlockSpec((1, CTX_LEN, LANES), lambda b, p, m: (b, 0, p))
    return pl.pallas_call(
        _na_body,
        out_shape=jax.ShapeDtypeStruct((bsz, SEQ, NA_WIDTH), BF16),
        grid=(bsz, N_NA_HEADS // 2, N_ROWS // NA_ROWS_PER_STEP),
        in_specs=[qspec, qspec, kspec, kspec, cspec, cspec,
                  pl.BlockSpec((1, NA_WIN_H, 2, GRID_W, NA_KEYS), lambda b, p, m: (p, 0, 0, 0, 0))],
        out_specs=qspec,
        compiler_params=_cparams(("arbitrary", "arbitrary", "arbitrary")),
        name="na",
    )(qrot, qpl, krot, v, kc, vc, bias)


def _split3(v):
    hi = v.astype(BF16)
    r1 = v - hi.astype(F32)
    mid = r1.astype(BF16)
    lo = (r1 - mid.astype(F32)).astype(BF16)
    return hi, mid, lo


HG_NCHUNK = HG_BLOCK // HG_CHUNK


def _hg_consts(tri_ref, keep_ref, spread_ref):
    nb = HG_BLOCK
    rr = lax.broadcasted_iota(I32, (nb, nb), 0)
    cc = lax.broadcasted_iota(I32, (nb, nb), 1)
    same = (rr // HG_CHUNK) == (cc // HG_CHUNK)
    fwd = jnp.where(same & (cc <= rr), 1.0, 0.0)
    bwd = jnp.where(same & (cc >= rr), 1.0, 0.0)
    tri_ref[0] = fwd.astype(BF16)
    tri_ref[1] = bwd.astype(BF16)
    keep_ref[...] = jnp.concatenate([fwd, bwd], axis=0)
    r8 = lax.broadcasted_iota(I32, (nb, HG_NCHUNK * HG_DK), 0) // HG_CHUNK
    g8 = lax.broadcasted_iota(I32, (nb, HG_NCHUNK * HG_DK), 1) // HG_DK
    spread_ref[...] = _mask_bf16(r8 == g8)


def _hg_pair(zf, zb, qf, qb, vf, vb, stf, stb, lb, tri_ref, keep_ref, spread_ref, *, want_out):
    nb = HG_BLOCK
    spread = spread_ref[...]
    v = (vf, vb)

    sig = jax.nn.sigmoid(jnp.concatenate([zf, zb], axis=0))
    lf = jnp.log(lb + (1.0 - lb) * sig)
    kk = (1.0 - lb) * (1.0 - sig)
    parts = jnp.concatenate(_split3(lf), axis=1)
    sums = jnp.concatenate([jnp.dot(tri_ref[d], parts[d * nb:(d + 1) * nb], preferred_element_type=F32)
                            for d in range(2)], axis=0)
    bcum = sums[:, :HG_DK] + sums[:, HG_DK:2 * HG_DK] + sums[:, 2 * HG_DK:]
    end_rows = [d * nb + c * HG_CHUNK + (HG_CHUNK - 1 if d == 0 else 0) for d in range(2) for c in range(HG_NCHUNK)]
    ends = [bcum[r:r + 1, :] for r in end_rows]
    btot = jnp.concatenate([jnp.broadcast_to(e, (HG_CHUNK, HG_DK)) for e in ends], axis=0)
    dec = jnp.exp(jnp.concatenate(ends, axis=0))
    ke = (kk * jnp.exp(btot - bcum)).astype(BF16)

    new_states, entering = [], []
    for d, st in enumerate((stf, stb)):
        kvt = lax.dot_general(v[d], jnp.concatenate([ke[d * nb:(d + 1) * nb]] * HG_NCHUNK, axis=1) * spread,
                              (((0,), (0,)), ((), ())), preferred_element_type=F32)
        ent = [None] * HG_NCHUNK
        for c in (range(HG_NCHUNK) if d == 0 else range(HG_NCHUNK - 1, -1, -1)):
            ent[c] = st
            i_dec = d * HG_NCHUNK + c
            st = st * dec[i_dec:i_dec + 1, :] + kvt[:, c * HG_DK:(c + 1) * HG_DK]
        new_states.append(st)
        entering.append(ent)
    if not want_out:
        return None, None, new_states[0], new_states[1]

    qd = (jnp.concatenate([qf, qb], axis=0).astype(F32) * jnp.exp(bcum)).astype(BF16)
    kd = (kk * jnp.exp(-bcum)).astype(BF16)
    nt = (((1,), (1,)), ((), ()))
    a = jnp.concatenate([lax.dot_general(qd[d * nb:(d + 1) * nb], kd[d * nb:(d + 1) * nb], nt,
                                         preferred_element_type=F32) for d in range(2)], axis=0)
    a = jnp.where(keep_ref[...] > 0.5, a, 0.0).astype(BF16)
    outs = []
    for d in range(2):
        qd_d = qd[d * nb:(d + 1) * nb]
        inter = jnp.concatenate(
            [lax.dot_general(qd_d[c * HG_CHUNK:(c + 1) * HG_CHUNK], entering[d][c].astype(BF16), nt,
                             preferred_element_type=F32) for c in range(HG_NCHUNK)], axis=0)
        outs.append(jnp.dot(a[d * nb:(d + 1) * nb], v[d], preferred_element_type=F32) + inter)
    return outs[0], outs[1], new_states[0], new_states[1]


def _hgrn_body(q_ref, zf_ref, zb_ref, i_ref, g_ref, qc_ref, zfc_ref, zbc_ref, ic_ref, lb_ref, ng_ref,
               o_ref, accf_ref, accb_ref, stf_ref, stb_ref, tri_ref, keep_ref, spread_ref):
    lb = lb_ref[...]
    _hg_consts(tri_ref, keep_ref, spread_ref)
    pair = functools.partial(_hg_pair, lb=lb, tri_ref=tri_ref, keep_ref=keep_ref, spread_ref=spread_ref)
    zero_state = jnp.zeros((HG_DK, HG_DK), F32)
    _, _, stf, stb = pair(zfc_ref[0], zbc_ref[0], qc_ref[0], qc_ref[0], ic_ref[0], ic_ref[0],
                          zero_state, zero_state, want_out=False)
    stf_ref[...] = stf
    stb_ref[...] = stb
    n_blk = SEQ // HG_BLOCK

    def scan_step(n, carry):
        rows_f = pl.ds(pl.multiple_of(n * HG_BLOCK, HG_BLOCK), HG_BLOCK)
        rows_b = pl.ds(pl.multiple_of((n_blk - 1 - n) * HG_BLOCK, HG_BLOCK), HG_BLOCK)
        o_f, o_b, st_f, st_b = pair(zf_ref[0, rows_f, :], zb_ref[0, rows_b, :], q_ref[0, rows_f, :],
                                    q_ref[0, rows_b, :], i_ref[0, rows_f, :], i_ref[0, rows_b, :],
                                    stf_ref[...], stb_ref[...], want_out=True)
        stf_ref[...] = st_f
        stb_ref[...] = st_b
        accf_ref[rows_f, :] = o_f
        accb_ref[rows_b, :] = o_b
        return carry

    lax.fori_loop(0, n_blk, scan_step, 0, unroll=8)

    def readout_step(n, carry):
        rows = pl.ds(pl.multiple_of(n * HG_BLOCK, HG_BLOCK), HG_BLOCK)
        tot = accf_ref[rows, :] + accb_ref[rows, :]
        y = _rms(tot, ng_ref[...]) * jax.nn.sigmoid(g_ref[0, rows, :].astype(F32))
        o_ref[0, rows, :] = y.astype(BF16)
        return carry

    lax.fori_loop(0, n_blk, readout_step, 0, unroll=2)


def _hgrn_call(qh, zf, zb, ih, gh, qc, zfc, zbc, ic, lb, norm_g):
    bsz = qh.shape[0]
    seq = pl.BlockSpec((1, SEQ, HG_DK), lambda b, h: (b, 0, h))
    ctx = pl.BlockSpec((1, CTX_LEN, HG_DK), lambda b, h: (b, 0, h))
    return pl.pallas_call(
        _hgrn_body,
        out_shape=jax.ShapeDtypeStruct((bsz, SEQ, N_HG_HEADS * HG_DK), BF16),
        grid=(bsz, N_HG_HEADS),
        in_specs=[seq, seq, seq, seq, seq, ctx, ctx, ctx, ctx,
                  pl.BlockSpec((1, HG_DK), lambda b, h: (0, h)),
                  pl.BlockSpec((1, HG_DK), lambda b, h: (0, 0))],
        out_specs=seq,
        scratch_shapes=[pltpu.VMEM((SEQ, HG_DK), F32),
                        pltpu.VMEM((SEQ, HG_DK), F32),
                        pltpu.VMEM((HG_DK, HG_DK), F32),
                        pltpu.VMEM((HG_DK, HG_DK), F32),
                        pltpu.VMEM((2, HG_BLOCK, HG_BLOCK), BF16),
                        pltpu.VMEM((2 * HG_BLOCK, HG_BLOCK), F32),
                        pltpu.VMEM((HG_BLOCK, HG_NCHUNK * HG_DK), BF16)],
        compiler_params=_cparams(("arbitrary", "arbitrary")),
        name="hgrn",
    )(qh, zf, zb, ih, gh, qc, zfc, zbc, ic, lb, norm_g)


def _outproj_body(na_ref, hg_ref, x_ref, w_ref, gt1_ref, sh2_ref, sc2_ref, gpost_ref, gpre_ref, wr_ref,
                  x1_ref, h2_ref, lt_ref):
    mix = (jnp.dot(na_ref[0], w_ref[:NA_WIDTH, :], preferred_element_type=F32)
           + jnp.dot(hg_ref[0], w_ref[NA_WIDTH:, :], preferred_element_type=F32))
    x1 = x_ref[0] + gt1_ref[0] * _rms(mix, gpost_ref[...])
    x1_ref[0] = x1
    h2 = _rms(x1, gpre_ref[...]) * (1.0 + sc2_ref[0]) + sh2_ref[0]
    h2_ref[0] = h2.astype(BF16)
    lt_ref[0] = lax.dot_general(wr_ref[...], h2, (((1,), (1,)), ((), ())), precision=HIGHEST,
                                preferred_element_type=F32)


def _outproj_call(na_o, hg_o, x, w_out_b, gt1, sh2, sc2, g_post1, g_pre2, w_router_t):
    bsz, s, d = x.shape
    tm = OUT_TM
    half = pl.BlockSpec((1, tm, NA_WIDTH), lambda b, i: (b, i, 0))
    tok = pl.BlockSpec((1, tm, d), lambda b, i: (b, i, 0))
    vec = pl.BlockSpec((1, 1, d), lambda b, i: (b, 0, 0))
    par = pl.BlockSpec((1, d), lambda b, i: (0, 0))
    return pl.pallas_call(
        _outproj_body,
        out_shape=[jax.ShapeDtypeStruct((bsz, s, d), F32),
                   jax.ShapeDtypeStruct((bsz, s, d), BF16),
                   jax.ShapeDtypeStruct((bsz, N_EXPERTS, s), F32)],
        grid=(bsz, s // tm),
        in_specs=[half, half, tok, pl.BlockSpec((d, d), lambda b, i: (0, 0)), vec, vec, vec, par, par,
                  pl.BlockSpec((N_EXPERTS, d), lambda b, i: (0, 0))],
        out_specs=[tok, tok, pl.BlockSpec((1, N_EXPERTS, tm), lambda b, i: (b, 0, i))],
        compiler_params=_cparams(("arbitrary", "arbitrary")),
        name="outproj",
    )(na_o, hg_o, x, w_out_b, gt1, sh2, sc2, g_post1, g_pre2, w_router_t)


def _route_body(lt_ref, posm_ref, gate_ref, tab_ref, *, cap):
    l = lt_ref[0]
    mx = jnp.max(l, axis=0, keepdims=True)
    ex = jnp.exp(l - mx)
    aff = ex / jnp.sum(ex, axis=0, keepdims=True)
    capf = jnp.float32(cap)

    def count(mask):
        return jnp.sum(jnp.where(mask, 1.0, 0.0), axis=1, keepdims=True)

    def enough(v):
        return count(aff >= v) >= capf

    def bit_step(it, thr):
        cand = thr | (jnp.int32(1) << (30 - it))
        return jnp.where(enough(pltpu.bitcast(cand, F32)), cand, thr)

    thr = lax.fori_loop(0, 31, bit_step, jnp.zeros((N_EXPERTS, 1), I32))
    lo = pltpu.bitcast(thr, F32)
    hi = pltpu.bitcast(thr + 1, F32)

    def mid_step(it, lohi):
        lo, hi = lohi
        mid = 0.5 * (lo + hi)
        ok = enough(mid)
        return jnp.where(ok, mid, lo), jnp.where(ok, hi, mid)

    lo, hi = lax.fori_loop(0, 30, mid_step, (lo, hi))
    gt = aff >= hi
    eq = (aff >= lo) & jnp.logical_not(gt)
    need = capf - count(gt)

    rr = lax.broadcasted_iota(I32, (ROUTE_BLK, ROUTE_BLK), 0)
    cc = lax.broadcasted_iota(I32, (ROUTE_BLK, ROUTE_BLK), 1)
    upper = _mask_bf16(rr <= cc)
    lane = lax.broadcasted_iota(I32, (N_EXPERTS, LANES), 1)

    off_eq = jnp.zeros((N_EXPERTS, 1), F32)
    off_sel = jnp.zeros((N_EXPERTS, 1), F32)
    tab = jnp.zeros((N_EXPERTS, LANES), F32)
    for j in range(N_ROUTE_BLK):
        sl = slice(j * ROUTE_BLK, (j + 1) * ROUTE_BLK)
        eq_j = eq[:, sl]
        eq_b = _mask_bf16(eq_j)
        incl_eq = jnp.dot(eq_b, upper, preferred_element_type=F32) + off_eq
        rank_eq = incl_eq - eq_b.astype(F32)
        sel_j = gt[:, sl] | (eq_j & (rank_eq < need))
        sel_b = _mask_bf16(sel_j)
        incl_sel = jnp.dot(sel_b, upper, preferred_element_type=F32) + off_sel
        pos = incl_sel - sel_b.astype(F32)
        posm_ref[0, :, sl] = jnp.where(sel_j, pos.astype(I32), -1)
        gate_ref[0, :, sl] = jnp.where(sel_j, aff[:, sl], 0.0)
        tab = jnp.where(lane == j, off_sel, tab)
        off_eq = incl_eq[:, ROUTE_BLK - 1:ROUTE_BLK]
        off_sel = incl_sel[:, ROUTE_BLK - 1:ROUTE_BLK]
    tab = jnp.where(lane == N_ROUTE_BLK, off_sel, tab)
    tab_ref[0] = tab.astype(I32)


def _route_call(lt, cap):
    bsz = lt.shape[0]
    big = pl.BlockSpec((1, N_EXPERTS, SEQ), lambda b: (b, 0, 0))
    return pl.pallas_call(
        functools.partial(_route_body, cap=cap),
        out_shape=[jax.ShapeDtypeStruct((bsz, N_EXPERTS, SEQ), I32),
                   jax.ShapeDtypeStruct((bsz, N_EXPERTS, SEQ), F32),
                   jax.ShapeDtypeStruct((bsz, N_EXPERTS, LANES), I32)],
        grid=(bsz,),
        in_specs=[big],
        out_specs=[big, big, pl.BlockSpec((1, N_EXPERTS, LANES), lambda b: (b, 0, 0))],
        compiler_params=_cparams(("arbitrary",)),
        name="route",
    )(lt)


def _chunk_range(start, end):
    c_lo = start // SLOT_CHUNK
    n = jnp.where(end > start, (end - 1) // SLOT_CHUNK - c_lo + 1, 0)
    return c_lo, n


DISP_NE = 8


def _dispatch_body(tab_ref, posm_ref, gate_ref, h2_ref, x_ref, gsl_ref, *, cap):
    b = pl.program_id(0)
    hf = pl.program_id(1)
    blk = pl.program_id(2)
    tiles_per_step = MOE_TD // ROUTE_BLK
    chunks_per_batch = cap // SLOT_CHUNK

    @pl.when(blk == 0)
    def _():
        x_ref[...] = jnp.zeros_like(x_ref)
        gsl_ref[...] = jnp.zeros_like(gsl_ref)

    slot = lax.broadcasted_iota(I32, (SLOT_CHUNK, ROUTE_BLK), 0)

    def expert_step(k, carry):
        tab_base = (b * N_EXPERTS + hf * DISP_NE + k) * (N_ROUTE_BLK + 1) + blk * tiles_per_step

        def add_chunk(sub, chunk):
            cols = slice(sub * ROUTE_BLK, (sub + 1) * ROUTE_BLK)
            base = pl.multiple_of(chunk * SLOT_CHUNK, SLOT_CHUNK)
            onehot = _mask_bf16(posm_ref[0, pl.ds(k, 1), cols] == slot + base)
            got = jnp.dot(onehot, h2_ref[cols, :], preferred_element_type=F32).astype(BF16)
            x_ref[k, pl.ds(base, SLOT_CHUNK), :] = x_ref[k, pl.ds(base, SLOT_CHUNK), :] + got
            g3 = [p.astype(F32) for p in _split3(gate_ref[0, pl.ds(k, 1), cols])]
            g8 = jnp.concatenate(g3 + [jnp.zeros((8 - len(g3), ROUTE_BLK), F32)], axis=0).astype(BF16)
            gsl_ref[k, chunk] = gsl_ref[k, chunk] + lax.dot_general(g8, onehot, (((1,), (1,)), ((), ())),
                                                                    preferred_element_type=F32)

        ranges = [_chunk_range(tab_ref[tab_base + sub], tab_ref[tab_base + sub + 1]) for sub in range(tiles_per_step)]
        for sub, (c_lo, _) in enumerate(ranges):
            add_chunk(sub, jnp.minimum(c_lo, chunks_per_batch - 1))
        for sub, (c_lo, n) in enumerate(ranges):
            def more(kk, c, sub=sub, c_lo=c_lo):
                add_chunk(sub, c_lo + kk)
                return c

            lax.fori_loop(1, n, more, 0)
        return carry

    lax.fori_loop(0, DISP_NE, expert_step, 0)


def _dispatch_call(tab_flat, posm, gate, h2_flat, *, bsz, cap):
    d = D_MODEL
    n_blk = SEQ // MOE_TD
    chunks = cap // SLOT_CHUNK
    sel = pl.BlockSpec((1, DISP_NE, MOE_TD), lambda b, hf, blk, tab: (b, hf, blk))
    grid_spec = pltpu.PrefetchScalarGridSpec(
        num_scalar_prefetch=1,
        grid=(bsz, N_EXPERTS // DISP_NE, n_blk),
        in_specs=[sel, sel, pl.BlockSpec((MOE_TD, d), lambda b, hf, blk, tab: (b * n_blk + blk, 0))],
        out_specs=[pl.BlockSpec((DISP_NE, cap, d), lambda b, hf, blk, tab: (hf, b, 0)),
                   pl.BlockSpec((DISP_NE, chunks, 8, SLOT_CHUNK), lambda b, hf, blk, tab: (hf, b, 0, 0))],
    )
    return pl.pallas_call(
        functools.partial(_dispatch_body, cap=cap),
        out_shape=[jax.ShapeDtypeStruct((N_EXPERTS, bsz * cap, d), BF16),
                   jax.ShapeDtypeStruct((N_EXPERTS, bsz * chunks, 8, SLOT_CHUNK), F32)],
        grid_spec=grid_spec,
        compiler_params=_cparams(("arbitrary", "arbitrary", "arbitrary")),
        name="dispatch",
    )(tab_flat, posm, gate, h2_flat)


def _moe_body(x_ref, gsl_ref, wg_ref, wu_ref, wd_ref, y_ref, acc_scr):
    s = pl.program_id(1)

    chunks_per_tile = MOE_TM // SLOT_CHUNK

    def mlp(width, first, last):
        nt = (((1,), (1,)), ((), ()))

        def m_step(mi, carry):
            r0 = pl.multiple_of(mi * MOE_TM, MOE_TM)
            rows = pl.ds(r0, MOE_TM)
            xm = x_ref[0, rows, :]
            g = lax.dot_general(xm, wg_ref[0, :width, :].astype(BF16), nt, preferred_element_type=F32)
            u = lax.dot_general(xm, wu_ref[0, :width, :].astype(BF16), nt, preferred_element_type=F32)
            hid = (g * jax.nn.sigmoid(g) * u).astype(BF16)
            out = jnp.dot(hid, wd_ref[0, :width, :].astype(BF16), preferred_element_type=F32)
            if not first:
                out = acc_scr[rows, :] + out
            if last:
                gsl = gsl_ref[0, pl.ds(mi * chunks_per_tile, chunks_per_tile)]
                gcol = jnp.concatenate([jnp.sum(gsl[ci].T, axis=1, keepdims=True) for ci in range(chunks_per_tile)],
                                       axis=0)
                y_ref[0, rows, :] = (out * gcol).astype(BF16)
            else:
                acc_scr[rows, :] = out
            return carry

        lax.fori_loop(0, acc_scr.shape[0] // MOE_TM, m_step, 0)

    @pl.when(s == 0)
    def _():
        mlp(MOE_TF, True, False)

    @pl.when((s > 0) & (s < MOE_NF - 1))
    def _():
        mlp(MOE_TF, False, False)

    @pl.when(s == MOE_NF - 1)
    def _():
        mlp(MOE_F_LAST, False, True)


def _moe_call(xs, gsl, w_gate_t, w_up_t, w_down):
    n_e, slots, d = xs.shape
    per_e = pl.BlockSpec((1, slots, d), lambda e, s: (e, 0, 0))
    wspec = pl.BlockSpec((1, MOE_TF, d), lambda e, s: (e, s, 0))
    return pl.pallas_call(
        _moe_body,
        out_shape=jax.ShapeDtypeStruct((n_e, slots, d), BF16),
        grid=(n_e, MOE_NF),
        in_specs=[per_e, pl.BlockSpec((1,) + gsl.shape[1:], lambda e, s: (e, 0, 0, 0)), wspec, wspec, wspec],
        out_specs=per_e,
        scratch_shapes=[pltpu.VMEM((slots, d), F32)],
        compiler_params=_cparams(("arbitrary", "arbitrary")),
        name="moe",
    )(xs, gsl, w_gate_t, w_up_t, w_down)


def _combine_body(tab_ref, y_ref, posm_ref, x1_ref, gt2_ref, gpost_ref, o_ref, acc_ref, *, cap):
    n_e = N_EXPERTS
    b = pl.program_id(0)
    j = pl.program_id(1)
    pm = posm_ref[0].T
    win = 2 * SLOT_CHUNK
    slot_w = lax.broadcasted_iota(I32, (ROUTE_BLK, win), 1)
    slot_c = lax.broadcasted_iota(I32, (ROUTE_BLK, SLOT_CHUNK), 1)
    total = jnp.zeros((ROUTE_BLK, D_MODEL), F32)
    bases, extras = [], []
    for k in range(n_e):
        tab_base = (b * N_EXPERTS + k) * (N_ROUTE_BLK + 1)
        start = tab_ref[tab_base + j]
        end = tab_ref[tab_base + j + 1]
        base = pl.multiple_of(jnp.minimum((start // SLOT_CHUNK) * SLOT_CHUNK, cap - win), SLOT_CHUNK)
        total = total + jnp.dot(_mask_bf16(pm[:, k:k + 1] == slot_w + base), y_ref[k, pl.ds(base, win), :],
                                preferred_element_type=F32)
        bases.append(base)
        extras.append(jnp.maximum(end - (base + win) + SLOT_CHUNK - 1, 0) // SLOT_CHUNK)
    acc_ref[...] = total

    @pl.when(sum(extras) > 0)
    def _():
        for k in range(n_e):
            def extra_step(kk, carry, k=k):
                b2 = pl.multiple_of(bases[k] + win + kk * SLOT_CHUNK, SLOT_CHUNK)
                acc_ref[...] += jnp.dot(_mask_bf16(pm[:, k:k + 1] == slot_c + b2),
                                        y_ref[k, pl.ds(b2, SLOT_CHUNK), :], preferred_element_type=F32)
                return carry

            lax.fori_loop(0, extras[k], extra_step, 0)

    o_ref[0] = x1_ref[0] + gt2_ref[0] * _rms(acc_ref[...], gpost_ref[...])


def _combine_call(tab_flat, y, posm, x1, gt2, g_post2, *, cap):
    bsz, n_e, s = posm.shape
    d = D_MODEL
    tok = pl.BlockSpec((1, ROUTE_BLK, d), lambda b, j, tab: (b, j, 0))
    grid_spec = pltpu.PrefetchScalarGridSpec(
        num_scalar_prefetch=1,
        grid=(bsz, s // ROUTE_BLK),
        in_specs=[
            pl.BlockSpec((n_e, cap, d), lambda b, j, tab: (0, b, 0), pipeline_mode=pl.Buffered(1)),
            pl.BlockSpec((1, n_e, ROUTE_BLK), lambda b, j, tab: (b, 0, j)),
            tok,
            pl.BlockSpec((1, 1, d), lambda b, j, tab: (b, 0, 0)),
            pl.BlockSpec((1, d), lambda b, j, tab: (0, 0))],
        out_specs=tok,
        scratch_shapes=[pltpu.VMEM((ROUTE_BLK, d), F32)],
    )
    return pl.pallas_call(
        functools.partial(_combine_body, cap=cap),
        out_shape=jax.ShapeDtypeStruct((bsz, s, d), F32),
        grid_spec=grid_spec,
        compiler_params=_cparams(("arbitrary", "arbitrary")),
        name="combine",
    )(tab_flat, y, posm, x1, gt2, g_post2)


def kernel(x, c, ctx, c_ctx, w_mod, b_mod, g_pre1, g_post1, g_pre2, g_post2, w_in, w_out, na_rpb,
           hg_lb_logits, hg_norm, w_router, w_gate, w_up, w_down):
    bsz, seq, d = x.shape
    assert (seq, d) == (SEQ, D_MODEL) and ctx.shape[1] == CTX_LEN and w_mod.shape[0] == 1
    cap = 2 * seq // N_EXPERTS
    assert cap % SLOT_CHUNK == 0

    c_rows = jnp.zeros((8, d), F32).at[:bsz].set(c).at[bsz].set(c_ctx)
    mod = _mod_call(c_rows, w_mod[0], b_mod[0][None, :])
    sh1, sc1, gt1, sh2, sc2, gt2 = [m[:bsz, None, :] for m in jnp.split(mod, 6, axis=1)]
    sh1c, sc1c = mod[bsz:bsz + 1, None, :d], mod[bsz:bsz + 1, None, d:2 * d]

    lb_all = jnp.cumsum(jax.nn.softmax(hg_lb_logits.astype(F32), axis=0), axis=0)
    lb = lb_all[0][None, :]

    w_in_b = w_in[0].astype(BF16)
    w_out_b = w_out[0].astype(BF16)
    cos, sa, sb = _rope_tables(seq)

    qrot, qpl, krot, v, qh, zf, zb, ih, gh = _inproj_call(
        x, sh1, sc1, g_pre1[0][None, :], w_in_b, cos, sa, sb, rope=True, tm=IN_TM)
    ctx_flat = ctx.reshape(1, bsz * CTX_LEN, d)
    ctx_out = _inproj_call(ctx_flat, sh1c, sc1c, g_pre1[0][None, :], w_in_b,
                           cos[:bsz * CTX_LEN], sa[:bsz * CTX_LEN], sb[:bsz * CTX_LEN], rope=False, tm=CTX_LEN)
    _, _, kc, vc, qc, zfc, zbc, ic, _ = [a.reshape(bsz, CTX_LEN, COL_GROUP) for a in ctx_out]

    na_o = _na_call(qrot, qpl, krot, v, kc, vc, _na_bias(na_rpb[0]))
    hg_o = _hgrn_call(qh, zf, zb, ih, gh, qc, zfc, zbc, ic, lb, hg_norm[0][None, :])

    x1, h2, lt = _outproj_call(na_o, hg_o, x, w_out_b, gt1, sh2, sc2, g_post1[0][None, :],
                               g_pre2[0][None, :], w_router[0].T)

    posm, gate, tab = _route_call(lt, cap)
    tab_flat = tab[:, :, :N_ROUTE_BLK + 1].reshape(-1)
    xs, gsl = _dispatch_call(tab_flat, posm, gate, h2.reshape(bsz * seq, d), bsz=bsz, cap=cap)
    y = _moe_call(xs, gsl, jnp.swapaxes(w_gate[0], 1, 2), jnp.swapaxes(w_up[0], 1, 2), w_down[0])

    return _combine_call(tab_flat, y, posm, x1, gt2, g_post2[0][None, :], cap=cap)
```

```python
import functools

import jax
import jax.numpy as jnp
import numpy as np
from jax import lax
from jax.experimental import pallas as pl
from jax.experimental.pallas import tpu as pltpu

F32 = jnp.float32
BF16 = jnp.bfloat16
I32 = jnp.int32
HIGHEST = lax.Precision.HIGHEST

D_MODEL = 1024
GRID_W = 64
N_ROWS = 128
SEQ = 8192
CTX_LEN = 256
NA_HEAD_DIM = 64
N_NA_HEADS = 8
NA_WIDTH = 512
NA_WIN_H = 8
NA_WIN_W = 16
NA_QBLOCK_W = 16
NA_KSPAN_W = 32
ROPE_BASE = 10000.0
HG_DK = 128
N_HG_HEADS = 4
HG_CHUNK = 32
N_EXPERTS = 16
D_EXPERT = 2752
RMS_EPS = 1e-6
COL_GROUP = 512

LANES = 128
VMEM_LIMIT_BYTES = 58 * 1024 * 1024

IN_TM = 1024
NA_ROWS_PER_STEP = 64
NA_KEYS = NA_WIN_H * GRID_W
HG_BLOCK = 256
OUT_TM = 512
ROUTE_BLK = 256
N_ROUTE_BLK = SEQ // ROUTE_BLK
SLOT_CHUNK = 128
TAB_STRIDE = LANES
MOE_TD = 2048
MOE_TF = 768
MOE_NF = -(-D_EXPERT // MOE_TF)
MOE_F_LAST = D_EXPERT - (MOE_NF - 1) * MOE_TF
MOE_TM = 1024


def _cparams(sem):
    return pltpu.CompilerParams(dimension_semantics=sem, vmem_limit_bytes=VMEM_LIMIT_BYTES)


def _mask_bf16(mask):
    return jnp.where(mask, 1.0, 0.0).astype(BF16)


def _rms(v, g):
    return v * lax.rsqrt(jnp.mean(v * v, axis=-1, keepdims=True) + RMS_EPS) * g


MOD_ROWS = 8
N_MOD = 6


def _mod_body(c_ref, cc_ref, w_ref, b_ref, o_ref, rows_scr):
    bsz = c_ref.shape[0]
    rows_scr[...] = jnp.zeros_like(rows_scr)
    rows_scr[0:bsz, :] = c_ref[...]
    rows_scr[bsz:bsz + 1, :] = cc_ref[...]
    cv = rows_scr[...]
    s = cv * jax.nn.sigmoid(cv)
    o_ref[...] = jnp.dot(s, w_ref[0], precision=HIGHEST, preferred_element_type=F32) + b_ref[...]


def _mod_call(c, c_ctx, w_mod, b_mod):
    bsz, d = c.shape
    n = w_mod.shape[2]
    tn = d
    mod = pl.pallas_call(
        _mod_body,
        out_shape=jax.ShapeDtypeStruct((MOD_ROWS, n), F32),
        grid=(n // tn,),
        in_specs=[pl.BlockSpec((bsz, d), lambda j: (0, 0)),
                  pl.BlockSpec((1, d), lambda j: (0, 0)),
                  pl.BlockSpec((1, d, tn), lambda j: (0, 0, j)),
                  pl.BlockSpec((1, tn), lambda j: (0, j))],
        out_specs=pl.BlockSpec((MOD_ROWS, tn), lambda j: (0, j)),
        scratch_shapes=[pltpu.VMEM((MOD_ROWS, d), F32)],
        compiler_params=_cparams(("arbitrary",)),
        name="mod",
    )(c, c_ctx[None, :], w_mod, b_mod)
    return mod.reshape(MOD_ROWS * N_MOD, 1, d)


def _mod_spec(k, row_of, n_grid):
    if n_grid == 2:
        return pl.BlockSpec((1, 1, D_MODEL), lambda b, i: (row_of(b) * N_MOD + k, 0, 0))
    return pl.BlockSpec((1, 1, D_MODEL), lambda b, i, tab: (row_of(b) * N_MOD + k, 0, 0))


def _rope128(pg, cos, sa, sb):
    return pg * cos + pltpu.roll(pg, LANES - 16, axis=1) * sa + pltpu.roll(pg, 16, axis=1) * sb


def _inproj_body(x_ref, sh_ref, sc_ref, g_ref, w_ref, cos_ref, sa_ref, sb_ref,
                 qrot_ref, qpl_ref, krot_ref, v_ref, qh_ref, zf_ref, zb_ref, ih_ref, gh_ref, *, rope):
    xv = x_ref[0]
    h = _rms(xv, g_ref[...]) * (1.0 + sc_ref[0]) + sh_ref[0]
    hb = h.astype(BF16)

    def group(j):
        return jnp.dot(hb, w_ref[:, j * COL_GROUP:(j + 1) * COL_GROUP], preferred_element_type=F32)

    def rotated(p):
        if not rope:
            return p
        cos, sa, sb = cos_ref[...], sa_ref[...], sb_ref[...]
        return jnp.concatenate(
            [_rope128(p[:, k * LANES:(k + 1) * LANES], cos, sa, sb) for k in range(COL_GROUP // LANES)], axis=1)

    scale = NA_HEAD_DIM ** -0.5
    p = group(0)
    qpl_ref[0] = (p * scale).astype(BF16)
    qrot_ref[0] = (rotated(p) * scale).astype(BF16)
    krot_ref[0] = rotated(group(1)).astype(BF16)
    v_ref[0] = group(2).astype(BF16)
    qh_ref[0] = group(3).astype(BF16)
    zf_ref[0] = group(4)
    zb_ref[0] = group(5)
    ih_ref[0] = group(6).astype(BF16)
    gh_ref[0] = group(7).astype(BF16)


def _inproj_call(x3, mod, row_of, g_pre, w_in_b, cos, sa, sb, *, rope, tm):
    g, t, d = x3.shape
    n_out = 9
    dts = [BF16, BF16, BF16, BF16, BF16, F32, F32, BF16, BF16]
    tok = pl.BlockSpec((1, tm, d), lambda b, i: (b, i, 0))
    tab = pl.BlockSpec((tm, LANES), lambda b, i: (i, 0))
    outs = [pl.BlockSpec((1, tm, COL_GROUP), lambda b, i: (b, i, 0)) for _ in range(n_out)]
    return pl.pallas_call(
        functools.partial(_inproj_body, rope=rope),
        out_shape=[jax.ShapeDtypeStruct((g, t, COL_GROUP), dt) for dt in dts],
        grid=(g, t // tm),
        in_specs=[tok, _mod_spec(0, row_of, 2), _mod_spec(1, row_of, 2),
                  pl.BlockSpec((1, d), lambda b, i: (0, 0)),
                  pl.BlockSpec(w_in_b.shape, lambda b, i: (0, 0)),
                  tab, tab, tab],
        out_specs=outs,
        compiler_params=_cparams(("arbitrary", "arbitrary")),
        name="inproj_rope" if rope else "inproj_ctx",
    )(x3, mod, mod, g_pre, w_in_b, cos, sa, sb)


def _rope_tables(n_tok):
    half = NA_HEAD_DIM // 2
    t = np.arange(n_tok)
    row = (t // GRID_W).astype(np.float32)
    col = (t % GRID_W).astype(np.float32)
    lane = np.arange(LANES)
    d = lane % NA_HEAD_DIM
    dd = d % half
    fi = dd % (half // 2)
    inv_freq = (ROPE_BASE ** (-(2.0 * fi.astype(np.float32)) / half)).astype(np.float32)
    pos = np.where((d < half)[None, :], row[:, None], col[:, None])
    ang = (pos * inv_freq[None, :]).astype(np.float32)
    first = (dd < half // 2)[None, :]
    cos = np.cos(ang.astype(np.float64)).astype(np.float32)
    sin = np.sin(ang.astype(np.float64)).astype(np.float32)
    zero = np.zeros_like(sin)
    return jnp.asarray(cos), jnp.asarray(np.where(first, -sin, zero)), jnp.asarray(np.where(first, zero, sin))


def _na_body(qrot_ref, qpl_ref, k_ref, v_ref, kc_ref, vc_ref, bias_ref, o_ref):
    m = pl.program_id(2)
    lane = lax.broadcasted_iota(I32, (1, LANES), 1)
    first_head = lane < NA_HEAD_DIM
    hm0, hm1 = _mask_bf16(first_head), _mask_bf16(lane >= NA_HEAD_DIM)
    nt = (((1,), (1,)), ((), ()))
    rows = range(NA_ROWS_PER_STEP)

    def stacked(ref):
        parts = []
        for i in rows:
            q = ref[0, i * GRID_W:(i + 1) * GRID_W, :]
            parts += [q * hm0, q * hm1]
        return jnp.concatenate(parts, axis=0)

    sc = lax.dot_general(stacked(qpl_ref), kc_ref[0], nt, preferred_element_type=F32)

    qs = stacked(qrot_ref)
    k0s, s_parts = [], []
    blk = 2 * GRID_W
    for i in rows:
        r = m * NA_ROWS_PER_STEP + i
        r0 = jnp.clip(r - NA_WIN_H // 2, 0, N_ROWS - NA_WIN_H)
        di = r0 - r + (NA_WIN_H - 1)
        k0s.append(pl.multiple_of(r0 * GRID_W, GRID_W))
        s_parts.append(lax.dot_general(qs[i * blk:(i + 1) * blk], k_ref[0, pl.ds(k0s[i], NA_KEYS), :], nt,
                                       preferred_element_type=F32) + bias_ref[0, di].reshape(blk, NA_KEYS))
    s = jnp.concatenate([jnp.concatenate(s_parts, axis=0), sc], axis=1)
    p = jnp.exp(s - jnp.max(s, axis=-1, keepdims=True))
    den = jnp.sum(p, axis=-1, keepdims=True)
    pb = p.astype(BF16)
    acc = jnp.concatenate([jnp.dot(pb[i * blk:(i + 1) * blk, :NA_KEYS], v_ref[0, pl.ds(k0s[i], NA_KEYS), :],
                                   preferred_element_type=F32) for i in rows], axis=0)
    o = (acc + jnp.dot(pb[:, NA_KEYS:], vc_ref[0], preferred_element_type=F32)) / den
    o_ref[0] = jnp.concatenate([jnp.where(first_head, o[i * blk:i * blk + GRID_W], o[i * blk + GRID_W:(i + 1) * blk])
                                for i in rows], axis=0).astype(BF16)


def _na_bias(rpb):
    c = np.arange(GRID_W)
    cq = np.arange(GRID_W)
    win_c0 = np.clip(cq - NA_WIN_W // 2, 0, GRID_W - NA_WIN_W)
    in_win = (c[None, :] >= win_c0[:, None]) & (c[None, :] < win_c0[:, None] + NA_WIN_W)
    dc = np.clip(c[None, :] - cq[:, None] + NA_WIN_W - 1, 0, 2 * NA_WIN_W - 2)
    pick = (dc[None, :, :] == np.arange(2 * NA_WIN_W - 1)[:, None, None]).astype(np.float32)
    t = jnp.einsum('hrj,jqc->hqrc', rpb.astype(F32), jnp.asarray(pick), precision=HIGHEST)
    t = jnp.where(jnp.asarray(in_win)[None, :, None, :], t, -jnp.inf)
    n_dr = 2 * NA_WIN_H - 1
    t = t.reshape(N_NA_HEADS // 2, 2, GRID_W, n_dr * GRID_W)
    return jnp.stack([t[..., di * GRID_W:di * GRID_W + NA_KEYS] for di in range(NA_WIN_H)], axis=1)


def _na_call(qrot, qpl, krot, v, kc, vc, bias):
    bsz = qrot.shape[0]
    tq = NA_ROWS_PER_STEP * GRID_W
    qspec = pl.BlockSpec((1, tq, LANES), lambda b, p, m: (b, m, p))
    kspec = pl.BlockSpec((1, SEQ, LANES), lambda b, p, m: (b, 0, p))
    cspec = pl.BlockSpec((1, CTX_LEN, LANES), lambda b, p, m: (b, 0, p))
    return pl.pallas_call(
        _na_body,
        out_shape=jax.ShapeDtypeStruct((bsz, SEQ, NA_WIDTH), BF16),
        grid=(bsz, N_NA_HEADS // 2, N_ROWS // NA_ROWS_PER_STEP),
        in_specs=[qspec, qspec, kspec, kspec, cspec, cspec,
                  pl.BlockSpec((1, NA_WIN_H, 2, GRID_W, NA_KEYS), lambda b, p, m: (p, 0, 0, 0, 0))],
        out_specs=qspec,
        compiler_params=_cparams(("arbitrary", "arbitrary", "arbitrary")),
        name="na",
    )(qrot, qpl, krot, v, kc, vc, bias)


def _split3(v):
    hi = v.astype(BF16)
    r1 = v - hi.astype(F32)
    mid = r1.astype(BF16)
    lo = (r1 - mid.astype(F32)).astype(BF16)
    return hi, mid, lo


HG_NCHUNK = HG_BLOCK // HG_CHUNK


def _hg_consts(tri_ref, keep_ref, spread_ref):
    nb = HG_BLOCK
    rr = lax.broadcasted_iota(I32, (nb, nb), 0)
    cc = lax.broadcasted_iota(I32, (nb, nb), 1)
    same = (rr // HG_CHUNK) == (cc // HG_CHUNK)
    fwd = jnp.where(same & (cc <= rr), 1.0, 0.0)
    bwd = jnp.where(same & (cc >= rr), 1.0, 0.0)
    tri_ref[0] = fwd.astype(BF16)
    tri_ref[1] = bwd.astype(BF16)
    keep_ref[...] = jnp.concatenate([fwd, bwd], axis=0)
    r8 = lax.broadcasted_iota(I32, (nb, HG_NCHUNK * HG_DK), 0) // HG_CHUNK
    g8 = lax.broadcasted_iota(I32, (nb, HG_NCHUNK * HG_DK), 1) // HG_DK
    spread_ref[...] = _mask_bf16(r8 == g8)


def _hg_pair(zf, zb, qf, qb, vf, vb, stf, stb, lb, tri_ref, keep_ref, spread_ref, *, want_out):
    nb = HG_BLOCK
    spread = spread_ref[...]
    v = (vf, vb)

    sig = jax.nn.sigmoid(jnp.concatenate([zf, zb], axis=0))
    lf = jnp.log(lb + (1.0 - lb) * sig)
    kk = (1.0 - lb) * (1.0 - sig)
    parts = jnp.concatenate(_split3(lf), axis=1)
    sums = jnp.concatenate([jnp.dot(tri_ref[d], parts[d * nb:(d + 1) * nb], preferred_element_type=F32)
                            for d in range(2)], axis=0)
    bcum = sums[:, :HG_DK] + sums[:, HG_DK:2 * HG_DK] + sums[:, 2 * HG_DK:]
    end_rows = [d * nb + c * HG_CHUNK + (HG_CHUNK - 1 if d == 0 else 0) for d in range(2) for c in range(HG_NCHUNK)]
    ends = [bcum[r:r + 1, :] for r in end_rows]
    btot = jnp.concatenate([jnp.broadcast_to(e, (HG_CHUNK, HG_DK)) for e in ends], axis=0)
    dec = jnp.exp(jnp.concatenate(ends, axis=0))
    ke = (kk * jnp.exp(btot - bcum)).astype(BF16)

    new_states, entering = [], []
    for d, st in enumerate((stf, stb)):
        kvt = lax.dot_general(v[d], jnp.concatenate([ke[d * nb:(d + 1) * nb]] * HG_NCHUNK, axis=1) * spread,
                              (((0,), (0,)), ((), ())), preferred_element_type=F32)
        ent = [None] * HG_NCHUNK
        for c in (range(HG_NCHUNK) if d == 0 else range(HG_NCHUNK - 1, -1, -1)):
            ent[c] = st
            i_dec = d * HG_NCHUNK + c
            st = st * dec[i_dec:i_dec + 1, :] + kvt[:, c * HG_DK:(c + 1) * HG_DK]
        new_states.append(st)
        entering.append(ent)
    if not want_out:
        return None, None, new_states[0], new_states[1]

    qd = (jnp.concatenate([qf, qb], axis=0).astype(F32) * jnp.exp(bcum)).astype(BF16)
    kd = (kk * jnp.exp(-bcum)).astype(BF16)
    nt = (((1,), (1,)), ((), ()))
    a = jnp.concatenate([lax.dot_general(qd[d * nb:(d + 1) * nb], kd[d * nb:(d + 1) * nb], nt,
                                         preferred_element_type=F32) for d in range(2)], axis=0)
    a = jnp.where(keep_ref[...] > 0.5, a, 0.0).astype(BF16)
    outs = []
    for d in range(2):
        qd_d = qd[d * nb:(d + 1) * nb]
        inter = jnp.concatenate(
            [lax.dot_general(qd_d[c * HG_CHUNK:(c + 1) * HG_CHUNK], entering[d][c].astype(BF16), nt,
                             preferred_element_type=F32) for c in range(HG_NCHUNK)], axis=0)
        outs.append(jnp.dot(a[d * nb:(d + 1) * nb], v[d], preferred_element_type=F32) + inter)
    return outs[0], outs[1], new_states[0], new_states[1]


def _hgrn_body(q_ref, zf_ref, zb_ref, i_ref, g_ref, qc_ref, zfc_ref, zbc_ref, ic_ref, lb_ref, ng_ref,
               o_ref, accf_ref, accb_ref, stf_ref, stb_ref, tri_ref, keep_ref, spread_ref):
    lbl = lb_ref[...]
    lbe = jnp.exp(lbl - jnp.max(lbl, axis=0, keepdims=True))
    lb = lbe[0:1, :] / jnp.sum(lbe, axis=0, keepdims=True)
    _hg_consts(tri_ref, keep_ref, spread_ref)
    pair = functools.partial(_hg_pair, lb=lb, tri_ref=tri_ref, keep_ref=keep_ref, spread_ref=spread_ref)
    zero_state = jnp.zeros((HG_DK, HG_DK), F32)
    _, _, stf, stb = pair(zfc_ref[0], zbc_ref[0], qc_ref[0], qc_ref[0], ic_ref[0], ic_ref[0],
                          zero_state, zero_state, want_out=False)
    stf_ref[...] = stf
    stb_ref[...] = stb
    n_blk = SEQ // HG_BLOCK

    def scan_step(n, carry):
        rows_f = pl.ds(pl.multiple_of(n * HG_BLOCK, HG_BLOCK), HG_BLOCK)
        rows_b = pl.ds(pl.multiple_of((n_blk - 1 - n) * HG_BLOCK, HG_BLOCK), HG_BLOCK)
        o_f, o_b, st_f, st_b = pair(zf_ref[0, rows_f, :], zb_ref[0, rows_b, :], q_ref[0, rows_f, :],
                                    q_ref[0, rows_b, :], i_ref[0, rows_f, :], i_ref[0, rows_b, :],
                                    stf_ref[...], stb_ref[...], want_out=True)
        stf_ref[...] = st_f
        stb_ref[...] = st_b
        accf_ref[rows_f, :] = o_f
        accb_ref[rows_b, :] = o_b
        return carry

    lax.fori_loop(0, n_blk, scan_step, 0, unroll=8)

    def readout_step(n, carry):
        rows = pl.ds(pl.multiple_of(n * HG_BLOCK, HG_BLOCK), HG_BLOCK)
        tot = accf_ref[rows, :] + accb_ref[rows, :]
        y = _rms(tot, ng_ref[...]) * jax.nn.sigmoid(g_ref[0, rows, :].astype(F32))
        o_ref[0, rows, :] = y.astype(BF16)
        return carry

    lax.fori_loop(0, n_blk, readout_step, 0, unroll=2)


def _hgrn_call(qh, zf, zb, ih, gh, qc, zfc, zbc, ic, lb_logits, norm_g):
    bsz = qh.shape[0]
    seq = pl.BlockSpec((1, SEQ, HG_DK), lambda b, h: (b, 0, h))
    ctx = pl.BlockSpec((1, CTX_LEN, HG_DK), lambda b, h: (b, 0, h))
    return pl.pallas_call(
        _hgrn_body,
        out_shape=jax.ShapeDtypeStruct((bsz, SEQ, N_HG_HEADS * HG_DK), BF16),
        grid=(bsz, N_HG_HEADS),
        in_specs=[seq, seq, seq, seq, seq, ctx, ctx, ctx, ctx,
                  pl.BlockSpec((lb_logits.shape[0], HG_DK), lambda b, h: (0, h)),
                  pl.BlockSpec((1, HG_DK), lambda b, h: (0, 0))],
        out_specs=seq,
        scratch_shapes=[pltpu.VMEM((SEQ, HG_DK), F32),
                        pltpu.VMEM((SEQ, HG_DK), F32),
                        pltpu.VMEM((HG_DK, HG_DK), F32),
                        pltpu.VMEM((HG_DK, HG_DK), F32),
                        pltpu.VMEM((2, HG_BLOCK, HG_BLOCK), BF16),
                        pltpu.VMEM((2 * HG_BLOCK, HG_BLOCK), F32),
                        pltpu.VMEM((HG_BLOCK, HG_NCHUNK * HG_DK), BF16)],
        compiler_params=_cparams(("arbitrary", "arbitrary")),
        name="hgrn",
    )(qh, zf, zb, ih, gh, qc, zfc, zbc, ic, lb_logits, norm_g)


def _outproj_body(na_ref, hg_ref, x_ref, w_ref, gt1_ref, sh2_ref, sc2_ref, gpost_ref, gpre_ref, wr_ref,
                  x1_ref, h2_ref, lt_ref):
    mix = (jnp.dot(na_ref[0], w_ref[:NA_WIDTH, :], preferred_element_type=F32)
           + jnp.dot(hg_ref[0], w_ref[NA_WIDTH:, :], preferred_element_type=F32))
    x1 = x_ref[0] + gt1_ref[0] * _rms(mix, gpost_ref[...])
    x1_ref[0] = x1
    h2 = _rms(x1, gpre_ref[...]) * (1.0 + sc2_ref[0]) + sh2_ref[0]
    h2_ref[0] = h2.astype(BF16)
    lt_ref[0] = lax.dot_general(wr_ref[...], h2, (((1,), (1,)), ((), ())), precision=HIGHEST,
                                preferred_element_type=F32)


def _outproj_call(na_o, hg_o, x, w_out_b, mod, g_post1, g_pre2, w_router_t):
    bsz, s, d = x.shape
    tm = OUT_TM
    half = pl.BlockSpec((1, tm, NA_WIDTH), lambda b, i: (b, i, 0))
    tok = pl.BlockSpec((1, tm, d), lambda b, i: (b, i, 0))
    gt1, sh2, sc2 = (_mod_spec(k, lambda b: b, 2) for k in (2, 3, 4))
    par = pl.BlockSpec((1, d), lambda b, i: (0, 0))
    return pl.pallas_call(
        _outproj_body,
        out_shape=[jax.ShapeDtypeStruct((bsz, s, d), F32),
                   jax.ShapeDtypeStruct((bsz, s, d), BF16),
                   jax.ShapeDtypeStruct((bsz, N_EXPERTS, s), F32)],
        grid=(bsz, s // tm),
        in_specs=[half, half, tok, pl.BlockSpec((d, d), lambda b, i: (0, 0)), gt1, sh2, sc2, par, par,
                  pl.BlockSpec((N_EXPERTS, d), lambda b, i: (0, 0))],
        out_specs=[tok, tok, pl.BlockSpec((1, N_EXPERTS, tm), lambda b, i: (b, 0, i))],
        compiler_params=_cparams(("arbitrary", "arbitrary")),
        name="outproj",
    )(na_o, hg_o, x, w_out_b, mod, mod, mod, g_post1, g_pre2, w_router_t)


def _route_body(lt_ref, posm_ref, gate_ref, tab_ref, *, cap):
    l = lt_ref[0]
    mx = jnp.max(l, axis=0, keepdims=True)
    ex = jnp.exp(l - mx)
    aff = ex / jnp.sum(ex, axis=0, keepdims=True)
    capf = jnp.float32(cap)

    def count(mask):
        return jnp.sum(jnp.where(mask, 1.0, 0.0), axis=1, keepdims=True)

    def enough(v):
        return count(aff >= v) >= capf

    def bit_step(it, thr):
        cand = thr | (jnp.int32(1) << (30 - it))
        return jnp.where(enough(pltpu.bitcast(cand, F32)), cand, thr)

    thr = lax.fori_loop(0, 31, bit_step, jnp.zeros((N_EXPERTS, 1), I32))
    lo = pltpu.bitcast(thr, F32)
    hi = pltpu.bitcast(thr + 1, F32)

    def mid_step(it, lohi):
        lo, hi = lohi
        mid = 0.5 * (lo + hi)
        ok = enough(mid)
        return jnp.where(ok, mid, lo), jnp.where(ok, hi, mid)

    lo, hi = lax.fori_loop(0, 30, mid_step, (lo, hi))
    gt = aff >= hi
    eq = (aff >= lo) & jnp.logical_not(gt)
    need = capf - count(gt)

    rr = lax.broadcasted_iota(I32, (ROUTE_BLK, ROUTE_BLK), 0)
    cc = lax.broadcasted_iota(I32, (ROUTE_BLK, ROUTE_BLK), 1)
    upper = _mask_bf16(rr <= cc)
    lane = lax.broadcasted_iota(I32, (N_EXPERTS, LANES), 1)

    off_eq = jnp.zeros((N_EXPERTS, 1), F32)
    off_sel = jnp.zeros((N_EXPERTS, 1), F32)
    tab = jnp.zeros((N_EXPERTS, LANES), F32)
    for j in range(N_ROUTE_BLK):
        sl = slice(j * ROUTE_BLK, (j + 1) * ROUTE_BLK)
        eq_j = eq[:, sl]
        eq_b = _mask_bf16(eq_j)
        incl_eq = jnp.dot(eq_b, upper, preferred_element_type=F32) + off_eq
        rank_eq = incl_eq - eq_b.astype(F32)
        sel_j = gt[:, sl] | (eq_j & (rank_eq < need))
        sel_b = _mask_bf16(sel_j)
        incl_sel = jnp.dot(sel_b, upper, preferred_element_type=F32) + off_sel
        pos = incl_sel - sel_b.astype(F32)
        posm_ref[0, :, sl] = jnp.where(sel_j, pos.astype(I32), -1)
        gate_ref[0, :, sl] = jnp.where(sel_j, aff[:, sl], 0.0)
        tab = jnp.where(lane == j, off_sel, tab)
        off_eq = incl_eq[:, ROUTE_BLK - 1:ROUTE_BLK]
        off_sel = incl_sel[:, ROUTE_BLK - 1:ROUTE_BLK]
    tab = jnp.where(lane == N_ROUTE_BLK, off_sel, tab)
    tab_ref[0] = tab.astype(I32)


def _route_call(lt, cap):
    bsz = lt.shape[0]
    big = pl.BlockSpec((1, N_EXPERTS, SEQ), lambda b: (b, 0, 0))
    return pl.pallas_call(
        functools.partial(_route_body, cap=cap),
        out_shape=[jax.ShapeDtypeStruct((bsz, N_EXPERTS, SEQ), I32),
                   jax.ShapeDtypeStruct((bsz, N_EXPERTS, SEQ), F32),
                   jax.ShapeDtypeStruct((bsz, N_EXPERTS, LANES), I32)],
        grid=(bsz,),
        in_specs=[big],
        out_specs=[big, big, pl.BlockSpec((1, N_EXPERTS, LANES), lambda b: (b, 0, 0))],
        compiler_params=_cparams(("arbitrary",)),
        name="route",
    )(lt)


def _chunk_range(start, end):
    c_lo = start // SLOT_CHUNK
    n = jnp.where(end > start, (end - 1) // SLOT_CHUNK - c_lo + 1, 0)
    return c_lo, n


DISP_NE = 8


def _dispatch_body(tab_ref, posm_ref, gate_ref, h2_ref, x_ref, gsl_ref, *, cap):
    b = pl.program_id(0)
    hf = pl.program_id(1)
    blk = pl.program_id(2)
    tiles_per_step = MOE_TD // ROUTE_BLK
    chunks_per_batch = cap // SLOT_CHUNK

    @pl.when(blk == 0)
    def _():
        x_ref[...] = jnp.zeros_like(x_ref)
        gsl_ref[...] = jnp.zeros_like(gsl_ref)

    slot = lax.broadcasted_iota(I32, (SLOT_CHUNK, ROUTE_BLK), 0)

    def expert_step(k, carry):
        tab_base = (b * N_EXPERTS + hf * DISP_NE + k) * TAB_STRIDE + blk * tiles_per_step

        def add_chunk(sub, chunk):
            cols = slice(sub * ROUTE_BLK, (sub + 1) * ROUTE_BLK)
            base = pl.multiple_of(chunk * SLOT_CHUNK, SLOT_CHUNK)
            onehot = _mask_bf16(posm_ref[0, pl.ds(k, 1), cols] == slot + base)
            got = jnp.dot(onehot, h2_ref[cols, :], preferred_element_type=F32).astype(BF16)
            x_ref[k, pl.ds(base, SLOT_CHUNK), :] = x_ref[k, pl.ds(base, SLOT_CHUNK), :] + got
            g3 = [p.astype(F32) for p in _split3(gate_ref[0, pl.ds(k, 1), cols])]
            g8 = jnp.concatenate(g3 + [jnp.zeros((8 - len(g3), ROUTE_BLK), F32)], axis=0).astype(BF16)
            gsl_ref[k, chunk] = gsl_ref[k, chunk] + lax.dot_general(g8, onehot, (((1,), (1,)), ((), ())),
                                                                    preferred_element_type=F32)

        ranges = [_chunk_range(tab_ref[tab_base + sub], tab_ref[tab_base + sub + 1]) for sub in range(tiles_per_step)]
        for sub, (c_lo, _) in enumerate(ranges):
            add_chunk(sub, jnp.minimum(c_lo, chunks_per_batch - 1))
        for sub, (c_lo, n) in enumerate(ranges):
            def more(kk, c, sub=sub, c_lo=c_lo):
                add_chunk(sub, c_lo + kk)
                return c

            lax.fori_loop(1, n, more, 0)
        return carry

    lax.fori_loop(0, DISP_NE, expert_step, 0)


def _dispatch_call(tab_flat, posm, gate, h2_flat, *, bsz, cap):
    d = D_MODEL
    n_blk = SEQ // MOE_TD
    chunks = cap // SLOT_CHUNK
    sel = pl.BlockSpec((1, DISP_NE, MOE_TD), lambda b, hf, blk, tab: (b, hf, blk))
    grid_spec = pltpu.PrefetchScalarGridSpec(
        num_scalar_prefetch=1,
        grid=(bsz, N_EXPERTS // DISP_NE, n_blk),
        in_specs=[sel, sel, pl.BlockSpec((MOE_TD, d), lambda b, hf, blk, tab: (b * n_blk + blk, 0))],
        out_specs=[pl.BlockSpec((DISP_NE, cap, d), lambda b, hf, blk, tab: (hf, b, 0)),
                   pl.BlockSpec((DISP_NE, chunks, 8, SLOT_CHUNK), lambda b, hf, blk, tab: (hf, b, 0, 0))],
    )
    return pl.pallas_call(
        functools.partial(_dispatch_body, cap=cap),
        out_shape=[jax.ShapeDtypeStruct((N_EXPERTS, bsz * cap, d), BF16),
                   jax.ShapeDtypeStruct((N_EXPERTS, bsz * chunks, 8, SLOT_CHUNK), F32)],
        grid_spec=grid_spec,
        compiler_params=_cparams(("arbitrary", "arbitrary", "arbitrary")),
        name="dispatch",
    )(tab_flat, posm, gate, h2_flat)


def _moe_body(x_ref, gsl_ref, wg_ref, wu_ref, wd_ref, y_ref, acc_scr):
    s = pl.program_id(1)

    chunks_per_tile = MOE_TM // SLOT_CHUNK

    def mlp(width, first, last):
        nt = (((1,), (1,)), ((), ()))

        def m_step(mi, carry):
            r0 = pl.multiple_of(mi * MOE_TM, MOE_TM)
            rows = pl.ds(r0, MOE_TM)
            xm = x_ref[0, rows, :]
            g = lax.dot_general(xm, wg_ref[0, :width, :].astype(BF16), nt, preferred_element_type=F32)
            u = lax.dot_general(xm, wu_ref[0, :width, :].astype(BF16), nt, preferred_element_type=F32)
            hid = (g * jax.nn.sigmoid(g) * u).astype(BF16)
            out = jnp.dot(hid, wd_ref[0, :width, :].astype(BF16), preferred_element_type=F32)
            if not first:
                out = acc_scr[rows, :] + out
            if last:
                gsl = gsl_ref[0, pl.ds(mi * chunks_per_tile, chunks_per_tile)]
                gcol = jnp.concatenate([jnp.sum(gsl[ci].T, axis=1, keepdims=True) for ci in range(chunks_per_tile)],
                                       axis=0)
                y_ref[0, rows, :] = (out * gcol).astype(BF16)
            else:
                acc_scr[rows, :] = out
            return carry

        lax.fori_loop(0, acc_scr.shape[0] // MOE_TM, m_step, 0)

    @pl.when(s == 0)
    def _():
        mlp(MOE_TF, True, False)

    @pl.when((s > 0) & (s < MOE_NF - 1))
    def _():
        mlp(MOE_TF, False, False)

    @pl.when(s == MOE_NF - 1)
    def _():
        mlp(MOE_F_LAST, False, True)


def _moe_call(xs, gsl, w_gate_t, w_up_t, w_down):
    n_e, slots, d = xs.shape
    per_e = pl.BlockSpec((1, slots, d), lambda e, s: (e, 0, 0))
    wspec = pl.BlockSpec((1, MOE_TF, d), lambda e, s: (e, s, 0))
    return pl.pallas_call(
        _moe_body,
        out_shape=jax.ShapeDtypeStruct((n_e, slots, d), BF16),
        grid=(n_e, MOE_NF),
        in_specs=[per_e, pl.BlockSpec((1,) + gsl.shape[1:], lambda e, s: (e, 0, 0, 0)), wspec, wspec, wspec],
        out_specs=per_e,
        scratch_shapes=[pltpu.VMEM((slots, d), F32)],
        compiler_params=_cparams(("arbitrary", "arbitrary")),
        name="moe",
    )(xs, gsl, w_gate_t, w_up_t, w_down)


def _combine_body(tab_ref, y_ref, posm_ref, x1_ref, gt2_ref, gpost_ref, o_ref, acc_ref, *, cap):
    n_e = N_EXPERTS
    b = pl.program_id(0)
    j = pl.program_id(1)
    pm = posm_ref[0].T
    win = 2 * SLOT_CHUNK
    slot_w = lax.broadcasted_iota(I32, (ROUTE_BLK, win), 1)
    slot_c = lax.broadcasted_iota(I32, (ROUTE_BLK, SLOT_CHUNK), 1)
    total = jnp.zeros((ROUTE_BLK, D_MODEL), F32)
    bases, extras = [], []
    for k in range(n_e):
        tab_base = (b * N_EXPERTS + k) * TAB_STRIDE
        start = tab_ref[tab_base + j]
        end = tab_ref[tab_base + j + 1]
        base = pl.multiple_of(jnp.minimum((start // SLOT_CHUNK) * SLOT_CHUNK, cap - win), SLOT_CHUNK)
        total = total + jnp.dot(_mask_bf16(pm[:, k:k + 1] == slot_w + base), y_ref[k, pl.ds(base, win), :],
                                preferred_element_type=F32)
        bases.append(base)
        extras.append(jnp.maximum(end - (base + win) + SLOT_CHUNK - 1, 0) // SLOT_CHUNK)
    acc_ref[...] = total

    @pl.when(sum(extras) > 0)
    def _():
        for k in range(n_e):
            def extra_step(kk, carry, k=k):
                b2 = pl.multiple_of(bases[k] + win + kk * SLOT_CHUNK, SLOT_CHUNK)
                acc_ref[...] += jnp.dot(_mask_bf16(pm[:, k:k + 1] == slot_c + b2),
                                        y_ref[k, pl.ds(b2, SLOT_CHUNK), :], preferred_element_type=F32)
                return carry

            lax.fori_loop(0, extras[k], extra_step, 0)

    o_ref[0] = x1_ref[0] + gt2_ref[0] * _rms(acc_ref[...], gpost_ref[...])


def _combine_call(tab_flat, y, posm, x1, mod, g_post2, *, cap):
    bsz, n_e, s = posm.shape
    d = D_MODEL
    tok = pl.BlockSpec((1, ROUTE_BLK, d), lambda b, j, tab: (b, j, 0))
    grid_spec = pltpu.PrefetchScalarGridSpec(
        num_scalar_prefetch=1,
        grid=(bsz, s // ROUTE_BLK),
        in_specs=[
            pl.BlockSpec((n_e, cap, d), lambda b, j, tab: (0, b, 0), pipeline_mode=pl.Buffered(1)),
            pl.BlockSpec((1, n_e, ROUTE_BLK), lambda b, j, tab: (b, 0, j)),
            tok,
            _mod_spec(5, lambda b: b, 3),
            pl.BlockSpec((1, d), lambda b, j, tab: (0, 0))],
        out_specs=tok,
        scratch_shapes=[pltpu.VMEM((ROUTE_BLK, d), F32)],
    )
    return pl.pallas_call(
        functools.partial(_combine_body, cap=cap),
        out_shape=jax.ShapeDtypeStruct((bsz, s, d), F32),
        grid_spec=grid_spec,
        compiler_params=_cparams(("arbitrary", "arbitrary")),
        name="combine",
    )(tab_flat, y, posm, x1, mod, g_post2)


def kernel(x, c, ctx, c_ctx, w_mod, b_mod, g_pre1, g_post1, g_pre2, g_post2, w_in, w_out, na_rpb,
           hg_lb_logits, hg_norm, w_router, w_gate, w_up, w_down):
    bsz, seq, d = x.shape
    assert (seq, d) == (SEQ, D_MODEL) and ctx.shape[1] == CTX_LEN and w_mod.shape[0] == 1
    cap = 2 * seq // N_EXPERTS
    assert cap % SLOT_CHUNK == 0

    assert bsz < MOD_ROWS
    mod = _mod_call(c, c_ctx, w_mod, b_mod)

    w_in_b = w_in[0].astype(BF16)
    w_out_b = w_out[0].astype(BF16)
    cos, sa, sb = _rope_tables(seq)

    qrot, qpl, krot, v, qh, zf, zb, ih, gh = _inproj_call(
        x, mod, lambda b: b, g_pre1, w_in_b, cos, sa, sb, rope=True, tm=IN_TM)
    ctx_flat = ctx.reshape(1, bsz * CTX_LEN, d)
    ctx_out = _inproj_call(ctx_flat, mod, lambda b: bsz, g_pre1, w_in_b,
                           cos[:bsz * CTX_LEN], sa[:bsz * CTX_LEN], sb[:bsz * CTX_LEN], rope=False, tm=CTX_LEN)
    _, _, kc, vc, qc, zfc, zbc, ic, _ = [a.reshape(bsz, CTX_LEN, COL_GROUP) for a in ctx_out]

    na_o = _na_call(qrot, qpl, krot, v, kc, vc, _na_bias(na_rpb[0]))
    hg_o = _hgrn_call(qh, zf, zb, ih, gh, qc, zfc, zbc, ic, hg_lb_logits, hg_norm)

    x1, h2, lt = _outproj_call(na_o, hg_o, x, w_out_b, mod, g_post1, g_pre2, w_router[0].T)

    posm, gate, tab = _route_call(lt, cap)
    tab_flat = tab.reshape(-1)
    xs, gsl = _dispatch_call(tab_flat, posm, gate, h2.reshape(bsz * seq, d), bsz=bsz, cap=cap)
    y = _moe_call(xs, gsl, jnp.swapaxes(w_gate[0], 1, 2), jnp.swapaxes(w_up[0], 1, 2), w_down[0])

    return _combine_call(tab_flat, y, posm, x1, mod, g_post2, cap=cap)
```

```python
import functools

import jax
import jax.numpy as jnp
import numpy as np
from jax import lax
from jax.experimental import pallas as pl
from jax.experimental.pallas import tpu as pltpu

F32 = jnp.float32
BF16 = jnp.bfloat16
I32 = jnp.int32
HIGHEST = lax.Precision.HIGHEST

D_MODEL = 1024
GRID_W = 64
N_ROWS = 128
SEQ = 8192
CTX_LEN = 256
NA_HEAD_DIM = 64
N_NA_HEADS = 8
NA_WIDTH = 512
NA_WIN_H = 8
NA_WIN_W = 16
ROPE_BASE = 10000.0
HG_DK = 128
N_HG_HEADS = 4
HG_CHUNK = 32
N_EXPERTS = 16
D_EXPERT = 2752
RMS_EPS = 1e-6
COL_GROUP = 512

LANES = 128
SUBLANES = 8
VMEM_LIMIT_BYTES = 58 * 1024 * 1024

IN_TM = 1024
NA_ROWS_PER_STEP = 64
NA_KEYS = NA_WIN_H * GRID_W
HG_BLOCK = 256
OUT_TM = 1024
ROUTE_BLK = 256
N_ROUTE_BLK = SEQ // ROUTE_BLK
SLOT_CHUNK = 128
TAB_STRIDE = LANES
MOE_TD = 2048
MOE_TF = 768
MOE_NF = -(-D_EXPERT // MOE_TF)
MOE_F_LAST = D_EXPERT - (MOE_NF - 1) * MOE_TF
MOE_TM = 1024


def _cparams(sem):
    return pltpu.CompilerParams(dimension_semantics=sem, vmem_limit_bytes=VMEM_LIMIT_BYTES)


def _mask_bf16(mask):
    return jnp.where(mask, 1.0, 0.0).astype(BF16)


def _rms(v, g):
    return v * lax.rsqrt(jnp.mean(v * v, axis=-1, keepdims=True) + RMS_EPS) * g


MOD_ROWS = 8
N_MOD = 6


def _mod_body(c_ref, cc_ref, w_ref, b_ref, o_ref, rows_scr):
    bsz = c_ref.shape[0]
    rows_scr[...] = jnp.zeros_like(rows_scr)
    rows_scr[0:bsz, :] = c_ref[...]
    rows_scr[bsz:bsz + 1, :] = cc_ref[...]
    cv = rows_scr[...]
    s = cv * jax.nn.sigmoid(cv)
    o_ref[...] = jnp.dot(s, w_ref[0], precision=HIGHEST, preferred_element_type=F32) + b_ref[...]


def _mod_call(c, c_ctx, w_mod, b_mod):
    bsz, d = c.shape
    n = w_mod.shape[2]
    tn = d
    mod = pl.pallas_call(
        _mod_body,
        out_shape=jax.ShapeDtypeStruct((MOD_ROWS, n), F32),
        grid=(n // tn,),
        in_specs=[pl.BlockSpec((bsz, d), lambda j: (0, 0)),
                  pl.BlockSpec((1, d), lambda j: (0, 0)),
                  pl.BlockSpec((1, d, tn), lambda j: (0, 0, j)),
                  pl.BlockSpec((1, tn), lambda j: (0, j))],
        out_specs=pl.BlockSpec((MOD_ROWS, tn), lambda j: (0, j)),
        scratch_shapes=[pltpu.VMEM((MOD_ROWS, d), F32)],
        compiler_params=_cparams(("arbitrary",)),
        name="mod",
    )(c, c_ctx[None, :], w_mod, b_mod)
    return mod.reshape(MOD_ROWS * N_MOD, 1, d)


def _mod_spec(k, row_of, n_grid):
    if n_grid == 2:
        return pl.BlockSpec((1, 1, D_MODEL), lambda b, i: (row_of(b) * N_MOD + k, 0, 0))
    return pl.BlockSpec((1, 1, D_MODEL), lambda b, i, tab: (row_of(b) * N_MOD + k, 0, 0))


def _rope128(pg, cos, sa, sb):
    return pg * cos + pltpu.roll(pg, LANES - 16, axis=1) * sa + pltpu.roll(pg, 16, axis=1) * sb


def _inproj_body(x_ref, sh_ref, sc_ref, g_ref, w_ref, cos_ref, sa_ref, sb_ref,
                 qrot_ref, qpl_ref, krot_ref, v_ref, qh_ref, zf_ref, zb_ref, ih_ref, gh_ref, *, rope):
    xv = x_ref[0]
    h = _rms(xv, g_ref[...]) * (1.0 + sc_ref[0]) + sh_ref[0]
    hb = h.astype(BF16)

    def group(j):
        return jnp.dot(hb, w_ref[:, j * COL_GROUP:(j + 1) * COL_GROUP], preferred_element_type=F32)

    def rotated(p):
        if not rope:
            return p
        cos, sa, sb = cos_ref[...], sa_ref[...], sb_ref[...]
        return jnp.concatenate(
            [_rope128(p[:, k * LANES:(k + 1) * LANES], cos, sa, sb) for k in range(COL_GROUP // LANES)], axis=1)

    scale = NA_HEAD_DIM ** -0.5
    p = group(0)
    qpl_ref[0] = (p * scale).astype(BF16)
    qrot_ref[0] = (rotated(p) * scale).astype(BF16)
    krot_ref[0] = rotated(group(1)).astype(BF16)
    v_ref[0] = group(2).astype(BF16)
    qh_ref[0] = group(3).astype(BF16)
    zf_ref[0] = group(4)
    zb_ref[0] = group(5)
    ih_ref[0] = group(6).astype(BF16)
    gh_ref[0] = group(7).astype(BF16)


def _inproj_call(x3, mod, row_of, g_pre, w_in_b, cos, sa, sb, *, rope, tm):
    g, t, d = x3.shape
    n_out = 9
    dts = [BF16, BF16, BF16, BF16, BF16, F32, F32, BF16, BF16]
    tok = pl.BlockSpec((1, tm, d), lambda b, i: (b, i, 0))
    tab = pl.BlockSpec((tm, LANES), lambda b, i: (i, 0))
    outs = [pl.BlockSpec((1, tm, COL_GROUP), lambda b, i: (b, i, 0)) for _ in range(n_out)]
    return pl.pallas_call(
        functools.partial(_inproj_body, rope=rope),
        out_shape=[jax.ShapeDtypeStruct((g, t, COL_GROUP), dt) for dt in dts],
        grid=(g, t // tm),
        in_specs=[tok, _mod_spec(0, row_of, 2), _mod_spec(1, row_of, 2),
                  pl.BlockSpec((1, d), lambda b, i: (0, 0)),
                  pl.BlockSpec(w_in_b.shape, lambda b, i: (0, 0)),
                  tab, tab, tab],
        out_specs=outs,
        compiler_params=_cparams(("arbitrary", "arbitrary")),
        name="inproj_rope" if rope else "inproj_ctx",
    )(x3, mod, mod, g_pre, w_in_b, cos, sa, sb)


def _rope_tables(n_tok):
    half = NA_HEAD_DIM // 2
    t = np.arange(n_tok)
    row = (t // GRID_W).astype(np.float32)
    col = (t % GRID_W).astype(np.float32)
    lane = np.arange(LANES)
    d = lane % NA_HEAD_DIM
    dd = d % half
    fi = dd % (half // 2)
    inv_freq = (ROPE_BASE ** (-(2.0 * fi.astype(np.float32)) / half)).astype(np.float32)
    pos = np.where((d < half)[None, :], row[:, None], col[:, None])
    ang = (pos * inv_freq[None, :]).astype(np.float32)
    first = (dd < half // 2)[None, :]
    cos = np.cos(ang.astype(np.float64)).astype(np.float32)
    sin = np.sin(ang.astype(np.float64)).astype(np.float32)
    zero = np.zeros_like(sin)
    return jnp.asarray(cos), jnp.asarray(np.where(first, -sin, zero)), jnp.asarray(np.where(first, zero, sin))


def _na_body(qrot_ref, qpl_ref, k_ref, v_ref, kc_ref, vc_ref, bias_ref, o_ref):
    m = pl.program_id(2)
    lane = lax.broadcasted_iota(I32, (1, LANES), 1)
    first_head = lane < NA_HEAD_DIM
    hm0, hm1 = _mask_bf16(first_head), _mask_bf16(lane >= NA_HEAD_DIM)
    nt = (((1,), (1,)), ((), ()))
    rows = range(NA_ROWS_PER_STEP)

    def stacked(ref):
        parts = []
        for i in rows:
            q = ref[0, i * GRID_W:(i + 1) * GRID_W, :]
            parts += [q * hm0, q * hm1]
        return jnp.concatenate(parts, axis=0)

    sc = lax.dot_general(stacked(qpl_ref), kc_ref[0], nt, preferred_element_type=F32)

    qs = stacked(qrot_ref)
    k0s, s_parts = [], []
    blk = 2 * GRID_W
    for i in rows:
        r = m * NA_ROWS_PER_STEP + i
        r0 = jnp.clip(r - NA_WIN_H // 2, 0, N_ROWS - NA_WIN_H)
        di = r0 - r + (NA_WIN_H - 1)
        k0s.append(pl.multiple_of(r0 * GRID_W, GRID_W))
        s_parts.append(lax.dot_general(qs[i * blk:(i + 1) * blk], k_ref[0, pl.ds(k0s[i], NA_KEYS), :], nt,
                                       preferred_element_type=F32) + bias_ref[0, di].reshape(blk, NA_KEYS))
    s = jnp.concatenate([jnp.concatenate(s_parts, axis=0), sc], axis=1)
    p = jnp.exp(s - jnp.max(s, axis=-1, keepdims=True))
    den = jnp.sum(p, axis=-1, keepdims=True)
    pb = p.astype(BF16)
    acc = jnp.concatenate([jnp.dot(pb[i * blk:(i + 1) * blk, :NA_KEYS], v_ref[0, pl.ds(k0s[i], NA_KEYS), :],
                                   preferred_element_type=F32) for i in rows], axis=0)
    o = (acc + jnp.dot(pb[:, NA_KEYS:], vc_ref[0], preferred_element_type=F32)) / den
    o_ref[0] = jnp.concatenate([jnp.where(first_head, o[i * blk:i * blk + GRID_W], o[i * blk + GRID_W:(i + 1) * blk])
                                for i in rows], axis=0).astype(BF16)


def _na_bias(rpb):
    c = np.arange(GRID_W)
    cq = np.arange(GRID_W)
    win_c0 = np.clip(cq - NA_WIN_W // 2, 0, GRID_W - NA_WIN_W)
    in_win = (c[None, :] >= win_c0[:, None]) & (c[None, :] < win_c0[:, None] + NA_WIN_W)
    dc = np.clip(c[None, :] - cq[:, None] + NA_WIN_W - 1, 0, 2 * NA_WIN_W - 2)
    pick = (dc[None, :, :] == np.arange(2 * NA_WIN_W - 1)[:, None, None]).astype(np.float32)
    t = jnp.einsum('hrj,jqc->hqrc', rpb.astype(F32), jnp.asarray(pick), precision=HIGHEST)
    t = jnp.where(jnp.asarray(in_win)[None, :, None, :], t, -jnp.inf)
    n_dr = 2 * NA_WIN_H - 1
    t = t.reshape(N_NA_HEADS // 2, 2, GRID_W, n_dr * GRID_W)
    return jnp.stack([t[..., di * GRID_W:di * GRID_W + NA_KEYS] for di in range(NA_WIN_H)], axis=1)


def _na_call(qrot, qpl, krot, v, kc, vc, bias):
    bsz = qrot.shape[0]
    tq = NA_ROWS_PER_STEP * GRID_W
    qspec = pl.BlockSpec((1, tq, LANES), lambda b, p, m: (b, m, p))
    kspec = pl.BlockSpec((1, SEQ, LANES), lambda b, p, m: (b, 0, p))
    cspec = pl.BlockSpec((1, CTX_LEN, LANES), lambda b, p, m: (b, 0, p))
    return pl.pallas_call(
        _na_body,
        out_shape=jax.ShapeDtypeStruct((bsz, SEQ, NA_WIDTH), BF16),
        grid=(bsz, N_NA_HEADS // 2, N_ROWS // NA_ROWS_PER_STEP),
        in_specs=[qspec, qspec, kspec, kspec, cspec, cspec,
                  pl.BlockSpec((1, NA_WIN_H, 2, GRID_W, NA_KEYS), lambda b, p, m: (p, 0, 0, 0, 0))],
        out_specs=qspec,
        compiler_params=_cparams(("arbitrary", "arbitrary", "arbitrary")),
        name="na",
    )(qrot, qpl, krot, v, kc, vc, bias)


def _split3(v):
    hi = v.astype(BF16)
    r1 = v - hi.astype(F32)
    mid = r1.astype(BF16)
    lo = (r1 - mid.astype(F32)).astype(BF16)
    return hi, mid, lo


HG_NCHUNK = HG_BLOCK // HG_CHUNK


def _hg_consts(tri_ref, keep_ref, spread_ref):
    nb = HG_BLOCK
    rr = lax.broadcasted_iota(I32, (nb, nb), 0)
    cc = lax.broadcasted_iota(I32, (nb, nb), 1)
    same = (rr // HG_CHUNK) == (cc // HG_CHUNK)
    fwd = jnp.where(same & (cc <= rr), 1.0, 0.0)
    bwd = jnp.where(same & (cc >= rr), 1.0, 0.0)
    tri_ref[0] = fwd.astype(BF16)
    tri_ref[1] = bwd.astype(BF16)
    keep_ref[...] = jnp.concatenate([fwd, bwd], axis=0)
    r8 = lax.broadcasted_iota(I32, (nb, HG_NCHUNK * HG_DK), 0) // HG_CHUNK
    g8 = lax.broadcasted_iota(I32, (nb, HG_NCHUNK * HG_DK), 1) // HG_DK
    spread_ref[...] = _mask_bf16(r8 == g8)


def _hg_pair(zf, zb, qf, qb, vf, vb, stf, stb, lb, tri_ref, keep_ref, spread_ref, *, want_out):
    nb = HG_BLOCK
    spread = spread_ref[...]
    v = (vf, vb)

    sig = jax.nn.sigmoid(jnp.concatenate([zf, zb], axis=0))
    lf = jnp.log(lb + (1.0 - lb) * sig)
    kk = (1.0 - lb) * (1.0 - sig)
    parts = jnp.concatenate(_split3(lf), axis=1)
    sums = jnp.concatenate([jnp.dot(tri_ref[d], parts[d * nb:(d + 1) * nb], preferred_element_type=F32)
                            for d in range(2)], axis=0)
    bcum = sums[:, :HG_DK] + sums[:, HG_DK:2 * HG_DK] + sums[:, 2 * HG_DK:]
    end_rows = [d * nb + c * HG_CHUNK + (HG_CHUNK - 1 if d == 0 else 0) for d in range(2) for c in range(HG_NCHUNK)]
    ends = [bcum[r:r + 1, :] for r in end_rows]
    btot = jnp.concatenate([jnp.broadcast_to(e, (HG_CHUNK, HG_DK)) for e in ends], axis=0)
    dec = jnp.exp(jnp.concatenate(ends, axis=0))
    ke = (kk * jnp.exp(btot - bcum)).astype(BF16)

    new_states, entering = [], []
    for d, st in enumerate((stf, stb)):
        kvt = lax.dot_general(v[d], jnp.concatenate([ke[d * nb:(d + 1) * nb]] * HG_NCHUNK, axis=1) * spread,
                              (((0,), (0,)), ((), ())), preferred_element_type=F32)
        ent = [None] * HG_NCHUNK
        for c in (range(HG_NCHUNK) if d == 0 else range(HG_NCHUNK - 1, -1, -1)):
            ent[c] = st
            i_dec = d * HG_NCHUNK + c
            st = st * dec[i_dec:i_dec + 1, :] + kvt[:, c * HG_DK:(c + 1) * HG_DK]
        new_states.append(st)
        entering.append(ent)
    if not want_out:
        return None, None, new_states[0], new_states[1]

    qd = (jnp.concatenate([qf, qb], axis=0).astype(F32) * jnp.exp(bcum)).astype(BF16)
    kd = (kk * jnp.exp(-bcum)).astype(BF16)
    nt = (((1,), (1,)), ((), ()))
    a = jnp.concatenate([lax.dot_general(qd[d * nb:(d + 1) * nb], kd[d * nb:(d + 1) * nb], nt,
                                         preferred_element_type=F32) for d in range(2)], axis=0)
    a = jnp.where(keep_ref[...] > 0.5, a, 0.0).astype(BF16)
    outs = []
    for d in range(2):
        qd_d = qd[d * nb:(d + 1) * nb]
        inter = jnp.concatenate(
            [lax.dot_general(qd_d[c * HG_CHUNK:(c + 1) * HG_CHUNK], entering[d][c].astype(BF16), nt,
                             preferred_element_type=F32) for c in range(HG_NCHUNK)], axis=0)
        outs.append(jnp.dot(a[d * nb:(d + 1) * nb], v[d], preferred_element_type=F32) + inter)
    return outs[0], outs[1], new_states[0], new_states[1]


def _hgrn_body(q_ref, zf_ref, zb_ref, i_ref, g_ref, qc_ref, zfc_ref, zbc_ref, ic_ref, lb_ref, ng_ref,
               o_ref, accf_ref, accb_ref, stf_ref, stb_ref, tri_ref, keep_ref, spread_ref):
    lbl = lb_ref[...]
    lbe = jnp.exp(lbl - jnp.max(lbl, axis=0, keepdims=True))
    lb = lbe[0:1, :] / jnp.sum(lbe, axis=0, keepdims=True)
    _hg_consts(tri_ref, keep_ref, spread_ref)
    pair = functools.partial(_hg_pair, lb=lb, tri_ref=tri_ref, keep_ref=keep_ref, spread_ref=spread_ref)
    zero_state = jnp.zeros((HG_DK, HG_DK), F32)
    _, _, stf, stb = pair(zfc_ref[0], zbc_ref[0], qc_ref[0], qc_ref[0], ic_ref[0], ic_ref[0],
                          zero_state, zero_state, want_out=False)
    stf_ref[...] = stf
    stb_ref[...] = stb
    n_blk = SEQ // HG_BLOCK

    def scan_step(n, carry):
        rows_f = pl.ds(pl.multiple_of(n * HG_BLOCK, HG_BLOCK), HG_BLOCK)
        rows_b = pl.ds(pl.multiple_of((n_blk - 1 - n) * HG_BLOCK, HG_BLOCK), HG_BLOCK)
        o_f, o_b, st_f, st_b = pair(zf_ref[0, rows_f, :], zb_ref[0, rows_b, :], q_ref[0, rows_f, :],
                                    q_ref[0, rows_b, :], i_ref[0, rows_f, :], i_ref[0, rows_b, :],
                                    stf_ref[...], stb_ref[...], want_out=True)
        stf_ref[...] = st_f
        stb_ref[...] = st_b
        accf_ref[rows_f, :] = o_f
        accb_ref[rows_b, :] = o_b
        return carry

    lax.fori_loop(0, n_blk, scan_step, 0, unroll=8)

    def readout_step(n, carry):
        rows = pl.ds(pl.multiple_of(n * HG_BLOCK, HG_BLOCK), HG_BLOCK)
        tot = accf_ref[rows, :] + accb_ref[rows, :]
        y = _rms(tot, ng_ref[...]) * jax.nn.sigmoid(g_ref[0, rows, :].astype(F32))
        o_ref[0, rows, :] = y.astype(BF16)
        return carry

    lax.fori_loop(0, n_blk, readout_step, 0, unroll=2)


def _hgrn_call(qh, zf, zb, ih, gh, qc, zfc, zbc, ic, lb_logits, norm_g):
    bsz = qh.shape[0]
    seq = pl.BlockSpec((1, SEQ, HG_DK), lambda b, h: (b, 0, h))
    ctx = pl.BlockSpec((1, CTX_LEN, HG_DK), lambda b, h: (b, 0, h))
    return pl.pallas_call(
        _hgrn_body,
        out_shape=jax.ShapeDtypeStruct((bsz, SEQ, N_HG_HEADS * HG_DK), BF16),
        grid=(bsz, N_HG_HEADS),
        in_specs=[seq, seq, seq, seq, seq, ctx, ctx, ctx, ctx,
                  pl.BlockSpec((lb_logits.shape[0], HG_DK), lambda b, h: (0, h)),
                  pl.BlockSpec((1, HG_DK), lambda b, h: (0, 0))],
        out_specs=seq,
        scratch_shapes=[pltpu.VMEM((SEQ, HG_DK), F32),
                        pltpu.VMEM((SEQ, HG_DK), F32),
                        pltpu.VMEM((HG_DK, HG_DK), F32),
                        pltpu.VMEM((HG_DK, HG_DK), F32),
                        pltpu.VMEM((2, HG_BLOCK, HG_BLOCK), BF16),
                        pltpu.VMEM((2 * HG_BLOCK, HG_BLOCK), F32),
                        pltpu.VMEM((HG_BLOCK, HG_NCHUNK * HG_DK), BF16)],
        compiler_params=_cparams(("arbitrary", "arbitrary")),
        name="hgrn",
    )(qh, zf, zb, ih, gh, qc, zfc, zbc, ic, lb_logits, norm_g)


def _outproj_body(na_ref, hg_ref, x_ref, w_ref, gt1_ref, sh2_ref, sc2_ref, gpost_ref, gpre_ref, wr_ref,
                  x1_ref, h2_ref, lt_ref):
    mix = (jnp.dot(na_ref[0], w_ref[:NA_WIDTH, :], preferred_element_type=F32)
           + jnp.dot(hg_ref[0], w_ref[NA_WIDTH:, :], preferred_element_type=F32))
    x1 = x_ref[0] + gt1_ref[0] * _rms(mix, gpost_ref[...])
    x1_ref[0] = x1
    h2 = _rms(x1, gpre_ref[...]) * (1.0 + sc2_ref[0]) + sh2_ref[0]
    h2_ref[0] = h2.astype(BF16)
    lt_ref[0] = lax.dot_general(wr_ref[...], h2, (((1,), (1,)), ((), ())), precision=HIGHEST,
                                preferred_element_type=F32)


def _outproj_call(na_o, hg_o, x, w_out_b, mod, g_post1, g_pre2, w_router_t):
    bsz, s, d = x.shape
    tm = OUT_TM
    half = pl.BlockSpec((1, tm, NA_WIDTH), lambda b, i: (b, i, 0))
    tok = pl.BlockSpec((1, tm, d), lambda b, i: (b, i, 0))
    gt1, sh2, sc2 = (_mod_spec(k, lambda b: b, 2) for k in (2, 3, 4))
    par = pl.BlockSpec((1, d), lambda b, i: (0, 0))
    return pl.pallas_call(
        _outproj_body,
        out_shape=[jax.ShapeDtypeStruct((bsz, s, d), F32),
                   jax.ShapeDtypeStruct((bsz, s, d), BF16),
                   jax.ShapeDtypeStruct((bsz, N_EXPERTS, s), F32)],
        grid=(bsz, s // tm),
        in_specs=[half, half, tok, pl.BlockSpec((d, d), lambda b, i: (0, 0)), gt1, sh2, sc2, par, par,
                  pl.BlockSpec((N_EXPERTS, d), lambda b, i: (0, 0))],
        out_specs=[tok, tok, pl.BlockSpec((1, N_EXPERTS, tm), lambda b, i: (b, 0, i))],
        compiler_params=_cparams(("arbitrary", "arbitrary")),
        name="outproj",
    )(na_o, hg_o, x, w_out_b, mod, mod, mod, g_post1, g_pre2, w_router_t)


def _route_body(lt_ref, posm_ref, gate_ref, tab_ref, *, cap):
    l = lt_ref[0]
    mx = jnp.max(l, axis=0, keepdims=True)
    ex = jnp.exp(l - mx)
    aff = ex / jnp.sum(ex, axis=0, keepdims=True)
    capf = jnp.float32(cap)

    def count(mask):
        return jnp.sum(jnp.where(mask, 1.0, 0.0), axis=1, keepdims=True)

    def enough(v):
        return count(aff >= v) >= capf

    def bit_step(it, thr):
        cand = thr | (jnp.int32(1) << (30 - it))
        return jnp.where(enough(pltpu.bitcast(cand, F32)), cand, thr)

    thr = lax.fori_loop(0, 31, bit_step, jnp.zeros((N_EXPERTS, 1), I32))
    lo = pltpu.bitcast(thr, F32)
    hi = pltpu.bitcast(thr + 1, F32)

    def mid_step(it, lohi):
        lo, hi = lohi
        mid = 0.5 * (lo + hi)
        ok = enough(mid)
        return jnp.where(ok, mid, lo), jnp.where(ok, hi, mid)

    lo, hi = lax.fori_loop(0, 30, mid_step, (lo, hi))
    gt = aff >= hi
    eq = (aff >= lo) & jnp.logical_not(gt)
    need = capf - count(gt)

    rr = lax.broadcasted_iota(I32, (ROUTE_BLK, ROUTE_BLK), 0)
    cc = lax.broadcasted_iota(I32, (ROUTE_BLK, ROUTE_BLK), 1)
    upper = _mask_bf16(rr <= cc)
    lane = lax.broadcasted_iota(I32, (N_EXPERTS, LANES), 1)

    off_eq = jnp.zeros((N_EXPERTS, 1), F32)
    off_sel = jnp.zeros((N_EXPERTS, 1), F32)
    tab = jnp.zeros((N_EXPERTS, LANES), F32)
    for j in range(N_ROUTE_BLK):
        sl = slice(j * ROUTE_BLK, (j + 1) * ROUTE_BLK)
        eq_j = eq[:, sl]
        eq_b = _mask_bf16(eq_j)
        incl_eq = jnp.dot(eq_b, upper, preferred_element_type=F32) + off_eq
        rank_eq = incl_eq - eq_b.astype(F32)
        sel_j = gt[:, sl] | (eq_j & (rank_eq < need))
        sel_b = _mask_bf16(sel_j)
        incl_sel = jnp.dot(sel_b, upper, preferred_element_type=F32) + off_sel
        pos = incl_sel - sel_b.astype(F32)
        posm_ref[0, :, sl] = jnp.where(sel_j, pos.astype(I32), -1)
        gate_ref[0, :, sl] = jnp.where(sel_j, aff[:, sl], 0.0)
        tab = jnp.where(lane == j, off_sel, tab)
        off_eq = incl_eq[:, ROUTE_BLK - 1:ROUTE_BLK]
        off_sel = incl_sel[:, ROUTE_BLK - 1:ROUTE_BLK]
    tab = jnp.where(lane == N_ROUTE_BLK, off_sel, tab)
    tab_ref[0] = tab.astype(I32)


def _route_call(lt, cap):
    bsz = lt.shape[0]
    big = pl.BlockSpec((1, N_EXPERTS, SEQ), lambda b: (b, 0, 0))
    return pl.pallas_call(
        functools.partial(_route_body, cap=cap),
        out_shape=[jax.ShapeDtypeStruct((bsz, N_EXPERTS, SEQ), I32),
                   jax.ShapeDtypeStruct((bsz, N_EXPERTS, SEQ), F32),
                   jax.ShapeDtypeStruct((bsz, N_EXPERTS, LANES), I32)],
        grid=(bsz,),
        in_specs=[big],
        out_specs=[big, big, pl.BlockSpec((1, N_EXPERTS, LANES), lambda b: (b, 0, 0))],
        compiler_params=_cparams(("arbitrary",)),
        name="route",
    )(lt)


def _chunk_range(start, end):
    c_lo = start // SLOT_CHUNK
    n = jnp.where(end > start, (end - 1) // SLOT_CHUNK - c_lo + 1, 0)
    return c_lo, n


DISP_NE = 8


def _dispatch_body(tab_ref, posm_ref, gate_ref, h2_ref, x_ref, gsl_ref, *, cap):
    b = pl.program_id(0)
    hf = pl.program_id(1)
    blk = pl.program_id(2)
    tiles_per_step = MOE_TD // ROUTE_BLK
    chunks_per_batch = cap // SLOT_CHUNK

    @pl.when(blk == 0)
    def _():
        x_ref[...] = jnp.zeros_like(x_ref)
        gsl_ref[...] = jnp.zeros_like(gsl_ref)

    slot = lax.broadcasted_iota(I32, (SLOT_CHUNK, ROUTE_BLK), 0)

    def expert_step(k, carry):
        tab_base = (b * N_EXPERTS + hf * DISP_NE + k) * TAB_STRIDE + blk * tiles_per_step

        def add_chunk(sub, chunk):
            cols = slice(sub * ROUTE_BLK, (sub + 1) * ROUTE_BLK)
            base = pl.multiple_of(chunk * SLOT_CHUNK, SLOT_CHUNK)
            onehot = _mask_bf16(posm_ref[0, pl.ds(k, 1), cols] == slot + base)
            got = jnp.dot(onehot, h2_ref[cols, :], preferred_element_type=F32).astype(BF16)
            x_ref[k, pl.ds(base, SLOT_CHUNK), :] = x_ref[k, pl.ds(base, SLOT_CHUNK), :] + got
            g3 = [p.astype(F32) for p in _split3(gate_ref[0, pl.ds(k, 1), cols])]
            g8 = jnp.concatenate(g3 + [jnp.zeros((SUBLANES - len(g3), ROUTE_BLK), F32)], axis=0).astype(BF16)
            gsl_ref[k, chunk] = gsl_ref[k, chunk] + lax.dot_general(g8, onehot, (((1,), (1,)), ((), ())),
                                                                    preferred_element_type=F32)

        ranges = [_chunk_range(tab_ref[tab_base + sub], tab_ref[tab_base + sub + 1]) for sub in range(tiles_per_step)]
        for sub, (c_lo, _) in enumerate(ranges):
            add_chunk(sub, jnp.minimum(c_lo, chunks_per_batch - 1))
        for sub, (c_lo, n) in enumerate(ranges):
            def more(kk, c, sub=sub, c_lo=c_lo):
                add_chunk(sub, c_lo + kk)
                return c

            lax.fori_loop(1, n, more, 0)
        return carry

    lax.fori_loop(0, DISP_NE, expert_step, 0)


def _dispatch_call(tab_flat, posm, gate, h2_flat, *, bsz, cap):
    d = D_MODEL
    n_blk = SEQ // MOE_TD
    chunks = cap // SLOT_CHUNK
    sel = pl.BlockSpec((1, DISP_NE, MOE_TD), lambda b, hf, blk, tab: (b, hf, blk))
    grid_spec = pltpu.PrefetchScalarGridSpec(
        num_scalar_prefetch=1,
        grid=(bsz, N_EXPERTS // DISP_NE, n_blk),
        in_specs=[sel, sel, pl.BlockSpec((MOE_TD, d), lambda b, hf, blk, tab: (b * n_blk + blk, 0))],
        out_specs=[pl.BlockSpec((DISP_NE, cap, d), lambda b, hf, blk, tab: (hf, b, 0)),
                   pl.BlockSpec((DISP_NE, chunks, SUBLANES, SLOT_CHUNK), lambda b, hf, blk, tab: (hf, b, 0, 0))],
    )
    return pl.pallas_call(
        functools.partial(_dispatch_body, cap=cap),
        out_shape=[jax.ShapeDtypeStruct((N_EXPERTS, bsz * cap, d), BF16),
                   jax.ShapeDtypeStruct((N_EXPERTS, bsz * chunks, SUBLANES, SLOT_CHUNK), F32)],
        grid_spec=grid_spec,
        compiler_params=_cparams(("arbitrary", "arbitrary", "arbitrary")),
        name="dispatch",
    )(tab_flat, posm, gate, h2_flat)


def _moe_body(x_ref, gsl_ref, wg_ref, wu_ref, wd_ref, y_ref, acc_scr):
    s = pl.program_id(1)

    chunks_per_tile = MOE_TM // SLOT_CHUNK

    def mlp(width, first, last):
        nt = (((1,), (1,)), ((), ()))

        def m_step(mi, carry):
            r0 = pl.multiple_of(mi * MOE_TM, MOE_TM)
            rows = pl.ds(r0, MOE_TM)
            xm = x_ref[0, rows, :]
            g = lax.dot_general(xm, wg_ref[0, :width, :].astype(BF16), nt, preferred_element_type=F32)
            u = lax.dot_general(xm, wu_ref[0, :width, :].astype(BF16), nt, preferred_element_type=F32)
            hid = (g * jax.nn.sigmoid(g) * u).astype(BF16)
            out = jnp.dot(hid, wd_ref[0, :width, :].astype(BF16), preferred_element_type=F32)
            if not first:
                out = acc_scr[rows, :] + out
            if last:
                gsl = gsl_ref[0, pl.ds(mi * chunks_per_tile, chunks_per_tile)]
                gcol = jnp.concatenate([jnp.sum(gsl[ci].T, axis=1, keepdims=True) for ci in range(chunks_per_tile)],
                                       axis=0)
                y_ref[0, rows, :] = (out * gcol).astype(BF16)
            else:
                acc_scr[rows, :] = out
            return carry

        lax.fori_loop(0, acc_scr.shape[0] // MOE_TM, m_step, 0)

    @pl.when(s == 0)
    def _():
        mlp(MOE_TF, True, False)

    @pl.when((s > 0) & (s < MOE_NF - 1))
    def _():
        mlp(MOE_TF, False, False)

    @pl.when(s == MOE_NF - 1)
    def _():
        mlp(MOE_F_LAST, False, True)


def _moe_call(xs, gsl, w_gate_t, w_up_t, w_down):
    n_e, slots, d = xs.shape
    per_e = pl.BlockSpec((1, slots, d), lambda e, s: (e, 0, 0))
    wspec = pl.BlockSpec((1, MOE_TF, d), lambda e, s: (e, s, 0))
    return pl.pallas_call(
        _moe_body,
        out_shape=jax.ShapeDtypeStruct((n_e, slots, d), BF16),
        grid=(n_e, MOE_NF),
        in_specs=[per_e, pl.BlockSpec((1,) + gsl.shape[1:], lambda e, s: (e, 0, 0, 0)), wspec, wspec, wspec],
        out_specs=per_e,
        scratch_shapes=[pltpu.VMEM((slots, d), F32)],
        compiler_params=_cparams(("arbitrary", "arbitrary")),
        name="moe",
    )(xs, gsl, w_gate_t, w_up_t, w_down)


def _combine_body(tab_ref, y_ref, posm_ref, x1_ref, gt2_ref, gpost_ref, o_ref, acc_ref, *, cap):
    n_e = N_EXPERTS
    b = pl.program_id(0)
    j = pl.program_id(1)
    pm = posm_ref[0].T
    win = 2 * SLOT_CHUNK
    slot_w = lax.broadcasted_iota(I32, (ROUTE_BLK, win), 1)
    slot_c = lax.broadcasted_iota(I32, (ROUTE_BLK, SLOT_CHUNK), 1)
    total = jnp.zeros((ROUTE_BLK, D_MODEL), F32)
    bases, extras = [], []
    for k in range(n_e):
        tab_base = (b * N_EXPERTS + k) * TAB_STRIDE
        start = tab_ref[tab_base + j]
        end = tab_ref[tab_base + j + 1]
        base = pl.multiple_of(jnp.minimum((start // SLOT_CHUNK) * SLOT_CHUNK, cap - win), SLOT_CHUNK)
        total = total + jnp.dot(_mask_bf16(pm[:, k:k + 1] == slot_w + base), y_ref[k, pl.ds(base, win), :],
                                preferred_element_type=F32)
        bases.append(base)
        extras.append(jnp.maximum(end - (base + win) + SLOT_CHUNK - 1, 0) // SLOT_CHUNK)
    acc_ref[...] = total

    @pl.when(sum(extras) > 0)
    def _():
        for k in range(n_e):
            def extra_step(kk, carry, k=k):
                b2 = pl.multiple_of(bases[k] + win + kk * SLOT_CHUNK, SLOT_CHUNK)
                acc_ref[...] += jnp.dot(_mask_bf16(pm[:, k:k + 1] == slot_c + b2),
                                        y_ref[k, pl.ds(b2, SLOT_CHUNK), :], preferred_element_type=F32)
                return carry

            lax.fori_loop(0, extras[k], extra_step, 0)

    o_ref[0] = x1_ref[0] + gt2_ref[0] * _rms(acc_ref[...], gpost_ref[...])


def _combine_call(tab_flat, y, posm, x1, mod, g_post2, *, cap):
    bsz, n_e, s = posm.shape
    d = D_MODEL
    tok = pl.BlockSpec((1, ROUTE_BLK, d), lambda b, j, tab: (b, j, 0))
    grid_spec = pltpu.PrefetchScalarGridSpec(
        num_scalar_prefetch=1,
        grid=(bsz, s // ROUTE_BLK),
        in_specs=[
            pl.BlockSpec((n_e, cap, d), lambda b, j, tab: (0, b, 0), pipeline_mode=pl.Buffered(1)),
            pl.BlockSpec((1, n_e, ROUTE_BLK), lambda b, j, tab: (b, 0, j)),
            tok,
            _mod_spec(5, lambda b: b, 3),
            pl.BlockSpec((1, d), lambda b, j, tab: (0, 0))],
        out_specs=tok,
        scratch_shapes=[pltpu.VMEM((ROUTE_BLK, d), F32)],
    )
    return pl.pallas_call(
        functools.partial(_combine_body, cap=cap),
        out_shape=jax.ShapeDtypeStruct((bsz, s, d), F32),
        grid_spec=grid_spec,
        compiler_params=_cparams(("arbitrary", "arbitrary")),
        name="combine",
    )(tab_flat, y, posm, x1, mod, g_post2)


def kernel(x, c, ctx, c_ctx, w_mod, b_mod, g_pre1, g_post1, g_pre2, g_post2, w_in, w_out, na_rpb,
           hg_lb_logits, hg_norm, w_router, w_gate, w_up, w_down):
    bsz, seq, d = x.shape
    assert (seq, d) == (SEQ, D_MODEL) and ctx.shape[1] == CTX_LEN and w_mod.shape[0] == 1
    cap = 2 * seq // N_EXPERTS
    assert cap % SLOT_CHUNK == 0

    assert bsz < MOD_ROWS
    mod = _mod_call(c, c_ctx, w_mod, b_mod)

    w_in_b = w_in[0].astype(BF16)
    w_out_b = w_out[0].astype(BF16)
    cos, sa, sb = _rope_tables(seq)

    qrot, qpl, krot, v, qh, zf, zb, ih, gh = _inproj_call(
        x, mod, lambda b: b, g_pre1, w_in_b, cos, sa, sb, rope=True, tm=IN_TM)
    ctx_flat = ctx.reshape(1, bsz * CTX_LEN, d)
    ctx_out = _inproj_call(ctx_flat, mod, lambda b: bsz, g_pre1, w_in_b,
                           cos[:bsz * CTX_LEN], sa[:bsz * CTX_LEN], sb[:bsz * CTX_LEN], rope=False, tm=CTX_LEN)
    _, _, kc, vc, qc, zfc, zbc, ic, _ = [a.reshape(bsz, CTX_LEN, COL_GROUP) for a in ctx_out]

    na_o = _na_call(qrot, qpl, krot, v, kc, vc, _na_bias(na_rpb[0]))
    hg_o = _hgrn_call(qh, zf, zb, ih, gh, qc, zfc, zbc, ic, hg_lb_logits, hg_norm)

    x1, h2, lt = _outproj_call(na_o, hg_o, x, w_out_b, mod, g_post1, g_pre2, w_router[0].T)

    posm, gate, tab = _route_call(lt, cap)
    tab_flat = tab.reshape(-1)
    xs, gsl = _dispatch_call(tab_flat, posm, gate, h2.reshape(bsz * seq, d), bsz=bsz, cap=cap)
    y = _moe_call(xs, gsl, jnp.swapaxes(w_gate[0], 1, 2), jnp.swapaxes(w_up[0], 1, 2), w_down[0])

    return _combine_call(tab_flat, y, posm, x1, mod, g_post2, cap=cap)
```

```python
import functools

import jax
import jax.numpy as jnp
import numpy as np
from jax import lax
from jax.experimental import pallas as pl
from jax.experimental.pallas import tpu as pltpu

F32 = jnp.float32
BF16 = jnp.bfloat16
I32 = jnp.int32
HIGHEST = lax.Precision.HIGHEST

D_MODEL = 1024
GRID_W = 64
N_ROWS = 128
SEQ = 8192
CTX_LEN = 256
NA_HEAD_DIM = 64
N_NA_HEADS = 8
NA_WIDTH = 512
NA_WIN_H = 8
NA_WIN_W = 16
ROPE_BASE = 10000.0
HG_DK = 128
N_HG_HEADS = 4
HG_CHUNK = 32
N_EXPERTS = 16
D_EXPERT = 2752
RMS_EPS = 1e-6
COL_GROUP = 512

LANES = 128
SUBLANES = 8
VMEM_LIMIT_BYTES = 58 * 1024 * 1024

IN_TM = 1024
NA_ROWS_PER_STEP = 64
NA_KEYS = NA_WIN_H * GRID_W
HG_BLOCK = 256
OUT_TM = 1024
ROUTE_BLK = 256
N_ROUTE_BLK = SEQ // ROUTE_BLK
SLOT_CHUNK = 128
TAB_STRIDE = LANES
MOE_TD = 2048
MOE_TF = 768
MOE_NF = -(-D_EXPERT // MOE_TF)
MOE_F_LAST = D_EXPERT - (MOE_NF - 1) * MOE_TF
MOE_TM = 1024


def _cparams(sem):
    return pltpu.CompilerParams(dimension_semantics=sem, vmem_limit_bytes=VMEM_LIMIT_BYTES)


def _mask_bf16(mask):
    return jnp.where(mask, 1.0, 0.0).astype(BF16)


def _rms(v, g):
    return v * lax.rsqrt(jnp.mean(v * v, axis=-1, keepdims=True) + RMS_EPS) * g


MOD_ROWS = 8
N_MOD = 6


def _mod_body(c_ref, cc_ref, w_ref, b_ref, o_ref, rows_scr):
    bsz = c_ref.shape[0]
    rows_scr[...] = jnp.zeros_like(rows_scr)
    rows_scr[0:bsz, :] = c_ref[...]
    rows_scr[bsz:bsz + 1, :] = cc_ref[...]
    cv = rows_scr[...]
    s = cv * jax.nn.sigmoid(cv)
    o_ref[...] = jnp.dot(s, w_ref[0], precision=HIGHEST, preferred_element_type=F32) + b_ref[...]


def _mod_call(c, c_ctx, w_mod, b_mod):
    bsz, d = c.shape
    n = w_mod.shape[2]
    tn = d
    mod = pl.pallas_call(
        _mod_body,
        out_shape=jax.ShapeDtypeStruct((MOD_ROWS, n), F32),
        grid=(n // tn,),
        in_specs=[pl.BlockSpec((bsz, d), lambda j: (0, 0)),
                  pl.BlockSpec((1, d), lambda j: (0, 0)),
                  pl.BlockSpec((1, d, tn), lambda j: (0, 0, j)),
                  pl.BlockSpec((1, tn), lambda j: (0, j))],
        out_specs=pl.BlockSpec((MOD_ROWS, tn), lambda j: (0, j)),
        scratch_shapes=[pltpu.VMEM((MOD_ROWS, d), F32)],
        compiler_params=_cparams(("arbitrary",)),
        name="mod",
    )(c, c_ctx[None, :], w_mod, b_mod)
    return mod.reshape(MOD_ROWS * N_MOD, 1, d)


def _mod_spec(k, row_of, n_grid):
    if n_grid == 2:
        return pl.BlockSpec((1, 1, D_MODEL), lambda b, i: (row_of(b) * N_MOD + k, 0, 0))
    return pl.BlockSpec((1, 1, D_MODEL), lambda b, i, tab: (row_of(b) * N_MOD + k, 0, 0))


def _rope128(pg, cos, sa, sb):
    return pg * cos + pltpu.roll(pg, LANES - 16, axis=1) * sa + pltpu.roll(pg, 16, axis=1) * sb


def _inproj_body(x_ref, sh_ref, sc_ref, g_ref, w_ref, cos_ref, sa_ref, sb_ref,
                 qrot_ref, qpl_ref, krot_ref, v_ref, qh_ref, zf_ref, zb_ref, ih_ref, gh_ref, *, rope):
    xv = x_ref[0]
    h = _rms(xv, g_ref[...]) * (1.0 + sc_ref[0]) + sh_ref[0]
    hb = h.astype(BF16)

    def group(j):
        return jnp.dot(hb, w_ref[:, j * COL_GROUP:(j + 1) * COL_GROUP], preferred_element_type=F32)

    def rotated(p):
        if not rope:
            return p
        cos, sa, sb = cos_ref[...], sa_ref[...], sb_ref[...]
        return jnp.concatenate(
            [_rope128(p[:, k * LANES:(k + 1) * LANES], cos, sa, sb) for k in range(COL_GROUP // LANES)], axis=1)

    scale = NA_HEAD_DIM ** -0.5
    p = group(0)
    qpl_ref[0] = (p * scale).astype(BF16)
    qrot_ref[0] = (rotated(p) * scale).astype(BF16)
    krot_ref[0] = rotated(group(1)).astype(BF16)
    v_ref[0] = group(2).astype(BF16)
    qh_ref[0] = group(3).astype(BF16)
    zf_ref[0] = group(4)
    zb_ref[0] = group(5)
    ih_ref[0] = group(6).astype(BF16)
    gh_ref[0] = group(7).astype(BF16)


def _inproj_call(x3, mod, row_of, g_pre, w_in_b, cos, sa, sb, *, rope, tm):
    g, t, d = x3.shape
    n_out = 9
    dts = [BF16, BF16, BF16, BF16, BF16, F32, F32, BF16, BF16]
    tok = pl.BlockSpec((1, tm, d), lambda b, i: (b, i, 0))
    tab = pl.BlockSpec((tm, LANES), lambda b, i: (i, 0))
    outs = [pl.BlockSpec((1, tm, COL_GROUP), lambda b, i: (b, i, 0)) for _ in range(n_out)]
    return pl.pallas_call(
        functools.partial(_inproj_body, rope=rope),
        out_shape=[jax.ShapeDtypeStruct((g, t, COL_GROUP), dt) for dt in dts],
        grid=(g, t // tm),
        in_specs=[tok, _mod_spec(0, row_of, 2), _mod_spec(1, row_of, 2),
                  pl.BlockSpec((1, d), lambda b, i: (0, 0)),
                  pl.BlockSpec(w_in_b.shape, lambda b, i: (0, 0)),
                  tab, tab, tab],
        out_specs=outs,
        compiler_params=_cparams(("arbitrary", "arbitrary")),
        name="inproj_rope" if rope else "inproj_ctx",
    )(x3, mod, mod, g_pre, w_in_b, cos, sa, sb)


def _rope_tables(n_tok):
    half = NA_HEAD_DIM // 2
    t = np.arange(n_tok)
    row = (t // GRID_W).astype(np.float32)
    col = (t % GRID_W).astype(np.float32)
    lane = np.arange(LANES)
    d = lane % NA_HEAD_DIM
    dd = d % half
    fi = dd % (half // 2)
    inv_freq = (ROPE_BASE ** (-(2.0 * fi.astype(np.float32)) / half)).astype(np.float32)
    pos = np.where((d < half)[None, :], row[:, None], col[:, None])
    ang = (pos * inv_freq[None, :]).astype(np.float32)
    first = (dd < half // 2)[None, :]
    cos = np.cos(ang.astype(np.float64)).astype(np.float32)
    sin = np.sin(ang.astype(np.float64)).astype(np.float32)
    zero = np.zeros_like(sin)
    return jnp.asarray(cos), jnp.asarray(np.where(first, -sin, zero)), jnp.asarray(np.where(first, zero, sin))


def _na_body(qrot_ref, qpl_ref, k_ref, v_ref, kc_ref, vc_ref, bias_ref, o_ref):
    m = pl.program_id(2)
    lane = lax.broadcasted_iota(I32, (1, LANES), 1)
    first_head = lane < NA_HEAD_DIM
    hm0, hm1 = _mask_bf16(first_head), _mask_bf16(lane >= NA_HEAD_DIM)
    nt = (((1,), (1,)), ((), ()))
    rows = range(NA_ROWS_PER_STEP)

    def stacked(ref):
        parts = []
        for i in rows:
            q = ref[0, i * GRID_W:(i + 1) * GRID_W, :]
            parts += [q * hm0, q * hm1]
        return jnp.concatenate(parts, axis=0)

    sc = lax.dot_general(stacked(qpl_ref), kc_ref[0], nt, preferred_element_type=F32)

    qs = stacked(qrot_ref)
    k0s, s_parts = [], []
    blk = 2 * GRID_W
    for i in rows:
        r = m * NA_ROWS_PER_STEP + i
        r0 = jnp.clip(r - NA_WIN_H // 2, 0, N_ROWS - NA_WIN_H)
        di = r0 - r + (NA_WIN_H - 1)
        k0s.append(pl.multiple_of(r0 * GRID_W, GRID_W))
        s_parts.append(lax.dot_general(qs[i * blk:(i + 1) * blk], k_ref[0, pl.ds(k0s[i], NA_KEYS), :], nt,
                                       preferred_element_type=F32) + bias_ref[0, di].reshape(blk, NA_KEYS))
    s = jnp.concatenate([jnp.concatenate(s_parts, axis=0), sc], axis=1)
    p = jnp.exp(s - jnp.max(s, axis=-1, keepdims=True))
    den = jnp.sum(p, axis=-1, keepdims=True)
    pb = p.astype(BF16)
    acc = jnp.concatenate([jnp.dot(pb[i * blk:(i + 1) * blk, :NA_KEYS], v_ref[0, pl.ds(k0s[i], NA_KEYS), :],
                                   preferred_element_type=F32) for i in rows], axis=0)
    o = (acc + jnp.dot(pb[:, NA_KEYS:], vc_ref[0], preferred_element_type=F32)) / den
    o_ref[0] = jnp.concatenate([jnp.where(first_head, o[i * blk:i * blk + GRID_W], o[i * blk + GRID_W:(i + 1) * blk])
                                for i in rows], axis=0).astype(BF16)


def _na_bias(rpb):
    c = np.arange(GRID_W)
    cq = np.arange(GRID_W)
    win_c0 = np.clip(cq - NA_WIN_W // 2, 0, GRID_W - NA_WIN_W)
    in_win = (c[None, :] >= win_c0[:, None]) & (c[None, :] < win_c0[:, None] + NA_WIN_W)
    dc = np.clip(c[None, :] - cq[:, None] + NA_WIN_W - 1, 0, 2 * NA_WIN_W - 2)
    pick = (dc[None, :, :] == np.arange(2 * NA_WIN_W - 1)[:, None, None]).astype(np.float32)
    t = jnp.einsum('hrj,jqc->hqrc', rpb.astype(F32), jnp.asarray(pick), precision=HIGHEST)
    t = jnp.where(jnp.asarray(in_win)[None, :, None, :], t, -jnp.inf)
    n_dr = 2 * NA_WIN_H - 1
    t = t.reshape(N_NA_HEADS // 2, 2, GRID_W, n_dr * GRID_W)
    return jnp.stack([t[..., di * GRID_W:di * GRID_W + NA_KEYS] for di in range(NA_WIN_H)], axis=1)


def _na_call(qrot, qpl, krot, v, kc, vc, bias):
    bsz = qrot.shape[0]
    tq = NA_ROWS_PER_STEP * GRID_W
    qspec = pl.BlockSpec((1, tq, LANES), lambda b, p, m: (b, m, p))
    kspec = pl.BlockSpec((1, SEQ, LANES), lambda b, p, m: (b, 0, p))
    cspec = pl.BlockSpec((1, CTX_LEN, LANES), lambda b, p, m: (b, 0, p))
    return pl.pallas_call(
        _na_body,
        out_shape=jax.ShapeDtypeStruct((bsz, SEQ, NA_WIDTH), BF16),
        grid=(bsz, N_NA_HEADS // 2, N_ROWS // NA_ROWS_PER_STEP),
        in_specs=[qspec, qspec, kspec, kspec, cspec, cspec,
                  pl.BlockSpec((1, NA_WIN_H, 2, GRID_W, NA_KEYS), lambda b, p, m: (p, 0, 0, 0, 0))],
        out_specs=qspec,
        compiler_params=_cparams(("arbitrary", "arbitrary", "arbitrary")),
        name="na",
    )(qrot, qpl, krot, v, kc, vc, bias)


def _split3(v):
    hi = v.astype(BF16)
    r1 = v - hi.astype(F32)
    mid = r1.astype(BF16)
    lo = (r1 - mid.astype(F32)).astype(BF16)
    return hi, mid, lo


HG_NCHUNK = HG_BLOCK // HG_CHUNK


def _hg_consts(tri_ref, keep_ref, spread_ref):
    nb = HG_BLOCK
    rr = lax.broadcasted_iota(I32, (nb, nb), 0)
    cc = lax.broadcasted_iota(I32, (nb, nb), 1)
    same = (rr // HG_CHUNK) == (cc // HG_CHUNK)
    fwd = jnp.where(same & (cc <= rr), 1.0, 0.0)
    bwd = jnp.where(same & (cc >= rr), 1.0, 0.0)
    tri_ref[0] = fwd.astype(BF16)
    tri_ref[1] = bwd.astype(BF16)
    keep_ref[...] = jnp.concatenate([fwd, bwd], axis=0)
    r8 = lax.broadcasted_iota(I32, (nb, HG_NCHUNK * HG_DK), 0) // HG_CHUNK
    g8 = lax.broadcasted_iota(I32, (nb, HG_NCHUNK * HG_DK), 1) // HG_DK
    spread_ref[...] = _mask_bf16(r8 == g8)


def _hg_pair(zf, zb, qf, qb, vf, vb, stf, stb, lb, tri_ref, keep_ref, spread_ref, *, want_out):
    nb = HG_BLOCK
    spread = spread_ref[...]
    v = (vf, vb)

    sig = jax.nn.sigmoid(jnp.concatenate([zf, zb], axis=0))
    lf = jnp.log(lb + (1.0 - lb) * sig)
    kk = (1.0 - lb) * (1.0 - sig)
    parts = jnp.concatenate(_split3(lf), axis=1)
    sums = jnp.concatenate([jnp.dot(tri_ref[d], parts[d * nb:(d + 1) * nb], preferred_element_type=F32)
                            for d in range(2)], axis=0)
    bcum = sums[:, :HG_DK] + sums[:, HG_DK:2 * HG_DK] + sums[:, 2 * HG_DK:]
    end_rows = [d * nb + c * HG_CHUNK + (HG_CHUNK - 1 if d == 0 else 0) for d in range(2) for c in range(HG_NCHUNK)]
    ends = [bcum[r:r + 1, :] for r in end_rows]
    btot = jnp.concatenate([jnp.broadcast_to(e, (HG_CHUNK, HG_DK)) for e in ends], axis=0)
    dec = jnp.exp(jnp.concatenate(ends, axis=0))
    ke = (kk * jnp.exp(btot - bcum)).astype(BF16)

    new_states, entering = [], []
    for d, st in enumerate((stf, stb)):
        kvt = lax.dot_general(v[d], jnp.concatenate([ke[d * nb:(d + 1) * nb]] * HG_NCHUNK, axis=1) * spread,
                              (((0,), (0,)), ((), ())), preferred_element_type=F32)
        ent = [None] * HG_NCHUNK
        for c in (range(HG_NCHUNK) if d == 0 else range(HG_NCHUNK - 1, -1, -1)):
            ent[c] = st
            i_dec = d * HG_NCHUNK + c
            st = st * dec[i_dec:i_dec + 1, :] + kvt[:, c * HG_DK:(c + 1) * HG_DK]
        new_states.append(st)
        entering.append(ent)
    if not want_out:
        return None, None, new_states[0], new_states[1]

    qd = (jnp.concatenate([qf, qb], axis=0).astype(F32) * jnp.exp(bcum)).astype(BF16)
    kd = (kk * jnp.exp(-bcum)).astype(BF16)
    nt = (((1,), (1,)), ((), ()))
    a = jnp.concatenate([lax.dot_general(qd[d * nb:(d + 1) * nb], kd[d * nb:(d + 1) * nb], nt,
                                         preferred_element_type=F32) for d in range(2)], axis=0)
    a = jnp.where(keep_ref[...] > 0.5, a, 0.0).astype(BF16)
    outs = []
    for d in range(2):
        qd_d = qd[d * nb:(d + 1) * nb]
        inter = jnp.concatenate(
            [lax.dot_general(qd_d[c * HG_CHUNK:(c + 1) * HG_CHUNK], entering[d][c].astype(BF16), nt,
                             preferred_element_type=F32) for c in range(HG_NCHUNK)], axis=0)
        outs.append(jnp.dot(a[d * nb:(d + 1) * nb], v[d], preferred_element_type=F32) + inter)
    return outs[0], outs[1], new_states[0], new_states[1]


def _hgrn_body(q_ref, zf_ref, zb_ref, i_ref, g_ref, qc_ref, zfc_ref, zbc_ref, ic_ref, lb_ref, ng_ref,
               o_ref, accf_ref, accb_ref, stf_ref, stb_ref, tri_ref, keep_ref, spread_ref):
    lbl = lb_ref[...]
    lbe = jnp.exp(lbl - jnp.max(lbl, axis=0, keepdims=True))
    lb = lbe[0:1, :] / jnp.sum(lbe, axis=0, keepdims=True)
    _hg_consts(tri_ref, keep_ref, spread_ref)
    pair = functools.partial(_hg_pair, lb=lb, tri_ref=tri_ref, keep_ref=keep_ref, spread_ref=spread_ref)
    zero_state = jnp.zeros((HG_DK, HG_DK), F32)
    _, _, stf, stb = pair(zfc_ref[0], zbc_ref[0], qc_ref[0], qc_ref[0], ic_ref[0], ic_ref[0],
                          zero_state, zero_state, want_out=False)
    stf_ref[...] = stf
    stb_ref[...] = stb
    n_blk = SEQ // HG_BLOCK

    def scan_step(n, carry):
        rows_f = pl.ds(pl.multiple_of(n * HG_BLOCK, HG_BLOCK), HG_BLOCK)
        rows_b = pl.ds(pl.multiple_of((n_blk - 1 - n) * HG_BLOCK, HG_BLOCK), HG_BLOCK)
        o_f, o_b, st_f, st_b = pair(zf_ref[0, rows_f, :], zb_ref[0, rows_b, :], q_ref[0, rows_f, :],
                                    q_ref[0, rows_b, :], i_ref[0, rows_f, :], i_ref[0, rows_b, :],
                                    stf_ref[...], stb_ref[...], want_out=True)
        stf_ref[...] = st_f
        stb_ref[...] = st_b
        accf_ref[rows_f, :] = o_f
        accb_ref[rows_b, :] = o_b
        return carry

    lax.fori_loop(0, n_blk, scan_step, 0, unroll=8)

    def readout_step(n, carry):
        rows = pl.ds(pl.multiple_of(n * HG_BLOCK, HG_BLOCK), HG_BLOCK)
        tot = accf_ref[rows, :] + accb_ref[rows, :]
        y = _rms(tot, ng_ref[...]) * jax.nn.sigmoid(g_ref[0, rows, :].astype(F32))
        o_ref[0, rows, :] = y.astype(BF16)
        return carry

    lax.fori_loop(0, n_blk, readout_step, 0, unroll=2)


def _hgrn_call(qh, zf, zb, ih, gh, qc, zfc, zbc, ic, lb_logits, norm_g):
    bsz = qh.shape[0]
    seq = pl.BlockSpec((1, SEQ, HG_DK), lambda b, h: (b, 0, h))
    ctx = pl.BlockSpec((1, CTX_LEN, HG_DK), lambda b, h: (b, 0, h))
    return pl.pallas_call(
        _hgrn_body,
        out_shape=jax.ShapeDtypeStruct((bsz, SEQ, N_HG_HEADS * HG_DK), BF16),
        grid=(bsz, N_HG_HEADS),
        in_specs=[seq, seq, seq, seq, seq, ctx, ctx, ctx, ctx,
                  pl.BlockSpec((lb_logits.shape[0], HG_DK), lambda b, h: (0, h)),
                  pl.BlockSpec((1, HG_DK), lambda b, h: (0, 0))],
        out_specs=seq,
        scratch_shapes=[pltpu.VMEM((SEQ, HG_DK), F32),
                        pltpu.VMEM((SEQ, HG_DK), F32),
                        pltpu.VMEM((HG_DK, HG_DK), F32),
                        pltpu.VMEM((HG_DK, HG_DK), F32),
                        pltpu.VMEM((2, HG_BLOCK, HG_BLOCK), BF16),
                        pltpu.VMEM((2 * HG_BLOCK, HG_BLOCK), F32),
                        pltpu.VMEM((HG_BLOCK, HG_NCHUNK * HG_DK), BF16)],
        compiler_params=_cparams(("arbitrary", "arbitrary")),
        name="hgrn",
    )(qh, zf, zb, ih, gh, qc, zfc, zbc, ic, lb_logits, norm_g)


def _outproj_body(na_ref, hg_ref, x_ref, w_ref, gt1_ref, sh2_ref, sc2_ref, gpost_ref, gpre_ref, wr_ref,
                  x1_ref, h2_ref, lt_ref):
    mix = (jnp.dot(na_ref[0], w_ref[:NA_WIDTH, :], preferred_element_type=F32)
           + jnp.dot(hg_ref[0], w_ref[NA_WIDTH:, :], preferred_element_type=F32))
    x1 = x_ref[0] + gt1_ref[0] * _rms(mix, gpost_ref[...])
    x1_ref[0] = x1
    h2 = _rms(x1, gpre_ref[...]) * (1.0 + sc2_ref[0]) + sh2_ref[0]
    h2_ref[0] = h2.astype(BF16)
    lt_ref[0] = lax.dot_general(wr_ref[...], h2, (((1,), (1,)), ((), ())), precision=HIGHEST,
                                preferred_element_type=F32)


def _outproj_call(na_o, hg_o, x, w_out_b, mod, g_post1, g_pre2, w_router_t):
    bsz, s, d = x.shape
    tm = OUT_TM
    half = pl.BlockSpec((1, tm, NA_WIDTH), lambda b, i: (b, i, 0))
    tok = pl.BlockSpec((1, tm, d), lambda b, i: (b, i, 0))
    gt1, sh2, sc2 = (_mod_spec(k, lambda b: b, 2) for k in (2, 3, 4))
    par = pl.BlockSpec((1, d), lambda b, i: (0, 0))
    return pl.pallas_call(
        _outproj_body,
        out_shape=[jax.ShapeDtypeStruct((bsz, s, d), F32),
                   jax.ShapeDtypeStruct((bsz, s, d), BF16),
                   jax.ShapeDtypeStruct((bsz, N_EXPERTS, s), F32)],
        grid=(bsz, s // tm),
        in_specs=[half, half, tok, pl.BlockSpec((d, d), lambda b, i: (0, 0)), gt1, sh2, sc2, par, par,
                  pl.BlockSpec((N_EXPERTS, d), lambda b, i: (0, 0))],
        out_specs=[tok, tok, pl.BlockSpec((1, N_EXPERTS, tm), lambda b, i: (b, 0, i))],
        compiler_params=_cparams(("arbitrary", "arbitrary")),
        name="outproj",
    )(na_o, hg_o, x, w_out_b, mod, mod, mod, g_post1, g_pre2, w_router_t)


def _route_body(lt_ref, posm_ref, gate_ref, tab_ref, *, cap):
    l = lt_ref[0]
    mx = jnp.max(l, axis=0, keepdims=True)
    ex = jnp.exp(l - mx)
    aff = ex / jnp.sum(ex, axis=0, keepdims=True)
    capf = jnp.float32(cap)

    def count(mask):
        return jnp.sum(jnp.where(mask, 1.0, 0.0), axis=1, keepdims=True)

    def enough(v):
        return count(aff >= v) >= capf

    def bit_step(it, thr):
        cand = thr | (jnp.int32(1) << (30 - it))
        return jnp.where(enough(pltpu.bitcast(cand, F32)), cand, thr)

    thr = lax.fori_loop(0, 31, bit_step, jnp.zeros((N_EXPERTS, 1), I32))
    lo = pltpu.bitcast(thr, F32)
    hi = pltpu.bitcast(thr + 1, F32)

    def mid_step(it, lohi):
        lo, hi = lohi
        mid = 0.5 * (lo + hi)
        ok = enough(mid)
        return jnp.where(ok, mid, lo), jnp.where(ok, hi, mid)

    lo, hi = lax.fori_loop(0, 30, mid_step, (lo, hi))
    gt = aff >= hi
    eq = (aff >= lo) & jnp.logical_not(gt)
    need = capf - count(gt)

    rr = lax.broadcasted_iota(I32, (ROUTE_BLK, ROUTE_BLK), 0)
    cc = lax.broadcasted_iota(I32, (ROUTE_BLK, ROUTE_BLK), 1)
    upper = _mask_bf16(rr <= cc)
    lane = lax.broadcasted_iota(I32, (N_EXPERTS, LANES), 1)

    off_eq = jnp.zeros((N_EXPERTS, 1), F32)
    off_sel = jnp.zeros((N_EXPERTS, 1), F32)
    tab = jnp.zeros((N_EXPERTS, LANES), F32)
    for j in range(N_ROUTE_BLK):
        sl = slice(j * ROUTE_BLK, (j + 1) * ROUTE_BLK)
        eq_j = eq[:, sl]
        eq_b = _mask_bf16(eq_j)
        incl_eq = jnp.dot(eq_b, upper, preferred_element_type=F32) + off_eq
        rank_eq = incl_eq - eq_b.astype(F32)
        sel_j = gt[:, sl] | (eq_j & (rank_eq < need))
        sel_b = _mask_bf16(sel_j)
        incl_sel = jnp.dot(sel_b, upper, preferred_element_type=F32) + off_sel
        pos = incl_sel - sel_b.astype(F32)
        posm_ref[0, :, sl] = jnp.where(sel_j, pos.astype(I32), -1)
        gate_ref[0, :, sl] = jnp.where(sel_j, aff[:, sl], 0.0)
        tab = jnp.where(lane == j, off_sel, tab)
        off_eq = incl_eq[:, ROUTE_BLK - 1:ROUTE_BLK]
        off_sel = incl_sel[:, ROUTE_BLK - 1:ROUTE_BLK]
    tab = jnp.where(lane == N_ROUTE_BLK, off_sel, tab)
    tab_ref[0] = tab.astype(I32)


def _route_call(lt, cap):
    bsz = lt.shape[0]
    big = pl.BlockSpec((1, N_EXPERTS, SEQ), lambda b: (b, 0, 0))
    return pl.pallas_call(
        functools.partial(_route_body, cap=cap),
        out_shape=[jax.ShapeDtypeStruct((bsz, N_EXPERTS, SEQ), I32),
                   jax.ShapeDtypeStruct((bsz, N_EXPERTS, SEQ), F32),
                   jax.ShapeDtypeStruct((bsz, N_EXPERTS, LANES), I32)],
        grid=(bsz,),
        in_specs=[big],
        out_specs=[big, big, pl.BlockSpec((1, N_EXPERTS, LANES), lambda b: (b, 0, 0))],
        compiler_params=_cparams(("arbitrary",)),
        name="route",
    )(lt)


def _chunk_range(start, end):
    c_lo = start // SLOT_CHUNK
    n = jnp.where(end > start, (end - 1) // SLOT_CHUNK - c_lo + 1, 0)
    return c_lo, n


DISP_NE = 8


def _dispatch_body(tab_ref, posm_ref, gate_ref, h2_ref, x_ref, gsl_ref, *, cap):
    b = pl.program_id(0)
    hf = pl.program_id(1)
    blk = pl.program_id(2)
    tiles_per_step = MOE_TD // ROUTE_BLK
    chunks_per_batch = cap // SLOT_CHUNK

    @pl.when(blk == 0)
    def _():
        x_ref[...] = jnp.zeros_like(x_ref)
        gsl_ref[...] = jnp.zeros_like(gsl_ref)

    slot = lax.broadcasted_iota(I32, (SLOT_CHUNK, ROUTE_BLK), 0)

    def chunk_ranges(k):
        tab_base = (b * N_EXPERTS + hf * DISP_NE + k) * TAB_STRIDE + blk * tiles_per_step
        return [_chunk_range(tab_ref[tab_base + sub], tab_ref[tab_base + sub + 1]) for sub in range(tiles_per_step)]

    def add_chunk(k, sub, chunk):
        cols = slice(sub * ROUTE_BLK, (sub + 1) * ROUTE_BLK)
        base = pl.multiple_of(chunk * SLOT_CHUNK, SLOT_CHUNK)
        onehot = _mask_bf16(posm_ref[0, pl.ds(k, 1), cols] == slot + base)
        got = jnp.dot(onehot, h2_ref[cols, :], preferred_element_type=F32).astype(BF16)
        x_ref[k, pl.ds(base, SLOT_CHUNK), :] = x_ref[k, pl.ds(base, SLOT_CHUNK), :] + got
        g3 = [p.astype(F32) for p in _split3(gate_ref[0, pl.ds(k, 1), cols])]
        g8 = jnp.concatenate(g3 + [jnp.zeros((SUBLANES - len(g3), ROUTE_BLK), F32)], axis=0).astype(BF16)
        gsl_ref[k, chunk] = gsl_ref[k, chunk] + lax.dot_general(g8, onehot, (((1,), (1,)), ((), ())),
                                                                preferred_element_type=F32)

    for k in range(DISP_NE):
        for sub, (c_lo, _) in enumerate(chunk_ranges(k)):
            add_chunk(k, sub, jnp.minimum(c_lo, chunks_per_batch - 1))

    def expert_step(k, carry):
        for sub, (c_lo, n) in enumerate(chunk_ranges(k)):
            def more(kk, c, sub=sub, c_lo=c_lo):
                add_chunk(k, sub, c_lo + kk)
                return c

            lax.fori_loop(1, n, more, 0)
        return carry

    lax.fori_loop(0, DISP_NE, expert_step, 0)


def _dispatch_call(tab_flat, posm, gate, h2_flat, *, bsz, cap):
    d = D_MODEL
    n_blk = SEQ // MOE_TD
    chunks = cap // SLOT_CHUNK
    sel = pl.BlockSpec((1, DISP_NE, MOE_TD), lambda b, hf, blk, tab: (b, hf, blk))
    grid_spec = pltpu.PrefetchScalarGridSpec(
        num_scalar_prefetch=1,
        grid=(bsz, N_EXPERTS // DISP_NE, n_blk),
        in_specs=[sel, sel, pl.BlockSpec((MOE_TD, d), lambda b, hf, blk, tab: (b * n_blk + blk, 0))],
        out_specs=[pl.BlockSpec((DISP_NE, cap, d), lambda b, hf, blk, tab: (hf, b, 0)),
                   pl.BlockSpec((DISP_NE, chunks, SUBLANES, SLOT_CHUNK), lambda b, hf, blk, tab: (hf, b, 0, 0))],
    )
    return pl.pallas_call(
        functools.partial(_dispatch_body, cap=cap),
        out_shape=[jax.ShapeDtypeStruct((N_EXPERTS, bsz * cap, d), BF16),
                   jax.ShapeDtypeStruct((N_EXPERTS, bsz * chunks, SUBLANES, SLOT_CHUNK), F32)],
        grid_spec=grid_spec,
        compiler_params=_cparams(("arbitrary", "arbitrary", "arbitrary")),
        name="dispatch",
    )(tab_flat, posm, gate, h2_flat)


def _moe_body(x_ref, gsl_ref, wg_ref, wu_ref, wd_ref, y_ref, acc_scr):
    s = pl.program_id(1)

    chunks_per_tile = MOE_TM // SLOT_CHUNK

    def mlp(width, first, last):
        nt = (((1,), (1,)), ((), ()))

        def m_step(mi, carry):
            r0 = pl.multiple_of(mi * MOE_TM, MOE_TM)
            rows = pl.ds(r0, MOE_TM)
            xm = x_ref[0, rows, :]
            g = lax.dot_general(xm, wg_ref[0, :width, :].astype(BF16), nt, preferred_element_type=F32)
            u = lax.dot_general(xm, wu_ref[0, :width, :].astype(BF16), nt, preferred_element_type=F32)
            hid = (g * jax.nn.sigmoid(g) * u).astype(BF16)
            out = jnp.dot(hid, wd_ref[0, :width, :].astype(BF16), preferred_element_type=F32)
            if not first:
                out = acc_scr[rows, :] + out
            if last:
                gsl = gsl_ref[0, pl.ds(mi * chunks_per_tile, chunks_per_tile)]
                gcol = jnp.concatenate([jnp.sum(gsl[ci].T, axis=1, keepdims=True) for ci in range(chunks_per_tile)],
                                       axis=0)
                y_ref[0, rows, :] = (out * gcol).astype(BF16)
            else:
                acc_scr[rows, :] = out
            return carry

        lax.fori_loop(0, acc_scr.shape[0] // MOE_TM, m_step, 0)

    @pl.when(s == 0)
    def _():
        mlp(MOE_TF, True, False)

    @pl.when((s > 0) & (s < MOE_NF - 1))
    def _():
        mlp(MOE_TF, False, False)

    @pl.when(s == MOE_NF - 1)
    def _():
        mlp(MOE_F_LAST, False, True)


def _moe_call(xs, gsl, w_gate_t, w_up_t, w_down):
    n_e, slots, d = xs.shape
    per_e = pl.BlockSpec((1, slots, d), lambda e, s: (e, 0, 0))
    wspec = pl.BlockSpec((1, MOE_TF, d), lambda e, s: (e, s, 0))
    return pl.pallas_call(
        _moe_body,
        out_shape=jax.ShapeDtypeStruct((n_e, slots, d), BF16),
        grid=(n_e, MOE_NF),
        in_specs=[per_e, pl.BlockSpec((1,) + gsl.shape[1:], lambda e, s: (e, 0, 0, 0)), wspec, wspec, wspec],
        out_specs=per_e,
        scratch_shapes=[pltpu.VMEM((slots, d), F32)],
        compiler_params=_cparams(("arbitrary", "arbitrary")),
        name="moe",
    )(xs, gsl, w_gate_t, w_up_t, w_down)


def _combine_body(tab_ref, y_ref, posm_ref, x1_ref, gt2_ref, gpost_ref, o_ref, acc_ref, *, cap):
    n_e = N_EXPERTS
    b = pl.program_id(0)
    j = pl.program_id(1)
    pm = posm_ref[0].T
    win = 2 * SLOT_CHUNK
    slot_w = lax.broadcasted_iota(I32, (ROUTE_BLK, win), 1)
    slot_c = lax.broadcasted_iota(I32, (ROUTE_BLK, SLOT_CHUNK), 1)
    total = jnp.zeros((ROUTE_BLK, D_MODEL), F32)
    bases, extras = [], []
    for k in range(n_e):
        tab_base = (b * N_EXPERTS + k) * TAB_STRIDE
        start = tab_ref[tab_base + j]
        end = tab_ref[tab_base + j + 1]
        base = pl.multiple_of(jnp.minimum((start // SLOT_CHUNK) * SLOT_CHUNK, cap - win), SLOT_CHUNK)
        total = total + jnp.dot(_mask_bf16(pm[:, k:k + 1] == slot_w + base), y_ref[k, pl.ds(base, win), :],
                                preferred_element_type=F32)
        bases.append(base)
        extras.append(jnp.maximum(end - (base + win) + SLOT_CHUNK - 1, 0) // SLOT_CHUNK)
    acc_ref[...] = total

    @pl.when(sum(extras) > 0)
    def _():
        for k in range(n_e):
            def extra_step(kk, carry, k=k):
                b2 = pl.multiple_of(bases[k] + win + kk * SLOT_CHUNK, SLOT_CHUNK)
                acc_ref[...] += jnp.dot(_mask_bf16(pm[:, k:k + 1] == slot_c + b2),
                                        y_ref[k, pl.ds(b2, SLOT_CHUNK), :], preferred_element_type=F32)
                return carry

            lax.fori_loop(0, extras[k], extra_step, 0)

    o_ref[0] = x1_ref[0] + gt2_ref[0] * _rms(acc_ref[...], gpost_ref[...])


def _combine_call(tab_flat, y, posm, x1, mod, g_post2, *, cap):
    bsz, n_e, s = posm.shape
    d = D_MODEL
    tok = pl.BlockSpec((1, ROUTE_BLK, d), lambda b, j, tab: (b, j, 0))
    grid_spec = pltpu.PrefetchScalarGridSpec(
        num_scalar_prefetch=1,
        grid=(bsz, s // ROUTE_BLK),
        in_specs=[
            pl.BlockSpec((n_e, cap, d), lambda b, j, tab: (0, b, 0), pipeline_mode=pl.Buffered(1)),
            pl.BlockSpec((1, n_e, ROUTE_BLK), lambda b, j, tab: (b, 0, j)),
            tok,
            _mod_spec(5, lambda b: b, 3),
            pl.BlockSpec((1, d), lambda b, j, tab: (0, 0))],
        out_specs=tok,
        scratch_shapes=[pltpu.VMEM((ROUTE_BLK, d), F32)],
    )
    return pl.pallas_call(
        functools.partial(_combine_body, cap=cap),
        out_shape=jax.ShapeDtypeStruct((bsz, s, d), F32),
        grid_spec=grid_spec,
        compiler_params=_cparams(("arbitrary", "arbitrary")),
        name="combine",
    )(tab_flat, y, posm, x1, mod, g_post2)


def kernel(x, c, ctx, c_ctx, w_mod, b_mod, g_pre1, g_post1, g_pre2, g_post2, w_in, w_out, na_rpb,
           hg_lb_logits, hg_norm, w_router, w_gate, w_up, w_down):
    bsz, seq, d = x.shape
    assert (seq, d) == (SEQ, D_MODEL) and ctx.shape[1] == CTX_LEN and w_mod.shape[0] == 1
    cap = 2 * seq // N_EXPERTS
    assert cap % SLOT_CHUNK == 0

    assert bsz < MOD_ROWS
    mod = _mod_call(c, c_ctx, w_mod, b_mod)

    w_in_b = w_in[0].astype(BF16)
    w_out_b = w_out[0].astype(BF16)
    cos, sa, sb = _rope_tables(seq)

    qrot, qpl, krot, v, qh, zf, zb, ih, gh = _inproj_call(
        x, mod, lambda b: b, g_pre1, w_in_b, cos, sa, sb, rope=True, tm=IN_TM)
    ctx_flat = ctx.reshape(1, bsz * CTX_LEN, d)
    ctx_out = _inproj_call(ctx_flat, mod, lambda b: bsz, g_pre1, w_in_b,
                           cos[:bsz * CTX_LEN], sa[:bsz * CTX_LEN], sb[:bsz * CTX_LEN], rope=False, tm=CTX_LEN)
    _, _, kc, vc, qc, zfc, zbc, ic, _ = [a.reshape(bsz, CTX_LEN, COL_GROUP) for a in ctx_out]

    na_o = _na_call(qrot, qpl, krot, v, kc, vc, _na_bias(na_rpb[0]))
    hg_o = _hgrn_call(qh, zf, zb, ih, gh, qc, zfc, zbc, ic, hg_lb_logits, hg_norm)

    x1, h2, lt = _outproj_call(na_o, hg_o, x, w_out_b, mod, g_post1, g_pre2, w_router[0].T)

    posm, gate, tab = _route_call(lt, cap)
    tab_flat = tab.reshape(-1)
    xs, gsl = _dispatch_call(tab_flat, posm, gate, h2.reshape(bsz * seq, d), bsz=bsz, cap=cap)
    y = _moe_call(xs, gsl, jnp.swapaxes(w_gate[0], 1, 2), jnp.swapaxes(w_up[0], 1, 2), w_down[0])

    return _combine_call(tab_flat, y, posm, x1, mod, g_post2, cap=cap)
```

```python
import functools

import jax
import jax.numpy as jnp
import numpy as np
from jax import lax
from jax.experimental import pallas as pl
from jax.experimental.pallas import tpu as pltpu

F32 = jnp.float32
BF16 = jnp.bfloat16
I32 = jnp.int32
HIGHEST = lax.Precision.HIGHEST

D_MODEL = 1024
GRID_W = 64
N_ROWS = 128
SEQ = 8192
CTX_LEN = 256
NA_HEAD_DIM = 64
N_NA_HEADS = 8
NA_WIDTH = 512
NA_WIN_H = 8
NA_WIN_W = 16
ROPE_BASE = 10000.0
HG_DK = 128
N_HG_HEADS = 4
HG_CHUNK = 32
N_EXPERTS = 16
D_EXPERT = 2752
RMS_EPS = 1e-6
COL_GROUP = 512

LANES = 128
SUBLANES = 8
VMEM_LIMIT_BYTES = 58 * 1024 * 1024

IN_TM = 1024
NA_ROWS_PER_STEP = 64
NA_KEYS = NA_WIN_H * GRID_W
HG_BLOCK = 256
OUT_TM = 1024
ROUTE_BLK = 256
N_ROUTE_BLK = SEQ // ROUTE_BLK
SLOT_CHUNK = 128
TAB_STRIDE = LANES
MOE_TD = 2048
MOE_TF = 768
MOE_NF = -(-D_EXPERT // MOE_TF)
MOE_F_LAST = D_EXPERT - (MOE_NF - 1) * MOE_TF
MOE_TM = 1024


def _cparams(sem):
    return pltpu.CompilerParams(dimension_semantics=sem, vmem_limit_bytes=VMEM_LIMIT_BYTES)


def _mask_bf16(mask):
    return jnp.where(mask, 1.0, 0.0).astype(BF16)


def _rms(v, g):
    return v * lax.rsqrt(jnp.mean(v * v, axis=-1, keepdims=True) + RMS_EPS) * g


MOD_ROWS = 8
N_MOD = 6


def _mod_body(c_ref, cc_ref, w_ref, b_ref, o_ref, rows_scr):
    bsz = c_ref.shape[0]
    rows_scr[...] = jnp.zeros_like(rows_scr)
    rows_scr[0:bsz, :] = c_ref[...]
    rows_scr[bsz:bsz + 1, :] = cc_ref[...]
    cv = rows_scr[...]
    s = cv * jax.nn.sigmoid(cv)
    o_ref[...] = jnp.dot(s, w_ref[0], precision=HIGHEST, preferred_element_type=F32) + b_ref[...]


def _mod_call(c, c_ctx, w_mod, b_mod):
    bsz, d = c.shape
    n = w_mod.shape[2]
    tn = d
    mod = pl.pallas_call(
        _mod_body,
        out_shape=jax.ShapeDtypeStruct((MOD_ROWS, n), F32),
        grid=(n // tn,),
        in_specs=[pl.BlockSpec((bsz, d), lambda j: (0, 0)),
                  pl.BlockSpec((1, d), lambda j: (0, 0)),
                  pl.BlockSpec((1, d, tn), lambda j: (0, 0, j)),
                  pl.BlockSpec((1, tn), lambda j: (0, j))],
        out_specs=pl.BlockSpec((MOD_ROWS, tn), lambda j: (0, j)),
        scratch_shapes=[pltpu.VMEM((MOD_ROWS, d), F32)],
        compiler_params=_cparams(("arbitrary",)),
        name="mod",
    )(c, c_ctx[None, :], w_mod, b_mod)
    return mod.reshape(MOD_ROWS * N_MOD, 1, d)


def _mod_spec(k, row_of, n_grid):
    if n_grid == 2:
        return pl.BlockSpec((1, 1, D_MODEL), lambda b, i: (row_of(b) * N_MOD + k, 0, 0))
    return pl.BlockSpec((1, 1, D_MODEL), lambda b, i, tab: (row_of(b) * N_MOD + k, 0, 0))


def _rope128(pg, cos, sa, sb):
    return pg * cos + pltpu.roll(pg, LANES - 16, axis=1) * sa + pltpu.roll(pg, 16, axis=1) * sb


def _inproj_body(x_ref, sh_ref, sc_ref, g_ref, w_ref, cos_ref, sa_ref, sb_ref,
                 qrot_ref, qpl_ref, krot_ref, v_ref, qh_ref, zf_ref, zb_ref, ih_ref, gh_ref, *, rope):
    xv = x_ref[0]
    h = _rms(xv, g_ref[...]) * (1.0 + sc_ref[0]) + sh_ref[0]
    hb = h.astype(BF16)

    def group(j):
        return jnp.dot(hb, w_ref[:, j * COL_GROUP:(j + 1) * COL_GROUP], preferred_element_type=F32)

    def rotated(p):
        if not rope:
            return p
        cos, sa, sb = cos_ref[...], sa_ref[...], sb_ref[...]
        return jnp.concatenate(
            [_rope128(p[:, k * LANES:(k + 1) * LANES], cos, sa, sb) for k in range(COL_GROUP // LANES)], axis=1)

    scale = NA_HEAD_DIM ** -0.5
    p = group(0)
    qpl_ref[0] = (p * scale).astype(BF16)
    qrot_ref[0] = (rotated(p) * scale).astype(BF16)
    krot_ref[0] = rotated(group(1)).astype(BF16)
    v_ref[0] = group(2).astype(BF16)
    for j, (ref, dt) in enumerate(((qh_ref, BF16), (zf_ref, F32), (zb_ref, F32), (ih_ref, BF16), (gh_ref, BF16))):
        p = group(3 + j).astype(dt)
        for h in range(N_HG_HEADS):
            ref[0, h] = p[:, h * HG_DK:(h + 1) * HG_DK]


def _inproj_call(x3, mod, row_of, g_pre, w_in_b, cos, sa, sb, *, rope, tm):
    g, t, d = x3.shape
    n_out = 9
    dts = [BF16, BF16, BF16, BF16, BF16, F32, F32, BF16, BF16]
    tok = pl.BlockSpec((1, tm, d), lambda b, i: (b, i, 0))
    tab = pl.BlockSpec((tm, LANES), lambda b, i: (i, 0))
    n_na = 4
    outs = ([pl.BlockSpec((1, tm, COL_GROUP), lambda b, i: (b, i, 0))] * n_na
            + [pl.BlockSpec((1, N_HG_HEADS, tm, HG_DK), lambda b, i: (b, 0, i, 0))] * (n_out - n_na))
    shapes = [(g, t, COL_GROUP)] * n_na + [(g, N_HG_HEADS, t, HG_DK)] * (n_out - n_na)
    return pl.pallas_call(
        functools.partial(_inproj_body, rope=rope),
        out_shape=[jax.ShapeDtypeStruct(sh, dt) for sh, dt in zip(shapes, dts)],
        grid=(g, t // tm),
        in_specs=[tok, _mod_spec(0, row_of, 2), _mod_spec(1, row_of, 2),
                  pl.BlockSpec((1, d), lambda b, i: (0, 0)),
                  pl.BlockSpec(w_in_b.shape, lambda b, i: (0, 0)),
                  tab, tab, tab],
        out_specs=outs,
        compiler_params=_cparams(("arbitrary", "arbitrary")),
        name="inproj_rope" if rope else "inproj_ctx",
    )(x3, mod, mod, g_pre, w_in_b, cos, sa, sb)


def _rope_tables(n_tok):
    half = NA_HEAD_DIM // 2
    t = np.arange(n_tok)
    row = (t // GRID_W).astype(np.float32)
    col = (t % GRID_W).astype(np.float32)
    lane = np.arange(LANES)
    d = lane % NA_HEAD_DIM
    dd = d % half
    fi = dd % (half // 2)
    inv_freq = (ROPE_BASE ** (-(2.0 * fi.astype(np.float32)) / half)).astype(np.float32)
    pos = np.where((d < half)[None, :], row[:, None], col[:, None])
    ang = (pos * inv_freq[None, :]).astype(np.float32)
    first = (dd < half // 2)[None, :]
    cos = np.cos(ang.astype(np.float64)).astype(np.float32)
    sin = np.sin(ang.astype(np.float64)).astype(np.float32)
    zero = np.zeros_like(sin)
    return jnp.asarray(cos), jnp.asarray(np.where(first, -sin, zero)), jnp.asarray(np.where(first, zero, sin))


def _na_body(qrot_ref, qpl_ref, k_ref, v_ref, kc_ref, vc_ref, bias_ref, o_ref):
    m = pl.program_id(2)
    lane = lax.broadcasted_iota(I32, (1, LANES), 1)
    first_head = lane < NA_HEAD_DIM
    hm0, hm1 = _mask_bf16(first_head), _mask_bf16(lane >= NA_HEAD_DIM)
    nt = (((1,), (1,)), ((), ()))
    rows = range(NA_ROWS_PER_STEP)

    def stacked(ref):
        parts = []
        for i in rows:
            q = ref[0, i * GRID_W:(i + 1) * GRID_W, :]
            parts += [q * hm0, q * hm1]
        return jnp.concatenate(parts, axis=0)

    sc = lax.dot_general(stacked(qpl_ref), kc_ref[0], nt, preferred_element_type=F32)

    qs = stacked(qrot_ref)
    k0s, s_parts = [], []
    blk = 2 * GRID_W
    for i in rows:
        r = m * NA_ROWS_PER_STEP + i
        r0 = jnp.clip(r - NA_WIN_H // 2, 0, N_ROWS - NA_WIN_H)
        di = r0 - r + (NA_WIN_H - 1)
        k0s.append(pl.multiple_of(r0 * GRID_W, GRID_W))
        s_parts.append(lax.dot_general(qs[i * blk:(i + 1) * blk], k_ref[0, pl.ds(k0s[i], NA_KEYS), :], nt,
                                       preferred_element_type=F32) + bias_ref[0, di].reshape(blk, NA_KEYS))
    s = jnp.concatenate([jnp.concatenate(s_parts, axis=0), sc], axis=1)
    p = jnp.exp(s - jnp.max(s, axis=-1, keepdims=True))
    den = jnp.sum(p, axis=-1, keepdims=True)
    pb = p.astype(BF16)
    acc = jnp.concatenate([jnp.dot(pb[i * blk:(i + 1) * blk, :NA_KEYS], v_ref[0, pl.ds(k0s[i], NA_KEYS), :],
                                   preferred_element_type=F32) for i in rows], axis=0)
    o = (acc + jnp.dot(pb[:, NA_KEYS:], vc_ref[0], preferred_element_type=F32)) / den
    o_ref[0] = jnp.concatenate([jnp.where(first_head, o[i * blk:i * blk + GRID_W], o[i * blk + GRID_W:(i + 1) * blk])
                                for i in rows], axis=0).astype(BF16)


def _na_bias(rpb):
    c = np.arange(GRID_W)
    cq = np.arange(GRID_W)
    win_c0 = np.clip(cq - NA_WIN_W // 2, 0, GRID_W - NA_WIN_W)
    in_win = (c[None, :] >= win_c0[:, None]) & (c[None, :] < win_c0[:, None] + NA_WIN_W)
    dc = np.clip(c[None, :] - cq[:, None] + NA_WIN_W - 1, 0, 2 * NA_WIN_W - 2)
    pick = (dc[None, :, :] == np.arange(2 * NA_WIN_W - 1)[:, None, None]).astype(np.float32)
    t = jnp.einsum('hrj,jqc->hqrc', rpb.astype(F32), jnp.asarray(pick), precision=HIGHEST)
    t = jnp.where(jnp.asarray(in_win)[None, :, None, :], t, -jnp.inf)
    n_dr = 2 * NA_WIN_H - 1
    t = t.reshape(N_NA_HEADS // 2, 2, GRID_W, n_dr * GRID_W)
    return jnp.stack([t[..., di * GRID_W:di * GRID_W + NA_KEYS] for di in range(NA_WIN_H)], axis=1)


def _na_call(qrot, qpl, krot, v, kc, vc, bias):
    bsz = qrot.shape[0]
    tq = NA_ROWS_PER_STEP * GRID_W
    qspec = pl.BlockSpec((1, tq, LANES), lambda b, p, m: (b, m, p))
    kspec = pl.BlockSpec((1, SEQ, LANES), lambda b, p, m: (b, 0, p))
    cspec = pl.BlockSpec((1, CTX_LEN, LANES), lambda b, p, m: (b, 0, p))
    return pl.pallas_call(
        _na_body,
        out_shape=jax.ShapeDtypeStruct((bsz, SEQ, NA_WIDTH), BF16),
        grid=(bsz, N_NA_HEADS // 2, N_ROWS // NA_ROWS_PER_STEP),
        in_specs=[qspec, qspec, kspec, kspec, cspec, cspec,
                  pl.BlockSpec((1, NA_WIN_H, 2, GRID_W, NA_KEYS), lambda b, p, m: (p, 0, 0, 0, 0))],
        out_specs=qspec,
        compiler_params=_cparams(("arbitrary", "arbitrary", "arbitrary")),
        name="na",
    )(qrot, qpl, krot, v, kc, vc, bias)


def _split3(v):
    hi = v.astype(BF16)
    r1 = v - hi.astype(F32)
    mid = r1.astype(BF16)
    lo = (r1 - mid.astype(F32)).astype(BF16)
    return hi, mid, lo


HG_NCHUNK = HG_BLOCK // HG_CHUNK


def _hg_consts(tri_ref, keep_ref, spread_ref):
    nb = HG_BLOCK
    rr = lax.broadcasted_iota(I32, (nb, nb), 0)
    cc = lax.broadcasted_iota(I32, (nb, nb), 1)
    same = (rr // HG_CHUNK) == (cc // HG_CHUNK)
    fwd = jnp.where(same & (cc <= rr), 1.0, 0.0)
    bwd = jnp.where(same & (cc >= rr), 1.0, 0.0)
    tri_ref[0] = fwd.astype(BF16)
    tri_ref[1] = bwd.astype(BF16)
    keep_ref[...] = jnp.concatenate([fwd, bwd], axis=0)
    r8 = lax.broadcasted_iota(I32, (nb, HG_NCHUNK * HG_DK), 0) // HG_CHUNK
    g8 = lax.broadcasted_iota(I32, (nb, HG_NCHUNK * HG_DK), 1) // HG_DK
    spread_ref[...] = _mask_bf16(r8 == g8)


def _hg_pair(zf, zb, qf, qb, vf, vb, stf, stb, lb, tri_ref, keep_ref, spread_ref, *, want_out):
    nb = HG_BLOCK
    spread = spread_ref[...]
    v = (vf, vb)

    sig = jax.nn.sigmoid(jnp.concatenate([zf, zb], axis=0))
    lf = jnp.log(lb + (1.0 - lb) * sig)
    kk = (1.0 - lb) * (1.0 - sig)
    parts = jnp.concatenate(_split3(lf), axis=1)
    sums = jnp.concatenate([jnp.dot(tri_ref[d], parts[d * nb:(d + 1) * nb], preferred_element_type=F32)
                            for d in range(2)], axis=0)
    bcum = sums[:, :HG_DK] + sums[:, HG_DK:2 * HG_DK] + sums[:, 2 * HG_DK:]
    end_rows = [d * nb + c * HG_CHUNK + (HG_CHUNK - 1 if d == 0 else 0) for d in range(2) for c in range(HG_NCHUNK)]
    ends = [bcum[r:r + 1, :] for r in end_rows]
    btot = jnp.concatenate([jnp.broadcast_to(e, (HG_CHUNK, HG_DK)) for e in ends], axis=0)
    dec = jnp.exp(jnp.concatenate(ends, axis=0))
    ke = (kk * jnp.exp(btot - bcum)).astype(BF16)

    new_states, entering = [], []
    for d, st in enumerate((stf, stb)):
        kvt = lax.dot_general(v[d], jnp.concatenate([ke[d * nb:(d + 1) * nb]] * HG_NCHUNK, axis=1) * spread,
                              (((0,), (0,)), ((), ())), preferred_element_type=F32)
        ent = [None] * HG_NCHUNK
        for c in (range(HG_NCHUNK) if d == 0 else range(HG_NCHUNK - 1, -1, -1)):
            ent[c] = st
            i_dec = d * HG_NCHUNK + c
            st = st * dec[i_dec:i_dec + 1, :] + kvt[:, c * HG_DK:(c + 1) * HG_DK]
        new_states.append(st)
        entering.append(ent)
    if not want_out:
        return None, None, new_states[0], new_states[1]

    qd = (jnp.concatenate([qf, qb], axis=0).astype(F32) * jnp.exp(bcum)).astype(BF16)
    kd = (kk * jnp.exp(-bcum)).astype(BF16)
    nt = (((1,), (1,)), ((), ()))
    a = jnp.concatenate([lax.dot_general(qd[d * nb:(d + 1) * nb], kd[d * nb:(d + 1) * nb], nt,
                                         preferred_element_type=F32) for d in range(2)], axis=0)
    a = jnp.where(keep_ref[...] > 0.5, a, 0.0).astype(BF16)
    outs = []
    for d in range(2):
        qd_d = qd[d * nb:(d + 1) * nb]
        inter = jnp.concatenate(
            [lax.dot_general(qd_d[c * HG_CHUNK:(c + 1) * HG_CHUNK], entering[d][c].astype(BF16), nt,
                             preferred_element_type=F32) for c in range(HG_NCHUNK)], axis=0)
        outs.append(jnp.dot(a[d * nb:(d + 1) * nb], v[d], preferred_element_type=F32) + inter)
    return outs[0], outs[1], new_states[0], new_states[1]


def _hgrn_body(q_ref, zf_ref, zb_ref, i_ref, g_ref, qc_ref, zfc_ref, zbc_ref, ic_ref, lb_ref, ng_ref,
               o_ref, accf_ref, accb_ref, stf_ref, stb_ref, tri_ref, keep_ref, spread_ref):
    lbl = lb_ref[...]
    lbe = jnp.exp(lbl - jnp.max(lbl, axis=0, keepdims=True))
    lb = lbe[0:1, :] / jnp.sum(lbe, axis=0, keepdims=True)
    _hg_consts(tri_ref, keep_ref, spread_ref)
    pair = functools.partial(_hg_pair, lb=lb, tri_ref=tri_ref, keep_ref=keep_ref, spread_ref=spread_ref)
    zero_state = jnp.zeros((HG_DK, HG_DK), F32)
    _, _, stf, stb = pair(zfc_ref[0], zbc_ref[0], qc_ref[0], qc_ref[0], ic_ref[0], ic_ref[0],
                          zero_state, zero_state, want_out=False)
    stf_ref[...] = stf
    stb_ref[...] = stb
    n_blk = SEQ // HG_BLOCK

    def scan_step(n, carry):
        rows_f = pl.ds(pl.multiple_of(n * HG_BLOCK, HG_BLOCK), HG_BLOCK)
        rows_b = pl.ds(pl.multiple_of((n_blk - 1 - n) * HG_BLOCK, HG_BLOCK), HG_BLOCK)
        o_f, o_b, st_f, st_b = pair(zf_ref[0, rows_f, :], zb_ref[0, rows_b, :], q_ref[0, rows_f, :],
                                    q_ref[0, rows_b, :], i_ref[0, rows_f, :], i_ref[0, rows_b, :],
                                    stf_ref[...], stb_ref[...], want_out=True)
        stf_ref[...] = st_f
        stb_ref[...] = st_b
        accf_ref[rows_f, :] = o_f
        accb_ref[rows_b, :] = o_b
        return carry

    lax.fori_loop(0, n_blk, scan_step, 0, unroll=8)

    def readout_step(n, carry):
        rows = pl.ds(pl.multiple_of(n * HG_BLOCK, HG_BLOCK), HG_BLOCK)
        tot = accf_ref[rows, :] + accb_ref[rows, :]
        y = _rms(tot, ng_ref[...]) * jax.nn.sigmoid(g_ref[0, rows, :].astype(F32))
        o_ref[0, rows, :] = y.astype(BF16)
        return carry

    lax.fori_loop(0, n_blk, readout_step, 0, unroll=2)


def _hgrn_call(qh, zf, zb, ih, gh, qc, zfc, zbc, ic, lb_logits, norm_g):
    bsz = qh.shape[0]
    seq = pl.BlockSpec((None, 1, SEQ, HG_DK), lambda b, h: (b, h, 0, 0))
    ctx = pl.BlockSpec((None, 1, CTX_LEN, HG_DK), lambda b, h: (h, b, 0, 0))
    return pl.pallas_call(
        _hgrn_body,
        out_shape=jax.ShapeDtypeStruct((bsz, SEQ, N_HG_HEADS * HG_DK), BF16),
        grid=(bsz, N_HG_HEADS),
        in_specs=[seq, seq, seq, seq, seq, ctx, ctx, ctx, ctx,
                  pl.BlockSpec((lb_logits.shape[0], HG_DK), lambda b, h: (0, h)),
                  pl.BlockSpec((1, HG_DK), lambda b, h: (0, 0))],
        out_specs=pl.BlockSpec((1, SEQ, HG_DK), lambda b, h: (b, 0, h)),
        scratch_shapes=[pltpu.VMEM((SEQ, HG_DK), F32),
                        pltpu.VMEM((SEQ, HG_DK), F32),
                        pltpu.VMEM((HG_DK, HG_DK), F32),
                        pltpu.VMEM((HG_DK, HG_DK), F32),
                        pltpu.VMEM((2, HG_BLOCK, HG_BLOCK), BF16),
                        pltpu.VMEM((2 * HG_BLOCK, HG_BLOCK), F32),
                        pltpu.VMEM((HG_BLOCK, HG_NCHUNK * HG_DK), BF16)],
        compiler_params=_cparams(("arbitrary", "arbitrary")),
        name="hgrn",
    )(qh, zf, zb, ih, gh, qc, zfc, zbc, ic, lb_logits, norm_g)


def _outproj_body(na_ref, hg_ref, x_ref, w_ref, gt1_ref, sh2_ref, sc2_ref, gpost_ref, gpre_ref, wr_ref,
                  x1_ref, h2_ref, lt_ref):
    mix = (jnp.dot(na_ref[0], w_ref[:NA_WIDTH, :], preferred_element_type=F32)
           + jnp.dot(hg_ref[0], w_ref[NA_WIDTH:, :], preferred_element_type=F32))
    x1 = x_ref[0] + gt1_ref[0] * _rms(mix, gpost_ref[...])
    x1_ref[0] = x1
    h2 = _rms(x1, gpre_ref[...]) * (1.0 + sc2_ref[0]) + sh2_ref[0]
    h2_ref[0] = h2.astype(BF16)
    lt_ref[0] = lax.dot_general(wr_ref[...], h2, (((1,), (1,)), ((), ())), precision=HIGHEST,
                                preferred_element_type=F32)


def _outproj_call(na_o, hg_o, x, w_out_b, mod, g_post1, g_pre2, w_router_t):
    bsz, s, d = x.shape
    tm = OUT_TM
    half = pl.BlockSpec((1, tm, NA_WIDTH), lambda b, i: (b, i, 0))
    tok = pl.BlockSpec((1, tm, d), lambda b, i: (b, i, 0))
    gt1, sh2, sc2 = (_mod_spec(k, lambda b: b, 2) for k in (2, 3, 4))
    par = pl.BlockSpec((1, d), lambda b, i: (0, 0))
    return pl.pallas_call(
        _outproj_body,
        out_shape=[jax.ShapeDtypeStruct((bsz, s, d), F32),
                   jax.ShapeDtypeStruct((bsz, s, d), BF16),
                   jax.ShapeDtypeStruct((bsz, N_EXPERTS, s), F32)],
        grid=(bsz, s // tm),
        in_specs=[half, half, tok, pl.BlockSpec((d, d), lambda b, i: (0, 0)), gt1, sh2, sc2, par, par,
                  pl.BlockSpec((N_EXPERTS, d), lambda b, i: (0, 0))],
        out_specs=[tok, tok, pl.BlockSpec((1, N_EXPERTS, tm), lambda b, i: (b, 0, i))],
        compiler_params=_cparams(("arbitrary", "arbitrary")),
        name="outproj",
    )(na_o, hg_o, x, w_out_b, mod, mod, mod, g_post1, g_pre2, w_router_t)


def _route_body(lt_ref, posm_ref, gate_ref, tab_ref, *, cap):
    l = lt_ref[0]
    mx = jnp.max(l, axis=0, keepdims=True)
    ex = jnp.exp(l - mx)
    aff = ex / jnp.sum(ex, axis=0, keepdims=True)
    capf = jnp.float32(cap)

    def count(mask):
        return jnp.sum(jnp.where(mask, 1.0, 0.0), axis=1, keepdims=True)

    def enough(v):
        return count(aff >= v) >= capf

    def bit_step(it, thr):
        cand = thr | (jnp.int32(1) << (30 - it))
        return jnp.where(enough(pltpu.bitcast(cand, F32)), cand, thr)

    thr = lax.fori_loop(0, 31, bit_step, jnp.zeros((N_EXPERTS, 1), I32))
    lo = pltpu.bitcast(thr, F32)
    hi = pltpu.bitcast(thr + 1, F32)

    def mid_step(it, lohi):
        lo, hi = lohi
        mid = 0.5 * (lo + hi)
        ok = enough(mid)
        return jnp.where(ok, mid, lo), jnp.where(ok, hi, mid)

    lo, hi = lax.fori_loop(0, 30, mid_step, (lo, hi))
    gt = aff >= hi
    eq = (aff >= lo) & jnp.logical_not(gt)
    need = capf - count(gt)

    rr = lax.broadcasted_iota(I32, (ROUTE_BLK, ROUTE_BLK), 0)
    cc = lax.broadcasted_iota(I32, (ROUTE_BLK, ROUTE_BLK), 1)
    upper = _mask_bf16(rr <= cc)
    lane = lax.broadcasted_iota(I32, (N_EXPERTS, LANES), 1)

    off_eq = jnp.zeros((N_EXPERTS, 1), F32)
    off_sel = jnp.zeros((N_EXPERTS, 1), F32)
    tab = jnp.zeros((N_EXPERTS, LANES), F32)
    for j in range(N_ROUTE_BLK):
        sl = slice(j * ROUTE_BLK, (j + 1) * ROUTE_BLK)
        eq_j = eq[:, sl]
        eq_b = _mask_bf16(eq_j)
        incl_eq = jnp.dot(eq_b, upper, preferred_element_type=F32) + off_eq
        rank_eq = incl_eq - eq_b.astype(F32)
        sel_j = gt[:, sl] | (eq_j & (rank_eq < need))
        sel_b = _mask_bf16(sel_j)
        incl_sel = jnp.dot(sel_b, upper, preferred_element_type=F32) + off_sel
        pos = incl_sel - sel_b.astype(F32)
        posm_ref[0, :, sl] = jnp.where(sel_j, pos.astype(I32), -1)
        gate_ref[0, :, sl] = jnp.where(sel_j, aff[:, sl], 0.0)
        tab = jnp.where(lane == j, off_sel, tab)
        off_eq = incl_eq[:, ROUTE_BLK - 1:ROUTE_BLK]
        off_sel = incl_sel[:, ROUTE_BLK - 1:ROUTE_BLK]
    tab = jnp.where(lane == N_ROUTE_BLK, off_sel, tab)
    tab_ref[0] = tab.astype(I32)


def _route_call(lt, cap):
    bsz = lt.shape[0]
    big = pl.BlockSpec((1, N_EXPERTS, SEQ), lambda b: (b, 0, 0))
    return pl.pallas_call(
        functools.partial(_route_body, cap=cap),
        out_shape=[jax.ShapeDtypeStruct((bsz, N_EXPERTS, SEQ), I32),
                   jax.ShapeDtypeStruct((bsz, N_EXPERTS, SEQ), F32),
                   jax.ShapeDtypeStruct((bsz, N_EXPERTS, LANES), I32)],
        grid=(bsz,),
        in_specs=[big],
        out_specs=[big, big, pl.BlockSpec((1, N_EXPERTS, LANES), lambda b: (b, 0, 0))],
        compiler_params=_cparams(("arbitrary",)),
        name="route",
    )(lt)


def _chunk_range(start, end):
    c_lo = start // SLOT_CHUNK
    n = jnp.where(end > start, (end - 1) // SLOT_CHUNK - c_lo + 1, 0)
    return c_lo, n


DISP_NE = 8


def _dispatch_body(tab_ref, posm_ref, gate_ref, h2_ref, x_ref, gsl_ref, *, cap):
    b = pl.program_id(0)
    hf = pl.program_id(1)
    blk = pl.program_id(2)
    tiles_per_step = MOE_TD // ROUTE_BLK
    chunks_per_batch = cap // SLOT_CHUNK

    @pl.when(blk == 0)
    def _():
        x_ref[...] = jnp.zeros_like(x_ref)
        gsl_ref[...] = jnp.zeros_like(gsl_ref)

    slot = lax.broadcasted_iota(I32, (SLOT_CHUNK, ROUTE_BLK), 0)

    def chunk_ranges(k):
        tab_base = (b * N_EXPERTS + hf * DISP_NE + k) * TAB_STRIDE + blk * tiles_per_step
        return [_chunk_range(tab_ref[tab_base + sub], tab_ref[tab_base + sub + 1]) for sub in range(tiles_per_step)]

    def add_chunk(k, sub, chunk):
        cols = slice(sub * ROUTE_BLK, (sub + 1) * ROUTE_BLK)
        base = pl.multiple_of(chunk * SLOT_CHUNK, SLOT_CHUNK)
        onehot = _mask_bf16(posm_ref[0, pl.ds(k, 1), cols] == slot + base)
        got = jnp.dot(onehot, h2_ref[cols, :], preferred_element_type=F32).astype(BF16)
        x_ref[k, pl.ds(base, SLOT_CHUNK), :] = x_ref[k, pl.ds(base, SLOT_CHUNK), :] + got
        g3 = [p.astype(F32) for p in _split3(gate_ref[0, pl.ds(k, 1), cols])]
        g8 = jnp.concatenate(g3 + [jnp.zeros((SUBLANES - len(g3), ROUTE_BLK), F32)], axis=0).astype(BF16)
        gsl_ref[k, chunk] = gsl_ref[k, chunk] + lax.dot_general(g8, onehot, (((1,), (1,)), ((), ())),
                                                                preferred_element_type=F32)

    for k in range(DISP_NE):
        for sub, (c_lo, _) in enumerate(chunk_ranges(k)):
            add_chunk(k, sub, jnp.minimum(c_lo, chunks_per_batch - 1))

    def expert_step(k, carry):
        for sub, (c_lo, n) in enumerate(chunk_ranges(k)):
            def more(kk, c, sub=sub, c_lo=c_lo):
                add_chunk(k, sub, c_lo + kk)
                return c

            lax.fori_loop(1, n, more, 0)
        return carry

    lax.fori_loop(0, DISP_NE, expert_step, 0)


def _dispatch_call(tab_flat, posm, gate, h2_flat, *, bsz, cap):
    d = D_MODEL
    n_blk = SEQ // MOE_TD
    chunks = cap // SLOT_CHUNK
    sel = pl.BlockSpec((1, DISP_NE, MOE_TD), lambda b, hf, blk, tab: (b, hf, blk))
    grid_spec = pltpu.PrefetchScalarGridSpec(
        num_scalar_prefetch=1,
        grid=(bsz, N_EXPERTS // DISP_NE, n_blk),
        in_specs=[sel, sel, pl.BlockSpec((MOE_TD, d), lambda b, hf, blk, tab: (b * n_blk + blk, 0))],
        out_specs=[pl.BlockSpec((DISP_NE, cap, d), lambda b, hf, blk, tab: (hf, b, 0)),
                   pl.BlockSpec((DISP_NE, chunks, SUBLANES, SLOT_CHUNK), lambda b, hf, blk, tab: (hf, b, 0, 0))],
    )
    return pl.pallas_call(
        functools.partial(_dispatch_body, cap=cap),
        out_shape=[jax.ShapeDtypeStruct((N_EXPERTS, bsz * cap, d), BF16),
                   jax.ShapeDtypeStruct((N_EXPERTS, bsz * chunks, SUBLANES, SLOT_CHUNK), F32)],
        grid_spec=grid_spec,
        compiler_params=_cparams(("arbitrary", "arbitrary", "arbitrary")),
        name="dispatch",
    )(tab_flat, posm, gate, h2_flat)


def _moe_body(x_ref, gsl_ref, wg_ref, wu_ref, wd_ref, y_ref, acc_scr):
    s = pl.program_id(1)

    chunks_per_tile = MOE_TM // SLOT_CHUNK

    def mlp(width, first, last):
        nt = (((1,), (1,)), ((), ()))

        def m_step(mi, carry):
            r0 = pl.multiple_of(mi * MOE_TM, MOE_TM)
            rows = pl.ds(r0, MOE_TM)
            xm = x_ref[0, rows, :]
            g = lax.dot_general(xm, wg_ref[0, :width, :].astype(BF16), nt, preferred_element_type=F32)
            u = lax.dot_general(xm, wu_ref[0, :width, :].astype(BF16), nt, preferred_element_type=F32)
            hid = (g * jax.nn.sigmoid(g) * u).astype(BF16)
            out = jnp.dot(hid, wd_ref[0, :width, :].astype(BF16), preferred_element_type=F32)
            if not first:
                out = acc_scr[rows, :] + out
            if last:
                gsl = gsl_ref[0, pl.ds(mi * chunks_per_tile, chunks_per_tile)]
                gcol = jnp.concatenate([jnp.sum(gsl[ci].T, axis=1, keepdims=True) for ci in range(chunks_per_tile)],
                                       axis=0)
                y_ref[0, rows, :] = (out * gcol).astype(BF16)
            else:
                acc_scr[rows, :] = out
            return carry

        lax.fori_loop(0, acc_scr.shape[0] // MOE_TM, m_step, 0)

    @pl.when(s == 0)
    def _():
        mlp(MOE_TF, True, False)

    @pl.when((s > 0) & (s < MOE_NF - 1))
    def _():
        mlp(MOE_TF, False, False)

    @pl.when(s == MOE_NF - 1)
    def _():
        mlp(MOE_F_LAST, False, True)


def _moe_call(xs, gsl, w_gate_t, w_up_t, w_down):
    n_e, slots, d = xs.shape
    per_e = pl.BlockSpec((1, slots, d), lambda e, s: (e, 0, 0))
    wspec = pl.BlockSpec((1, MOE_TF, d), lambda e, s: (e, s, 0))
    return pl.pallas_call(
        _moe_body,
        out_shape=jax.ShapeDtypeStruct((n_e, slots, d), BF16),
        grid=(n_e, MOE_NF),
        in_specs=[per_e, pl.BlockSpec((1,) + gsl.shape[1:], lambda e, s: (e, 0, 0, 0)), wspec, wspec, wspec],
        out_specs=per_e,
        scratch_shapes=[pltpu.VMEM((slots, d), F32)],
        compiler_params=_cparams(("arbitrary", "arbitrary")),
        name="moe",
    )(xs, gsl, w_gate_t, w_up_t, w_down)


def _combine_body(tab_ref, y_ref, posm_ref, x1_ref, gt2_ref, gpost_ref, o_ref, acc_ref, *, cap):
    n_e = N_EXPERTS
    b = pl.program_id(0)
    j = pl.program_id(1)
    pm = posm_ref[0].T
    win = 2 * SLOT_CHUNK
    slot_w = lax.broadcasted_iota(I32, (ROUTE_BLK, win), 1)
    slot_c = lax.broadcasted_iota(I32, (ROUTE_BLK, SLOT_CHUNK), 1)
    total = jnp.zeros((ROUTE_BLK, D_MODEL), F32)
    bases, extras = [], []
    for k in range(n_e):
        tab_base = (b * N_EXPERTS + k) * TAB_STRIDE
        start = tab_ref[tab_base + j]
        end = tab_ref[tab_base + j + 1]
        base = pl.multiple_of(jnp.minimum((start // SLOT_CHUNK) * SLOT_CHUNK, cap - win), SLOT_CHUNK)
        total = total + jnp.dot(_mask_bf16(pm[:, k:k + 1] == slot_w + base), y_ref[k, pl.ds(base, win), :],
                                preferred_element_type=F32)
        bases.append(base)
        extras.append(jnp.maximum(end - (base + win) + SLOT_CHUNK - 1, 0) // SLOT_CHUNK)
    acc_ref[...] = total

    @pl.when(sum(extras) > 0)
    def _():
        for k in range(n_e):
            def extra_step(kk, carry, k=k):
                b2 = pl.multiple_of(bases[k] + win + kk * SLOT_CHUNK, SLOT_CHUNK)
                acc_ref[...] += jnp.dot(_mask_bf16(pm[:, k:k + 1] == slot_c + b2),
                                        y_ref[k, pl.ds(b2, SLOT_CHUNK), :], preferred_element_type=F32)
                return carry

            lax.fori_loop(0, extras[k], extra_step, 0)

    o_ref[0] = x1_ref[0] + gt2_ref[0] * _rms(acc_ref[...], gpost_ref[...])


def _combine_call(tab_flat, y, posm, x1, mod, g_post2, *, cap):
    bsz, n_e, s = posm.shape
    d = D_MODEL
    tok = pl.BlockSpec((1, ROUTE_BLK, d), lambda b, j, tab: (b, j, 0))
    grid_spec = pltpu.PrefetchScalarGridSpec(
        num_scalar_prefetch=1,
        grid=(bsz, s // ROUTE_BLK),
        in_specs=[
            pl.BlockSpec((n_e, cap, d), lambda b, j, tab: (0, b, 0), pipeline_mode=pl.Buffered(1)),
            pl.BlockSpec((1, n_e, ROUTE_BLK), lambda b, j, tab: (b, 0, j)),
            tok,
            _mod_spec(5, lambda b: b, 3),
            pl.BlockSpec((1, d), lambda b, j, tab: (0, 0))],
        out_specs=tok,
        scratch_shapes=[pltpu.VMEM((ROUTE_BLK, d), F32)],
    )
    return pl.pallas_call(
        functools.partial(_combine_body, cap=cap),
        out_shape=jax.ShapeDtypeStruct((bsz, s, d), F32),
        grid_spec=grid_spec,
        compiler_params=_cparams(("arbitrary", "arbitrary")),
        name="combine",
    )(tab_flat, y, posm, x1, mod, g_post2)


def kernel(x, c, ctx, c_ctx, w_mod, b_mod, g_pre1, g_post1, g_pre2, g_post2, w_in, w_out, na_rpb,
           hg_lb_logits, hg_norm, w_router, w_gate, w_up, w_down):
    bsz, seq, d = x.shape
    assert (seq, d) == (SEQ, D_MODEL) and ctx.shape[1] == CTX_LEN and w_mod.shape[0] == 1
    cap = 2 * seq // N_EXPERTS
    assert cap % SLOT_CHUNK == 0

    assert bsz < MOD_ROWS
    mod = _mod_call(c, c_ctx, w_mod, b_mod)

    w_in_b = w_in[0].astype(BF16)
    w_out_b = w_out[0].astype(BF16)
    cos, sa, sb = _rope_tables(seq)

    qrot, qpl, krot, v, qh, zf, zb, ih, gh = _inproj_call(
        x, mod, lambda b: b, g_pre1, w_in_b, cos, sa, sb, rope=True, tm=IN_TM)
    ctx_flat = ctx.reshape(1, bsz * CTX_LEN, d)
    ctx_out = _inproj_call(ctx_flat, mod, lambda b: bsz, g_pre1, w_in_b,
                           cos[:bsz * CTX_LEN], sa[:bsz * CTX_LEN], sb[:bsz * CTX_LEN], rope=False, tm=CTX_LEN)
    kc, vc = [a.reshape(bsz, CTX_LEN, COL_GROUP) for a in ctx_out[2:4]]
    qc, zfc, zbc, ic = [a.reshape(N_HG_HEADS, bsz, CTX_LEN, HG_DK) for a in ctx_out[4:8]]

    na_o = _na_call(qrot, qpl, krot, v, kc, vc, _na_bias(na_rpb[0]))
    hg_o = _hgrn_call(qh, zf, zb, ih, gh, qc, zfc, zbc, ic, hg_lb_logits, hg_norm)

    x1, h2, lt = _outproj_call(na_o, hg_o, x, w_out_b, mod, g_post1, g_pre2, w_router[0].T)

    posm, gate, tab = _route_call(lt, cap)
    tab_flat = tab.reshape(-1)
    xs, gsl = _dispatch_call(tab_flat, posm, gate, h2.reshape(bsz * seq, d), bsz=bsz, cap=cap)
    y = _moe_call(xs, gsl, jnp.swapaxes(w_gate[0], 1, 2), jnp.swapaxes(w_up[0], 1, 2), w_down[0])

    return _combine_call(tab_flat, y, posm, x1, mod, g_post2, cap=cap)
```

```python
import functools

import jax
import jax.numpy as jnp
import numpy as np
from jax import lax
from jax.experimental import pallas as pl
from jax.experimental.pallas import tpu as pltpu

F32 = jnp.float32
BF16 = jnp.bfloat16
I32 = jnp.int32
HIGHEST = lax.Precision.HIGHEST

D_MODEL = 1024
GRID_W = 64
N_ROWS = 128
SEQ = 8192
CTX_LEN = 256
NA_HEAD_DIM = 64
N_NA_HEADS = 8
NA_WIDTH = 512
NA_WIN_H = 8
NA_WIN_W = 16
ROPE_BASE = 10000.0
HG_DK = 128
N_HG_HEADS = 4
HG_CHUNK = 32
N_EXPERTS = 16
D_EXPERT = 2752
RMS_EPS = 1e-6
COL_GROUP = 512

LANES = 128
SUBLANES = 8
VMEM_LIMIT_BYTES = 58 * 1024 * 1024

IN_TM = 1024
NA_ROWS_PER_STEP = 64
NA_KEYS = NA_WIN_H * GRID_W
HG_BLOCK = 256
OUT_TM = 1024
ROUTE_BLK = 256
N_ROUTE_BLK = SEQ // ROUTE_BLK
SLOT_CHUNK = 128
TAB_STRIDE = LANES
MOE_TD = 2048
MOE_TF = 768
MOE_NF = -(-D_EXPERT // MOE_TF)
MOE_F_LAST = D_EXPERT - (MOE_NF - 1) * MOE_TF
MOE_TM = 1024


def _cparams(sem):
    return pltpu.CompilerParams(dimension_semantics=sem, vmem_limit_bytes=VMEM_LIMIT_BYTES)


def _mask_bf16(mask):
    return jnp.where(mask, 1.0, 0.0).astype(BF16)


def _rms(v, g):
    return v * lax.rsqrt(jnp.mean(v * v, axis=-1, keepdims=True) + RMS_EPS) * g


MOD_ROWS = 8
N_MOD = 6


def _mod_body(c_ref, cc_ref, w_ref, b_ref, o_ref, rows_scr):
    bsz = c_ref.shape[0]
    rows_scr[...] = jnp.zeros_like(rows_scr)
    rows_scr[0:bsz, :] = c_ref[...]
    rows_scr[bsz:bsz + 1, :] = cc_ref[...]
    cv = rows_scr[...]
    s = cv * jax.nn.sigmoid(cv)
    o_ref[...] = jnp.dot(s, w_ref[0], precision=HIGHEST, preferred_element_type=F32) + b_ref[...]


def _mod_call(c, c_ctx, w_mod, b_mod):
    bsz, d = c.shape
    n = w_mod.shape[2]
    tn = d
    mod = pl.pallas_call(
        _mod_body,
        out_shape=jax.ShapeDtypeStruct((MOD_ROWS, n), F32),
        grid=(n // tn,),
        in_specs=[pl.BlockSpec((bsz, d), lambda j: (0, 0)),
                  pl.BlockSpec((1, d), lambda j: (0, 0)),
                  pl.BlockSpec((1, d, tn), lambda j: (0, 0, j)),
                  pl.BlockSpec((1, tn), lambda j: (0, j))],
        out_specs=pl.BlockSpec((MOD_ROWS, tn), lambda j: (0, j)),
        scratch_shapes=[pltpu.VMEM((MOD_ROWS, d), F32)],
        compiler_params=_cparams(("arbitrary",)),
        name="mod",
    )(c, c_ctx[None, :], w_mod, b_mod)
    return mod.reshape(MOD_ROWS * N_MOD, 1, d)


def _mod_spec(k, row_of, n_grid):
    if n_grid == 2:
        return pl.BlockSpec((1, 1, D_MODEL), lambda b, i: (row_of(b) * N_MOD + k, 0, 0))
    return pl.BlockSpec((1, 1, D_MODEL), lambda b, i, tab: (row_of(b) * N_MOD + k, 0, 0))


def _rope128(pg, cos, sa, sb):
    return pg * cos + pltpu.roll(pg, LANES - 16, axis=1) * sa + pltpu.roll(pg, 16, axis=1) * sb


def _inproj_body(x_ref, sh_ref, sc_ref, g_ref, w_ref, cos_ref, sa_ref, sb_ref,
                 qrot_ref, qpl_ref, krot_ref, v_ref, qh_ref, zf_ref, zb_ref, ih_ref, gh_ref, *, rope):
    xv = x_ref[0]
    h = _rms(xv, g_ref[...]) * (1.0 + sc_ref[0]) + sh_ref[0]
    hb = h.astype(BF16)

    def group(j):
        return jnp.dot(hb, w_ref[0, :, j * COL_GROUP:(j + 1) * COL_GROUP].astype(BF16), preferred_element_type=F32)

    def rotated(p):
        if not rope:
            return p
        cos, sa, sb = cos_ref[...], sa_ref[...], sb_ref[...]
        return jnp.concatenate(
            [_rope128(p[:, k * LANES:(k + 1) * LANES], cos, sa, sb) for k in range(COL_GROUP // LANES)], axis=1)

    scale = NA_HEAD_DIM ** -0.5
    p = group(0)
    qpl_ref[0] = (p * scale).astype(BF16)
    qrot_ref[0] = (rotated(p) * scale).astype(BF16)
    krot_ref[0] = rotated(group(1)).astype(BF16)
    v_ref[0] = group(2).astype(BF16)
    qh_ref[0] = group(3).astype(BF16)
    zf_ref[0] = group(4)
    zb_ref[0] = group(5)
    ih_ref[0] = group(6).astype(BF16)
    gh_ref[0] = group(7).astype(BF16)


def _inproj_call(x3, mod, row_of, g_pre, w_in, cos, sa, sb, *, rope, tm):
    g, t, d = x3.shape
    n_out = 9
    dts = [BF16, BF16, BF16, BF16, BF16, F32, F32, BF16, BF16]
    tok = pl.BlockSpec((1, tm, d), lambda b, i: (b, i, 0))
    tab = pl.BlockSpec((tm, LANES), lambda b, i: (i, 0))
    outs = [pl.BlockSpec((1, tm, COL_GROUP), lambda b, i: (b, i, 0)) for _ in range(n_out)]
    return pl.pallas_call(
        functools.partial(_inproj_body, rope=rope),
        out_shape=[jax.ShapeDtypeStruct((g, t, COL_GROUP), dt) for dt in dts],
        grid=(g, t // tm),
        in_specs=[tok, _mod_spec(0, row_of, 2), _mod_spec(1, row_of, 2),
                  pl.BlockSpec((1, d), lambda b, i: (0, 0)),
                  pl.BlockSpec(w_in.shape, lambda b, i: (0, 0, 0), pipeline_mode=pl.Buffered(1)),
                  tab, tab, tab],
        out_specs=outs,
        compiler_params=_cparams(("arbitrary", "arbitrary")),
        name="inproj_rope" if rope else "inproj_ctx",
    )(x3, mod, mod, g_pre, w_in, cos, sa, sb)


def _rope_tables(n_tok):
    half = NA_HEAD_DIM // 2
    t = np.arange(n_tok)
    row = (t // GRID_W).astype(np.float32)
    col = (t % GRID_W).astype(np.float32)
    lane = np.arange(LANES)
    d = lane % NA_HEAD_DIM
    dd = d % half
    fi = dd % (half // 2)
    inv_freq = (ROPE_BASE ** (-(2.0 * fi.astype(np.float32)) / half)).astype(np.float32)
    pos = np.where((d < half)[None, :], row[:, None], col[:, None])
    ang = (pos * inv_freq[None, :]).astype(np.float32)
    first = (dd < half // 2)[None, :]
    cos = np.cos(ang.astype(np.float64)).astype(np.float32)
    sin = np.sin(ang.astype(np.float64)).astype(np.float32)
    zero = np.zeros_like(sin)
    return jnp.asarray(cos), jnp.asarray(np.where(first, -sin, zero)), jnp.asarray(np.where(first, zero, sin))


def _na_body(qrot_ref, qpl_ref, k_ref, v_ref, kc_ref, vc_ref, bias_ref, o_ref):
    m = pl.program_id(2)
    lane = lax.broadcasted_iota(I32, (1, LANES), 1)
    first_head = lane < NA_HEAD_DIM
    hm0, hm1 = _mask_bf16(first_head), _mask_bf16(lane >= NA_HEAD_DIM)
    nt = (((1,), (1,)), ((), ()))
    rows = range(NA_ROWS_PER_STEP)

    def stacked(ref):
        parts = []
        for i in rows:
            q = ref[0, i * GRID_W:(i + 1) * GRID_W, :]
            parts += [q * hm0, q * hm1]
        return jnp.concatenate(parts, axis=0)

    sc = lax.dot_general(stacked(qpl_ref), kc_ref[0], nt, preferred_element_type=F32)

    qs = stacked(qrot_ref)
    k0s, s_parts = [], []
    blk = 2 * GRID_W
    for i in rows:
        r = m * NA_ROWS_PER_STEP + i
        r0 = jnp.clip(r - NA_WIN_H // 2, 0, N_ROWS - NA_WIN_H)
        di = r0 - r + (NA_WIN_H - 1)
        k0s.append(pl.multiple_of(r0 * GRID_W, GRID_W))
        s_parts.append(lax.dot_general(qs[i * blk:(i + 1) * blk], k_ref[0, pl.ds(k0s[i], NA_KEYS), :], nt,
                                       preferred_element_type=F32) + bias_ref[0, di].reshape(blk, NA_KEYS))
    s = jnp.concatenate([jnp.concatenate(s_parts, axis=0), sc], axis=1)
    p = jnp.exp(s - jnp.max(s, axis=-1, keepdims=True))
    den = jnp.sum(p, axis=-1, keepdims=True)
    pb = p.astype(BF16)
    acc = jnp.concatenate([jnp.dot(pb[i * blk:(i + 1) * blk, :NA_KEYS], v_ref[0, pl.ds(k0s[i], NA_KEYS), :],
                                   preferred_element_type=F32) for i in rows], axis=0)
    o = (acc + jnp.dot(pb[:, NA_KEYS:], vc_ref[0], preferred_element_type=F32)) / den
    o_ref[0] = jnp.concatenate([jnp.where(first_head, o[i * blk:i * blk + GRID_W], o[i * blk + GRID_W:(i + 1) * blk])
                                for i in rows], axis=0).astype(BF16)


def _na_bias(rpb):
    c = np.arange(GRID_W)
    cq = np.arange(GRID_W)
    win_c0 = np.clip(cq - NA_WIN_W // 2, 0, GRID_W - NA_WIN_W)
    in_win = (c[None, :] >= win_c0[:, None]) & (c[None, :] < win_c0[:, None] + NA_WIN_W)
    dc = np.clip(c[None, :] - cq[:, None] + NA_WIN_W - 1, 0, 2 * NA_WIN_W - 2)
    pick = (dc[None, :, :] == np.arange(2 * NA_WIN_W - 1)[:, None, None]).astype(np.float32)
    t = jnp.einsum('hrj,jqc->hqrc', rpb.astype(F32), jnp.asarray(pick), precision=HIGHEST)
    t = jnp.where(jnp.asarray(in_win)[None, :, None, :], t, -jnp.inf)
    n_dr = 2 * NA_WIN_H - 1
    t = t.reshape(N_NA_HEADS // 2, 2, GRID_W, n_dr * GRID_W)
    return jnp.stack([t[..., di * GRID_W:di * GRID_W + NA_KEYS] for di in range(NA_WIN_H)], axis=1)


def _na_call(qrot, qpl, krot, v, kc, vc, bias):
    bsz = qrot.shape[0]
    tq = NA_ROWS_PER_STEP * GRID_W
    qspec = pl.BlockSpec((1, tq, LANES), lambda b, p, m: (b, m, p))
    kspec = pl.BlockSpec((1, SEQ, LANES), lambda b, p, m: (b, 0, p))
    cspec = pl.BlockSpec((1, CTX_LEN, LANES), lambda b, p, m: (b, 0, p))
    return pl.pallas_call(
        _na_body,
        out_shape=jax.ShapeDtypeStruct((bsz, SEQ, NA_WIDTH), BF16),
        grid=(bsz, N_NA_HEADS // 2, N_ROWS // NA_ROWS_PER_STEP),
        in_specs=[qspec, qspec, kspec, kspec, cspec, cspec,
                  pl.BlockSpec((1, NA_WIN_H, 2, GRID_W, NA_KEYS), lambda b, p, m: (p, 0, 0, 0, 0))],
        out_specs=qspec,
        compiler_params=_cparams(("arbitrary", "arbitrary", "arbitrary")),
        name="na",
    )(qrot, qpl, krot, v, kc, vc, bias)


def _split3(v):
    hi = v.astype(BF16)
    r1 = v - hi.astype(F32)
    mid = r1.astype(BF16)
    lo = (r1 - mid.astype(F32)).astype(BF16)
    return hi, mid, lo


HG_NCHUNK = HG_BLOCK // HG_CHUNK


def _hg_consts(tri_ref, keep_ref, spread_ref):
    nb = HG_BLOCK
    rr = lax.broadcasted_iota(I32, (nb, nb), 0)
    cc = lax.broadcasted_iota(I32, (nb, nb), 1)
    same = (rr // HG_CHUNK) == (cc // HG_CHUNK)
    fwd = jnp.where(same & (cc <= rr), 1.0, 0.0)
    bwd = jnp.where(same & (cc >= rr), 1.0, 0.0)
    tri_ref[0] = fwd.astype(BF16)
    tri_ref[1] = bwd.astype(BF16)
    keep_ref[...] = jnp.concatenate([fwd, bwd], axis=0)
    r8 = lax.broadcasted_iota(I32, (nb, HG_NCHUNK * HG_DK), 0) // HG_CHUNK
    g8 = lax.broadcasted_iota(I32, (nb, HG_NCHUNK * HG_DK), 1) // HG_DK
    spread_ref[...] = _mask_bf16(r8 == g8)


def _hg_pair(zf, zb, qf, qb, vf, vb, stf, stb, lb, tri_ref, keep_ref, spread_ref, *, want_out):
    nb = HG_BLOCK
    spread = spread_ref[...]
    v = (vf, vb)

    sig = jax.nn.sigmoid(jnp.concatenate([zf, zb], axis=0))
    lf = jnp.log(lb + (1.0 - lb) * sig)
    kk = (1.0 - lb) * (1.0 - sig)
    parts = jnp.concatenate(_split3(lf), axis=1)
    sums = jnp.concatenate([jnp.dot(tri_ref[d], parts[d * nb:(d + 1) * nb], preferred_element_type=F32)
                            for d in range(2)], axis=0)
    bcum = sums[:, :HG_DK] + sums[:, HG_DK:2 * HG_DK] + sums[:, 2 * HG_DK:]
    end_rows = [d * nb + c * HG_CHUNK + (HG_CHUNK - 1 if d == 0 else 0) for d in range(2) for c in range(HG_NCHUNK)]
    ends = [bcum[r:r + 1, :] for r in end_rows]
    btot = jnp.concatenate([jnp.broadcast_to(e, (HG_CHUNK, HG_DK)) for e in ends], axis=0)
    dec = jnp.exp(jnp.concatenate(ends, axis=0))
    ke = (kk * jnp.exp(btot - bcum)).astype(BF16)

    new_states, entering = [], []
    for d, st in enumerate((stf, stb)):
        kvt = lax.dot_general(v[d], jnp.concatenate([ke[d * nb:(d + 1) * nb]] * HG_NCHUNK, axis=1) * spread,
                              (((0,), (0,)), ((), ())), preferred_element_type=F32)
        ent = [None] * HG_NCHUNK
        for c in (range(HG_NCHUNK) if d == 0 else range(HG_NCHUNK - 1, -1, -1)):
            ent[c] = st
            i_dec = d * HG_NCHUNK + c
            st = st * dec[i_dec:i_dec + 1, :] + kvt[:, c * HG_DK:(c + 1) * HG_DK]
        new_states.append(st)
        entering.append(ent)
    if not want_out:
        return None, None, new_states[0], new_states[1]

    qd = (jnp.concatenate([qf, qb], axis=0).astype(F32) * jnp.exp(bcum)).astype(BF16)
    kd = (kk * jnp.exp(-bcum)).astype(BF16)
    nt = (((1,), (1,)), ((), ()))
    a = jnp.concatenate([lax.dot_general(qd[d * nb:(d + 1) * nb], kd[d * nb:(d + 1) * nb], nt,
                                         preferred_element_type=F32) for d in range(2)], axis=0)
    a = jnp.where(keep_ref[...] > 0.5, a, 0.0).astype(BF16)
    outs = []
    for d in range(2):
        qd_d = qd[d * nb:(d + 1) * nb]
        inter = jnp.concatenate(
            [lax.dot_general(qd_d[c * HG_CHUNK:(c + 1) * HG_CHUNK], entering[d][c].astype(BF16), nt,
                             preferred_element_type=F32) for c in range(HG_NCHUNK)], axis=0)
        outs.append(jnp.dot(a[d * nb:(d + 1) * nb], v[d], preferred_element_type=F32) + inter)
    return outs[0], outs[1], new_states[0], new_states[1]


def _hgrn_body(q_ref, zf_ref, zb_ref, i_ref, g_ref, qc_ref, zfc_ref, zbc_ref, ic_ref, lb_ref, ng_ref,
               o_ref, accf_ref, accb_ref, stf_ref, stb_ref, tri_ref, keep_ref, spread_ref):
    lbl = lb_ref[...]
    lbe = jnp.exp(lbl - jnp.max(lbl, axis=0, keepdims=True))
    lb = lbe[0:1, :] / jnp.sum(lbe, axis=0, keepdims=True)
    _hg_consts(tri_ref, keep_ref, spread_ref)
    pair = functools.partial(_hg_pair, lb=lb, tri_ref=tri_ref, keep_ref=keep_ref, spread_ref=spread_ref)
    zero_state = jnp.zeros((HG_DK, HG_DK), F32)
    _, _, stf, stb = pair(zfc_ref[0], zbc_ref[0], qc_ref[0], qc_ref[0], ic_ref[0], ic_ref[0],
                          zero_state, zero_state, want_out=False)
    stf_ref[...] = stf
    stb_ref[...] = stb
    n_blk = SEQ // HG_BLOCK

    def scan_step(n, carry):
        rows_f = pl.ds(pl.multiple_of(n * HG_BLOCK, HG_BLOCK), HG_BLOCK)
        rows_b = pl.ds(pl.multiple_of((n_blk - 1 - n) * HG_BLOCK, HG_BLOCK), HG_BLOCK)
        o_f, o_b, st_f, st_b = pair(zf_ref[0, rows_f, :], zb_ref[0, rows_b, :], q_ref[0, rows_f, :],
                                    q_ref[0, rows_b, :], i_ref[0, rows_f, :], i_ref[0, rows_b, :],
                                    stf_ref[...], stb_ref[...], want_out=True)
        stf_ref[...] = st_f
        stb_ref[...] = st_b
        accf_ref[rows_f, :] = o_f
        accb_ref[rows_b, :] = o_b
        return carry

    lax.fori_loop(0, n_blk, scan_step, 0, unroll=8)

    def readout_step(n, carry):
        rows = pl.ds(pl.multiple_of(n * HG_BLOCK, HG_BLOCK), HG_BLOCK)
        tot = accf_ref[rows, :] + accb_ref[rows, :]
        y = _rms(tot, ng_ref[...]) * jax.nn.sigmoid(g_ref[0, rows, :].astype(F32))
        o_ref[0, rows, :] = y.astype(BF16)
        return carry

    lax.fori_loop(0, n_blk, readout_step, 0, unroll=2)


def _hgrn_call(qh, zf, zb, ih, gh, qc, zfc, zbc, ic, lb_logits, norm_g):
    bsz = qh.shape[0]
    seq = pl.BlockSpec((1, SEQ, HG_DK), lambda b, h: (b, 0, h))
    ctx = pl.BlockSpec((1, CTX_LEN, HG_DK), lambda b, h: (b, 0, h))
    return pl.pallas_call(
        _hgrn_body,
        out_shape=jax.ShapeDtypeStruct((bsz, SEQ, N_HG_HEADS * HG_DK), BF16),
        grid=(bsz, N_HG_HEADS),
        in_specs=[seq, seq, seq, seq, seq, ctx, ctx, ctx, ctx,
                  pl.BlockSpec((lb_logits.shape[0], HG_DK), lambda b, h: (0, h)),
                  pl.BlockSpec((1, HG_DK), lambda b, h: (0, 0))],
        out_specs=seq,
        scratch_shapes=[pltpu.VMEM((SEQ, HG_DK), F32),
                        pltpu.VMEM((SEQ, HG_DK), F32),
                        pltpu.VMEM((HG_DK, HG_DK), F32),
                        pltpu.VMEM((HG_DK, HG_DK), F32),
                        pltpu.VMEM((2, HG_BLOCK, HG_BLOCK), BF16),
                        pltpu.VMEM((2 * HG_BLOCK, HG_BLOCK), F32),
                        pltpu.VMEM((HG_BLOCK, HG_NCHUNK * HG_DK), BF16)],
        compiler_params=_cparams(("arbitrary", "arbitrary")),
        name="hgrn",
    )(qh, zf, zb, ih, gh, qc, zfc, zbc, ic, lb_logits, norm_g)


def _outproj_body(na_ref, hg_ref, x_ref, w_ref, gt1_ref, sh2_ref, sc2_ref, gpost_ref, gpre_ref, wr_ref,
                  x1_ref, h2_ref, lt_ref):
    mix = (jnp.dot(na_ref[0], w_ref[0, :NA_WIDTH, :].astype(BF16), preferred_element_type=F32)
           + jnp.dot(hg_ref[0], w_ref[0, NA_WIDTH:, :].astype(BF16), preferred_element_type=F32))
    x1 = x_ref[0] + gt1_ref[0] * _rms(mix, gpost_ref[...])
    x1_ref[0] = x1
    h2 = _rms(x1, gpre_ref[...]) * (1.0 + sc2_ref[0]) + sh2_ref[0]
    h2_ref[0] = h2.astype(BF16)
    lt_ref[0] = lax.dot_general(wr_ref[...], h2, (((1,), (1,)), ((), ())), precision=HIGHEST,
                                preferred_element_type=F32)


def _outproj_call(na_o, hg_o, x, w_out, mod, g_post1, g_pre2, w_router_t):
    bsz, s, d = x.shape
    tm = OUT_TM
    half = pl.BlockSpec((1, tm, NA_WIDTH), lambda b, i: (b, i, 0))
    tok = pl.BlockSpec((1, tm, d), lambda b, i: (b, i, 0))
    gt1, sh2, sc2 = (_mod_spec(k, lambda b: b, 2) for k in (2, 3, 4))
    par = pl.BlockSpec((1, d), lambda b, i: (0, 0))
    return pl.pallas_call(
        _outproj_body,
        out_shape=[jax.ShapeDtypeStruct((bsz, s, d), F32),
                   jax.ShapeDtypeStruct((bsz, s, d), BF16),
                   jax.ShapeDtypeStruct((bsz, N_EXPERTS, s), F32)],
        grid=(bsz, s // tm),
        in_specs=[half, half, tok, pl.BlockSpec((1, d, d), lambda b, i: (0, 0, 0)), gt1, sh2, sc2, par, par,
                  pl.BlockSpec((N_EXPERTS, d), lambda b, i: (0, 0))],
        out_specs=[tok, tok, pl.BlockSpec((1, N_EXPERTS, tm), lambda b, i: (b, 0, i))],
        compiler_params=_cparams(("arbitrary", "arbitrary")),
        name="outproj",
    )(na_o, hg_o, x, w_out, mod, mod, mod, g_post1, g_pre2, w_router_t)


def _route_body(lt_ref, posm_ref, gate_ref, tab_ref, *, cap):
    l = lt_ref[0]
    mx = jnp.max(l, axis=0, keepdims=True)
    ex = jnp.exp(l - mx)
    aff = ex / jnp.sum(ex, axis=0, keepdims=True)
    capf = jnp.float32(cap)

    def count(mask):
        return jnp.sum(jnp.where(mask, 1.0, 0.0), axis=1, keepdims=True)

    def enough(v):
        return count(aff >= v) >= capf

    def bit_step(it, thr):
        cand = thr | (jnp.int32(1) << (30 - it))
        return jnp.where(enough(pltpu.bitcast(cand, F32)), cand, thr)

    thr = lax.fori_loop(0, 31, bit_step, jnp.zeros((N_EXPERTS, 1), I32))
    lo = pltpu.bitcast(thr, F32)
    hi = pltpu.bitcast(thr + 1, F32)

    def mid_step(it, lohi):
        lo, hi = lohi
        mid = 0.5 * (lo + hi)
        ok = enough(mid)
        return jnp.where(ok, mid, lo), jnp.where(ok, hi, mid)

    lo, hi = lax.fori_loop(0, 30, mid_step, (lo, hi))
    gt = aff >= hi
    eq = (aff >= lo) & jnp.logical_not(gt)
    need = capf - count(gt)

    rr = lax.broadcasted_iota(I32, (ROUTE_BLK, ROUTE_BLK), 0)
    cc = lax.broadcasted_iota(I32, (ROUTE_BLK, ROUTE_BLK), 1)
    upper = _mask_bf16(rr <= cc)
    lane = lax.broadcasted_iota(I32, (N_EXPERTS, LANES), 1)

    off_eq = jnp.zeros((N_EXPERTS, 1), F32)
    off_sel = jnp.zeros((N_EXPERTS, 1), F32)
    tab = jnp.zeros((N_EXPERTS, LANES), F32)
    for j in range(N_ROUTE_BLK):
        sl = slice(j * ROUTE_BLK, (j + 1) * ROUTE_BLK)
        eq_j = eq[:, sl]
        eq_b = _mask_bf16(eq_j)
        incl_eq = jnp.dot(eq_b, upper, preferred_element_type=F32) + off_eq
        rank_eq = incl_eq - eq_b.astype(F32)
        sel_j = gt[:, sl] | (eq_j & (rank_eq < need))
        sel_b = _mask_bf16(sel_j)
        incl_sel = jnp.dot(sel_b, upper, preferred_element_type=F32) + off_sel
        pos = incl_sel - sel_b.astype(F32)
        posm_ref[0, :, sl] = jnp.where(sel_j, pos.astype(I32), -1)
        gate_ref[0, :, sl] = jnp.where(sel_j, aff[:, sl], 0.0)
        tab = jnp.where(lane == j, off_sel, tab)
        off_eq = incl_eq[:, ROUTE_BLK - 1:ROUTE_BLK]
        off_sel = incl_sel[:, ROUTE_BLK - 1:ROUTE_BLK]
    tab = jnp.where(lane == N_ROUTE_BLK, off_sel, tab)
    tab_ref[0] = tab.astype(I32)


def _route_call(lt, cap):
    bsz = lt.shape[0]
    big = pl.BlockSpec((1, N_EXPERTS, SEQ), lambda b: (b, 0, 0))
    return pl.pallas_call(
        functools.partial(_route_body, cap=cap),
        out_shape=[jax.ShapeDtypeStruct((bsz, N_EXPERTS, SEQ), I32),
                   jax.ShapeDtypeStruct((bsz, N_EXPERTS, SEQ), F32),
                   jax.ShapeDtypeStruct((bsz, N_EXPERTS, LANES), I32)],
        grid=(bsz,),
        in_specs=[big],
        out_specs=[big, big, pl.BlockSpec((1, N_EXPERTS, LANES), lambda b: (b, 0, 0))],
        compiler_params=_cparams(("arbitrary",)),
        name="route",
    )(lt)


def _chunk_range(start, end):
    c_lo = start // SLOT_CHUNK
    n = jnp.where(end > start, (end - 1) // SLOT_CHUNK - c_lo + 1, 0)
    return c_lo, n


DISP_NE = 8


def _dispatch_body(tab_ref, posm_ref, gate_ref, h2_ref, x_ref, gsl_ref, *, cap):
    b = pl.program_id(0)
    hf = pl.program_id(1)
    blk = pl.program_id(2)
    tiles_per_step = MOE_TD // ROUTE_BLK
    chunks_per_batch = cap // SLOT_CHUNK

    @pl.when(blk == 0)
    def _():
        x_ref[...] = jnp.zeros_like(x_ref)
        gsl_ref[...] = jnp.zeros_like(gsl_ref)

    slot = lax.broadcasted_iota(I32, (SLOT_CHUNK, ROUTE_BLK), 0)

    def chunk_ranges(k):
        tab_base = (b * N_EXPERTS + hf * DISP_NE + k) * TAB_STRIDE + blk * tiles_per_step
        return [_chunk_range(tab_ref[tab_base + sub], tab_ref[tab_base + sub + 1]) for sub in range(tiles_per_step)]

    def add_chunk(k, sub, chunk):
        cols = slice(sub * ROUTE_BLK, (sub + 1) * ROUTE_BLK)
        base = pl.multiple_of(chunk * SLOT_CHUNK, SLOT_CHUNK)
        onehot = _mask_bf16(posm_ref[0, pl.ds(k, 1), cols] == slot + base)
        got = jnp.dot(onehot, h2_ref[cols, :], preferred_element_type=F32).astype(BF16)
        x_ref[k, pl.ds(base, SLOT_CHUNK), :] = x_ref[k, pl.ds(base, SLOT_CHUNK), :] + got
        g3 = [p.astype(F32) for p in _split3(gate_ref[0, pl.ds(k, 1), cols])]
        g8 = jnp.concatenate(g3 + [jnp.zeros((SUBLANES - len(g3), ROUTE_BLK), F32)], axis=0).astype(BF16)
        gsl_ref[k, chunk] = gsl_ref[k, chunk] + lax.dot_general(g8, onehot, (((1,), (1,)), ((), ())),
                                                                preferred_element_type=F32)

    for k in range(DISP_NE):
        for sub, (c_lo, _) in enumerate(chunk_ranges(k)):
            add_chunk(k, sub, jnp.minimum(c_lo, chunks_per_batch - 1))

    def expert_step(k, carry):
        for sub, (c_lo, n) in enumerate(chunk_ranges(k)):
            def more(kk, c, sub=sub, c_lo=c_lo):
                add_chunk(k, sub, c_lo + kk)
                return c

            lax.fori_loop(1, n, more, 0)
        return carry

    lax.fori_loop(0, DISP_NE, expert_step, 0)


def _dispatch_call(tab_flat, posm, gate, h2_flat, *, bsz, cap):
    d = D_MODEL
    n_blk = SEQ // MOE_TD
    chunks = cap // SLOT_CHUNK
    sel = pl.BlockSpec((1, DISP_NE, MOE_TD), lambda b, hf, blk, tab: (b, hf, blk))
    grid_spec = pltpu.PrefetchScalarGridSpec(
        num_scalar_prefetch=1,
        grid=(bsz, N_EXPERTS // DISP_NE, n_blk),
        in_specs=[sel, sel, pl.BlockSpec((MOE_TD, d), lambda b, hf, blk, tab: (b * n_blk + blk, 0))],
        out_specs=[pl.BlockSpec((DISP_NE, cap, d), lambda b, hf, blk, tab: (hf, b, 0)),
                   pl.BlockSpec((DISP_NE, chunks, SUBLANES, SLOT_CHUNK), lambda b, hf, blk, tab: (hf, b, 0, 0))],
    )
    return pl.pallas_call(
        functools.partial(_dispatch_body, cap=cap),
        out_shape=[jax.ShapeDtypeStruct((N_EXPERTS, bsz * cap, d), BF16),
                   jax.ShapeDtypeStruct((N_EXPERTS, bsz * chunks, SUBLANES, SLOT_CHUNK), F32)],
        grid_spec=grid_spec,
        compiler_params=_cparams(("arbitrary", "arbitrary", "arbitrary")),
        name="dispatch",
    )(tab_flat, posm, gate, h2_flat)


def _moe_body(x_ref, gsl_ref, wg_ref, wu_ref, wd_ref, y_ref, acc_scr):
    s = pl.program_id(1)

    chunks_per_tile = MOE_TM // SLOT_CHUNK

    def mlp(width, first, last):
        nt = (((1,), (1,)), ((), ()))

        def m_step(mi, carry):
            r0 = pl.multiple_of(mi * MOE_TM, MOE_TM)
            rows = pl.ds(r0, MOE_TM)
            xm = x_ref[0, rows, :]
            g = lax.dot_general(xm, wg_ref[0, :width, :].astype(BF16), nt, preferred_element_type=F32)
            u = lax.dot_general(xm, wu_ref[0, :width, :].astype(BF16), nt, preferred_element_type=F32)
            hid = (g * jax.nn.sigmoid(g) * u).astype(BF16)
            out = jnp.dot(hid, wd_ref[0, :width, :].astype(BF16), preferred_element_type=F32)
            if not first:
                out = acc_scr[rows, :] + out
            if last:
                gsl = gsl_ref[0, pl.ds(mi * chunks_per_tile, chunks_per_tile)]
                gcol = jnp.concatenate([jnp.sum(gsl[ci].T, axis=1, keepdims=True) for ci in range(chunks_per_tile)],
                                       axis=0)
                y_ref[0, rows, :] = (out * gcol).astype(BF16)
            else:
                acc_scr[rows, :] = out
            return carry

        lax.fori_loop(0, acc_scr.shape[0] // MOE_TM, m_step, 0)

    @pl.when(s == 0)
    def _():
        mlp(MOE_TF, True, False)

    @pl.when((s > 0) & (s < MOE_NF - 1))
    def _():
        mlp(MOE_TF, False, False)

    @pl.when(s == MOE_NF - 1)
    def _():
        mlp(MOE_F_LAST, False, True)


def _moe_call(xs, gsl, w_gate_t, w_up_t, w_down):
    n_e, slots, d = xs.shape
    per_e = pl.BlockSpec((1, slots, d), lambda e, s: (e, 0, 0))
    wspec = pl.BlockSpec((1, MOE_TF, d), lambda e, s: (e, s, 0))
    return pl.pallas_call(
        _moe_body,
        out_shape=jax.ShapeDtypeStruct((n_e, slots, d), BF16),
        grid=(n_e, MOE_NF),
        in_specs=[per_e, pl.BlockSpec((1,) + gsl.shape[1:], lambda e, s: (e, 0, 0, 0)), wspec, wspec, wspec],
        out_specs=per_e,
        scratch_shapes=[pltpu.VMEM((slots, d), F32)],
        compiler_params=_cparams(("arbitrary", "arbitrary")),
        name="moe",
    )(xs, gsl, w_gate_t, w_up_t, w_down)


def _combine_body(tab_ref, y_ref, posm_ref, x1_ref, gt2_ref, gpost_ref, o_ref, acc_ref, *, cap):
    n_e = N_EXPERTS
    b = pl.program_id(0)
    j = pl.program_id(1)
    pm = posm_ref[0].T
    win = 2 * SLOT_CHUNK
    slot_w = lax.broadcasted_iota(I32, (ROUTE_BLK, win), 1)
    slot_c = lax.broadcasted_iota(I32, (ROUTE_BLK, SLOT_CHUNK), 1)
    total = jnp.zeros((ROUTE_BLK, D_MODEL), F32)
    bases, extras = [], []
    for k in range(n_e):
        tab_base = (b * N_EXPERTS + k) * TAB_STRIDE
        start = tab_ref[tab_base + j]
        end = tab_ref[tab_base + j + 1]
        base = pl.multiple_of(jnp.minimum((start // SLOT_CHUNK) * SLOT_CHUNK, cap - win), SLOT_CHUNK)
        total = total + jnp.dot(_mask_bf16(pm[:, k:k + 1] == slot_w + base), y_ref[k, pl.ds(base, win), :],
                                preferred_element_type=F32)
        bases.append(base)
        extras.append(jnp.maximum(end - (base + win) + SLOT_CHUNK - 1, 0) // SLOT_CHUNK)
    acc_ref[...] = total

    @pl.when(sum(extras) > 0)
    def _():
        for k in range(n_e):
            def extra_step(kk, carry, k=k):
                b2 = pl.multiple_of(bases[k] + win + kk * SLOT_CHUNK, SLOT_CHUNK)
                acc_ref[...] += jnp.dot(_mask_bf16(pm[:, k:k + 1] == slot_c + b2),
                                        y_ref[k, pl.ds(b2, SLOT_CHUNK), :], preferred_element_type=F32)
                return carry

            lax.fori_loop(0, extras[k], extra_step, 0)

    o_ref[0] = x1_ref[0] + gt2_ref[0] * _rms(acc_ref[...], gpost_ref[...])


def _combine_call(tab_flat, y, posm, x1, mod, g_post2, *, cap):
    bsz, n_e, s = posm.shape
    d = D_MODEL
    tok = pl.BlockSpec((1, ROUTE_BLK, d), lambda b, j, tab: (b, j, 0))
    grid_spec = pltpu.PrefetchScalarGridSpec(
        num_scalar_prefetch=1,
        grid=(bsz, s // ROUTE_BLK),
        in_specs=[
            pl.BlockSpec((n_e, cap, d), lambda b, j, tab: (0, b, 0), pipeline_mode=pl.Buffered(1)),
            pl.BlockSpec((1, n_e, ROUTE_BLK), lambda b, j, tab: (b, 0, j)),
            tok,
            _mod_spec(5, lambda b: b, 3),
            pl.BlockSpec((1, d), lambda b, j, tab: (0, 0))],
        out_specs=tok,
        scratch_shapes=[pltpu.VMEM((ROUTE_BLK, d), F32)],
    )
    return pl.pallas_call(
        functools.partial(_combine_body, cap=cap),
        out_shape=jax.ShapeDtypeStruct((bsz, s, d), F32),
        grid_spec=grid_spec,
        compiler_params=_cparams(("arbitrary", "arbitrary")),
        name="combine",
    )(tab_flat, y, posm, x1, mod, g_post2)


def kernel(x, c, ctx, c_ctx, w_mod, b_mod, g_pre1, g_post1, g_pre2, g_post2, w_in, w_out, na_rpb,
           hg_lb_logits, hg_norm, w_router, w_gate, w_up, w_down):
    bsz, seq, d = x.shape
    assert (seq, d) == (SEQ, D_MODEL) and ctx.shape[1] == CTX_LEN and w_mod.shape[0] == 1
    cap = 2 * seq // N_EXPERTS
    assert cap % SLOT_CHUNK == 0

    assert bsz < MOD_ROWS
    mod = _mod_call(c, c_ctx, w_mod, b_mod)

    cos, sa, sb = _rope_tables(seq)

    qrot, qpl, krot, v, qh, zf, zb, ih, gh = _inproj_call(
        x, mod, lambda b: b, g_pre1, w_in, cos, sa, sb, rope=True, tm=IN_TM)
    ctx_flat = ctx.reshape(1, bsz * CTX_LEN, d)
    ctx_out = _inproj_call(ctx_flat, mod, lambda b: bsz, g_pre1, w_in,
                           cos[:bsz * CTX_LEN], sa[:bsz * CTX_LEN], sb[:bsz * CTX_LEN], rope=False, tm=CTX_LEN)
    _, _, kc, vc, qc, zfc, zbc, ic, _ = [a.reshape(bsz, CTX_LEN, COL_GROUP) for a in ctx_out]

    na_o = _na_call(qrot, qpl, krot, v, kc, vc, _na_bias(na_rpb[0]))
    hg_o = _hgrn_call(qh, zf, zb, ih, gh, qc, zfc, zbc, ic, hg_lb_logits, hg_norm)

    x1, h2, lt = _outproj_call(na_o, hg_o, x, w_out, mod, g_post1, g_pre2, w_router[0].T)

    posm, gate, tab = _route_call(lt, cap)
    tab_flat = tab.reshape(-1)
    xs, gsl = _dispatch_call(tab_flat, posm, gate, h2.reshape(bsz * seq, d), bsz=bsz, cap=cap)
    y = _moe_call(xs, gsl, jnp.swapaxes(w_gate[0], 1, 2), jnp.swapaxes(w_up[0], 1, 2), w_down[0])

    return _combine_call(tab_flat, y, posm, x1, mod, g_post2, cap=cap)
```

```python
import functools

import jax
import jax.numpy as jnp
import numpy as np
from jax import lax
from jax.experimental import pallas as pl
from jax.experimental.pallas import tpu as pltpu

F32 = jnp.float32
BF16 = jnp.bfloat16
I32 = jnp.int32
HIGHEST = lax.Precision.HIGHEST

D_MODEL = 1024
GRID_W = 64
N_ROWS = 128
SEQ = 8192
CTX_LEN = 256
NA_HEAD_DIM = 64
N_NA_HEADS = 8
NA_WIDTH = 512
NA_WIN_H = 8
NA_WIN_W = 16
ROPE_BASE = 10000.0
HG_DK = 128
N_HG_HEADS = 4
HG_CHUNK = 32
N_EXPERTS = 16
D_EXPERT = 2752
RMS_EPS = 1e-6
COL_GROUP = 512

LANES = 128
SUBLANES = 8
VMEM_LIMIT_BYTES = 58 * 1024 * 1024

IN_TM = 1024
NA_ROWS_PER_STEP = 64
NA_KEYS = NA_WIN_H * GRID_W
HG_BLOCK = 256
OUT_TM = 1024
ROUTE_BLK = 256
N_ROUTE_BLK = SEQ // ROUTE_BLK
SLOT_CHUNK = 128
TAB_STRIDE = LANES
MOE_TD = 2048
MOE_TF = 768
MOE_NF = -(-D_EXPERT // MOE_TF)
MOE_F_LAST = D_EXPERT - (MOE_NF - 1) * MOE_TF
MOE_TM = 1024


def _cparams(sem):
    return pltpu.CompilerParams(dimension_semantics=sem, vmem_limit_bytes=VMEM_LIMIT_BYTES)


def _mask_bf16(mask):
    return jnp.where(mask, 1.0, 0.0).astype(BF16)


def _rms(v, g):
    return v * lax.rsqrt(jnp.mean(v * v, axis=-1, keepdims=True) + RMS_EPS) * g


MOD_ROWS = 8
N_MOD = 6


def _mod_body(c_ref, cc_ref, w_ref, b_ref, o_ref, rows_scr):
    bsz = c_ref.shape[0]
    rows_scr[...] = jnp.zeros_like(rows_scr)
    rows_scr[0:bsz, :] = c_ref[...]
    rows_scr[bsz:bsz + 1, :] = cc_ref[...]
    cv = rows_scr[...]
    s = cv * jax.nn.sigmoid(cv)
    o_ref[...] = jnp.dot(s, w_ref[0], precision=HIGHEST, preferred_element_type=F32) + b_ref[...]


def _mod_call(c, c_ctx, w_mod, b_mod):
    bsz, d = c.shape
    n = w_mod.shape[2]
    tn = d
    mod = pl.pallas_call(
        _mod_body,
        out_shape=jax.ShapeDtypeStruct((MOD_ROWS, n), F32),
        grid=(n // tn,),
        in_specs=[pl.BlockSpec((bsz, d), lambda j: (0, 0)),
                  pl.BlockSpec((1, d), lambda j: (0, 0)),
                  pl.BlockSpec((1, d, tn), lambda j: (0, 0, j)),
                  pl.BlockSpec((1, tn), lambda j: (0, j))],
        out_specs=pl.BlockSpec((MOD_ROWS, tn), lambda j: (0, j)),
        scratch_shapes=[pltpu.VMEM((MOD_ROWS, d), F32)],
        compiler_params=_cparams(("arbitrary",)),
        name="mod",
    )(c, c_ctx[None, :], w_mod, b_mod)
    return mod.reshape(MOD_ROWS * N_MOD, 1, d)


def _mod_spec(k, row_of, n_grid):
    if n_grid == 2:
        return pl.BlockSpec((1, 1, D_MODEL), lambda b, i: (row_of(b) * N_MOD + k, 0, 0))
    return pl.BlockSpec((1, 1, D_MODEL), lambda b, i, tab: (row_of(b) * N_MOD + k, 0, 0))


def _rope128(pg, cos, sa, sb):
    return pg * cos + pltpu.roll(pg, LANES - 16, axis=1) * sa + pltpu.roll(pg, 16, axis=1) * sb


def _inproj_body(x_ref, sh_ref, sc_ref, g_ref, w_ref, cos_ref, sa_ref, sb_ref,
                 qrot_ref, qpl_ref, krot_ref, v_ref, qh_ref, zf_ref, zb_ref, ih_ref, gh_ref, *, rope):
    xv = x_ref[0]
    h = _rms(xv, g_ref[...]) * (1.0 + sc_ref[0]) + sh_ref[0]
    hb = h.astype(BF16)

    def group(j):
        return jnp.dot(hb, w_ref[0, :, j * COL_GROUP:(j + 1) * COL_GROUP].astype(BF16), preferred_element_type=F32)

    def rotated(p):
        if not rope:
            return p
        cos, sa, sb = cos_ref[...], sa_ref[...], sb_ref[...]
        return jnp.concatenate(
            [_rope128(p[:, k * LANES:(k + 1) * LANES], cos, sa, sb) for k in range(COL_GROUP // LANES)], axis=1)

    scale = NA_HEAD_DIM ** -0.5
    p = group(0)
    qpl_ref[0] = (p * scale).astype(BF16)
    qrot_ref[0] = (rotated(p) * scale).astype(BF16)
    krot_ref[0] = rotated(group(1)).astype(BF16)
    v_ref[0] = group(2).astype(BF16)
    qh_ref[0] = group(3).astype(BF16)
    zf_ref[0] = group(4)
    zb_ref[0] = group(5)
    ih_ref[0] = group(6).astype(BF16)
    gh_ref[0] = group(7).astype(BF16)


def _inproj_call(x3, mod, row_of, g_pre, w_in, cos, sa, sb, *, rope, tm):
    g, t, d = x3.shape
    n_out = 9
    dts = [BF16, BF16, BF16, BF16, BF16, F32, F32, BF16, BF16]
    tok = pl.BlockSpec((1, tm, d), lambda b, i: (b, i, 0))
    tab = pl.BlockSpec((tm, LANES), lambda b, i: (i, 0))
    outs = [pl.BlockSpec((1, tm, COL_GROUP), lambda b, i: (b, i, 0)) for _ in range(n_out)]
    return pl.pallas_call(
        functools.partial(_inproj_body, rope=rope),
        out_shape=[jax.ShapeDtypeStruct((g, t, COL_GROUP), dt) for dt in dts],
        grid=(g, t // tm),
        in_specs=[tok, _mod_spec(0, row_of, 2), _mod_spec(1, row_of, 2),
                  pl.BlockSpec((1, d), lambda b, i: (0, 0)),
                  pl.BlockSpec(w_in.shape, lambda b, i: (0, 0, 0), pipeline_mode=pl.Buffered(1)),
                  tab, tab, tab],
        out_specs=outs,
        compiler_params=_cparams(("arbitrary", "arbitrary")),
        name="inproj_rope" if rope else "inproj_ctx",
    )(x3, mod, mod, g_pre, w_in, cos, sa, sb)


def _rope_tables(n_tok):
    half = NA_HEAD_DIM // 2
    t = np.arange(n_tok)
    row = (t // GRID_W).astype(np.float32)
    col = (t % GRID_W).astype(np.float32)
    lane = np.arange(LANES)
    d = lane % NA_HEAD_DIM
    dd = d % half
    fi = dd % (half // 2)
    inv_freq = (ROPE_BASE ** (-(2.0 * fi.astype(np.float32)) / half)).astype(np.float32)
    pos = np.where((d < half)[None, :], row[:, None], col[:, None])
    ang = (pos * inv_freq[None, :]).astype(np.float32)
    first = (dd < half // 2)[None, :]
    cos = np.cos(ang.astype(np.float64)).astype(np.float32)
    sin = np.sin(ang.astype(np.float64)).astype(np.float32)
    zero = np.zeros_like(sin)
    return jnp.asarray(cos), jnp.asarray(np.where(first, -sin, zero)), jnp.asarray(np.where(first, zero, sin))


def _na_body(qrot_ref, qpl_ref, k_ref, v_ref, kc_ref, vc_ref, bias_ref, o_ref):
    m = pl.program_id(2)
    lane = lax.broadcasted_iota(I32, (1, LANES), 1)
    first_head = lane < NA_HEAD_DIM
    hm0, hm1 = _mask_bf16(first_head), _mask_bf16(lane >= NA_HEAD_DIM)
    nt = (((1,), (1,)), ((), ()))
    rows = range(NA_ROWS_PER_STEP)

    def stacked(ref):
        parts = []
        for i in rows:
            q = ref[0, i * GRID_W:(i + 1) * GRID_W, :]
            parts += [q * hm0, q * hm1]
        return jnp.concatenate(parts, axis=0)

    sc = lax.dot_general(stacked(qpl_ref), kc_ref[0], nt, preferred_element_type=F32)

    qs = stacked(qrot_ref)
    k0s, s_parts = [], []
    blk = 2 * GRID_W
    for i in rows:
        r = m * NA_ROWS_PER_STEP + i
        r0 = jnp.clip(r - NA_WIN_H // 2, 0, N_ROWS - NA_WIN_H)
        di = r0 - r + (NA_WIN_H - 1)
        k0s.append(pl.multiple_of(r0 * GRID_W, GRID_W))
        s_parts.append(lax.dot_general(qs[i * blk:(i + 1) * blk], k_ref[0, pl.ds(k0s[i], NA_KEYS), :], nt,
                                       preferred_element_type=F32) + bias_ref[0, di].reshape(blk, NA_KEYS))
    s = jnp.concatenate([jnp.concatenate(s_parts, axis=0), sc], axis=1)
    p = jnp.exp(s - jnp.max(s, axis=-1, keepdims=True))
    den = jnp.sum(p, axis=-1, keepdims=True)
    pb = p.astype(BF16)
    acc = jnp.concatenate([jnp.dot(pb[i * blk:(i + 1) * blk, :NA_KEYS], v_ref[0, pl.ds(k0s[i], NA_KEYS), :],
                                   preferred_element_type=F32) for i in rows], axis=0)
    o = (acc + jnp.dot(pb[:, NA_KEYS:], vc_ref[0], preferred_element_type=F32)) / den
    o_ref[0] = jnp.concatenate([jnp.where(first_head, o[i * blk:i * blk + GRID_W], o[i * blk + GRID_W:(i + 1) * blk])
                                for i in rows], axis=0).astype(BF16)


def _na_bias(rpb):
    c = np.arange(GRID_W)
    cq = np.arange(GRID_W)
    win_c0 = np.clip(cq - NA_WIN_W // 2, 0, GRID_W - NA_WIN_W)
    in_win = (c[None, :] >= win_c0[:, None]) & (c[None, :] < win_c0[:, None] + NA_WIN_W)
    dc = np.clip(c[None, :] - cq[:, None] + NA_WIN_W - 1, 0, 2 * NA_WIN_W - 2)
    pick = (dc[None, :, :] == np.arange(2 * NA_WIN_W - 1)[:, None, None]).astype(np.float32)
    t = jnp.einsum('hrj,jqc->hqrc', rpb.astype(F32), jnp.asarray(pick), precision=HIGHEST)
    t = jnp.where(jnp.asarray(in_win)[None, :, None, :], t, -jnp.inf)
    n_dr = 2 * NA_WIN_H - 1
    t = t.reshape(N_NA_HEADS // 2, 2, GRID_W, n_dr * GRID_W)
    return jnp.stack([t[..., di * GRID_W:di * GRID_W + NA_KEYS] for di in range(NA_WIN_H)], axis=1)


def _na_call(qrot, qpl, krot, v, kc, vc, bias):
    bsz = qrot.shape[0]
    tq = NA_ROWS_PER_STEP * GRID_W
    qspec = pl.BlockSpec((1, tq, LANES), lambda b, p, m: (b, m, p))
    kspec = pl.BlockSpec((1, SEQ, LANES), lambda b, p, m: (b, 0, p))
    cspec = pl.BlockSpec((1, CTX_LEN, LANES), lambda b, p, m: (b, 0, p))
    return pl.pallas_call(
        _na_body,
        out_shape=jax.ShapeDtypeStruct((bsz, SEQ, NA_WIDTH), BF16),
        grid=(bsz, N_NA_HEADS // 2, N_ROWS // NA_ROWS_PER_STEP),
        in_specs=[qspec, qspec, kspec, kspec, cspec, cspec,
                  pl.BlockSpec((1, NA_WIN_H, 2, GRID_W, NA_KEYS), lambda b, p, m: (p, 0, 0, 0, 0))],
        out_specs=qspec,
        compiler_params=_cparams(("arbitrary", "arbitrary", "arbitrary")),
        name="na",
    )(qrot, qpl, krot, v, kc, vc, bias)


def _split3(v):
    hi = v.astype(BF16)
    r1 = v - hi.astype(F32)
    mid = r1.astype(BF16)
    lo = (r1 - mid.astype(F32)).astype(BF16)
    return hi, mid, lo


HG_NCHUNK = HG_BLOCK // HG_CHUNK


def _hg_consts(tri_ref, keep_ref, spread_ref):
    nb = HG_BLOCK
    rr = lax.broadcasted_iota(I32, (nb, nb), 0)
    cc = lax.broadcasted_iota(I32, (nb, nb), 1)
    same = (rr // HG_CHUNK) == (cc // HG_CHUNK)
    fwd = jnp.where(same & (cc <= rr), 1.0, 0.0)
    bwd = jnp.where(same & (cc >= rr), 1.0, 0.0)
    tri_ref[0] = fwd.astype(BF16)
    tri_ref[1] = bwd.astype(BF16)
    keep_ref[...] = jnp.concatenate([fwd, bwd], axis=0)
    r8 = lax.broadcasted_iota(I32, (nb, HG_NCHUNK * HG_DK), 0) // HG_CHUNK
    g8 = lax.broadcasted_iota(I32, (nb, HG_NCHUNK * HG_DK), 1) // HG_DK
    spread_ref[...] = _mask_bf16(r8 == g8)


def _hg_pair(zf, zb, qf, qb, vf, vb, stf, stb, lb, tri_ref, keep_ref, spread_ref, *, want_out):
    nb = HG_BLOCK
    spread = spread_ref[...]
    v = (vf, vb)

    sig = jax.nn.sigmoid(jnp.concatenate([zf, zb], axis=0))
    lf = jnp.log(lb + (1.0 - lb) * sig)
    kk = (1.0 - lb) * (1.0 - sig)
    parts = jnp.concatenate(_split3(lf), axis=1)
    sums = jnp.concatenate([jnp.dot(tri_ref[d], parts[d * nb:(d + 1) * nb], preferred_element_type=F32)
                            for d in range(2)], axis=0)
    bcum = sums[:, :HG_DK] + sums[:, HG_DK:2 * HG_DK] + sums[:, 2 * HG_DK:]
    end_rows = [d * nb + c * HG_CHUNK + (HG_CHUNK - 1 if d == 0 else 0) for d in range(2) for c in range(HG_NCHUNK)]
    ends = [bcum[r:r + 1, :] for r in end_rows]
    btot = jnp.concatenate([jnp.broadcast_to(e, (HG_CHUNK, HG_DK)) for e in ends], axis=0)
    dec = jnp.exp(jnp.concatenate(ends, axis=0))
    ke = (kk * jnp.exp(btot - bcum)).astype(BF16)

    new_states, entering = [], []
    for d, st in enumerate((stf, stb)):
        kvt = lax.dot_general(v[d], jnp.concatenate([ke[d * nb:(d + 1) * nb]] * HG_NCHUNK, axis=1) * spread,
                              (((0,), (0,)), ((), ())), preferred_element_type=F32)
        ent = [None] * HG_NCHUNK
        for c in (range(HG_NCHUNK) if d == 0 else range(HG_NCHUNK - 1, -1, -1)):
            ent[c] = st
            i_dec = d * HG_NCHUNK + c
            st = st * dec[i_dec:i_dec + 1, :] + kvt[:, c * HG_DK:(c + 1) * HG_DK]
        new_states.append(st)
        entering.append(ent)
    if not want_out:
        return None, None, new_states[0], new_states[1]

    qd = (jnp.concatenate([qf, qb], axis=0).astype(F32) * jnp.exp(bcum)).astype(BF16)
    kd = (kk * jnp.exp(-bcum)).astype(BF16)
    nt = (((1,), (1,)), ((), ()))
    a = jnp.concatenate([lax.dot_general(qd[d * nb:(d + 1) * nb], kd[d * nb:(d + 1) * nb], nt,
                                         preferred_element_type=F32) for d in range(2)], axis=0)
    a = jnp.where(keep_ref[...] > 0.5, a, 0.0).astype(BF16)
    outs = []
    for d in range(2):
        qd_d = qd[d * nb:(d + 1) * nb]
        inter = jnp.concatenate(
            [lax.dot_general(qd_d[c * HG_CHUNK:(c + 1) * HG_CHUNK], entering[d][c].astype(BF16), nt,
                             preferred_element_type=F32) for c in range(HG_NCHUNK)], axis=0)
        outs.append(jnp.dot(a[d * nb:(d + 1) * nb], v[d], preferred_element_type=F32) + inter)
    return outs[0], outs[1], new_states[0], new_states[1]


def _hgrn_body(q_ref, zf_ref, zb_ref, i_ref, g_ref, qc_ref, zfc_ref, zbc_ref, ic_ref, lb_ref, ng_ref,
               o_ref, accf_ref, accb_ref, stf_ref, stb_ref, tri_ref, keep_ref, spread_ref):
    lbl = lb_ref[...]
    lbe = jnp.exp(lbl - jnp.max(lbl, axis=0, keepdims=True))
    lb = lbe[0:1, :] / jnp.sum(lbe, axis=0, keepdims=True)
    _hg_consts(tri_ref, keep_ref, spread_ref)
    pair = functools.partial(_hg_pair, lb=lb, tri_ref=tri_ref, keep_ref=keep_ref, spread_ref=spread_ref)
    zero_state = jnp.zeros((HG_DK, HG_DK), F32)
    _, _, stf, stb = pair(zfc_ref[0], zbc_ref[0], qc_ref[0], qc_ref[0], ic_ref[0], ic_ref[0],
                          zero_state, zero_state, want_out=False)
    stf_ref[...] = stf
    stb_ref[...] = stb
    n_blk = SEQ // HG_BLOCK

    def readout(rows, tot):
        y = _rms(tot, ng_ref[...]) * jax.nn.sigmoid(g_ref[0, rows, :].astype(F32))
        o_ref[0, rows, :] = y.astype(BF16)

    def scan_step(n, carry, *, second_half):
        rows_f = pl.ds(pl.multiple_of(n * HG_BLOCK, HG_BLOCK), HG_BLOCK)
        rows_b = pl.ds(pl.multiple_of((n_blk - 1 - n) * HG_BLOCK, HG_BLOCK), HG_BLOCK)
        o_f, o_b, st_f, st_b = pair(zf_ref[0, rows_f, :], zb_ref[0, rows_b, :], q_ref[0, rows_f, :],
                                    q_ref[0, rows_b, :], i_ref[0, rows_f, :], i_ref[0, rows_b, :],
                                    stf_ref[...], stb_ref[...], want_out=True)
        stf_ref[...] = st_f
        stb_ref[...] = st_b
        if second_half:
            readout(rows_f, o_f + accb_ref[rows_f, :])
            readout(rows_b, accf_ref[rows_b, :] + o_b)
        else:
            accf_ref[rows_f, :] = o_f
            accb_ref[rows_b, :] = o_b
        return carry

    half = n_blk // 2
    lax.fori_loop(0, half, functools.partial(scan_step, second_half=False), 0, unroll=8)
    lax.fori_loop(half, n_blk, functools.partial(scan_step, second_half=True), 0, unroll=8)


def _hgrn_call(qh, zf, zb, ih, gh, qc, zfc, zbc, ic, lb_logits, norm_g):
    bsz = qh.shape[0]
    seq = pl.BlockSpec((1, SEQ, HG_DK), lambda b, h: (b, 0, h))
    ctx = pl.BlockSpec((1, CTX_LEN, HG_DK), lambda b, h: (b, 0, h))
    return pl.pallas_call(
        _hgrn_body,
        out_shape=jax.ShapeDtypeStruct((bsz, SEQ, N_HG_HEADS * HG_DK), BF16),
        grid=(bsz, N_HG_HEADS),
        in_specs=[seq, seq, seq, seq, seq, ctx, ctx, ctx, ctx,
                  pl.BlockSpec((lb_logits.shape[0], HG_DK), lambda b, h: (0, h)),
                  pl.BlockSpec((1, HG_DK), lambda b, h: (0, 0))],
        out_specs=seq,
        scratch_shapes=[pltpu.VMEM((SEQ, HG_DK), F32),
                        pltpu.VMEM((SEQ, HG_DK), F32),
                        pltpu.VMEM((HG_DK, HG_DK), F32),
                        pltpu.VMEM((HG_DK, HG_DK), F32),
                        pltpu.VMEM((2, HG_BLOCK, HG_BLOCK), BF16),
                        pltpu.VMEM((2 * HG_BLOCK, HG_BLOCK), F32),
                        pltpu.VMEM((HG_BLOCK, HG_NCHUNK * HG_DK), BF16)],
        compiler_params=_cparams(("arbitrary", "arbitrary")),
        name="hgrn",
    )(qh, zf, zb, ih, gh, qc, zfc, zbc, ic, lb_logits, norm_g)


def _outproj_body(na_ref, hg_ref, x_ref, w_ref, gt1_ref, sh2_ref, sc2_ref, gpost_ref, gpre_ref, wr_ref,
                  x1_ref, h2_ref, lt_ref):
    mix = (jnp.dot(na_ref[0], w_ref[0, :NA_WIDTH, :].astype(BF16), preferred_element_type=F32)
           + jnp.dot(hg_ref[0], w_ref[0, NA_WIDTH:, :].astype(BF16), preferred_element_type=F32))
    x1 = x_ref[0] + gt1_ref[0] * _rms(mix, gpost_ref[...])
    x1_ref[0] = x1
    h2 = _rms(x1, gpre_ref[...]) * (1.0 + sc2_ref[0]) + sh2_ref[0]
    h2_ref[0] = h2.astype(BF16)
    lt_ref[0] = lax.dot_general(wr_ref[...], h2, (((1,), (1,)), ((), ())), precision=HIGHEST,
                                preferred_element_type=F32)


def _outproj_call(na_o, hg_o, x, w_out, mod, g_post1, g_pre2, w_router_t):
    bsz, s, d = x.shape
    tm = OUT_TM
    half = pl.BlockSpec((1, tm, NA_WIDTH), lambda b, i: (b, i, 0))
    tok = pl.BlockSpec((1, tm, d), lambda b, i: (b, i, 0))
    gt1, sh2, sc2 = (_mod_spec(k, lambda b: b, 2) for k in (2, 3, 4))
    par = pl.BlockSpec((1, d), lambda b, i: (0, 0))
    return pl.pallas_call(
        _outproj_body,
        out_shape=[jax.ShapeDtypeStruct((bsz, s, d), F32),
                   jax.ShapeDtypeStruct((bsz, s, d), BF16),
                   jax.ShapeDtypeStruct((bsz, N_EXPERTS, s), F32)],
        grid=(bsz, s // tm),
        in_specs=[half, half, tok, pl.BlockSpec((1, d, d), lambda b, i: (0, 0, 0)), gt1, sh2, sc2, par, par,
                  pl.BlockSpec((N_EXPERTS, d), lambda b, i: (0, 0))],
        out_specs=[tok, tok, pl.BlockSpec((1, N_EXPERTS, tm), lambda b, i: (b, 0, i))],
        compiler_params=_cparams(("arbitrary", "arbitrary")),
        name="outproj",
    )(na_o, hg_o, x, w_out, mod, mod, mod, g_post1, g_pre2, w_router_t)


def _route_body(lt_ref, posm_ref, gate_ref, tab_ref, *, cap):
    l = lt_ref[0]
    mx = jnp.max(l, axis=0, keepdims=True)
    ex = jnp.exp(l - mx)
    aff = ex / jnp.sum(ex, axis=0, keepdims=True)
    capf = jnp.float32(cap)

    def count(mask):
        return jnp.sum(jnp.where(mask, 1.0, 0.0), axis=1, keepdims=True)

    def enough(v):
        return count(aff >= v) >= capf

    def bit_step(it, thr):
        cand = thr | (jnp.int32(1) << (30 - it))
        return jnp.where(enough(pltpu.bitcast(cand, F32)), cand, thr)

    thr = lax.fori_loop(0, 31, bit_step, jnp.zeros((N_EXPERTS, 1), I32))
    lo = pltpu.bitcast(thr, F32)
    hi = pltpu.bitcast(thr + 1, F32)

    def mid_step(it, lohi):
        lo, hi = lohi
        mid = 0.5 * (lo + hi)
        ok = enough(mid)
        return jnp.where(ok, mid, lo), jnp.where(ok, hi, mid)

    lo, hi = lax.fori_loop(0, 30, mid_step, (lo, hi))
    gt = aff >= hi
    eq = (aff >= lo) & jnp.logical_not(gt)
    need = capf - count(gt)

    rr = lax.broadcasted_iota(I32, (ROUTE_BLK, ROUTE_BLK), 0)
    cc = lax.broadcasted_iota(I32, (ROUTE_BLK, ROUTE_BLK), 1)
    upper = _mask_bf16(rr <= cc)
    lane = lax.broadcasted_iota(I32, (N_EXPERTS, LANES), 1)

    off_eq = jnp.zeros((N_EXPERTS, 1), F32)
    off_sel = jnp.zeros((N_EXPERTS, 1), F32)
    tab = jnp.zeros((N_EXPERTS, LANES), F32)
    for j in range(N_ROUTE_BLK):
        sl = slice(j * ROUTE_BLK, (j + 1) * ROUTE_BLK)
        eq_j = eq[:, sl]
        eq_b = _mask_bf16(eq_j)
        incl_eq = jnp.dot(eq_b, upper, preferred_element_type=F32) + off_eq
        rank_eq = incl_eq - eq_b.astype(F32)
        sel_j = gt[:, sl] | (eq_j & (rank_eq < need))
        sel_b = _mask_bf16(sel_j)
        incl_sel = jnp.dot(sel_b, upper, preferred_element_type=F32) + off_sel
        pos = incl_sel - sel_b.astype(F32)
        posm_ref[0, :, sl] = jnp.where(sel_j, pos.astype(I32), -1)
        gate_ref[0, :, sl] = jnp.where(sel_j, aff[:, sl], 0.0)
        tab = jnp.where(lane == j, off_sel, tab)
        off_eq = incl_eq[:, ROUTE_BLK - 1:ROUTE_BLK]
        off_sel = incl_sel[:, ROUTE_BLK - 1:ROUTE_BLK]
    tab = jnp.where(lane == N_ROUTE_BLK, off_sel, tab)
    tab_ref[0] = tab.astype(I32)


def _route_call(lt, cap):
    bsz = lt.shape[0]
    big = pl.BlockSpec((1, N_EXPERTS, SEQ), lambda b: (b, 0, 0))
    return pl.pallas_call(
        functools.partial(_route_body, cap=cap),
        out_shape=[jax.ShapeDtypeStruct((bsz, N_EXPERTS, SEQ), I32),
                   jax.ShapeDtypeStruct((bsz, N_EXPERTS, SEQ), F32),
                   jax.ShapeDtypeStruct((bsz, N_EXPERTS, LANES), I32)],
        grid=(bsz,),
        in_specs=[big],
        out_specs=[big, big, pl.BlockSpec((1, N_EXPERTS, LANES), lambda b: (b, 0, 0))],
        compiler_params=_cparams(("arbitrary",)),
        name="route",
    )(lt)


def _chunk_range(start, end):
    c_lo = start // SLOT_CHUNK
    n = jnp.where(end > start, (end - 1) // SLOT_CHUNK - c_lo + 1, 0)
    return c_lo, n


DISP_NE = 8


def _dispatch_body(tab_ref, posm_ref, gate_ref, h2_ref, x_ref, gsl_ref, *, cap):
    b = pl.program_id(0)
    hf = pl.program_id(1)
    blk = pl.program_id(2)
    tiles_per_step = MOE_TD // ROUTE_BLK
    chunks_per_batch = cap // SLOT_CHUNK

    @pl.when(blk == 0)
    def _():
        x_ref[...] = jnp.zeros_like(x_ref)
        gsl_ref[...] = jnp.zeros_like(gsl_ref)

    slot = lax.broadcasted_iota(I32, (SLOT_CHUNK, ROUTE_BLK), 0)

    def chunk_ranges(k):
        tab_base = (b * N_EXPERTS + hf * DISP_NE + k) * TAB_STRIDE + blk * tiles_per_step
        return [_chunk_range(tab_ref[tab_base + sub], tab_ref[tab_base + sub + 1]) for sub in range(tiles_per_step)]

    def add_chunk(k, sub, chunk):
        cols = slice(sub * ROUTE_BLK, (sub + 1) * ROUTE_BLK)
        base = pl.multiple_of(chunk * SLOT_CHUNK, SLOT_CHUNK)
        onehot = _mask_bf16(posm_ref[0, pl.ds(k, 1), cols] == slot + base)
        got = jnp.dot(onehot, h2_ref[cols, :], preferred_element_type=F32).astype(BF16)
        x_ref[k, pl.ds(base, SLOT_CHUNK), :] = x_ref[k, pl.ds(base, SLOT_CHUNK), :] + got
        g3 = [p.astype(F32) for p in _split3(gate_ref[0, pl.ds(k, 1), cols])]
        g8 = jnp.concatenate(g3 + [jnp.zeros((SUBLANES - len(g3), ROUTE_BLK), F32)], axis=0).astype(BF16)
        gsl_ref[k, chunk] = gsl_ref[k, chunk] + lax.dot_general(g8, onehot, (((1,), (1,)), ((), ())),
                                                                preferred_element_type=F32)

    for k in range(DISP_NE):
        for sub, (c_lo, _) in enumerate(chunk_ranges(k)):
            add_chunk(k, sub, jnp.minimum(c_lo, chunks_per_batch - 1))

    def expert_step(k, carry):
        for sub, (c_lo, n) in enumerate(chunk_ranges(k)):
            def more(kk, c, sub=sub, c_lo=c_lo):
                add_chunk(k, sub, c_lo + kk)
                return c

            lax.fori_loop(1, n, more, 0)
        return carry

    lax.fori_loop(0, DISP_NE, expert_step, 0)


def _dispatch_call(tab_flat, posm, gate, h2_flat, *, bsz, cap):
    d = D_MODEL
    n_blk = SEQ // MOE_TD
    chunks = cap // SLOT_CHUNK
    sel = pl.BlockSpec((1, DISP_NE, MOE_TD), lambda b, hf, blk, tab: (b, hf, blk))
    grid_spec = pltpu.PrefetchScalarGridSpec(
        num_scalar_prefetch=1,
        grid=(bsz, N_EXPERTS // DISP_NE, n_blk),
        in_specs=[sel, sel, pl.BlockSpec((MOE_TD, d), lambda b, hf, blk, tab: (b * n_blk + blk, 0))],
        out_specs=[pl.BlockSpec((DISP_NE, cap, d), lambda b, hf, blk, tab: (hf, b, 0)),
                   pl.BlockSpec((DISP_NE, chunks, SUBLANES, SLOT_CHUNK), lambda b, hf, blk, tab: (hf, b, 0, 0))],
    )
    return pl.pallas_call(
        functools.partial(_dispatch_body, cap=cap),
        out_shape=[jax.ShapeDtypeStruct((N_EXPERTS, bsz * cap, d), BF16),
                   jax.ShapeDtypeStruct((N_EXPERTS, bsz * chunks, SUBLANES, SLOT_CHUNK), F32)],
        grid_spec=grid_spec,
        compiler_params=_cparams(("arbitrary", "arbitrary", "arbitrary")),
        name="dispatch",
    )(tab_flat, posm, gate, h2_flat)


def _moe_body(x_ref, gsl_ref, wg_ref, wu_ref, wd_ref, y_ref, acc_scr):
    s = pl.program_id(1)

    chunks_per_tile = MOE_TM // SLOT_CHUNK

    def mlp(width, first, last):
        nt = (((1,), (1,)), ((), ()))

        def m_step(mi, carry):
            r0 = pl.multiple_of(mi * MOE_TM, MOE_TM)
            rows = pl.ds(r0, MOE_TM)
            xm = x_ref[0, rows, :]
            g = lax.dot_general(xm, wg_ref[0, :width, :].astype(BF16), nt, preferred_element_type=F32)
            u = lax.dot_general(xm, wu_ref[0, :width, :].astype(BF16), nt, preferred_element_type=F32)
            hid = (g * jax.nn.sigmoid(g) * u).astype(BF16)
            out = jnp.dot(hid, wd_ref[0, :width, :].astype(BF16), preferred_element_type=F32)
            if not first:
                out = acc_scr[rows, :] + out
            if last:
                gsl = gsl_ref[0, pl.ds(mi * chunks_per_tile, chunks_per_tile)]
                gcol = jnp.concatenate([jnp.sum(gsl[ci].T, axis=1, keepdims=True) for ci in range(chunks_per_tile)],
                                       axis=0)
                y_ref[0, rows, :] = (out * gcol).astype(BF16)
            else:
                acc_scr[rows, :] = out
            return carry

        lax.fori_loop(0, acc_scr.shape[0] // MOE_TM, m_step, 0)

    @pl.when(s == 0)
    def _():
        mlp(MOE_TF, True, False)

    @pl.when((s > 0) & (s < MOE_NF - 1))
    def _():
        mlp(MOE_TF, False, False)

    @pl.when(s == MOE_NF - 1)
    def _():
        mlp(MOE_F_LAST, False, True)


def _moe_call(xs, gsl, w_gate_t, w_up_t, w_down):
    n_e, slots, d = xs.shape
    per_e = pl.BlockSpec((1, slots, d), lambda e, s: (e, 0, 0))
    wspec = pl.BlockSpec((1, MOE_TF, d), lambda e, s: (e, s, 0))
    return pl.pallas_call(
        _moe_body,
        out_shape=jax.ShapeDtypeStruct((n_e, slots, d), BF16),
        grid=(n_e, MOE_NF),
        in_specs=[per_e, pl.BlockSpec((1,) + gsl.shape[1:], lambda e, s: (e, 0, 0, 0)), wspec, wspec, wspec],
        out_specs=per_e,
        scratch_shapes=[pltpu.VMEM((slots, d), F32)],
        compiler_params=_cparams(("arbitrary", "arbitrary")),
        name="moe",
    )(xs, gsl, w_gate_t, w_up_t, w_down)


def _combine_body(tab_ref, y_ref, posm_ref, x1_ref, gt2_ref, gpost_ref, o_ref, acc_ref, *, cap):
    n_e = N_EXPERTS
    b = pl.program_id(0)
    j = pl.program_id(1)
    pm = posm_ref[0].T
    win = 2 * SLOT_CHUNK
    slot_w = lax.broadcasted_iota(I32, (ROUTE_BLK, win), 1)
    slot_c = lax.broadcasted_iota(I32, (ROUTE_BLK, SLOT_CHUNK), 1)
    total = jnp.zeros((ROUTE_BLK, D_MODEL), F32)
    bases, extras = [], []
    for k in range(n_e):
        tab_base = (b * N_EXPERTS + k) * TAB_STRIDE
        start = tab_ref[tab_base + j]
        end = tab_ref[tab_base + j + 1]
        base = pl.multiple_of(jnp.minimum((start // SLOT_CHUNK) * SLOT_CHUNK, cap - win), SLOT_CHUNK)
        total = total + jnp.dot(_mask_bf16(pm[:, k:k + 1] == slot_w + base), y_ref[k, pl.ds(base, win), :],
                                preferred_element_type=F32)
        bases.append(base)
        extras.append(jnp.maximum(end - (base + win) + SLOT_CHUNK - 1, 0) // SLOT_CHUNK)
    acc_ref[...] = total

    @pl.when(sum(extras) > 0)
    def _():
        for k in range(n_e):
            def extra_step(kk, carry, k=k):
                b2 = pl.multiple_of(bases[k] + win + kk * SLOT_CHUNK, SLOT_CHUNK)
                acc_ref[...] += jnp.dot(_mask_bf16(pm[:, k:k + 1] == slot_c + b2),
                                        y_ref[k, pl.ds(b2, SLOT_CHUNK), :], preferred_element_type=F32)
                return carry

            lax.fori_loop(0, extras[k], extra_step, 0)

    o_ref[0] = x1_ref[0] + gt2_ref[0] * _rms(acc_ref[...], gpost_ref[...])


def _combine_call(tab_flat, y, posm, x1, mod, g_post2, *, cap):
    bsz, n_e, s = posm.shape
    d = D_MODEL
    tok = pl.BlockSpec((1, ROUTE_BLK, d), lambda b, j, tab: (b, j, 0))
    grid_spec = pltpu.PrefetchScalarGridSpec(
        num_scalar_prefetch=1,
        grid=(bsz, s // ROUTE_BLK),
        in_specs=[
            pl.BlockSpec((n_e, cap, d), lambda b, j, tab: (0, b, 0), pipeline_mode=pl.Buffered(1)),
            pl.BlockSpec((1, n_e, ROUTE_BLK), lambda b, j, tab: (b, 0, j)),
            tok,
            _mod_spec(5, lambda b: b, 3),
            pl.BlockSpec((1, d), lambda b, j, tab: (0, 0))],
        out_specs=tok,
        scratch_shapes=[pltpu.VMEM((ROUTE_BLK, d), F32)],
    )
    return pl.pallas_call(
        functools.partial(_combine_body, cap=cap),
        out_shape=jax.ShapeDtypeStruct((bsz, s, d), F32),
        grid_spec=grid_spec,
        compiler_params=_cparams(("arbitrary", "arbitrary")),
        name="combine",
    )(tab_flat, y, posm, x1, mod, g_post2)


def kernel(x, c, ctx, c_ctx, w_mod, b_mod, g_pre1, g_post1, g_pre2, g_post2, w_in, w_out, na_rpb,
           hg_lb_logits, hg_norm, w_router, w_gate, w_up, w_down):
    bsz, seq, d = x.shape
    assert (seq, d) == (SEQ, D_MODEL) and ctx.shape[1] == CTX_LEN and w_mod.shape[0] == 1
    cap = 2 * seq // N_EXPERTS
    assert cap % SLOT_CHUNK == 0

    assert bsz < MOD_ROWS
    mod = _mod_call(c, c_ctx, w_mod, b_mod)

    cos, sa, sb = _rope_tables(seq)

    qrot, qpl, krot, v, qh, zf, zb, ih, gh = _inproj_call(
        x, mod, lambda b: b, g_pre1, w_in, cos, sa, sb, rope=True, tm=IN_TM)
    ctx_flat = ctx.reshape(1, bsz * CTX_LEN, d)
    ctx_out = _inproj_call(ctx_flat, mod, lambda b: bsz, g_pre1, w_in,
                           cos[:bsz * CTX_LEN], sa[:bsz * CTX_LEN], sb[:bsz * CTX_LEN], rope=False, tm=CTX_LEN)
    _, _, kc, vc, qc, zfc, zbc, ic, _ = [a.reshape(bsz, CTX_LEN, COL_GROUP) for a in ctx_out]

    na_o = _na_call(qrot, qpl, krot, v, kc, vc, _na_bias(na_rpb[0]))
    hg_o = _hgrn_call(qh, zf, zb, ih, gh, qc, zfc, zbc, ic, hg_lb_logits, hg_norm)

    x1, h2, lt = _outproj_call(na_o, hg_o, x, w_out, mod, g_post1, g_pre2, w_router[0].T)

    posm, gate, tab = _route_call(lt, cap)
    tab_flat = tab.reshape(-1)
    xs, gsl = _dispatch_call(tab_flat, posm, gate, h2.reshape(bsz * seq, d), bsz=bsz, cap=cap)
    y = _moe_call(xs, gsl, jnp.swapaxes(w_gate[0], 1, 2), jnp.swapaxes(w_up[0], 1, 2), w_down[0])

    return _combine_call(tab_flat, y, posm, x1, mod, g_post2, cap=cap)
```

```python
import functools

import jax
import jax.numpy as jnp
import numpy as np
from jax import lax
from jax.experimental import pallas as pl
from jax.experimental.pallas import tpu as pltpu

F32 = jnp.float32
BF16 = jnp.bfloat16
I32 = jnp.int32
HIGHEST = lax.Precision.HIGHEST

D_MODEL = 1024
GRID_W = 64
N_ROWS = 128
SEQ = 8192
CTX_LEN = 256
NA_HEAD_DIM = 64
N_NA_HEADS = 8
NA_WIDTH = 512
NA_WIN_H = 8
NA_WIN_W = 16
ROPE_BASE = 10000.0
HG_DK = 128
N_HG_HEADS = 4
HG_CHUNK = 32
N_EXPERTS = 16
D_EXPERT = 2752
RMS_EPS = 1e-6
COL_GROUP = 512

LANES = 128
SUBLANES = 8
VMEM_LIMIT_BYTES = 58 * 1024 * 1024

IN_TM = 1024
NA_ROWS_PER_STEP = 64
NA_KEYS = NA_WIN_H * GRID_W
HG_BLOCK = 256
OUT_TM = 1024
ROUTE_BLK = 256
N_ROUTE_BLK = SEQ // ROUTE_BLK
SLOT_CHUNK = 128
TAB_STRIDE = LANES
MOE_TD = 2048
MOE_TF = 768
MOE_NF = -(-D_EXPERT // MOE_TF)
MOE_F_LAST = D_EXPERT - (MOE_NF - 1) * MOE_TF
MOE_TM = 1024


def _cparams(sem):
    return pltpu.CompilerParams(dimension_semantics=sem, vmem_limit_bytes=VMEM_LIMIT_BYTES)


def _mask_bf16(mask):
    return jnp.where(mask, 1.0, 0.0).astype(BF16)


def _rms(v, g):
    return v * lax.rsqrt(jnp.mean(v * v, axis=-1, keepdims=True) + RMS_EPS) * g


MOD_ROWS = 8
N_MOD = 6


def _mod_body(c_ref, cc_ref, w_ref, b_ref, o_ref, rows_scr):
    bsz = c_ref.shape[0]
    rows_scr[...] = jnp.zeros_like(rows_scr)
    rows_scr[0:bsz, :] = c_ref[...]
    rows_scr[bsz:bsz + 1, :] = cc_ref[...]
    cv = rows_scr[...]
    s = cv * jax.nn.sigmoid(cv)
    o_ref[...] = jnp.dot(s, w_ref[0], precision=HIGHEST, preferred_element_type=F32) + b_ref[...]


def _mod_call(c, c_ctx, w_mod, b_mod):
    bsz, d = c.shape
    n = w_mod.shape[2]
    tn = d
    mod = pl.pallas_call(
        _mod_body,
        out_shape=jax.ShapeDtypeStruct((MOD_ROWS, n), F32),
        grid=(n // tn,),
        in_specs=[pl.BlockSpec((bsz, d), lambda j: (0, 0)),
                  pl.BlockSpec((1, d), lambda j: (0, 0)),
                  pl.BlockSpec((1, d, tn), lambda j: (0, 0, j)),
                  pl.BlockSpec((1, tn), lambda j: (0, j))],
        out_specs=pl.BlockSpec((MOD_ROWS, tn), lambda j: (0, j)),
        scratch_shapes=[pltpu.VMEM((MOD_ROWS, d), F32)],
        compiler_params=_cparams(("arbitrary",)),
        name="mod",
    )(c, c_ctx[None, :], w_mod, b_mod)
    return mod.reshape(MOD_ROWS * N_MOD, 1, d)


def _mod_spec(k, row_of, n_grid):
    if n_grid == 2:
        return pl.BlockSpec((1, 1, D_MODEL), lambda b, i: (row_of(b) * N_MOD + k, 0, 0))
    return pl.BlockSpec((1, 1, D_MODEL), lambda b, i, tab: (row_of(b) * N_MOD + k, 0, 0))


def _rope128(pg, cos, sa, sb):
    return pg * cos + pltpu.roll(pg, LANES - 16, axis=1) * sa + pltpu.roll(pg, 16, axis=1) * sb


def _inproj_body(x_ref, sh_ref, sc_ref, g_ref, w_ref, cos_ref, sa_ref, sb_ref,
                 qrot_ref, qpl_ref, krot_ref, v_ref, qh_ref, zf_ref, zb_ref, ih_ref, gh_ref, *, rope):
    xv = x_ref[0]
    h = _rms(xv, g_ref[...]) * (1.0 + sc_ref[0]) + sh_ref[0]
    hb = h.astype(BF16)

    def group(j):
        return jnp.dot(hb, w_ref[0, :, j * COL_GROUP:(j + 1) * COL_GROUP].astype(BF16), preferred_element_type=F32)

    def rotated(p):
        if not rope:
            return p
        cos, sa, sb = cos_ref[...], sa_ref[...], sb_ref[...]
        return jnp.concatenate(
            [_rope128(p[:, k * LANES:(k + 1) * LANES], cos, sa, sb) for k in range(COL_GROUP // LANES)], axis=1)

    scale = NA_HEAD_DIM ** -0.5
    p = group(0)
    qpl_ref[0] = (p * scale).astype(BF16)
    qrot_ref[0] = (rotated(p) * scale).astype(BF16)
    krot_ref[0] = rotated(group(1)).astype(BF16)
    v_ref[0] = group(2).astype(BF16)
    qh_ref[0] = group(3).astype(BF16)
    zf_ref[0] = group(4)
    zb_ref[0] = group(5)
    ih_ref[0] = group(6).astype(BF16)
    gh_ref[0] = group(7).astype(BF16)


def _inproj_call(x3, mod, row_of, g_pre, w_in, cos, sa, sb, *, rope, tm):
    g, t, d = x3.shape
    n_out = 9
    dts = [BF16, BF16, BF16, BF16, BF16, F32, F32, BF16, BF16]
    tok = pl.BlockSpec((1, tm, d), lambda b, i: (b, i, 0))
    tab = pl.BlockSpec((tm, LANES), lambda b, i: (i, 0))
    outs = [pl.BlockSpec((1, tm, COL_GROUP), lambda b, i: (b, i, 0)) for _ in range(n_out)]
    return pl.pallas_call(
        functools.partial(_inproj_body, rope=rope),
        out_shape=[jax.ShapeDtypeStruct((g, t, COL_GROUP), dt) for dt in dts],
        grid=(g, t // tm),
        in_specs=[tok, _mod_spec(0, row_of, 2), _mod_spec(1, row_of, 2),
                  pl.BlockSpec((1, d), lambda b, i: (0, 0)),
                  pl.BlockSpec(w_in.shape, lambda b, i: (0, 0, 0), pipeline_mode=pl.Buffered(1)),
                  tab, tab, tab],
        out_specs=outs,
        compiler_params=_cparams(("arbitrary", "arbitrary")),
        name="inproj_rope" if rope else "inproj_ctx",
    )(x3, mod, mod, g_pre, w_in, cos, sa, sb)


def _rope_tables(n_tok):
    half = NA_HEAD_DIM // 2
    t = np.arange(n_tok)
    row = (t // GRID_W).astype(np.float32)
    col = (t % GRID_W).astype(np.float32)
    lane = np.arange(LANES)
    d = lane % NA_HEAD_DIM
    dd = d % half
    fi = dd % (half // 2)
    inv_freq = (ROPE_BASE ** (-(2.0 * fi.astype(np.float32)) / half)).astype(np.float32)
    pos = np.where((d < half)[None, :], row[:, None], col[:, None])
    ang = (pos * inv_freq[None, :]).astype(np.float32)
    first = (dd < half // 2)[None, :]
    cos = np.cos(ang.astype(np.float64)).astype(np.float32)
    sin = np.sin(ang.astype(np.float64)).astype(np.float32)
    zero = np.zeros_like(sin)
    return jnp.asarray(cos), jnp.asarray(np.where(first, -sin, zero)), jnp.asarray(np.where(first, zero, sin))


def _na_body(qrot_ref, qpl_ref, k_ref, v_ref, kc_ref, vc_ref, bias_ref, o_ref):
    m = pl.program_id(2)
    lane = lax.broadcasted_iota(I32, (1, LANES), 1)
    first_head = lane < NA_HEAD_DIM
    hm0, hm1 = _mask_bf16(first_head), _mask_bf16(lane >= NA_HEAD_DIM)
    nt = (((1,), (1,)), ((), ()))
    rows = range(NA_ROWS_PER_STEP)

    def stacked(ref):
        parts = []
        for i in rows:
            q = ref[0, i * GRID_W:(i + 1) * GRID_W, :]
            parts += [q * hm0, q * hm1]
        return jnp.concatenate(parts, axis=0)

    sc = lax.dot_general(stacked(qpl_ref), kc_ref[0], nt, preferred_element_type=F32)

    qs = stacked(qrot_ref)
    k0s, s_parts = [], []
    blk = 2 * GRID_W
    for i in rows:
        r = m * NA_ROWS_PER_STEP + i
        r0 = jnp.clip(r - NA_WIN_H // 2, 0, N_ROWS - NA_WIN_H)
        di = r0 - r + (NA_WIN_H - 1)
        k0s.append(pl.multiple_of(r0 * GRID_W, GRID_W))
        s_parts.append(lax.dot_general(qs[i * blk:(i + 1) * blk], k_ref[0, pl.ds(k0s[i], NA_KEYS), :], nt,
                                       preferred_element_type=F32) + bias_ref[0, di].reshape(blk, NA_KEYS))
    s = jnp.concatenate([jnp.concatenate(s_parts, axis=0), sc], axis=1)
    p = jnp.exp(s - jnp.max(s, axis=-1, keepdims=True))
    den = jnp.sum(p, axis=-1, keepdims=True)
    pb = p.astype(BF16)
    acc = jnp.concatenate([jnp.dot(pb[i * blk:(i + 1) * blk, :NA_KEYS], v_ref[0, pl.ds(k0s[i], NA_KEYS), :],
                                   preferred_element_type=F32) for i in rows], axis=0)
    o = (acc + jnp.dot(pb[:, NA_KEYS:], vc_ref[0], preferred_element_type=F32)) / den
    o_ref[0] = jnp.concatenate([jnp.where(first_head, o[i * blk:i * blk + GRID_W], o[i * blk + GRID_W:(i + 1) * blk])
                                for i in rows], axis=0).astype(BF16)


def _na_bias(rpb):
    c = np.arange(GRID_W)
    cq = np.arange(GRID_W)
    win_c0 = np.clip(cq - NA_WIN_W // 2, 0, GRID_W - NA_WIN_W)
    in_win = (c[None, :] >= win_c0[:, None]) & (c[None, :] < win_c0[:, None] + NA_WIN_W)
    dc = np.clip(c[None, :] - cq[:, None] + NA_WIN_W - 1, 0, 2 * NA_WIN_W - 2)
    pick = (dc[None, :, :] == np.arange(2 * NA_WIN_W - 1)[:, None, None]).astype(np.float32)
    t = jnp.einsum('hrj,jqc->hqrc', rpb.astype(F32), jnp.asarray(pick), precision=HIGHEST)
    t = jnp.where(jnp.asarray(in_win)[None, :, None, :], t, -jnp.inf)
    n_dr = 2 * NA_WIN_H - 1
    t = t.reshape(N_NA_HEADS // 2, 2, GRID_W, n_dr * GRID_W)
    return jnp.stack([t[..., di * GRID_W:di * GRID_W + NA_KEYS] for di in range(NA_WIN_H)], axis=1)


def _na_call(qrot, qpl, krot, v, kc, vc, bias):
    bsz = qrot.shape[0]
    tq = NA_ROWS_PER_STEP * GRID_W
    qspec = pl.BlockSpec((1, tq, LANES), lambda b, p, m: (b, m, p))
    kspec = pl.BlockSpec((1, SEQ, LANES), lambda b, p, m: (b, 0, p))
    cspec = pl.BlockSpec((1, CTX_LEN, LANES), lambda b, p, m: (b, 0, p))
    return pl.pallas_call(
        _na_body,
        out_shape=jax.ShapeDtypeStruct((bsz, SEQ, NA_WIDTH), BF16),
        grid=(bsz, N_NA_HEADS // 2, N_ROWS // NA_ROWS_PER_STEP),
        in_specs=[qspec, qspec, kspec, kspec, cspec, cspec,
                  pl.BlockSpec((1, NA_WIN_H, 2, GRID_W, NA_KEYS), lambda b, p, m: (p, 0, 0, 0, 0))],
        out_specs=qspec,
        compiler_params=_cparams(("arbitrary", "arbitrary", "arbitrary")),
        name="na",
    )(qrot, qpl, krot, v, kc, vc, bias)


def _split3(v):
    hi = v.astype(BF16)
    r1 = v - hi.astype(F32)
    mid = r1.astype(BF16)
    lo = (r1 - mid.astype(F32)).astype(BF16)
    return hi, mid, lo


HG_NCHUNK = HG_BLOCK // HG_CHUNK


def _hg_consts(tri_ref, keep_ref, spread_ref):
    nb = HG_BLOCK
    rr = lax.broadcasted_iota(I32, (nb, nb), 0)
    cc = lax.broadcasted_iota(I32, (nb, nb), 1)
    same = (rr // HG_CHUNK) == (cc // HG_CHUNK)
    fwd = jnp.where(same & (cc <= rr), 1.0, 0.0)
    bwd = jnp.where(same & (cc >= rr), 1.0, 0.0)
    tri_ref[0] = fwd.astype(BF16)
    tri_ref[1] = bwd.astype(BF16)
    keep_ref[...] = jnp.concatenate([fwd, bwd], axis=0)
    r8 = lax.broadcasted_iota(I32, (nb, HG_NCHUNK * HG_DK), 0) // HG_CHUNK
    g8 = lax.broadcasted_iota(I32, (nb, HG_NCHUNK * HG_DK), 1) // HG_DK
    spread_ref[...] = _mask_bf16(r8 == g8)


def _hg_pair(zf, zb, qf, qb, vf, vb, stf, stb, lb, tri_ref, keep_ref, spread_ref, *, want_out):
    nb = HG_BLOCK
    spread = spread_ref[...]
    v = (vf, vb)

    sig = jax.nn.sigmoid(jnp.concatenate([zf, zb], axis=0))
    lf = jnp.log(lb + (1.0 - lb) * sig)
    kk = (1.0 - lb) * (1.0 - sig)
    parts = jnp.concatenate(_split3(lf), axis=1)
    sums = jnp.concatenate([jnp.dot(tri_ref[d], parts[d * nb:(d + 1) * nb], preferred_element_type=F32)
                            for d in range(2)], axis=0)
    bcum = sums[:, :HG_DK] + sums[:, HG_DK:2 * HG_DK] + sums[:, 2 * HG_DK:]
    end_rows = [d * nb + c * HG_CHUNK + (HG_CHUNK - 1 if d == 0 else 0) for d in range(2) for c in range(HG_NCHUNK)]
    ends = [bcum[r:r + 1, :] for r in end_rows]
    btot = jnp.concatenate([jnp.broadcast_to(e, (HG_CHUNK, HG_DK)) for e in ends], axis=0)
    dec = jnp.exp(jnp.concatenate(ends, axis=0))
    ke = (kk * jnp.exp(btot - bcum)).astype(BF16)

    new_states, entering = [], []
    for d, st in enumerate((stf, stb)):
        kvt = lax.dot_general(v[d], jnp.concatenate([ke[d * nb:(d + 1) * nb]] * HG_NCHUNK, axis=1) * spread,
                              (((0,), (0,)), ((), ())), preferred_element_type=F32)
        ent = [None] * HG_NCHUNK
        for c in (range(HG_NCHUNK) if d == 0 else range(HG_NCHUNK - 1, -1, -1)):
            ent[c] = st
            i_dec = d * HG_NCHUNK + c
            st = st * dec[i_dec:i_dec + 1, :] + kvt[:, c * HG_DK:(c + 1) * HG_DK]
        new_states.append(st)
        entering.append(ent)
    if not want_out:
        return None, None, new_states[0], new_states[1]

    qd = (jnp.concatenate([qf, qb], axis=0).astype(F32) * jnp.exp(bcum)).astype(BF16)
    kd = (kk * jnp.exp(-bcum)).astype(BF16)
    nt = (((1,), (1,)), ((), ()))
    a = jnp.concatenate([lax.dot_general(qd[d * nb:(d + 1) * nb], kd[d * nb:(d + 1) * nb], nt,
                                         preferred_element_type=F32) for d in range(2)], axis=0)
    a = jnp.where(keep_ref[...] > 0.5, a, 0.0).astype(BF16)
    outs = []
    for d in range(2):
        qd_d = qd[d * nb:(d + 1) * nb]
        inter = jnp.concatenate(
            [lax.dot_general(qd_d[c * HG_CHUNK:(c + 1) * HG_CHUNK], entering[d][c].astype(BF16), nt,
                             preferred_element_type=F32) for c in range(HG_NCHUNK)], axis=0)
        outs.append(jnp.dot(a[d * nb:(d + 1) * nb], v[d], preferred_element_type=F32) + inter)
    return outs[0], outs[1], new_states[0], new_states[1]


def _hgrn_body(q_ref, zf_ref, zb_ref, i_ref, g_ref, qc_ref, zfc_ref, zbc_ref, ic_ref, lb_ref, ng_ref,
               o_ref, accf_ref, accb_ref, stf_ref, stb_ref, tri_ref, keep_ref, spread_ref):
    lbl = lb_ref[...]
    lbe = jnp.exp(lbl - jnp.max(lbl, axis=0, keepdims=True))
    lb = lbe[0:1, :] / jnp.sum(lbe, axis=0, keepdims=True)
    _hg_consts(tri_ref, keep_ref, spread_ref)
    pair = functools.partial(_hg_pair, lb=lb, tri_ref=tri_ref, keep_ref=keep_ref, spread_ref=spread_ref)
    zero_state = jnp.zeros((HG_DK, HG_DK), F32)
    _, _, stf, stb = pair(zfc_ref[0], zbc_ref[0], qc_ref[0], qc_ref[0], ic_ref[0], ic_ref[0],
                          zero_state, zero_state, want_out=False)
    stf_ref[...] = stf
    stb_ref[...] = stb
    n_blk = SEQ // HG_BLOCK

    def readout(rows, tot):
        y = _rms(tot, ng_ref[...]) * jax.nn.sigmoid(g_ref[0, rows, :].astype(F32))
        o_ref[0, rows, :] = y.astype(BF16)

    def scan_step(n, carry, *, second_half):
        rows_f = pl.ds(pl.multiple_of(n * HG_BLOCK, HG_BLOCK), HG_BLOCK)
        rows_b = pl.ds(pl.multiple_of((n_blk - 1 - n) * HG_BLOCK, HG_BLOCK), HG_BLOCK)
        o_f, o_b, st_f, st_b = pair(zf_ref[0, rows_f, :], zb_ref[0, rows_b, :], q_ref[0, rows_f, :],
                                    q_ref[0, rows_b, :], i_ref[0, rows_f, :], i_ref[0, rows_b, :],
                                    stf_ref[...], stb_ref[...], want_out=True)
        stf_ref[...] = st_f
        stb_ref[...] = st_b
        if second_half:
            readout(rows_f, o_f + accb_ref[rows_f, :])
            readout(rows_b, accf_ref[rows_b, :] + o_b)
        else:
            accf_ref[rows_f, :] = o_f
            accb_ref[rows_b, :] = o_b
        return carry

    half = n_blk // 2
    lax.fori_loop(0, half, functools.partial(scan_step, second_half=False), 0, unroll=8)
    lax.fori_loop(half, n_blk, functools.partial(scan_step, second_half=True), 0, unroll=8)


def _hgrn_call(qh, zf, zb, ih, gh, qc, zfc, zbc, ic, lb_logits, norm_g):
    bsz = qh.shape[0]
    seq = pl.BlockSpec((1, SEQ, HG_DK), lambda b, h: (b, 0, h))
    ctx = pl.BlockSpec((1, CTX_LEN, HG_DK), lambda b, h: (b, 0, h))
    return pl.pallas_call(
        _hgrn_body,
        out_shape=jax.ShapeDtypeStruct((bsz, SEQ, N_HG_HEADS * HG_DK), BF16),
        grid=(bsz, N_HG_HEADS),
        in_specs=[seq, seq, seq, seq, seq, ctx, ctx, ctx, ctx,
                  pl.BlockSpec((lb_logits.shape[0], HG_DK), lambda b, h: (0, h)),
                  pl.BlockSpec((1, HG_DK), lambda b, h: (0, 0))],
        out_specs=seq,
        scratch_shapes=[pltpu.VMEM((SEQ, HG_DK), F32),
                        pltpu.VMEM((SEQ, HG_DK), F32),
                        pltpu.VMEM((HG_DK, HG_DK), F32),
                        pltpu.VMEM((HG_DK, HG_DK), F32),
                        pltpu.VMEM((2, HG_BLOCK, HG_BLOCK), BF16),
                        pltpu.VMEM((2 * HG_BLOCK, HG_BLOCK), F32),
                        pltpu.VMEM((HG_BLOCK, HG_NCHUNK * HG_DK), BF16)],
        compiler_params=_cparams(("arbitrary", "arbitrary")),
        name="hgrn",
    )(qh, zf, zb, ih, gh, qc, zfc, zbc, ic, lb_logits, norm_g)


def _outproj_body(na_ref, hg_ref, x_ref, w_ref, gt1_ref, sh2_ref, sc2_ref, gpost_ref, gpre_ref, wr_ref,
                  x1_ref, h2_ref, lt_ref):
    mix = (jnp.dot(na_ref[0], w_ref[0, :NA_WIDTH, :].astype(BF16), preferred_element_type=F32)
           + jnp.dot(hg_ref[0], w_ref[0, NA_WIDTH:, :].astype(BF16), preferred_element_type=F32))
    x1 = x_ref[0] + gt1_ref[0] * _rms(mix, gpost_ref[...])
    x1_ref[0] = x1
    h2 = _rms(x1, gpre_ref[...]) * (1.0 + sc2_ref[0]) + sh2_ref[0]
    h2_ref[0] = h2.astype(BF16)
    lt_ref[0] = lax.dot_general(wr_ref[...], h2, (((1,), (1,)), ((), ())), precision=HIGHEST,
                                preferred_element_type=F32)


def _outproj_call(na_o, hg_o, x, w_out, mod, g_post1, g_pre2, w_router_t):
    bsz, s, d = x.shape
    tm = OUT_TM
    half = pl.BlockSpec((1, tm, NA_WIDTH), lambda b, i: (b, i, 0))
    tok = pl.BlockSpec((1, tm, d), lambda b, i: (b, i, 0))
    gt1, sh2, sc2 = (_mod_spec(k, lambda b: b, 2) for k in (2, 3, 4))
    par = pl.BlockSpec((1, d), lambda b, i: (0, 0))
    return pl.pallas_call(
        _outproj_body,
        out_shape=[jax.ShapeDtypeStruct((bsz, s, d), F32),
                   jax.ShapeDtypeStruct((bsz, s, d), BF16),
                   jax.ShapeDtypeStruct((bsz, N_EXPERTS, s), F32)],
        grid=(bsz, s // tm),
        in_specs=[half, half, tok, pl.BlockSpec((1, d, d), lambda b, i: (0, 0, 0)), gt1, sh2, sc2, par, par,
                  pl.BlockSpec((N_EXPERTS, d), lambda b, i: (0, 0))],
        out_specs=[tok, tok, pl.BlockSpec((1, N_EXPERTS, tm), lambda b, i: (b, 0, i))],
        compiler_params=_cparams(("arbitrary", "arbitrary")),
        name="outproj",
    )(na_o, hg_o, x, w_out, mod, mod, mod, g_post1, g_pre2, w_router_t)


def _route_body(lt_ref, posm_ref, gate_ref, tab_ref, *, cap):
    l = lt_ref[0]
    mx = jnp.max(l, axis=0, keepdims=True)
    ex = jnp.exp(l - mx)
    aff = ex / jnp.sum(ex, axis=0, keepdims=True)
    capf = jnp.float32(cap)

    def count(mask):
        return jnp.sum(jnp.where(mask, 1.0, 0.0), axis=1, keepdims=True)

    def enough(v):
        return count(aff >= v) >= capf

    def bit_step(it, thr):
        cand = thr | (jnp.int32(1) << (30 - it))
        return jnp.where(enough(pltpu.bitcast(cand, F32)), cand, thr)

    thr = lax.fori_loop(0, 31, bit_step, jnp.zeros((N_EXPERTS, 1), I32))
    lo = pltpu.bitcast(thr, F32)
    hi = pltpu.bitcast(thr + 1, F32)

    def mid_step(it, lohi):
        lo, hi = lohi
        mid = 0.5 * (lo + hi)
        ok = enough(mid)
        return jnp.where(ok, mid, lo), jnp.where(ok, hi, mid)

    lo, hi = lax.fori_loop(0, 30, mid_step, (lo, hi))
    gt = aff >= hi
    eq = (aff >= lo) & jnp.logical_not(gt)
    need = capf - count(gt)

    rr = lax.broadcasted_iota(I32, (ROUTE_BLK, ROUTE_BLK), 0)
    cc = lax.broadcasted_iota(I32, (ROUTE_BLK, ROUTE_BLK), 1)
    upper = _mask_bf16(rr <= cc)
    lane = lax.broadcasted_iota(I32, (N_EXPERTS, LANES), 1)

    off_eq = jnp.zeros((N_EXPERTS, 1), F32)
    off_sel = jnp.zeros((N_EXPERTS, 1), F32)
    tab = jnp.zeros((N_EXPERTS, LANES), F32)
    for j in range(N_ROUTE_BLK):
        sl = slice(j * ROUTE_BLK, (j + 1) * ROUTE_BLK)
        eq_j = eq[:, sl]
        eq_b = _mask_bf16(eq_j)
        incl_eq = jnp.dot(eq_b, upper, preferred_element_type=F32) + off_eq
        rank_eq = incl_eq - eq_b.astype(F32)
        sel_j = gt[:, sl] | (eq_j & (rank_eq < need))
        sel_b = _mask_bf16(sel_j)
        incl_sel = jnp.dot(sel_b, upper, preferred_element_type=F32) + off_sel
        pos = incl_sel - sel_b.astype(F32)
        posm_ref[0, :, sl] = jnp.where(sel_j, pos.astype(I32), -1)
        gate_ref[0, :, sl] = jnp.where(sel_j, aff[:, sl], 0.0)
        tab = jnp.where(lane == j, off_sel, tab)
        off_eq = incl_eq[:, ROUTE_BLK - 1:ROUTE_BLK]
        off_sel = incl_sel[:, ROUTE_BLK - 1:ROUTE_BLK]
    tab = jnp.where(lane == N_ROUTE_BLK, off_sel, tab)
    tab_ref[0] = tab.astype(I32)


def _route_call(lt, cap):
    bsz = lt.shape[0]
    big = pl.BlockSpec((1, N_EXPERTS, SEQ), lambda b: (b, 0, 0))
    return pl.pallas_call(
        functools.partial(_route_body, cap=cap),
        out_shape=[jax.ShapeDtypeStruct((bsz, N_EXPERTS, SEQ), I32),
                   jax.ShapeDtypeStruct((bsz, N_EXPERTS, SEQ), F32),
                   jax.ShapeDtypeStruct((bsz, N_EXPERTS, LANES), I32)],
        grid=(bsz,),
        in_specs=[big],
        out_specs=[big, big, pl.BlockSpec((1, N_EXPERTS, LANES), lambda b: (b, 0, 0))],
        compiler_params=_cparams(("arbitrary",)),
        name="route",
    )(lt)


def _chunk_range(start, end):
    c_lo = start // SLOT_CHUNK
    n = jnp.where(end > start, (end - 1) // SLOT_CHUNK - c_lo + 1, 0)
    return c_lo, n


DISP_NE = 8


def _dispatch_body(tab_ref, posm_ref, gate_ref, h2_ref, x_ref, gsl_ref, *, cap):
    b = pl.program_id(0)
    hf = pl.program_id(1)
    blk = pl.program_id(2)
    tiles_per_step = MOE_TD // ROUTE_BLK
    chunks_per_batch = cap // SLOT_CHUNK

    @pl.when(blk == 0)
    def _():
        x_ref[...] = jnp.zeros_like(x_ref)
        gsl_ref[...] = jnp.zeros_like(gsl_ref)

    slot = lax.broadcasted_iota(I32, (SLOT_CHUNK, ROUTE_BLK), 0)

    def chunk_ranges(k):
        tab_base = (b * N_EXPERTS + hf * DISP_NE + k) * TAB_STRIDE + blk * tiles_per_step
        return [_chunk_range(tab_ref[tab_base + sub], tab_ref[tab_base + sub + 1]) for sub in range(tiles_per_step)]

    def add_chunk(k, sub, chunk):
        cols = slice(sub * ROUTE_BLK, (sub + 1) * ROUTE_BLK)
        base = pl.multiple_of(chunk * SLOT_CHUNK, SLOT_CHUNK)
        onehot = _mask_bf16(posm_ref[0, pl.ds(k, 1), cols] == slot + base)
        got = jnp.dot(onehot, h2_ref[cols, :], preferred_element_type=F32).astype(BF16)
        x_ref[k, pl.ds(base, SLOT_CHUNK), :] = x_ref[k, pl.ds(base, SLOT_CHUNK), :] + got
        g3 = [p.astype(F32) for p in _split3(gate_ref[0, pl.ds(k, 1), cols])]
        g8 = jnp.concatenate(g3 + [jnp.zeros((SUBLANES - len(g3), ROUTE_BLK), F32)], axis=0).astype(BF16)
        gsl_ref[k, chunk] = gsl_ref[k, chunk] + lax.dot_general(g8, onehot, (((1,), (1,)), ((), ())),
                                                                preferred_element_type=F32)

    for k in range(DISP_NE):
        for sub, (c_lo, _) in enumerate(chunk_ranges(k)):
            add_chunk(k, sub, jnp.minimum(c_lo, chunks_per_batch - 1))

    def expert_step(k, carry):
        for sub, (c_lo, n) in enumerate(chunk_ranges(k)):
            def more(kk, c, sub=sub, c_lo=c_lo):
                add_chunk(k, sub, c_lo + kk)
                return c

            lax.fori_loop(1, n, more, 0)
        return carry

    lax.fori_loop(0, DISP_NE, expert_step, 0)


def _dispatch_call(tab_flat, posm, gate, h2_flat, *, bsz, cap):
    d = D_MODEL
    n_blk = SEQ // MOE_TD
    chunks = cap // SLOT_CHUNK
    sel = pl.BlockSpec((1, DISP_NE, MOE_TD), lambda b, hf, blk, tab: (b, hf, blk))
    grid_spec = pltpu.PrefetchScalarGridSpec(
        num_scalar_prefetch=1,
        grid=(bsz, N_EXPERTS // DISP_NE, n_blk),
        in_specs=[sel, sel, pl.BlockSpec((MOE_TD, d), lambda b, hf, blk, tab: (b * n_blk + blk, 0))],
        out_specs=[pl.BlockSpec((DISP_NE, cap, d), lambda b, hf, blk, tab: (hf, b, 0)),
                   pl.BlockSpec((DISP_NE, chunks, SUBLANES, SLOT_CHUNK), lambda b, hf, blk, tab: (hf, b, 0, 0))],
    )
    return pl.pallas_call(
        functools.partial(_dispatch_body, cap=cap),
        out_shape=[jax.ShapeDtypeStruct((N_EXPERTS, bsz * cap, d), BF16),
                   jax.ShapeDtypeStruct((N_EXPERTS, bsz * chunks, SUBLANES, SLOT_CHUNK), F32)],
        grid_spec=grid_spec,
        compiler_params=_cparams(("arbitrary", "arbitrary", "arbitrary")),
        name="dispatch",
    )(tab_flat, posm, gate, h2_flat)


def _moe_body(x_ref, gsl_ref, wg_ref, wu_ref, wd_ref, y_ref, acc_scr):
    s = pl.program_id(1)

    chunks_per_tile = MOE_TM // SLOT_CHUNK

    def mlp(width, first, last):
        nt = (((1,), (1,)), ((), ()))

        def m_step(mi, carry):
            r0 = pl.multiple_of(mi * MOE_TM, MOE_TM)
            rows = pl.ds(r0, MOE_TM)
            xm = x_ref[0, rows, :]
            g = lax.dot_general(xm, wg_ref[0, :width, :].astype(BF16), nt, preferred_element_type=F32)
            u = lax.dot_general(xm, wu_ref[0, :width, :].astype(BF16), nt, preferred_element_type=F32)
            hid = (g * jax.nn.sigmoid(g) * u).astype(BF16)
            out = jnp.dot(hid, wd_ref[0, :width, :].astype(BF16), preferred_element_type=F32)
            if not first:
                out = acc_scr[rows, :] + out
            if last:
                gsl = gsl_ref[0, pl.ds(mi * chunks_per_tile, chunks_per_tile)]
                gcol = jnp.concatenate([jnp.sum(gsl[ci].T, axis=1, keepdims=True) for ci in range(chunks_per_tile)],
                                       axis=0)
                y_ref[0, rows, :] = (out * gcol).astype(BF16)
            else:
                acc_scr[rows, :] = out
            return carry

        lax.fori_loop(0, acc_scr.shape[0] // MOE_TM, m_step, 0)

    @pl.when(s == 0)
    def _():
        mlp(MOE_TF, True, False)

    @pl.when((s > 0) & (s < MOE_NF - 1))
    def _():
        mlp(MOE_TF, False, False)

    @pl.when(s == MOE_NF - 1)
    def _():
        mlp(MOE_F_LAST, False, True)


def _moe_call(xs, gsl, w_gate_t, w_up_t, w_down):
    n_e, slots, d = xs.shape
    per_e = pl.BlockSpec((1, slots, d), lambda e, s: (e, 0, 0))
    wspec = pl.BlockSpec((1, MOE_TF, d), lambda e, s: (e, s, 0))
    return pl.pallas_call(
        _moe_body,
        out_shape=jax.ShapeDtypeStruct((n_e, slots, d), BF16),
        grid=(n_e, MOE_NF),
        in_specs=[per_e, pl.BlockSpec((1,) + gsl.shape[1:], lambda e, s: (e, 0, 0, 0)), wspec, wspec, wspec],
        out_specs=per_e,
        scratch_shapes=[pltpu.VMEM((slots, d), F32)],
        compiler_params=_cparams(("arbitrary", "arbitrary")),
        name="moe",
    )(xs, gsl, w_gate_t, w_up_t, w_down)


def _combine_body(tab_ref, y_ref, posm_ref, x1_ref, gt2_ref, gpost_ref, o_ref, acc_ref, *, cap):
    n_e = N_EXPERTS
    b = pl.program_id(0)
    j = pl.program_id(1)
    pm = posm_ref[0].T
    win = 2 * SLOT_CHUNK
    slot_w = lax.broadcasted_iota(I32, (ROUTE_BLK, win), 1)
    slot_c = lax.broadcasted_iota(I32, (ROUTE_BLK, SLOT_CHUNK), 1)
    total = jnp.zeros((ROUTE_BLK, D_MODEL), F32)
    bases, extras = [], []
    for k in range(n_e):
        tab_base = (b * N_EXPERTS + k) * TAB_STRIDE
        start = tab_ref[tab_base + j]
        end = tab_ref[tab_base + j + 1]
        base = pl.multiple_of(jnp.minimum((start // SLOT_CHUNK) * SLOT_CHUNK, cap - win), SLOT_CHUNK)
        total = total + jnp.dot(_mask_bf16(pm[:, k:k + 1] == slot_w + base), y_ref[k, pl.ds(base, win), :],
                                preferred_element_type=F32)
        bases.append(base)
        extras.append(jnp.maximum(end - (base + win) + SLOT_CHUNK - 1, 0) // SLOT_CHUNK)

    def finish(moe):
        o_ref[0] = x1_ref[0] + gt2_ref[0] * _rms(moe, gpost_ref[...])

    finish(total)

    @pl.when(sum(extras) > 0)
    def _():
        acc_ref[...] = total
        for k in range(n_e):
            def extra_step(kk, carry, k=k):
                b2 = pl.multiple_of(bases[k] + win + kk * SLOT_CHUNK, SLOT_CHUNK)
                acc_ref[...] += jnp.dot(_mask_bf16(pm[:, k:k + 1] == slot_c + b2),
                                        y_ref[k, pl.ds(b2, SLOT_CHUNK), :], preferred_element_type=F32)
                return carry

            lax.fori_loop(0, extras[k], extra_step, 0)
        finish(acc_ref[...])


def _combine_call(tab_flat, y, posm, x1, mod, g_post2, *, cap):
    bsz, n_e, s = posm.shape
    d = D_MODEL
    tok = pl.BlockSpec((1, ROUTE_BLK, d), lambda b, j, tab: (b, j, 0))
    grid_spec = pltpu.PrefetchScalarGridSpec(
        num_scalar_prefetch=1,
        grid=(bsz, s // ROUTE_BLK),
        in_specs=[
            pl.BlockSpec((n_e, cap, d), lambda b, j, tab: (0, b, 0), pipeline_mode=pl.Buffered(1)),
            pl.BlockSpec((1, n_e, ROUTE_BLK), lambda b, j, tab: (b, 0, j)),
            tok,
            _mod_spec(5, lambda b: b, 3),
            pl.BlockSpec((1, d), lambda b, j, tab: (0, 0))],
        out_specs=tok,
        scratch_shapes=[pltpu.VMEM((ROUTE_BLK, d), F32)],
    )
    return pl.pallas_call(
        functools.partial(_combine_body, cap=cap),
        out_shape=jax.ShapeDtypeStruct((bsz, s, d), F32),
        grid_spec=grid_spec,
        compiler_params=_cparams(("arbitrary", "arbitrary")),
        name="combine",
    )(tab_flat, y, posm, x1, mod, g_post2)


def kernel(x, c, ctx, c_ctx, w_mod, b_mod, g_pre1, g_post1, g_pre2, g_post2, w_in, w_out, na_rpb,
           hg_lb_logits, hg_norm, w_router, w_gate, w_up, w_down):
    bsz, seq, d = x.shape
    assert (seq, d) == (SEQ, D_MODEL) and ctx.shape[1] == CTX_LEN and w_mod.shape[0] == 1
    cap = 2 * seq // N_EXPERTS
    assert cap % SLOT_CHUNK == 0

    assert bsz < MOD_ROWS
    mod = _mod_call(c, c_ctx, w_mod, b_mod)

    cos, sa, sb = _rope_tables(seq)

    qrot, qpl, krot, v, qh, zf, zb, ih, gh = _inproj_call(
        x, mod, lambda b: b, g_pre1, w_in, cos, sa, sb, rope=True, tm=IN_TM)
    ctx_flat = ctx.reshape(1, bsz * CTX_LEN, d)
    ctx_out = _inproj_call(ctx_flat, mod, lambda b: bsz, g_pre1, w_in,
                           cos[:bsz * CTX_LEN], sa[:bsz * CTX_LEN], sb[:bsz * CTX_LEN], rope=False, tm=CTX_LEN)
    _, _, kc, vc, qc, zfc, zbc, ic, _ = [a.reshape(bsz, CTX_LEN, COL_GROUP) for a in ctx_out]

    na_o = _na_call(qrot, qpl, krot, v, kc, vc, _na_bias(na_rpb[0]))
    hg_o = _hgrn_call(qh, zf, zb, ih, gh, qc, zfc, zbc, ic, hg_lb_logits, hg_norm)

    x1, h2, lt = _outproj_call(na_o, hg_o, x, w_out, mod, g_post1, g_pre2, w_router[0].T)

    posm, gate, tab = _route_call(lt, cap)
    tab_flat = tab.reshape(-1)
    xs, gsl = _dispatch_call(tab_flat, posm, gate, h2.reshape(bsz * seq, d), bsz=bsz, cap=cap)
    y = _moe_call(xs, gsl, jnp.swapaxes(w_gate[0], 1, 2), jnp.swapaxes(w_up[0], 1, 2), w_down[0])

    return _combine_call(tab_flat, y, posm, x1, mod, g_post2, cap=cap)
```
